```python
import math
import jax, jax.numpy as jnp
from jax import lax
import numpy as np

D_MODEL = 1024
BATCH = 4
SEQ = 8192
DEPTH = 1
DEC_BATCH = 128
DEC_SEQ = 4
PAST_LEN = 16384
PAGE_SIZE = 128

SSM_WIDTH = D_MODEL // 2
SSM_GROUP = 16
N_SSM_GROUPS = SSM_WIDTH // SSM_GROUP
SSM_STATE = 64
SSM_BLOCK = 128

HEAD_DIM = 64
ATTN_WIDTH = D_MODEL - SSM_WIDTH
N_HEADS = ATTN_WIDTH // HEAD_DIM
N_KV_HEADS = 2
GQA_GROUP = N_HEADS // N_KV_HEADS
Q_WIDTH = N_HEADS * HEAD_DIM
KV_WIDTH = N_KV_HEADS * HEAD_DIM
WINDOW = 128
ROPE_THETA = 500000.0
ROPE_DIM = HEAD_DIM // 4

N_MEM = 256
MEM_HEADS = 4
MEM_HEAD_DIM = D_MODEL // MEM_HEADS

D_FF = 2816
RMS_EPS = 1e-6
IN_WIDTH = SSM_WIDTH + Q_WIDTH + 2 * KV_WIDTH
NEG_INF = -1e30

kernel_name = "hymba_s5_swa_sink_macaron_memory_step"


def rms_norm(x, g):
    xf = x.astype(jnp.float32)
    y = xf * lax.rsqrt(jnp.mean(xf * xf, axis=-1, keepdims=True) + RMS_EPS)
    return (y * g.astype(jnp.float32)).astype(x.dtype)


def swiglu(x, w_gate, w_up, w_down):
    return (jax.nn.silu(x @ w_gate) * (x @ w_up)) @ w_down


def rope_partial(x, pos):
    half = ROPE_DIM // 2
    inv = ROPE_THETA ** (-jnp.arange(half, dtype=jnp.float32) * (2.0 / ROPE_DIM))
    ang = pos.astype(jnp.float32)[:, None] * inv[None, :]
    cos = jnp.cos(ang)[:, None, :]
    sin = jnp.sin(ang)[:, None, :]
    xr = x[..., :ROPE_DIM].astype(jnp.float32)
    x1, x2 = xr[..., :half], xr[..., half:]
    rot = jnp.concatenate([x1 * cos - x2 * sin, x2 * cos + x1 * sin], axis=-1)
    return jnp.concatenate([rot.astype(x.dtype), x[..., ROPE_DIM:]], axis=-1)


def sink_softmax(sc, sink):
    sink = sink.astype(jnp.float32)
    m = jnp.maximum(jnp.max(sc, axis=-1, keepdims=True), sink)
    e = jnp.exp(sc - m)
    return e / (jnp.sum(e, axis=-1, keepdims=True) + jnp.exp(sink - m))


def s5_discretise(a_re, a_im, log_step, b_re, b_im):
    f32 = jnp.float32
    a_re, a_im = a_re.astype(f32), a_im.astype(f32)
    dt = jnp.exp(log_step.astype(f32))[:, None]
    mag = jnp.exp(a_re * dt)
    lb_re, lb_im = mag * jnp.cos(a_im * dt), mag * jnp.sin(a_im * dt)
    den = a_re * a_re + a_im * a_im
    n_re, n_im = lb_re - 1.0, lb_im
    c_re = (n_re * a_re + n_im * a_im) / den
    c_im = (n_im * a_re - n_re * a_im) / den
    b_re, b_im = b_re.astype(f32), b_im.astype(f32)
    bb_re = c_re[..., None] * b_re - c_im[..., None] * b_im
    bb_im = c_re[..., None] * b_im + c_im[..., None] * b_re
    return lb_re, lb_im, bb_re, bb_im


def _ssm_combine(e1, e2):
    a1r, a1i, b1r, b1i = e1
    a2r, a2i, b2r, b2i = e2
    return (a1r * a2r - a1i * a2i, a1r * a2i + a1i * a2r,
            a2r * b1r - a2i * b1i + b2r, a2r * b1i + a2i * b1r + b2i)


def s5_scan(u, s0_re, s0_im, lb_re, lb_im, bb_re, bb_im, c_re, c_im, d):
    n, L, _ = u.shape
    blk = math.gcd(L, SSM_BLOCK)
    nblk = L // blk
    ub = u.reshape(n, nblk, blk, N_SSM_GROUPS, SSM_GROUP).swapaxes(0, 1)
    a_re = jnp.broadcast_to(lb_re, (n, blk, N_SSM_GROUPS, SSM_STATE))
    a_im = jnp.broadcast_to(lb_im, (n, blk, N_SSM_GROUPS, SSM_STATE))

    def step(carry, u_blk):
        sr, si = carry
        bu_re = jnp.einsum('nlgh,gph->nlgp', u_blk, bb_re)
        bu_im = jnp.einsum('nlgh,gph->nlgp', u_blk, bb_im)
        bu_re = bu_re.at[:, 0].add(lb_re * sr - lb_im * si)
        bu_im = bu_im.at[:, 0].add(lb_re * si + lb_im * sr)
        _, _, hr, hi = lax.associative_scan(_ssm_combine, (a_re, a_im, bu_re, bu_im), axis=1)
        y = (jnp.einsum('nlgp,ghp->nlgh', hr, c_re)
             - jnp.einsum('nlgp,ghp->nlgh', hi, c_im)
             + d[None, None] * u_blk)
        return (hr[:, -1], hi[:, -1]), y

    (sr, si), ys = lax.scan(step, (s0_re, s0_im), ub)
    y = ys.swapaxes(0, 1).reshape(n, L, SSM_WIDTH)
    return y, sr, si


def swa_prompt(q, k, v, sinks):
    n, s = q.shape[:2]
    nb = s // WINDOW
    qb = q.reshape(n, nb, WINDOW, N_KV_HEADS, GQA_GROUP, HEAD_DIM)

    def band(t):
        tb = t.reshape(n, nb, WINDOW, N_KV_HEADS, HEAD_DIM)
        prev = jnp.concatenate([jnp.zeros_like(tb[:, :1]), tb[:, :-1]], axis=1)
        return jnp.concatenate([prev, tb], axis=2)

    kk, vv = band(k), band(v)
    qi = jnp.arange(WINDOW)[:, None] + WINDOW
    kj = jnp.arange(2 * WINDOW)[None, :]
    diff = qi - kj
    blk = jnp.arange(nb)[:, None, None]
    mask = (diff >= 0) & (diff <= WINDOW) & (blk * WINDOW + kj - WINDOW >= 0)
    sc = jnp.einsum('bnqkgd,bnskd->bnkgqs', qb, kk).astype(jnp.float32) * (HEAD_DIM ** -0.5)
    sc = jnp.where(mask[None, :, None, None], sc, NEG_INF)
    pr = sink_softmax(sc, sinks.reshape(N_KV_HEADS, GQA_GROUP, 1, 1))
    o = jnp.einsum('bnkgqs,bnskd->bnqkgd', pr.astype(vv.dtype), vv)
    w = min(WINDOW, s)
    return o.reshape(n, s, Q_WIDTH), k[:, -w:], v[:, -w:]


def swa_sample(q, k, v, win_k, win_v, sinks):
    n, t = q.shape[:2]
    wb = win_k.shape[1]
    kk = jnp.concatenate([win_k.astype(k.dtype), k], axis=1)
    vv = jnp.concatenate([win_v.astype(v.dtype), v], axis=1)
    qb = q.reshape(n, t, N_KV_HEADS, GQA_GROUP, HEAD_DIM)
    diff = (wb + jnp.arange(t))[:, None] - jnp.arange(wb + t)[None, :]
    mask = (diff >= 0) & (diff <= WINDOW)
    sc = jnp.einsum('btkgd,bskd->bkgts', qb, kk).astype(jnp.float32) * (HEAD_DIM ** -0.5)
    sc = jnp.where(mask, sc, NEG_INF)
    pr = sink_softmax(sc, sinks.reshape(N_KV_HEADS, GQA_GROUP, 1, 1))
    o = jnp.einsum('bkgts,bskd->btkgd', pr.astype(vv.dtype), vv)
    return (o.reshape(n, t, Q_WIDTH), kk[:, t:].astype(win_k.dtype), vv[:, t:].astype(win_v.dtype))


def memory_kv(mem, g, w_k, w_v):
    n = mem.shape[0]
    m = rms_norm(mem, g)
    mk = (m @ w_k).reshape(n, N_MEM, MEM_HEADS, MEM_HEAD_DIM)
    mv = (m @ w_v).reshape(n, N_MEM, MEM_HEADS, MEM_HEAD_DIM)
    return mk, mv


def memory_attend(h, mk, mv, w_q, w_o):
    n, L, _ = h.shape
    q = (h @ w_q).reshape(n, L, MEM_HEADS, MEM_HEAD_DIM)
    sc = jnp.einsum('blhd,bmhd->bhlm', q, mk.astype(q.dtype)).astype(jnp.float32) * (MEM_HEAD_DIM ** -0.5)
    pr = jax.nn.softmax(sc, axis=-1)
    o = jnp.einsum('bhlm,bmhd->blhd', pr.astype(h.dtype), mv.astype(h.dtype))
    return o.reshape(n, L, D_MODEL) @ w_o


def decoder_layer(x, pos, s0_re, s0_im, win_k, win_v, mem_k, mem_v, p):
    f32 = jnp.float32
    n, L, _ = x.shape
    h = x
    h = h + 0.5 * rms_norm(swiglu(rms_norm(h, p['ffn1_pre_g']), p['ffn1_w_gate'], p['ffn1_w_up'],
                                  p['ffn1_w_down']), p['ffn1_post_g'])
    z = rms_norm(h, p['mix_pre_g']) @ p['w_in']
    u = z[..., :SSM_WIDTH]
    o1 = SSM_WIDTH + Q_WIDTH
    q = z[..., SSM_WIDTH:o1].reshape(n, L, N_HEADS, HEAD_DIM)
    k = z[..., o1:o1 + KV_WIDTH].reshape(n, L, N_KV_HEADS, HEAD_DIM)
    v = z[..., o1 + KV_WIDTH:].reshape(n, L, N_KV_HEADS, HEAD_DIM)
    lb_re, lb_im, bb_re, bb_im = s5_discretise(p['ssm_a_re'], p['ssm_a_im'], p['ssm_log_step'],
                                               p['ssm_b_re'], p['ssm_b_im'])
    y, s_re, s_im = s5_scan(u.astype(f32), s0_re.astype(f32), s0_im.astype(f32),
                            lb_re, lb_im, bb_re, bb_im,
                            p['ssm_c_re'].astype(f32), p['ssm_c_im'].astype(f32),
                            p['ssm_d'].astype(f32).reshape(N_SSM_GROUPS, SSM_GROUP))
    g = jax.nn.gelu(y)
    y_ssm = (g * jax.nn.sigmoid(g @ p['ssm_w_glu'].astype(f32) + p['ssm_b_glu'].astype(f32))).astype(x.dtype)
    q = rope_partial(q, pos)
    k = rope_partial(k, pos)
    if win_k is None:
        o, nk, nv = swa_prompt(q, k, v, p['attn_sinks'])
    else:
        o, nk, nv = swa_sample(q, k, v, win_k, win_v, p['attn_sinks'])
    mixed = jnp.concatenate([rms_norm(y_ssm, p['ssm_out_g']), rms_norm(o, p['attn_out_g'])], axis=-1) @ p['w_out']
    h = h + rms_norm(mixed, p['mix_post_g'])
    c = memory_attend(rms_norm(h, p['xa_pre_g']), mem_k, mem_v, p['w_mem_q'], p['w_mem_o'])
    h = h + rms_norm(c, p['xa_post_g'])
    h = h + 0.5 * rms_norm(swiglu(rms_norm(h, p['ffn2_pre_g']), p['ffn2_w_gate'], p['ffn2_w_up'],
                                  p['ffn2_w_down']), p['ffn2_post_g'])
    return h, s_re.astype(s0_re.dtype), s_im.astype(s0_im.dtype), nk, nv


def setup_inputs(seed: int = 0) -> dict:
    key = jax.random.key(seed)
    keys = jax.random.split(key, 64)
    ctr = [0]
    f32 = jnp.float32

    def nk():
        ctr[0] += 1
        return keys[ctr[0]]

    def normal(shape, scale):
        return scale * jax.random.normal(nk(), shape, f32)

    def gain(m):
        return 1.0 + normal((m,), 0.01)

    win = min(WINDOW, PAST_LEN)
    G, P, H = N_SSM_GROUPS, SSM_STATE, SSM_GROUP
    d_in = D_MODEL ** -0.5
    return {
        'x_prompt': normal((BATCH, SEQ, D_MODEL), 1.0),
        'x_sample': normal((DEC_BATCH, DEC_SEQ, D_MODEL), 1.0),
        'state_ssm_re': normal((DEC_BATCH, G, P), 0.3),
        'state_ssm_im': normal((DEC_BATCH, G, P), 0.3),
        'cache_swa_k': normal((DEC_BATCH, win, N_KV_HEADS, HEAD_DIM), 1.0),
        'cache_swa_v': normal((DEC_BATCH, win, N_KV_HEADS, HEAD_DIM), 1.0),
        'cache_mem_k': normal((DEC_BATCH, N_MEM, MEM_HEADS, MEM_HEAD_DIM), 1.0),
        'cache_mem_v': normal((DEC_BATCH, N_MEM, MEM_HEADS, MEM_HEAD_DIM), 1.0),
        'mem_prompt': normal((BATCH, N_MEM, D_MODEL), 1.0),
        'ffn1_pre_g': gain(D_MODEL),
        'ffn1_w_gate': normal((D_MODEL, D_FF), d_in),
        'ffn1_w_up': normal((D_MODEL, D_FF), d_in),
        'ffn1_w_down': normal((D_FF, D_MODEL), D_FF ** -0.5),
        'ffn1_post_g': gain(D_MODEL),
        'mix_pre_g': gain(D_MODEL),
        'w_in': normal((D_MODEL, IN_WIDTH), d_in),
        'ssm_a_re': -0.5 + normal((G, P), 0.01),
        'ssm_a_im': math.pi * jnp.broadcast_to(jnp.arange(P, dtype=f32), (G, P)) + normal((G, P), 0.01),
        'ssm_log_step': jax.random.uniform(nk(), (G,), f32, math.log(1e-3), math.log(1e-1)),
        'ssm_b_re': normal((G, P, H), (2 * H) ** -0.5),
        'ssm_b_im': normal((G, P, H), (2 * H) ** -0.5),
        'ssm_c_re': normal((G, H, P), (2 * P) ** -0.5),
        'ssm_c_im': normal((G, H, P), (2 * P) ** -0.5),
        'ssm_d': normal((SSM_WIDTH,), 1.0),
        'ssm_w_glu': normal((SSM_WIDTH, SSM_WIDTH), SSM_WIDTH ** -0.5),
        'ssm_b_glu': normal((SSM_WIDTH,), 0.01),
        'attn_sinks': normal((N_HEADS,), 0.5),
        'ssm_out_g': gain(SSM_WIDTH),
        'attn_out_g': gain(Q_WIDTH),
        'w_out': normal((SSM_WIDTH + Q_WIDTH, D_MODEL), (SSM_WIDTH + Q_WIDTH) ** -0.5),
        'mix_post_g': gain(D_MODEL),
        'mem_norm_g': gain(D_MODEL),
        'w_mem_q': normal((D_MODEL, D_MODEL), d_in),
        'w_mem_k': normal((D_MODEL, D_MODEL), d_in),
        'w_mem_v': normal((D_MODEL, D_MODEL), d_in),
        'w_mem_o': normal((D_MODEL, D_MODEL), d_in),
        'xa_pre_g': gain(D_MODEL),
        'xa_post_g': gain(D_MODEL),
        'ffn2_pre_g': gain(D_MODEL),
        'ffn2_w_gate': normal((D_MODEL, D_FF), d_in),
        'ffn2_w_up': normal((D_MODEL, D_FF), d_in),
        'ffn2_w_down': normal((D_FF, D_MODEL), D_FF ** -0.5),
        'ffn2_post_g': gain(D_MODEL),
    }


def reference(x_prompt, x_sample, state_ssm_re, state_ssm_im, cache_swa_k, cache_swa_v,
              cache_mem_k, cache_mem_v, mem_prompt,
              ffn1_pre_g, ffn1_w_gate, ffn1_w_up, ffn1_w_down, ffn1_post_g,
              mix_pre_g, w_in, ssm_a_re, ssm_a_im, ssm_log_step, ssm_b_re, ssm_b_im,
              ssm_c_re, ssm_c_im, ssm_d, ssm_w_glu, ssm_b_glu, attn_sinks,
              ssm_out_g, attn_out_g, w_out, mix_post_g,
              mem_norm_g, w_mem_q, w_mem_k, w_mem_v, w_mem_o, xa_pre_g, xa_post_g,
              ffn2_pre_g, ffn2_w_gate, ffn2_w_up, ffn2_w_down, ffn2_post_g):
    p = dict(ffn1_pre_g=ffn1_pre_g, ffn1_w_gate=ffn1_w_gate, ffn1_w_up=ffn1_w_up,
             ffn1_w_down=ffn1_w_down, ffn1_post_g=ffn1_post_g,
             mix_pre_g=mix_pre_g, w_in=w_in, ssm_a_re=ssm_a_re, ssm_a_im=ssm_a_im,
             ssm_log_step=ssm_log_step, ssm_b_re=ssm_b_re, ssm_b_im=ssm_b_im,
             ssm_c_re=ssm_c_re, ssm_c_im=ssm_c_im, ssm_d=ssm_d, ssm_w_glu=ssm_w_glu,
             ssm_b_glu=ssm_b_glu, attn_sinks=attn_sinks, ssm_out_g=ssm_out_g,
             attn_out_g=attn_out_g, w_out=w_out, mix_post_g=mix_post_g,
             w_mem_q=w_mem_q, w_mem_o=w_mem_o, xa_pre_g=xa_pre_g, xa_post_g=xa_post_g,
             ffn2_pre_g=ffn2_pre_g, ffn2_w_gate=ffn2_w_gate, ffn2_w_up=ffn2_w_up,
             ffn2_w_down=ffn2_w_down, ffn2_post_g=ffn2_post_g)
    n_p, s_p, _ = x_prompt.shape
    n_s, t_s, _ = x_sample.shape
    pm_k, pm_v = memory_kv(mem_prompt, mem_norm_g, w_mem_k, w_mem_v)
    h_p = x_prompt
    h_s = x_sample
    for _ in range(DEPTH):
        z0 = jnp.zeros((n_p, N_SSM_GROUPS, SSM_STATE), x_prompt.dtype)
        h_p, p_sre, p_sim, p_wk, p_wv = decoder_layer(
            h_p, jnp.arange(s_p, dtype=jnp.int32), z0, z0, None, None, pm_k, pm_v, p)
        h_s, s_sre, s_sim, s_wk, s_wv = decoder_layer(
            h_s, PAST_LEN + jnp.arange(t_s, dtype=jnp.int32), state_ssm_re, state_ssm_im,
            cache_swa_k, cache_swa_v, cache_mem_k, cache_mem_v, p)
    return (h_p, h_s, p_sre, p_sim, p_wk, p_wv, pm_k, pm_v, s_sre, s_sim, s_wk, s_wv)
```

```python
import functools
import math

import jax
import jax.numpy as jnp
from jax import lax
from jax.experimental import pallas as pl
from jax.experimental.pallas import tpu as pltpu

F32 = jnp.float32
BF16 = jnp.bfloat16

D_MODEL = 1024
PAST_LEN = 16384
SSM_WIDTH = 512
SSM_GROUP = 16
N_SSM_GROUPS = 32
SSM_STATE = 64
HEAD_DIM = 64
N_HEADS = 8
N_KV_HEADS = 2
GQA_GROUP = 4
Q_WIDTH = 512
KV_WIDTH = 128
WINDOW = 128
ROPE_THETA = 500000.0
ROPE_DIM = 16
N_MEM = 256
MEM_HEADS = 4
MEM_HEAD_DIM = 256
D_FF = 2816
RMS_EPS = 1e-6
IN_WIDTH = SSM_WIDTH + Q_WIDTH + 2 * KV_WIDTH
NEG_INF = -1e30

LANES = 128
FF_CHUNK = 256
N_FF_CHUNKS = D_FF // FF_CHUNK
GROUPS_PER_LANE_BLOCK = LANES // SSM_GROUP
N_LANE_BLOCKS = SSM_WIDTH // LANES
STATE_LANES = GROUPS_PER_LANE_BLOCK * SSM_STATE
VMEM_LIMIT = 56 * 1024 * 1024


def _rms(x, g):
    return x * lax.rsqrt(jnp.mean(x * x, axis=-1, keepdims=True) + RMS_EPS) * g


def _const_spec(shape):
    nd = len(shape)
    return pl.BlockSpec(shape, lambda *_: (0,) * nd, pipeline_mode=pl.Buffered(1))


def _params(sem):
    return pltpu.CompilerParams(dimension_semantics=sem, vmem_limit_bytes=VMEM_LIMIT)


def _ffn_tile(x, pre_g, wgu_ref, wd_ref, post_g, act_ref):
    xn = _rms(x, pre_g).astype(BF16)
    for c in range(N_FF_CHUNKS):
        gu = jnp.dot(xn, wgu_ref[c], preferred_element_type=F32)
        gate = gu[:, :FF_CHUNK]
        up = gu[:, FF_CHUNK:]
        act = gate * (1.0 / (1.0 + jnp.exp(-gate))) * up
        act_ref[:, c * FF_CHUNK:(c + 1) * FF_CHUNK] = act.astype(BF16)
    down = jnp.dot(act_ref[...], wd_ref[...], preferred_element_type=F32)
    return x + 0.5 * _rms(down, post_g)


def _rope(x, cos, sin_lo, sin_hi):
    w = x.shape[1]
    half = ROPE_DIM // 2
    return (x * cos + pltpu.roll(x, w - half, 1) * sin_lo + pltpu.roll(x, half, 1) * sin_hi)


def _ffn_in_kernel(x_ref, pre_g_ref, wgu_ref, wd_ref, post_g_ref, mix_g_ref, win_ref,
                   cos_ref, slo_ref, shi_ref,
                   h_ref, u_ref, q_ref, k_ref, v_ref, act_ref):
    h = _ffn_tile(x_ref[...], pre_g_ref[...], wgu_ref, wd_ref, post_g_ref[...], act_ref)
    h_ref[...] = h
    z = jnp.dot(_rms(h, mix_g_ref[...]).astype(BF16), win_ref[...], preferred_element_type=F32)
    u_ref[...] = z[:, :SSM_WIDTH]
    o1 = SSM_WIDTH + Q_WIDTH
    cos, slo, shi = cos_ref[...], slo_ref[...], shi_ref[...]
    rep = Q_WIDTH // LANES
    q = _rope(z[:, SSM_WIDTH:o1], jnp.tile(cos, (1, rep)), jnp.tile(slo, (1, rep)), jnp.tile(shi, (1, rep)))
    q_ref[...] = (q * (HEAD_DIM ** -0.5)).astype(BF16)
    k_ref[...] = _rope(z[:, o1:o1 + KV_WIDTH], cos, slo, shi)
    v_ref[...] = z[:, o1 + KV_WIDTH:]


def _ffn_in(x, pre_g, wgu, wd, post_g, mix_g, win, cos, slo, shi, tm):
    t = x.shape[0]
    n_pos_tiles = cos.shape[0] // tm
    row = lambda w: pl.BlockSpec((tm, w), lambda i: (i, 0))
    tab = pl.BlockSpec((tm, LANES), lambda i: (i % n_pos_tiles, 0))
    return pl.pallas_call(
        _ffn_in_kernel,
        grid=(t // tm,),
        in_specs=[row(D_MODEL), _const_spec((1, D_MODEL)), _const_spec(wgu.shape), _const_spec(wd.shape),
                  _const_spec((1, D_MODEL)), _const_spec((1, D_MODEL)), _const_spec(win.shape),
                  tab, tab, tab],
        out_specs=[row(D_MODEL), row(SSM_WIDTH), row(Q_WIDTH), row(KV_WIDTH), row(KV_WIDTH)],
        out_shape=[jax.ShapeDtypeStruct((t, D_MODEL), F32), jax.ShapeDtypeStruct((t, SSM_WIDTH), F32),
                   jax.ShapeDtypeStruct((t, Q_WIDTH), BF16), jax.ShapeDtypeStruct((t, KV_WIDTH), F32),
                   jax.ShapeDtypeStruct((t, KV_WIDTH), F32)],
        scratch_shapes=[pltpu.VMEM((tm, D_FF), BF16)],
        compiler_params=_params(("arbitrary",)),
        name="ffn_in",
    )(x, pre_g, wgu, wd, post_g, mix_g, win, cos, slo, shi)


def _complex_step(s_re, s_im, l_re, l_im, x_re, x_im):
    return l_re * s_re - l_im * s_im + x_re, l_re * s_im + l_im * s_re + x_im


def _ssm_prompt_kernel(u_ref, m_ref, w_ref, v_ref, lam_ref, d_ref, y_ref, st_ref,
                       x_scr, ss_scr, s_scr, *, lc, nb, ncl):
    tt = pl.program_id(1)

    @pl.when(tt == 0)
    def _():
        s_scr[...] = jnp.zeros_like(s_scr)

    def piece(n, j):
        return u_ref[n, pl.ds(j, ncl, stride=lc), :]

    a = jnp.concatenate(
        [jnp.concatenate([piece(n, j) for j in range(lc)], axis=1) for n in range(nb)], axis=0).astype(BF16)
    x = jnp.dot(a, w_ref[0], preferred_element_type=F32)
    nq = STATE_LANES // LANES
    for qq in range(2 * nq):
        x_scr[qq] = x[:, qq * LANES:(qq + 1) * LANES]
    l_re = [jnp.broadcast_to(lam_ref[0, :, qq * LANES:(qq + 1) * LANES], (nb, LANES)) for qq in range(nq)]
    l_im = [jnp.broadcast_to(lam_ref[0, :, (nq + qq) * LANES:(nq + qq + 1) * LANES], (nb, LANES)) for qq in range(nq)]

    def body(c, carry):
        rows = pl.ds(c, nb, stride=ncl)
        new = []
        for qq in range(nq):
            s_re, s_im = carry[qq], carry[nq + qq]
            ss_scr[qq, rows, :] = s_re
            ss_scr[nq + qq, rows, :] = s_im
            new.append(_complex_step(s_re, s_im, l_re[qq], l_im[qq], x_scr[qq, rows, :], x_scr[nq + qq, rows, :]))
        return tuple(r for r, _ in new) + tuple(i for _, i in new)

    s_fin = lax.fori_loop(0, ncl, body, tuple(s_scr[qq] for qq in range(2 * nq)), unroll=4)
    for qq in range(2 * nq):
        s_scr[qq] = s_fin[qq]
    st_ref[0] = jnp.concatenate(s_fin, axis=1)
    s_start = jnp.concatenate([ss_scr[qq] for qq in range(2 * nq)], axis=1).astype(BF16)
    y = (jnp.dot(a, m_ref[0], preferred_element_type=F32)
         + jnp.dot(s_start, v_ref[0], preferred_element_type=F32))
    d = d_ref[...]
    for n in range(nb):
        for j in range(lc):
            y_ref[n, pl.ds(j, ncl, stride=lc), :] = (
                y[n * ncl:(n + 1) * ncl, j * LANES:(j + 1) * LANES] + d * piece(n, j))


def _ssm_prompt(u3, m, w, v, lam, d, lc, tl):
    nb, seq, _ = u3.shape
    ncl = tl // lc
    nslab = 2 * STATE_LANES // LANES
    kern = functools.partial(_ssm_prompt_kernel, lc=lc, nb=nb, ncl=ncl)
    wspec = lambda a: pl.BlockSpec((1,) + a.shape[1:], lambda b, t: (b, 0, 0))
    return pl.pallas_call(
        kern,
        grid=(N_LANE_BLOCKS, seq // tl),
        in_specs=[pl.BlockSpec((nb, tl, LANES), lambda b, t: (0, t, b)), wspec(m), wspec(w), wspec(v), wspec(lam),
                  pl.BlockSpec((1, LANES), lambda b, t: (0, b))],
        out_specs=[pl.BlockSpec((nb, tl, LANES), lambda b, t: (0, t, b)),
                   pl.BlockSpec((1, nb, 2 * STATE_LANES), lambda b, t: (b, 0, 0))],
        out_shape=[jax.ShapeDtypeStruct(u3.shape, F32),
                   jax.ShapeDtypeStruct((N_LANE_BLOCKS, nb, 2 * STATE_LANES), F32)],
        scratch_shapes=[pltpu.VMEM((nslab, nb * ncl, LANES), F32), pltpu.VMEM((nslab, nb * ncl, LANES), F32),
                        pltpu.VMEM((nslab, nb, LANES), F32)],
        compiler_params=_params(("arbitrary", "arbitrary")),
        name="ssm_prompt",
    )(u3, m, w, v, lam, d)


def _ssm_sample_kernel(u_ref, sre_ref, sim_ref, m_ref, w_ref, v_ref, lam_ref, d_ref,
                       y_ref, ore_ref, oim_ref, *, lc, ns):
    def piece(j):
        return u_ref[pl.ds(j, ns, stride=lc), :]

    a = jnp.concatenate([piece(j) for j in range(lc)], axis=1).astype(BF16)
    s_re, s_im = sre_ref[...], sim_ref[...]
    x = jnp.dot(a, w_ref[0], preferred_element_type=F32)
    e_re, e_im = _complex_step(s_re, s_im, lam_ref[0, :, :STATE_LANES], lam_ref[0, :, STATE_LANES:],
                               x[:, :STATE_LANES], x[:, STATE_LANES:])
    ore_ref[...] = e_re
    oim_ref[...] = e_im
    s0 = jnp.concatenate([s_re, s_im], axis=1).astype(BF16)
    y = (jnp.dot(a, m_ref[0], preferred_element_type=F32) + jnp.dot(s0, v_ref[0], preferred_element_type=F32))
    d = d_ref[...]
    for j in range(lc):
        y_ref[pl.ds(j, ns, stride=lc), :] = y[:, j * LANES:(j + 1) * LANES] + d * piece(j)


def _ssm_sample(u, s_re, s_im, m, w, v, lam, d, lc):
    t = u.shape[0]
    ns = t // lc
    kern = functools.partial(_ssm_sample_kernel, lc=lc, ns=ns)
    wspec = lambda a: pl.BlockSpec((1,) + a.shape[1:], lambda b: (b, 0, 0))
    col = lambda rows, width: pl.BlockSpec((rows, width), lambda b: (0, b))
    return pl.pallas_call(
        kern,
        grid=(N_LANE_BLOCKS,),
        in_specs=[col(t, LANES), col(ns, STATE_LANES), col(ns, STATE_LANES), wspec(m), wspec(w), wspec(v),
                  wspec(lam), col(1, LANES)],
        out_specs=[col(t, LANES), col(ns, STATE_LANES), col(ns, STATE_LANES)],
        out_shape=[jax.ShapeDtypeStruct(u.shape, F32), jax.ShapeDtypeStruct(s_re.shape, F32),
                   jax.ShapeDtypeStruct(s_im.shape, F32)],
        compiler_params=_params(("arbitrary",)),
        name="ssm_sample",
    )(u, s_re, s_im, m, w, v, lam, d)


def _ssm_tables(a_re, a_im, log_step, b_re, b_im, c_re, c_im, lc):
    g, p, h = b_re.shape
    nbk, r = N_LANE_BLOCKS, GROUPS_PER_LANE_BLOCK
    hi = lax.Precision.HIGHEST
    dt = jnp.exp(log_step)[:, None]
    mag = jnp.exp(a_re * dt)
    lb_re, lb_im = mag * jnp.cos(a_im * dt), mag * jnp.sin(a_im * dt)
    den = a_re * a_re + a_im * a_im
    n_re, n_im = lb_re - 1.0, lb_im
    cf_re = (n_re * a_re + n_im * a_im) / den
    cf_im = (n_im * a_re - n_re * a_im) / den
    bb_re = cf_re[..., None] * b_re - cf_im[..., None] * b_im
    bb_im = cf_re[..., None] * b_im + cf_im[..., None] * b_re
    pw_re, pw_im = [jnp.ones_like(lb_re)], [jnp.zeros_like(lb_re)]
    for _ in range(lc):
        pr, pi = pw_re[-1], pw_im[-1]
        pw_re.append(pr * lb_re - pi * lb_im)
        pw_im.append(pr * lb_im + pi * lb_re)
    pw_re, pw_im = jnp.stack(pw_re), jnp.stack(pw_im)
    e_re = pw_re[:lc, :, :, None] * bb_re[None] - pw_im[:lc, :, :, None] * bb_im[None]
    e_im = pw_re[:lc, :, :, None] * bb_im[None] + pw_im[:lc, :, :, None] * bb_re[None]
    kk = (jnp.einsum('kgph,gip->kghi', e_re, c_re, precision=hi)
          - jnp.einsum('kgph,gip->kghi', e_im, c_im, precision=hi))
    eye_r = jnp.eye(r, dtype=F32)
    lag = (jnp.arange(lc)[None, :, None] - jnp.arange(lc)[:, None, None] == jnp.arange(lc)[None, None, :])
    m = jnp.einsum('jJk,kbrhi,rR->bjrhJRi', lag.astype(F32), kk.reshape(lc, nbk, r, h, h), eye_r)
    m = m.reshape(nbk, lc * LANES, lc * LANES)
    w_re = jnp.einsum('jbrph,rR->bjrhRp', e_re[::-1].reshape(lc, nbk, r, p, h), eye_r)
    w_im = jnp.einsum('jbrph,rR->bjrhRp', e_im[::-1].reshape(lc, nbk, r, p, h), eye_r)
    w = jnp.concatenate([w_re.reshape(nbk, lc * LANES, STATE_LANES), w_im.reshape(nbk, lc * LANES, STATE_LANES)], axis=2)
    q_re, q_im = pw_re[1:, :, None, :], pw_im[1:, :, None, :]
    v_re = (c_re[None] * q_re - c_im[None] * q_im).reshape(lc, nbk, r, h, p)
    v_im = -(c_re[None] * q_im + c_im[None] * q_re).reshape(lc, nbk, r, h, p)
    v_re = jnp.einsum('jbrhp,rR->bRpjrh', v_re, eye_r).reshape(nbk, STATE_LANES, lc * LANES)
    v_im = jnp.einsum('jbrhp,rR->bRpjrh', v_im, eye_r).reshape(nbk, STATE_LANES, lc * LANES)
    v = jnp.concatenate([v_re, v_im], axis=1)
    lam = jnp.concatenate([pw_re[lc].reshape(nbk, 1, STATE_LANES), pw_im[lc].reshape(nbk, 1, STATE_LANES)], axis=2)
    return m.astype(BF16), w.astype(BF16), v.astype(BF16), lam


def _sink_softmax_rows(sc, valid, sink):
    sc = jnp.where(valid, sc, NEG_INF)
    m = jnp.maximum(jnp.max(sc, axis=-1, keepdims=True), sink)
    e = jnp.exp(sc - m)
    return e / (jnp.sum(e, axis=-1, keepdims=True) + jnp.exp(sink - m))


def _nt_dot(a, b):
    return lax.dot_general(a, b, (((1,), (1,)), ((), ())), preferred_element_type=F32)


def _swa_prompt_kernel(sink_ref, q_ref, kp_ref, kc_ref, vp_ref, vc_ref, g_ref, o_ref):
    blk = pl.program_id(1)
    q = q_ref[...]
    k2 = jnp.concatenate([kp_ref[...], kc_ref[...]], axis=0).astype(BF16)
    v2 = jnp.concatenate([vp_ref[...], vc_ref[...]], axis=0).astype(BF16)
    rows = GQA_GROUP * WINDOW
    qi = lax.broadcasted_iota(jnp.int32, (rows, 2 * WINDOW), 0) % WINDOW + WINDOW
    kj = lax.broadcasted_iota(jnp.int32, (rows, 2 * WINDOW), 1)
    diff = qi - kj
    valid = (diff >= 0) & (diff <= WINDOW) & ((kj >= WINDOW) | (blk > 0))
    outs = []
    for kh in range(N_KV_HEADS):
        q4 = jnp.concatenate([q[:, (kh * GQA_GROUP + g) * HEAD_DIM:(kh * GQA_GROUP + g + 1) * HEAD_DIM]
                              for g in range(GQA_GROUP)], axis=0)
        sc = _nt_dot(q4, k2[:, kh * HEAD_DIM:(kh + 1) * HEAD_DIM])
        sink = jnp.concatenate([jnp.full((WINDOW, 1), sink_ref[kh * GQA_GROUP + g], F32) for g in range(GQA_GROUP)],
                               axis=0)
        pr = _sink_softmax_rows(sc, valid, sink).astype(BF16)
        o4 = jnp.dot(pr, v2[:, kh * HEAD_DIM:(kh + 1) * HEAD_DIM], preferred_element_type=F32)
        outs += [o4[g * WINDOW:(g + 1) * WINDOW] for g in range(GQA_GROUP)]
    o = jnp.concatenate(outs, axis=1)
    o_ref[...] = _rms(o, g_ref[...]).astype(BF16)


def _swa_prompt(q, k, v, sinks, out_g, n, seq):
    nblk = seq // WINDOW
    cur = lambda w: pl.BlockSpec((WINDOW, w), lambda b, i: (b * nblk + i, 0))
    prev = lambda w: pl.BlockSpec((WINDOW, w), lambda b, i: (b * nblk + jnp.maximum(i - 1, 0), 0))
    return pl.pallas_call(
        _swa_prompt_kernel,
        grid=(n, nblk),
        in_specs=[pl.BlockSpec(memory_space=pltpu.SMEM), cur(Q_WIDTH), prev(KV_WIDTH), cur(KV_WIDTH),
                  prev(KV_WIDTH), cur(KV_WIDTH), pl.BlockSpec((1, Q_WIDTH), lambda b, i: (0, 0))],
        out_specs=cur(Q_WIDTH),
        out_shape=jax.ShapeDtypeStruct((n * seq, Q_WIDTH), BF16),
        compiler_params=_params(("arbitrary", "arbitrary")),
        name="swa_prompt",
    )(sinks, q, k, k, v, v, out_g)


def _swa_sample_kernel(sink_ref, q_ref, kn_ref, vn_ref, ck_ref, cv_ref, g_ref, o_ref, nk_ref, nv_ref, *, ns, t):
    rows = GQA_GROUP * t
    tok = lax.broadcasted_iota(jnp.int32, (rows, WINDOW), 0) % t
    valid_c = lax.broadcasted_iota(jnp.int32, (rows, WINDOW), 1) >= tok
    tok_n = lax.broadcasted_iota(jnp.int32, (rows, t), 0) % t
    valid_n = lax.broadcasted_iota(jnp.int32, (rows, t), 1) <= tok_n
    gain = g_ref[...]

    def body(s, _):
        q = q_ref[s].astype(F32)
        kn, vn = kn_ref[s], vn_ref[s]
        ck, cv = ck_ref[s], cv_ref[s]
        nk_ref[s, :WINDOW - t, :] = ck[t:]
        nk_ref[s, WINDOW - t:, :] = kn
        nv_ref[s, :WINDOW - t, :] = cv[t:]
        nv_ref[s, WINDOW - t:, :] = vn
        outs = []
        for kh in range(N_KV_HEADS):
            hs = slice(kh * HEAD_DIM, (kh + 1) * HEAD_DIM)
            q4 = jnp.concatenate([q[:, (kh * GQA_GROUP + g) * HEAD_DIM:(kh * GQA_GROUP + g + 1) * HEAD_DIM]
                                  for g in range(GQA_GROUP)], axis=0).astype(BF16)
            sc_c = jnp.where(valid_c, _nt_dot(q4, ck[:, hs].astype(BF16)), NEG_INF)
            sc_n = jnp.where(valid_n, _nt_dot(q4, kn[:, hs].astype(BF16)), NEG_INF)
            sink = jnp.concatenate([jnp.full((t, 1), sink_ref[kh * GQA_GROUP + g], F32) for g in range(GQA_GROUP)],
                                   axis=0)
            m = jnp.maximum(jnp.maximum(jnp.max(sc_c, axis=-1, keepdims=True),
                                        jnp.max(sc_n, axis=-1, keepdims=True)), sink)
            e_c, e_n = jnp.exp(sc_c - m), jnp.exp(sc_n - m)
            den = jnp.sum(e_c, axis=-1, keepdims=True) + jnp.sum(e_n, axis=-1, keepdims=True) + jnp.exp(sink - m)
            o4 = (jnp.dot((e_c / den).astype(BF16), cv[:, hs].astype(BF16), preferred_element_type=F32)
                  + jnp.dot((e_n / den).astype(BF16), vn[:, hs].astype(BF16), preferred_element_type=F32))
            outs += [o4[g * t:(g + 1) * t] for g in range(GQA_GROUP)]
        o = jnp.concatenate(outs, axis=1)
        o_ref[s] = _rms(o, gain).astype(BF16)
        return 0

    lax.fori_loop(0, ns, body, 0)


def _swa_sample(q3, kn3, vn3, ck, cv, sinks, out_g, sb):
    ns, t, _ = q3.shape
    blk = lambda a: pl.BlockSpec((sb,) + a.shape[1:], lambda i: (i, 0, 0))
    kern = functools.partial(_swa_sample_kernel, ns=sb, t=t)
    return pl.pallas_call(
        kern,
        grid=(ns // sb,),
        in_specs=[pl.BlockSpec(memory_space=pltpu.SMEM), blk(q3), blk(kn3), blk(vn3), blk(ck), blk(cv),
                  pl.BlockSpec((1, Q_WIDTH), lambda i: (0, 0))],
        out_specs=[blk(q3), blk(ck), blk(cv)],
        out_shape=[jax.ShapeDtypeStruct(q3.shape, BF16), jax.ShapeDtypeStruct(ck.shape, F32),
                   jax.ShapeDtypeStruct(cv.shape, F32)],
        compiler_params=_params(("arbitrary",)),
        name="swa_sample",
    )(sinks, q3, kn3, vn3, ck, cv, out_g)


def _gelu_tanh(x):
    return 0.5 * x * (1.0 + jnp.tanh(math.sqrt(2.0 / math.pi) * (x + 0.044715 * (x * x * x))))


def _merge_kernel(h_ref, y_ref, at_ref, wglu_ref, bglu_ref, sg_ref, wout_ref, post_g_ref, xa_g_ref, wq_ref,
                  h2_ref, qm_ref):
    g = _gelu_tanh(y_ref[...])
    lin = jnp.dot(g.astype(BF16), wglu_ref[...], preferred_element_type=F32) + bglu_ref[...]
    y_ssm = g * (1.0 / (1.0 + jnp.exp(-lin)))
    ssm_n = _rms(y_ssm, sg_ref[...]).astype(BF16)
    mixed = (jnp.dot(ssm_n, wout_ref[:SSM_WIDTH, :], preferred_element_type=F32)
             + jnp.dot(at_ref[...], wout_ref[SSM_WIDTH:, :], preferred_element_type=F32))
    h2 = h_ref[...] + _rms(mixed, post_g_ref[...])
    h2_ref[...] = h2
    qm = jnp.dot(_rms(h2, xa_g_ref[...]).astype(BF16), wq_ref[...], preferred_element_type=F32)
    qm_ref[...] = (qm * (MEM_HEAD_DIM ** -0.5)).astype(BF16)


def _merge(h, y, at, wglu, bglu, sg, wout, post_g, xa_g, wq, tm):
    t = h.shape[0]
    row = lambda w: pl.BlockSpec((tm, w), lambda i: (i, 0))
    return pl.pallas_call(
        _merge_kernel,
        grid=(t // tm,),
        in_specs=[row(D_MODEL), row(SSM_WIDTH), row(Q_WIDTH), _const_spec(wglu.shape), _const_spec((1, SSM_WIDTH)),
                  _const_spec((1, SSM_WIDTH)), _const_spec(wout.shape), _const_spec((1, D_MODEL)),
                  _const_spec((1, D_MODEL)), _const_spec(wq.shape)],
        out_specs=[row(D_MODEL), row(D_MODEL)],
        out_shape=[jax.ShapeDtypeStruct((t, D_MODEL), F32), jax.ShapeDtypeStruct((t, D_MODEL), BF16)],
        compiler_params=_params(("arbitrary",)),
        name="merge",
    )(h, y, at, wglu, bglu, sg, wout, post_g, xa_g, wq)


def _mem_attn_kernel(q_ref, k_ref, v_ref, o_ref, *, gb):
    for b in range(gb):
        q = q_ref[b]
        k = k_ref[b].astype(BF16)
        v = v_ref[b].astype(BF16)
        outs = []
        for hh in range(MEM_HEADS):
            hs = slice(hh * MEM_HEAD_DIM, (hh + 1) * MEM_HEAD_DIM)
            sc = _nt_dot(q[:, hs], k[:, hs])
            m = jnp.max(sc, axis=-1, keepdims=True)
            e = jnp.exp(sc - m)
            pr = (e / jnp.sum(e, axis=-1, keepdims=True)).astype(BF16)
            outs.append(jnp.dot(pr, v[:, hs], preferred_element_type=F32))
        o_ref[b] = jnp.concatenate(outs, axis=1).astype(BF16)


def _mem_attn(q3, k3, v3, gb, tq):
    nb, lq, _ = q3.shape
    qspec = pl.BlockSpec((gb, tq, D_MODEL), lambda b, i: (b, i, 0))
    kspec = pl.BlockSpec((gb, N_MEM, D_MODEL), lambda b, i: (b, 0, 0))
    return pl.pallas_call(
        functools.partial(_mem_attn_kernel, gb=gb),
        grid=(nb // gb, lq // tq),
        in_specs=[qspec, kspec, kspec],
        out_specs=qspec,
        out_shape=jax.ShapeDtypeStruct(q3.shape, BF16),
        compiler_params=_params(("arbitrary", "arbitrary")),
        name="mem_attn",
    )(q3, k3, v3)


def _ffn_out_kernel(h_ref, o_ref, wo_ref, xa_post_ref, pre_g_ref, wgu_ref, wd_ref, post_g_ref, out_ref, act_ref):
    c = jnp.dot(o_ref[...], wo_ref[...], preferred_element_type=F32)
    h3 = h_ref[...] + _rms(c, xa_post_ref[...])
    out_ref[...] = _ffn_tile(h3, pre_g_ref[...], wgu_ref, wd_ref, post_g_ref[...], act_ref)


def _ffn_out(h, o, wo, xa_post, pre_g, wgu, wd, post_g, tm):
    t = h.shape[0]
    row = pl.BlockSpec((tm, D_MODEL), lambda i: (i, 0))
    return pl.pallas_call(
        _ffn_out_kernel,
        grid=(t // tm,),
        in_specs=[row, row, _const_spec(wo.shape), _const_spec((1, D_MODEL)), _const_spec((1, D_MODEL)),
                  _const_spec(wgu.shape), _const_spec(wd.shape), _const_spec((1, D_MODEL))],
        out_specs=row,
        out_shape=jax.ShapeDtypeStruct((t, D_MODEL), F32),
        scratch_shapes=[pltpu.VMEM((tm, D_FF), BF16)],
        compiler_params=_params(("arbitrary",)),
        name="ffn_out",
    )(h, o, wo, xa_post, pre_g, wgu, wd, post_g)


def _mem_kv_kernel(m_ref, g_ref, wkv_ref, k_ref, v_ref):
    kv = jnp.dot(_rms(m_ref[...], g_ref[...]).astype(BF16), wkv_ref[...], preferred_element_type=F32)
    k_ref[...] = kv[:, :D_MODEL]
    v_ref[...] = kv[:, D_MODEL:]


def _mem_kv(mem, g, wkv, tm):
    t = mem.shape[0]
    row = pl.BlockSpec((tm, D_MODEL), lambda i: (i, 0))
    return pl.pallas_call(
        _mem_kv_kernel,
        grid=(t // tm,),
        in_specs=[row, _const_spec((1, D_MODEL)), _const_spec(wkv.shape)],
        out_specs=[row, row],
        out_shape=[jax.ShapeDtypeStruct((t, D_MODEL), F32)] * 2,
        compiler_params=_params(("arbitrary",)),
        name="mem_kv",
    )(mem, g, wkv)


def _rope_tables(pos):
    half = ROPE_DIM // 2
    inv = ROPE_THETA ** (-jnp.arange(half, dtype=F32) * (2.0 / ROPE_DIM))
    ang = pos.astype(F32)[:, None] * inv[None, :]
    cos, sin = jnp.cos(ang), jnp.sin(ang)
    n = pos.shape[0]
    pad = jnp.zeros((n, HEAD_DIM - ROPE_DIM), F32)
    zero = jnp.zeros((n, half), F32)
    cos_h = jnp.concatenate([cos, cos, pad + 1.0], axis=1)
    lo_h = jnp.concatenate([-sin, zero, pad], axis=1)
    hi_h = jnp.concatenate([zero, sin, pad], axis=1)
    rep = LANES // HEAD_DIM
    return tuple(jnp.tile(a, (1, rep)) for a in (cos_h, lo_h, hi_h))


def _ffn_weights(w_gate, w_up, w_down):
    wg = w_gate.astype(BF16).reshape(D_MODEL, N_FF_CHUNKS, FF_CHUNK)
    wu = w_up.astype(BF16).reshape(D_MODEL, N_FF_CHUNKS, FF_CHUNK)
    wgu = jnp.concatenate([wg, wu], axis=2).transpose(1, 0, 2)
    return wgu, w_down.astype(BF16)


def _lane_block_states(st, n):
    st = st.reshape(N_LANE_BLOCKS, n, 2, GROUPS_PER_LANE_BLOCK, SSM_STATE).transpose(2, 1, 0, 3, 4)
    st = st.reshape(2, n, N_SSM_GROUPS, SSM_STATE)
    return st[0], st[1]


def kernel(x_prompt, x_sample, state_ssm_re, state_ssm_im, cache_swa_k, cache_swa_v, cache_mem_k, cache_mem_v, mem_prompt, ffn1_pre_g, ffn1_w_gate, ffn1_w_up, ffn1_w_down, ffn1_post_g, mix_pre_g, w_in, ssm_a_re, ssm_a_im, ssm_log_step, ssm_b_re, ssm_b_im, ssm_c_re, ssm_c_im, ssm_d, ssm_w_glu, ssm_b_glu, attn_sinks, ssm_out_g, attn_out_g, w_out, mix_post_g, mem_norm_g, w_mem_q, w_mem_k, w_mem_v, w_mem_o, xa_pre_g, xa_post_g, ffn2_pre_g, ffn2_w_gate, ffn2_w_up, ffn2_w_down, ffn2_post_g):
    n_p, s_p, _ = x_prompt.shape
    n_s, t_s, _ = x_sample.shape
    tm = 512
    row = lambda a: a.reshape(1, -1).astype(F32)

    wgu1, wd1 = _ffn_weights(ffn1_w_gate, ffn1_w_up, ffn1_w_down)
    wgu2, wd2 = _ffn_weights(ffn2_w_gate, ffn2_w_up, ffn2_w_down)
    win = w_in.astype(BF16)
    wglu, wout = ssm_w_glu.astype(BF16), w_out.astype(BF16)
    wq, wo = w_mem_q.astype(BF16), w_mem_o.astype(BF16)
    wkv = jnp.concatenate([w_mem_k, w_mem_v], axis=1).astype(BF16)
    d_row = row(ssm_d)
    ssm_args = (ssm_a_re.astype(F32), ssm_a_im.astype(F32), ssm_log_step.astype(F32), ssm_b_re.astype(F32),
                ssm_b_im.astype(F32), ssm_c_re.astype(F32), ssm_c_im.astype(F32))
    sinks = attn_sinks.astype(F32)

    pm_k, pm_v = _mem_kv(mem_prompt.reshape(n_p * N_MEM, D_MODEL), row(mem_norm_g), wkv, N_MEM)

    def tokenwise_in(x2, pos_tab):
        return _ffn_in(x2, row(ffn1_pre_g), wgu1, wd1, row(ffn1_post_g), row(mix_pre_g), win, *pos_tab, tm)

    def tokenwise_out(h1, y, at, k3, v3, nb, lq, gb, tq):
        h2, qm = _merge(h1, y, at, wglu, row(ssm_b_glu), row(ssm_out_g), wout, row(mix_post_g), row(xa_pre_g), wq, tm)
        om = _mem_attn(qm.reshape(nb, lq, D_MODEL), k3, v3, gb, tq).reshape(nb * lq, D_MODEL)
        return _ffn_out(h2, om, wo, row(xa_post_g), row(ffn2_pre_g), wgu2, wd2, row(ffn2_post_g), tm)

    lc_p = 8
    h1, u, q, k, v = tokenwise_in(x_prompt.reshape(n_p * s_p, D_MODEL), _rope_tables(jnp.arange(s_p, dtype=jnp.int32)))
    y3, st_p = _ssm_prompt(u.reshape(n_p, s_p, SSM_WIDTH), *_ssm_tables(*ssm_args, lc_p), d_row, lc_p, 1024)
    at = _swa_prompt(q, k, v, sinks, row(attn_out_g), n_p, s_p)
    y_prompt = tokenwise_out(h1, y3.reshape(n_p * s_p, SSM_WIDTH), at,
                             pm_k.reshape(n_p, N_MEM, D_MODEL), pm_v.reshape(n_p, N_MEM, D_MODEL),
                             n_p, s_p, 1, tm).reshape(n_p, s_p, D_MODEL)
    p_sre, p_sim = _lane_block_states(st_p, n_p)
    p_wk = k.reshape(n_p, s_p, N_KV_HEADS, HEAD_DIM)[:, -WINDOW:]
    p_wv = v.reshape(n_p, s_p, N_KV_HEADS, HEAD_DIM)[:, -WINDOW:]

    pos_s = jnp.tile(PAST_LEN + jnp.arange(t_s, dtype=jnp.int32), n_s)
    h1s, us, qs, ks, vs = tokenwise_in(x_sample.reshape(n_s * t_s, D_MODEL), _rope_tables(pos_s))
    ys, s_sre, s_sim = _ssm_sample(us, state_ssm_re.reshape(n_s, -1).astype(F32), state_ssm_im.reshape(n_s, -1).astype(F32),
                                   *_ssm_tables(*ssm_args, t_s), d_row, t_s)
    win_len = cache_swa_k.shape[1]
    ats, s_wk, s_wv = _swa_sample(qs.reshape(n_s, t_s, Q_WIDTH), ks.reshape(n_s, t_s, KV_WIDTH), vs.reshape(n_s, t_s, KV_WIDTH),
                                  cache_swa_k.reshape(n_s, win_len, KV_WIDTH), cache_swa_v.reshape(n_s, win_len, KV_WIDTH),
                                  sinks, row(attn_out_g), 16)
    y_sample = tokenwise_out(h1s, ys, ats.reshape(n_s * t_s, Q_WIDTH),
                             cache_mem_k.reshape(n_s, N_MEM, D_MODEL), cache_mem_v.reshape(n_s, N_MEM, D_MODEL),
                             n_s, t_s, 4, t_s).reshape(n_s, t_s, D_MODEL)

    return (y_prompt, y_sample, p_sre, p_sim, p_wk, p_wv,
            pm_k.reshape(n_p, N_MEM, MEM_HEADS, MEM_HEAD_DIM), pm_v.reshape(n_p, N_MEM, MEM_HEADS, MEM_HEAD_DIM),
            s_sre.reshape(n_s, N_SSM_GROUPS, SSM_STATE), s_sim.reshape(n_s, N_SSM_GROUPS, SSM_STATE),
            s_wk.reshape(n_s, win_len, N_KV_HEADS, HEAD_DIM), s_wv.reshape(n_s, win_len, N_KV_HEADS, HEAD_DIM))
```

```python
import functools
import math

import jax
import jax.numpy as jnp
from jax import lax
from jax.experimental import pallas as pl
from jax.experimental.pallas import tpu as pltpu

F32 = jnp.float32
BF16 = jnp.bfloat16

D_MODEL = 1024
PAST_LEN = 16384
SSM_WIDTH = 512
SSM_GROUP = 16
N_SSM_GROUPS = 32
SSM_STATE = 64
HEAD_DIM = 64
N_HEADS = 8
N_KV_HEADS = 2
GQA_GROUP = 4
Q_WIDTH = 512
KV_WIDTH = 128
WINDOW = 128
ROPE_THETA = 500000.0
ROPE_DIM = 16
N_MEM = 256
MEM_HEADS = 4
MEM_HEAD_DIM = 256
D_FF = 2816
RMS_EPS = 1e-6
IN_WIDTH = SSM_WIDTH + Q_WIDTH + 2 * KV_WIDTH
NEG_INF = -1e30

LANES = 128
FF_CHUNK = 256
N_FF_CHUNKS = D_FF // FF_CHUNK
GROUPS_PER_LANE_BLOCK = LANES // SSM_GROUP
N_LANE_BLOCKS = SSM_WIDTH // LANES
STATE_LANES = GROUPS_PER_LANE_BLOCK * SSM_STATE
VMEM_LIMIT = 56 * 1024 * 1024


def _rms(x, g):
    return x * lax.rsqrt(jnp.mean(x * x, axis=-1, keepdims=True) + RMS_EPS) * g


def _const_spec(shape):
    nd = len(shape)
    return pl.BlockSpec(shape, lambda *_: (0,) * nd, pipeline_mode=pl.Buffered(1))


def _params(sem):
    return pltpu.CompilerParams(dimension_semantics=sem, vmem_limit_bytes=VMEM_LIMIT)


def _ffn_tile(x, pre_g, wgu_ref, wd_ref, post_g, act_ref):
    xn = _rms(x, pre_g).astype(BF16)
    for c in range(N_FF_CHUNKS):
        gu = jnp.dot(xn, wgu_ref[c], preferred_element_type=F32)
        gate = gu[:, :FF_CHUNK]
        up = gu[:, FF_CHUNK:]
        act = gate * (1.0 / (1.0 + jnp.exp(-gate))) * up
        act_ref[:, c * FF_CHUNK:(c + 1) * FF_CHUNK] = act.astype(BF16)
    down = jnp.dot(act_ref[...], wd_ref[...], preferred_element_type=F32)
    return x + 0.5 * _rms(down, post_g)


def _rope(x, cos, sin_lo, sin_hi):
    w = x.shape[1]
    half = ROPE_DIM // 2
    return (x * cos + pltpu.roll(x, w - half, 1) * sin_lo + pltpu.roll(x, half, 1) * sin_hi)


def _ffn_in_kernel(x_ref, pre_g_ref, wgu_ref, wd_ref, post_g_ref, mix_g_ref, win_ref,
                   cos_ref, slo_ref, shi_ref,
                   h_ref, u_ref, q_ref, k_ref, v_ref, act_ref):
    h = _ffn_tile(x_ref[...], pre_g_ref[...], wgu_ref, wd_ref, post_g_ref[...], act_ref)
    h_ref[...] = h
    z = jnp.dot(_rms(h, mix_g_ref[...]).astype(BF16), win_ref[...], preferred_element_type=F32)
    u_ref[...] = z[:, :SSM_WIDTH]
    o1 = SSM_WIDTH + Q_WIDTH
    cos, slo, shi = cos_ref[...], slo_ref[...], shi_ref[...]
    rep = Q_WIDTH // LANES
    q = _rope(z[:, SSM_WIDTH:o1], jnp.tile(cos, (1, rep)), jnp.tile(slo, (1, rep)), jnp.tile(shi, (1, rep)))
    q_ref[...] = (q * (HEAD_DIM ** -0.5)).astype(BF16)
    k_ref[...] = _rope(z[:, o1:o1 + KV_WIDTH], cos, slo, shi)
    v_ref[...] = z[:, o1 + KV_WIDTH:]


def _ffn_in(x, pre_g, wgu, wd, post_g, mix_g, win, cos, slo, shi, tm):
    t = x.shape[0]
    n_pos_tiles = cos.shape[0] // tm
    row = lambda w: pl.BlockSpec((tm, w), lambda i: (i, 0))
    tab = pl.BlockSpec((tm, LANES), lambda i: (i % n_pos_tiles, 0))
    return pl.pallas_call(
        _ffn_in_kernel,
        grid=(t // tm,),
        in_specs=[row(D_MODEL), _const_spec((1, D_MODEL)), _const_spec(wgu.shape), _const_spec(wd.shape),
                  _const_spec((1, D_MODEL)), _const_spec((1, D_MODEL)), _const_spec(win.shape),
                  tab, tab, tab],
        out_specs=[row(D_MODEL), row(SSM_WIDTH), row(Q_WIDTH), row(KV_WIDTH), row(KV_WIDTH)],
        out_shape=[jax.ShapeDtypeStruct((t, D_MODEL), F32), jax.ShapeDtypeStruct((t, SSM_WIDTH), F32),
                   jax.ShapeDtypeStruct((t, Q_WIDTH), BF16), jax.ShapeDtypeStruct((t, KV_WIDTH), F32),
                   jax.ShapeDtypeStruct((t, KV_WIDTH), F32)],
        scratch_shapes=[pltpu.VMEM((tm, D_FF), BF16)],
        compiler_params=_params(("arbitrary",)),
        name="ffn_in",
    )(x, pre_g, wgu, wd, post_g, mix_g, win, cos, slo, shi)


def _complex_step(s_re, s_im, l_re, l_im, x_re, x_im):
    return l_re * s_re - l_im * s_im + x_re, l_re * s_im + l_im * s_re + x_im


def _ssm_prompt_kernel(u_ref, m_ref, w_ref, v_ref, lam_ref, d_ref, y_ref, st_ref,
                       x_scr, ss_scr, s_scr, *, lc, nb, ncl):
    tt = pl.program_id(1)

    @pl.when(tt == 0)
    def _():
        s_scr[...] = jnp.zeros_like(s_scr)

    def piece(n, j):
        return u_ref[n, pl.ds(j, ncl, stride=lc), :]

    a = jnp.concatenate(
        [jnp.concatenate([piece(n, j) for j in range(lc)], axis=1) for n in range(nb)], axis=0).astype(BF16)
    x = jnp.dot(a, w_ref[0], preferred_element_type=F32)
    nq = STATE_LANES // LANES
    for qq in range(2 * nq):
        x_scr[qq] = x[:, qq * LANES:(qq + 1) * LANES]
    l_re = [jnp.broadcast_to(lam_ref[0, :, qq * LANES:(qq + 1) * LANES], (nb, LANES)) for qq in range(nq)]
    l_im = [jnp.broadcast_to(lam_ref[0, :, (nq + qq) * LANES:(nq + qq + 1) * LANES], (nb, LANES)) for qq in range(nq)]

    def body(c, carry):
        rows = pl.ds(c, nb, stride=ncl)
        new = []
        for qq in range(nq):
            s_re, s_im = carry[qq], carry[nq + qq]
            ss_scr[qq, rows, :] = s_re
            ss_scr[nq + qq, rows, :] = s_im
            new.append(_complex_step(s_re, s_im, l_re[qq], l_im[qq], x_scr[qq, rows, :], x_scr[nq + qq, rows, :]))
        return tuple(r for r, _ in new) + tuple(i for _, i in new)

    s_fin = lax.fori_loop(0, ncl, body, tuple(s_scr[qq] for qq in range(2 * nq)), unroll=4)
    for qq in range(2 * nq):
        s_scr[qq] = s_fin[qq]
    st_ref[0] = jnp.concatenate(s_fin, axis=1)
    s_start = jnp.concatenate([ss_scr[qq] for qq in range(2 * nq)], axis=1).astype(BF16)
    y = (jnp.dot(a, m_ref[0], preferred_element_type=F32)
         + jnp.dot(s_start, v_ref[0], preferred_element_type=F32))
    d = d_ref[...]
    for n in range(nb):
        for j in range(lc):
            y_ref[n, pl.ds(j, ncl, stride=lc), :] = (
                y[n * ncl:(n + 1) * ncl, j * LANES:(j + 1) * LANES] + d * piece(n, j))


def _ssm_prompt(u3, m, w, v, lam, d, lc, tl):
    nb, seq, _ = u3.shape
    ncl = tl // lc
    nslab = 2 * STATE_LANES // LANES
    kern = functools.partial(_ssm_prompt_kernel, lc=lc, nb=nb, ncl=ncl)
    wspec = lambda a: pl.BlockSpec((1,) + a.shape[1:], lambda b, t: (b, 0, 0))
    return pl.pallas_call(
        kern,
        grid=(N_LANE_BLOCKS, seq // tl),
        in_specs=[pl.BlockSpec((nb, tl, LANES), lambda b, t: (0, t, b)), wspec(m), wspec(w), wspec(v), wspec(lam),
                  pl.BlockSpec((1, LANES), lambda b, t: (0, b))],
        out_specs=[pl.BlockSpec((nb, tl, LANES), lambda b, t: (0, t, b)),
                   pl.BlockSpec((1, nb, 2 * STATE_LANES), lambda b, t: (b, 0, 0))],
        out_shape=[jax.ShapeDtypeStruct(u3.shape, F32),
                   jax.ShapeDtypeStruct((N_LANE_BLOCKS, nb, 2 * STATE_LANES), F32)],
        scratch_shapes=[pltpu.VMEM((nslab, nb * ncl, LANES), F32), pltpu.VMEM((nslab, nb * ncl, LANES), F32),
                        pltpu.VMEM((nslab, nb, LANES), F32)],
        compiler_params=_params(("arbitrary", "arbitrary")),
        name="ssm_prompt",
    )(u3, m, w, v, lam, d)


def _ssm_sample_kernel(u_ref, sre_ref, sim_ref, m_ref, w_ref, v_ref, lam_ref, d_ref,
                       y_ref, ore_ref, oim_ref, *, lc, ns):
    def piece(j):
        return u_ref[pl.ds(j, ns, stride=lc), :]

    a = jnp.concatenate([piece(j) for j in range(lc)], axis=1).astype(BF16)
    s_re, s_im = sre_ref[...], sim_ref[...]
    x = jnp.dot(a, w_ref[0], preferred_element_type=F32)
    e_re, e_im = _complex_step(s_re, s_im, lam_ref[0, :, :STATE_LANES], lam_ref[0, :, STATE_LANES:],
                               x[:, :STATE_LANES], x[:, STATE_LANES:])
    ore_ref[...] = e_re
    oim_ref[...] = e_im
    s0 = jnp.concatenate([s_re, s_im], axis=1).astype(BF16)
    y = (jnp.dot(a, m_ref[0], preferred_element_type=F32) + jnp.dot(s0, v_ref[0], preferred_element_type=F32))
    d = d_ref[...]
    for j in range(lc):
        y_ref[pl.ds(j, ns, stride=lc), :] = y[:, j * LANES:(j + 1) * LANES] + d * piece(j)


def _ssm_sample(u, s_re, s_im, m, w, v, lam, d, lc):
    t = u.shape[0]
    ns = t // lc
    kern = functools.partial(_ssm_sample_kernel, lc=lc, ns=ns)
    wspec = lambda a: pl.BlockSpec((1,) + a.shape[1:], lambda b: (b, 0, 0))
    col = lambda rows, width: pl.BlockSpec((rows, width), lambda b: (0, b))
    return pl.pallas_call(
        kern,
        grid=(N_LANE_BLOCKS,),
        in_specs=[col(t, LANES), col(ns, STATE_LANES), col(ns, STATE_LANES), wspec(m), wspec(w), wspec(v),
                  wspec(lam), col(1, LANES)],
        out_specs=[col(t, LANES), col(ns, STATE_LANES), col(ns, STATE_LANES)],
        out_shape=[jax.ShapeDtypeStruct(u.shape, F32), jax.ShapeDtypeStruct(s_re.shape, F32),
                   jax.ShapeDtypeStruct(s_im.shape, F32)],
        compiler_params=_params(("arbitrary",)),
        name="ssm_sample",
    )(u, s_re, s_im, m, w, v, lam, d)


def _ssm_tables(a_re, a_im, log_step, b_re, b_im, c_re, c_im, lc):
    g, p, h = b_re.shape
    nbk, r = N_LANE_BLOCKS, GROUPS_PER_LANE_BLOCK
    hi = lax.Precision.HIGHEST
    dt = jnp.exp(log_step)[:, None]
    mag = jnp.exp(a_re * dt)
    lb_re, lb_im = mag * jnp.cos(a_im * dt), mag * jnp.sin(a_im * dt)
    den = a_re * a_re + a_im * a_im
    n_re, n_im = lb_re - 1.0, lb_im
    cf_re = (n_re * a_re + n_im * a_im) / den
    cf_im = (n_im * a_re - n_re * a_im) / den
    bb_re = cf_re[..., None] * b_re - cf_im[..., None] * b_im
    bb_im = cf_re[..., None] * b_im + cf_im[..., None] * b_re
    pw_re, pw_im = [jnp.ones_like(lb_re)], [jnp.zeros_like(lb_re)]
    for _ in range(lc):
        pr, pi = pw_re[-1], pw_im[-1]
        pw_re.append(pr * lb_re - pi * lb_im)
        pw_im.append(pr * lb_im + pi * lb_re)
    pw_re, pw_im = jnp.stack(pw_re), jnp.stack(pw_im)
    e_re = pw_re[:lc, :, :, None] * bb_re[None] - pw_im[:lc, :, :, None] * bb_im[None]
    e_im = pw_re[:lc, :, :, None] * bb_im[None] + pw_im[:lc, :, :, None] * bb_re[None]
    kk = (jnp.einsum('kgph,gip->kghi', e_re, c_re, precision=hi)
          - jnp.einsum('kgph,gip->kghi', e_im, c_im, precision=hi))
    eye_r = jnp.eye(r, dtype=F32)
    lag = (jnp.arange(lc)[None, :, None] - jnp.arange(lc)[:, None, None] == jnp.arange(lc)[None, None, :])
    m = jnp.einsum('jJk,kbrhi,rR->bjrhJRi', lag.astype(F32), kk.reshape(lc, nbk, r, h, h), eye_r)
    m = m.reshape(nbk, lc * LANES, lc * LANES)
    w_re = jnp.einsum('jbrph,rR->bjrhRp', e_re[::-1].reshape(lc, nbk, r, p, h), eye_r)
    w_im = jnp.einsum('jbrph,rR->bjrhRp', e_im[::-1].reshape(lc, nbk, r, p, h), eye_r)
    w = jnp.concatenate([w_re.reshape(nbk, lc * LANES, STATE_LANES), w_im.reshape(nbk, lc * LANES, STATE_LANES)], axis=2)
    q_re, q_im = pw_re[1:, :, None, :], pw_im[1:, :, None, :]
    v_re = (c_re[None] * q_re - c_im[None] * q_im).reshape(lc, nbk, r, h, p)
    v_im = -(c_re[None] * q_im + c_im[None] * q_re).reshape(lc, nbk, r, h, p)
    v_re = jnp.einsum('jbrhp,rR->bRpjrh', v_re, eye_r).reshape(nbk, STATE_LANES, lc * LANES)
    v_im = jnp.einsum('jbrhp,rR->bRpjrh', v_im, eye_r).reshape(nbk, STATE_LANES, lc * LANES)
    v = jnp.concatenate([v_re, v_im], axis=1)
    lam = jnp.concatenate([pw_re[lc].reshape(nbk, 1, STATE_LANES), pw_im[lc].reshape(nbk, 1, STATE_LANES)], axis=2)
    return m.astype(BF16), w.astype(BF16), v.astype(BF16), lam


def _nt_dot(a, b):
    return lax.dot_general(a, b, (((1,), (1,)), ((), ())), preferred_element_type=F32)


def _swa_prompt_kernel(sink_ref, q_ref, kp_ref, kc_ref, vp_ref, vc_ref, mstd_ref, mfirst_ref, g_ref, o_ref, *, nblk):
    first_tile = pl.program_id(1) == 0
    low_half = lax.broadcasted_iota(jnp.int32, (1, LANES), 1) < HEAD_DIM

    def head_variants(prev_ref, cur_ref):
        x = jnp.concatenate([prev_ref[...], cur_ref[...]], axis=0)
        xr = pltpu.roll(x, HEAD_DIM, 1)
        zero = jnp.zeros_like(x)
        pick = lambda lo, hi: jnp.where(low_half, lo, hi).astype(BF16)
        return [[pick(x, zero), pick(zero, xr)], [pick(xr, zero), pick(zero, x)]]

    kvar = head_variants(kp_ref, kc_ref)
    vvar = head_variants(vp_ref, vc_ref)
    gain = g_ref[...]
    pair = GQA_GROUP // 2
    for j in range(nblk):
        keys = slice(j * WINDOW, (j + 2) * WINDOW)
        vmask = mstd_ref[...]
        if j == 0:
            vmask = jnp.where(first_tile, mfirst_ref[...], vmask)
        blocks = []
        for kh in range(N_KV_HEADS):
            qs = jnp.concatenate([q_ref[j * WINDOW:(j + 1) * WINDOW, (kh * pair + a) * LANES:(kh * pair + a + 1) * LANES]
                                  for a in range(pair)], axis=0)
            acc = None
            for par in range(2):
                sink = jnp.concatenate(
                    [jnp.full((WINDOW, 1), sink_ref[(kh * pair + a) * 2 + par], F32) for a in range(pair)], axis=0)
                sc = jnp.where(vmask > 0.0, _nt_dot(qs, kvar[kh][par][keys]), NEG_INF)
                m = jnp.maximum(jnp.max(sc, axis=-1, keepdims=True), sink)
                e = jnp.exp(sc - m)
                den = jnp.sum(e, axis=-1, keepdims=True) + jnp.exp(sink - m)
                o = jnp.dot(e.astype(BF16), vvar[kh][par][keys], preferred_element_type=F32) * (1.0 / den)
                acc = o if acc is None else acc + o
            blocks += [acc[a * WINDOW:(a + 1) * WINDOW] for a in range(pair)]
        o_ref[j * WINDOW:(j + 1) * WINDOW, :] = _rms(jnp.concatenate(blocks, axis=1), gain).astype(BF16)


def _swa_masks():
    rows = (GQA_GROUP // 2) * WINDOW
    qi = jnp.arange(rows)[:, None] % WINDOW + WINDOW
    kj = jnp.arange(2 * WINDOW)[None, :]
    diff = qi - kj
    std = (diff >= 0) & (diff <= WINDOW)
    return std.astype(F32), (std & (kj >= WINDOW)).astype(F32)


def _swa_prompt(q, k, v, sinks, out_g, n, seq, tq):
    nt, nblk = seq // tq, tq // WINDOW
    mstd, mfirst = _swa_masks()
    cur = lambda w: pl.BlockSpec((tq, w), lambda b, i: (b * nt + i, 0))
    prev = lambda w: pl.BlockSpec((WINDOW, w), lambda b, i: (jnp.maximum((b * nt + i) * nblk - 1, b * nt * nblk), 0))
    return pl.pallas_call(
        functools.partial(_swa_prompt_kernel, nblk=nblk),
        grid=(n, nt),
        in_specs=[pl.BlockSpec(memory_space=pltpu.SMEM), cur(Q_WIDTH), prev(KV_WIDTH), cur(KV_WIDTH),
                  prev(KV_WIDTH), cur(KV_WIDTH), _const_spec(mstd.shape), _const_spec(mfirst.shape),
                  _const_spec((1, Q_WIDTH))],
        out_specs=cur(Q_WIDTH),
        out_shape=jax.ShapeDtypeStruct((n * seq, Q_WIDTH), BF16),
        compiler_params=_params(("arbitrary", "arbitrary")),
        name="swa_prompt",
    )(sinks, q, k, k, v, v, mstd, mfirst, out_g)


def _swa_sample_kernel(sink_ref, q_ref, kn_ref, vn_ref, ck_ref, cv_ref, g_ref, o_ref, nk_ref, nv_ref, *, ns, t):
    rows = GQA_GROUP * t
    tok = lax.broadcasted_iota(jnp.int32, (rows, WINDOW), 0) % t
    valid_c = lax.broadcasted_iota(jnp.int32, (rows, WINDOW), 1) >= tok
    tok_n = lax.broadcasted_iota(jnp.int32, (rows, t), 0) % t
    valid_n = lax.broadcasted_iota(jnp.int32, (rows, t), 1) <= tok_n
    gain = g_ref[...]

    def body(s, _):
        q = q_ref[s].astype(F32)
        kn, vn = kn_ref[s], vn_ref[s]
        ck, cv = ck_ref[s], cv_ref[s]
        nk_ref[s, :WINDOW - t, :] = ck[t:]
        nk_ref[s, WINDOW - t:, :] = kn
        nv_ref[s, :WINDOW - t, :] = cv[t:]
        nv_ref[s, WINDOW - t:, :] = vn
        outs = []
        for kh in range(N_KV_HEADS):
            hs = slice(kh * HEAD_DIM, (kh + 1) * HEAD_DIM)
            q4 = jnp.concatenate([q[:, (kh * GQA_GROUP + g) * HEAD_DIM:(kh * GQA_GROUP + g + 1) * HEAD_DIM]
                                  for g in range(GQA_GROUP)], axis=0).astype(BF16)
            sc_c = jnp.where(valid_c, _nt_dot(q4, ck[:, hs].astype(BF16)), NEG_INF)
            sc_n = jnp.where(valid_n, _nt_dot(q4, kn[:, hs].astype(BF16)), NEG_INF)
            sink = jnp.concatenate([jnp.full((t, 1), sink_ref[kh * GQA_GROUP + g], F32) for g in range(GQA_GROUP)],
                                   axis=0)
            m = jnp.maximum(jnp.maximum(jnp.max(sc_c, axis=-1, keepdims=True),
                                        jnp.max(sc_n, axis=-1, keepdims=True)), sink)
            e_c, e_n = jnp.exp(sc_c - m), jnp.exp(sc_n - m)
            den = jnp.sum(e_c, axis=-1, keepdims=True) + jnp.sum(e_n, axis=-1, keepdims=True) + jnp.exp(sink - m)
            o4 = (jnp.dot((e_c / den).astype(BF16), cv[:, hs].astype(BF16), preferred_element_type=F32)
                  + jnp.dot((e_n / den).astype(BF16), vn[:, hs].astype(BF16), preferred_element_type=F32))
            outs += [o4[g * t:(g + 1) * t] for g in range(GQA_GROUP)]
        o = jnp.concatenate(outs, axis=1)
        o_ref[s] = _rms(o, gain).astype(BF16)
        return 0

    lax.fori_loop(0, ns, body, 0)


def _swa_sample(q3, kn3, vn3, ck, cv, sinks, out_g, sb):
    ns, t, _ = q3.shape
    blk = lambda a: pl.BlockSpec((sb,) + a.shape[1:], lambda i: (i, 0, 0))
    kern = functools.partial(_swa_sample_kernel, ns=sb, t=t)
    return pl.pallas_call(
        kern,
        grid=(ns // sb,),
        in_specs=[pl.BlockSpec(memory_space=pltpu.SMEM), blk(q3), blk(kn3), blk(vn3), blk(ck), blk(cv),
                  pl.BlockSpec((1, Q_WIDTH), lambda i: (0, 0))],
        out_specs=[blk(q3), blk(ck), blk(cv)],
        out_shape=[jax.ShapeDtypeStruct(q3.shape, BF16), jax.ShapeDtypeStruct(ck.shape, F32),
                   jax.ShapeDtypeStruct(cv.shape, F32)],
        compiler_params=_params(("arbitrary",)),
        name="swa_sample",
    )(sinks, q3, kn3, vn3, ck, cv, out_g)


def _gelu_tanh(x):
    return 0.5 * x * (1.0 + jnp.tanh(math.sqrt(2.0 / math.pi) * (x + 0.044715 * (x * x * x))))


def _merge_kernel(h_ref, y_ref, at_ref, wglu_ref, bglu_ref, sg_ref, wout_ref, post_g_ref, xa_g_ref, wq_ref,
                  h2_ref, qm_ref):
    g = _gelu_tanh(y_ref[...])
    lin = jnp.dot(g.astype(BF16), wglu_ref[...], preferred_element_type=F32) + bglu_ref[...]
    y_ssm = g * (1.0 / (1.0 + jnp.exp(-lin)))
    ssm_n = _rms(y_ssm, sg_ref[...]).astype(BF16)
    mixed = (jnp.dot(ssm_n, wout_ref[:SSM_WIDTH, :], preferred_element_type=F32)
             + jnp.dot(at_ref[...], wout_ref[SSM_WIDTH:, :], preferred_element_type=F32))
    h2 = h_ref[...] + _rms(mixed, post_g_ref[...])
    h2_ref[...] = h2
    qm = jnp.dot(_rms(h2, xa_g_ref[...]).astype(BF16), wq_ref[...], preferred_element_type=F32)
    qm_ref[...] = (qm * (MEM_HEAD_DIM ** -0.5)).astype(BF16)


def _merge(h, y, at, wglu, bglu, sg, wout, post_g, xa_g, wq, tm):
    t = h.shape[0]
    row = lambda w: pl.BlockSpec((tm, w), lambda i: (i, 0))
    return pl.pallas_call(
        _merge_kernel,
        grid=(t // tm,),
        in_specs=[row(D_MODEL), row(SSM_WIDTH), row(Q_WIDTH), _const_spec(wglu.shape), _const_spec((1, SSM_WIDTH)),
                  _const_spec((1, SSM_WIDTH)), _const_spec(wout.shape), _const_spec((1, D_MODEL)),
                  _const_spec((1, D_MODEL)), _const_spec(wq.shape)],
        out_specs=[row(D_MODEL), row(D_MODEL)],
        out_shape=[jax.ShapeDtypeStruct((t, D_MODEL), F32), jax.ShapeDtypeStruct((t, D_MODEL), BF16)],
        compiler_params=_params(("arbitrary",)),
        name="merge",
    )(h, y, at, wglu, bglu, sg, wout, post_g, xa_g, wq)


def _mem_attn_kernel(q_ref, k_ref, v_ref, o_ref, *, gb):
    for b in range(gb):
        q = q_ref[b]
        k = k_ref[b].astype(BF16)
        v = v_ref[b].astype(BF16)
        outs = []
        for hh in range(MEM_HEADS):
            hs = slice(hh * MEM_HEAD_DIM, (hh + 1) * MEM_HEAD_DIM)
            sc = _nt_dot(q[:, hs], k[:, hs])
            m = jnp.max(sc, axis=-1, keepdims=True)
            e = jnp.exp(sc - m)
            pr = (e / jnp.sum(e, axis=-1, keepdims=True)).astype(BF16)
            outs.append(jnp.dot(pr, v[:, hs], preferred_element_type=F32))
        o_ref[b] = jnp.concatenate(outs, axis=1).astype(BF16)


def _mem_attn(q3, k3, v3, gb, tq):
    nb, lq, _ = q3.shape
    qspec = pl.BlockSpec((gb, tq, D_MODEL), lambda b, i: (b, i, 0))
    kspec = pl.BlockSpec((gb, N_MEM, D_MODEL), lambda b, i: (b, 0, 0))
    return pl.pallas_call(
        functools.partial(_mem_attn_kernel, gb=gb),
        grid=(nb // gb, lq // tq),
        in_specs=[qspec, kspec, kspec],
        out_specs=qspec,
        out_shape=jax.ShapeDtypeStruct(q3.shape, BF16),
        compiler_params=_params(("arbitrary", "arbitrary")),
        name="mem_attn",
    )(q3, k3, v3)


def _ffn_out_kernel(h_ref, o_ref, wo_ref, xa_post_ref, pre_g_ref, wgu_ref, wd_ref, post_g_ref, out_ref, act_ref):
    c = jnp.dot(o_ref[...], wo_ref[...], preferred_element_type=F32)
    h3 = h_ref[...] + _rms(c, xa_post_ref[...])
    out_ref[...] = _ffn_tile(h3, pre_g_ref[...], wgu_ref, wd_ref, post_g_ref[...], act_ref)


def _ffn_out(h, o, wo, xa_post, pre_g, wgu, wd, post_g, tm):
    t = h.shape[0]
    row = pl.BlockSpec((tm, D_MODEL), lambda i: (i, 0))
    return pl.pallas_call(
        _ffn_out_kernel,
        grid=(t // tm,),
        in_specs=[row, row, _const_spec(wo.shape), _const_spec((1, D_MODEL)), _const_spec((1, D_MODEL)),
                  _const_spec(wgu.shape), _const_spec(wd.shape), _const_spec((1, D_MODEL))],
        out_specs=row,
        out_shape=jax.ShapeDtypeStruct((t, D_MODEL), F32),
        scratch_shapes=[pltpu.VMEM((tm, D_FF), BF16)],
        compiler_params=_params(("arbitrary",)),
        name="ffn_out",
    )(h, o, wo, xa_post, pre_g, wgu, wd, post_g)


def _mem_kv_kernel(m_ref, g_ref, wkv_ref, k_ref, v_ref):
    kv = jnp.dot(_rms(m_ref[...], g_ref[...]).astype(BF16), wkv_ref[...], preferred_element_type=F32)
    k_ref[...] = kv[:, :D_MODEL]
    v_ref[...] = kv[:, D_MODEL:]


def _mem_kv(mem, g, wkv, tm):
    t = mem.shape[0]
    row = pl.BlockSpec((tm, D_MODEL), lambda i: (i, 0))
    return pl.pallas_call(
        _mem_kv_kernel,
        grid=(t // tm,),
        in_specs=[row, _const_spec((1, D_MODEL)), _const_spec(wkv.shape)],
        out_specs=[row, row],
        out_shape=[jax.ShapeDtypeStruct((t, D_MODEL), F32)] * 2,
        compiler_params=_params(("arbitrary",)),
        name="mem_kv",
    )(mem, g, wkv)


def _rope_tables(pos):
    half = ROPE_DIM // 2
    inv = ROPE_THETA ** (-jnp.arange(half, dtype=F32) * (2.0 / ROPE_DIM))
    ang = pos.astype(F32)[:, None] * inv[None, :]
    cos, sin = jnp.cos(ang), jnp.sin(ang)
    n = pos.shape[0]
    pad = jnp.zeros((n, HEAD_DIM - ROPE_DIM), F32)
    zero = jnp.zeros((n, half), F32)
    cos_h = jnp.concatenate([cos, cos, pad + 1.0], axis=1)
    lo_h = jnp.concatenate([-sin, zero, pad], axis=1)
    hi_h = jnp.concatenate([zero, sin, pad], axis=1)
    rep = LANES // HEAD_DIM
    return tuple(jnp.tile(a, (1, rep)) for a in (cos_h, lo_h, hi_h))


def _ffn_weights(w_gate, w_up, w_down):
    wg = w_gate.astype(BF16).reshape(D_MODEL, N_FF_CHUNKS, FF_CHUNK)
    wu = w_up.astype(BF16).reshape(D_MODEL, N_FF_CHUNKS, FF_CHUNK)
    wgu = jnp.concatenate([wg, wu], axis=2).transpose(1, 0, 2)
    return wgu, w_down.astype(BF16)


def _lane_block_states(st, n):
    st = st.reshape(N_LANE_BLOCKS, n, 2, GROUPS_PER_LANE_BLOCK, SSM_STATE).transpose(2, 1, 0, 3, 4)
    st = st.reshape(2, n, N_SSM_GROUPS, SSM_STATE)
    return st[0], st[1]


def kernel(x_prompt, x_sample, state_ssm_re, state_ssm_im, cache_swa_k, cache_swa_v, cache_mem_k, cache_mem_v, mem_prompt, ffn1_pre_g, ffn1_w_gate, ffn1_w_up, ffn1_w_down, ffn1_post_g, mix_pre_g, w_in, ssm_a_re, ssm_a_im, ssm_log_step, ssm_b_re, ssm_b_im, ssm_c_re, ssm_c_im, ssm_d, ssm_w_glu, ssm_b_glu, attn_sinks, ssm_out_g, attn_out_g, w_out, mix_post_g, mem_norm_g, w_mem_q, w_mem_k, w_mem_v, w_mem_o, xa_pre_g, xa_post_g, ffn2_pre_g, ffn2_w_gate, ffn2_w_up, ffn2_w_down, ffn2_post_g):
    n_p, s_p, _ = x_prompt.shape
    n_s, t_s, _ = x_sample.shape
    tm = 512
    row = lambda a: a.reshape(1, -1).astype(F32)

    wgu1, wd1 = _ffn_weights(ffn1_w_gate, ffn1_w_up, ffn1_w_down)
    wgu2, wd2 = _ffn_weights(ffn2_w_gate, ffn2_w_up, ffn2_w_down)
    win = w_in.astype(BF16)
    wglu, wout = ssm_w_glu.astype(BF16), w_out.astype(BF16)
    wq, wo = w_mem_q.astype(BF16), w_mem_o.astype(BF16)
    wkv = jnp.concatenate([w_mem_k, w_mem_v], axis=1).astype(BF16)
    d_row = row(ssm_d)
    ssm_args = (ssm_a_re.astype(F32), ssm_a_im.astype(F32), ssm_log_step.astype(F32), ssm_b_re.astype(F32),
                ssm_b_im.astype(F32), ssm_c_re.astype(F32), ssm_c_im.astype(F32))
    sinks = attn_sinks.astype(F32)

    pm_k, pm_v = _mem_kv(mem_prompt.reshape(n_p * N_MEM, D_MODEL), row(mem_norm_g), wkv, N_MEM)

    def tokenwise_in(x2, pos_tab):
        return _ffn_in(x2, row(ffn1_pre_g), wgu1, wd1, row(ffn1_post_g), row(mix_pre_g), win, *pos_tab, tm)

    def tokenwise_out(h1, y, at, k3, v3, nb, lq, gb, tq):
        h2, qm = _merge(h1, y, at, wglu, row(ssm_b_glu), row(ssm_out_g), wout, row(mix_post_g), row(xa_pre_g), wq, tm)
        om = _mem_attn(qm.reshape(nb, lq, D_MODEL), k3, v3, gb, tq).reshape(nb * lq, D_MODEL)
        return _ffn_out(h2, om, wo, row(xa_post_g), row(ffn2_pre_g), wgu2, wd2, row(ffn2_post_g), tm)

    lc_p = 8
    h1, u, q, k, v = tokenwise_in(x_prompt.reshape(n_p * s_p, D_MODEL), _rope_tables(jnp.arange(s_p, dtype=jnp.int32)))
    y3, st_p = _ssm_prompt(u.reshape(n_p, s_p, SSM_WIDTH), *_ssm_tables(*ssm_args, lc_p), d_row, lc_p, 1024)
    at = _swa_prompt(q, k, v, sinks, row(attn_out_g), n_p, s_p, 512)
    y_prompt = tokenwise_out(h1, y3.reshape(n_p * s_p, SSM_WIDTH), at,
                             pm_k.reshape(n_p, N_MEM, D_MODEL), pm_v.reshape(n_p, N_MEM, D_MODEL),
                             n_p, s_p, 1, tm).reshape(n_p, s_p, D_MODEL)
    p_sre, p_sim = _lane_block_states(st_p, n_p)
    p_wk = k.reshape(n_p, s_p, N_KV_HEADS, HEAD_DIM)[:, -WINDOW:]
    p_wv = v.reshape(n_p, s_p, N_KV_HEADS, HEAD_DIM)[:, -WINDOW:]

    pos_s = jnp.tile(PAST_LEN + jnp.arange(t_s, dtype=jnp.int32), n_s)
    h1s, us, qs, ks, vs = tokenwise_in(x_sample.reshape(n_s * t_s, D_MODEL), _rope_tables(pos_s))
    ys, s_sre, s_sim = _ssm_sample(us, state_ssm_re.reshape(n_s, -1).astype(F32), state_ssm_im.reshape(n_s, -1).astype(F32),
                                   *_ssm_tables(*ssm_args, t_s), d_row, t_s)
    win_len = cache_swa_k.shape[1]
    ats, s_wk, s_wv = _swa_sample(qs.reshape(n_s, t_s, Q_WIDTH), ks.reshape(n_s, t_s, KV_WIDTH), vs.reshape(n_s, t_s, KV_WIDTH),
                                  cache_swa_k.reshape(n_s, win_len, KV_WIDTH), cache_swa_v.reshape(n_s, win_len, KV_WIDTH),
                                  sinks, row(attn_out_g), 16)
    y_sample = tokenwise_out(h1s, ys, ats.reshape(n_s * t_s, Q_WIDTH),
                             cache_mem_k.reshape(n_s, N_MEM, D_MODEL), cache_mem_v.reshape(n_s, N_MEM, D_MODEL),
                             n_s, t_s, 4, t_s).reshape(n_s, t_s, D_MODEL)

    return (y_prompt, y_sample, p_sre, p_sim, p_wk, p_wv,
            pm_k.reshape(n_p, N_MEM, MEM_HEADS, MEM_HEAD_DIM), pm_v.reshape(n_p, N_MEM, MEM_HEADS, MEM_HEAD_DIM),
            s_sre.reshape(n_s, N_SSM_GROUPS, SSM_STATE), s_sim.reshape(n_s, N_SSM_GROUPS, SSM_STATE),
            s_wk.reshape(n_s, win_len, N_KV_HEADS, HEAD_DIM), s_wv.reshape(n_s, win_len, N_KV_HEADS, HEAD_DIM))
```

```python
import functools
import math

import jax
import jax.numpy as jnp
from jax import lax
from jax.experimental import pallas as pl
from jax.experimental.pallas import tpu as pltpu

F32 = jnp.float32
BF16 = jnp.bfloat16

D_MODEL = 1024
PAST_LEN = 16384
SSM_WIDTH = 512
SSM_GROUP = 16
N_SSM_GROUPS = 32
SSM_STATE = 64
HEAD_DIM = 64
N_HEADS = 8
N_KV_HEADS = 2
GQA_GROUP = 4
Q_WIDTH = 512
KV_WIDTH = 128
WINDOW = 128
ROPE_THETA = 500000.0
ROPE_DIM = 16
N_MEM = 256
MEM_HEADS = 4
MEM_HEAD_DIM = 256
D_FF = 2816
RMS_EPS = 1e-6
IN_WIDTH = SSM_WIDTH + Q_WIDTH + 2 * KV_WIDTH
NEG_INF = -1e30

LANES = 128
FF_CHUNK = 256
N_FF_CHUNKS = D_FF // FF_CHUNK
GROUPS_PER_LANE_BLOCK = LANES // SSM_GROUP
N_LANE_BLOCKS = SSM_WIDTH // LANES
STATE_LANES = GROUPS_PER_LANE_BLOCK * SSM_STATE
VMEM_LIMIT = 56 * 1024 * 1024


def _rms(x, g):
    return x * lax.rsqrt(jnp.mean(x * x, axis=-1, keepdims=True) + RMS_EPS) * g


def _const_spec(shape):
    nd = len(shape)
    return pl.BlockSpec(shape, lambda *_: (0,) * nd, pipeline_mode=pl.Buffered(1))


def _params(sem):
    return pltpu.CompilerParams(dimension_semantics=sem, vmem_limit_bytes=VMEM_LIMIT)


def _ffn_tile(x, pre_g, wg_ref, wu_ref, wd_ref, post_g, act_ref):
    xn = _rms(x, pre_g).astype(BF16)
    for c in range(N_FF_CHUNKS):
        cols = slice(c * FF_CHUNK, (c + 1) * FF_CHUNK)
        gate = jnp.dot(xn, wg_ref[:, cols], preferred_element_type=F32)
        up = jnp.dot(xn, wu_ref[:, cols], preferred_element_type=F32)
        act = gate * (1.0 / (1.0 + jnp.exp(-gate))) * up
        act_ref[:, cols] = act.astype(BF16)
    down = jnp.dot(act_ref[...], wd_ref[...], preferred_element_type=F32)
    return x + 0.5 * _rms(down, post_g)


def _rope(x, cos, sin_lo, sin_hi):
    w = x.shape[1]
    half = ROPE_DIM // 2
    return (x * cos + pltpu.roll(x, w - half, 1) * sin_lo + pltpu.roll(x, half, 1) * sin_hi)


def _ffn_in_kernel(x_ref, pre_g_ref, wg_ref, wu_ref, wd_ref, post_g_ref, mix_g_ref, win_ref,
                   cos_ref, slo_ref, shi_ref,
                   h_ref, u_ref, q_ref, k_ref, v_ref, act_ref):
    h = _ffn_tile(x_ref[...], pre_g_ref[...], wg_ref, wu_ref, wd_ref, post_g_ref[...], act_ref)
    h_ref[...] = h
    z = jnp.dot(_rms(h, mix_g_ref[...]).astype(BF16), win_ref[...], preferred_element_type=F32)
    u_ref[...] = z[:, :SSM_WIDTH]
    o1 = SSM_WIDTH + Q_WIDTH
    cos, slo, shi = cos_ref[...], slo_ref[...], shi_ref[...]
    rep = Q_WIDTH // LANES
    q = _rope(z[:, SSM_WIDTH:o1], jnp.tile(cos, (1, rep)), jnp.tile(slo, (1, rep)), jnp.tile(shi, (1, rep)))
    q_ref[...] = (q * (HEAD_DIM ** -0.5)).astype(BF16)
    k_ref[...] = _rope(z[:, o1:o1 + KV_WIDTH], cos, slo, shi)
    v_ref[...] = z[:, o1 + KV_WIDTH:]


def _ffn_in(x, pre_g, ffn_w, post_g, mix_g, win, cos, slo, shi, tm):
    t = x.shape[0]
    n_pos_tiles = cos.shape[0] // tm
    row = lambda w: pl.BlockSpec((tm, w), lambda i: (i, 0))
    tab = pl.BlockSpec((tm, LANES), lambda i: (i % n_pos_tiles, 0))
    return pl.pallas_call(
        _ffn_in_kernel,
        grid=(t // tm,),
        in_specs=[row(D_MODEL), _const_spec((1, D_MODEL)), *[_const_spec(w.shape) for w in ffn_w],
                  _const_spec((1, D_MODEL)), _const_spec((1, D_MODEL)), _const_spec(win.shape),
                  tab, tab, tab],
        out_specs=[row(D_MODEL), row(SSM_WIDTH), row(Q_WIDTH), row(KV_WIDTH), row(KV_WIDTH)],
        out_shape=[jax.ShapeDtypeStruct((t, D_MODEL), F32), jax.ShapeDtypeStruct((t, SSM_WIDTH), F32),
                   jax.ShapeDtypeStruct((t, Q_WIDTH), BF16), jax.ShapeDtypeStruct((t, KV_WIDTH), F32),
                   jax.ShapeDtypeStruct((t, KV_WIDTH), F32)],
        scratch_shapes=[pltpu.VMEM((tm, D_FF), BF16)],
        compiler_params=_params(("arbitrary",)),
        name="ffn_in",
    )(x, pre_g, *ffn_w, post_g, mix_g, win, cos, slo, shi)


def _complex_step(s_re, s_im, l_re, l_im, x_re, x_im):
    return l_re * s_re - l_im * s_im + x_re, l_re * s_im + l_im * s_re + x_im


def _ssm_prompt_kernel(u_ref, m_ref, w_ref, v_ref, lam_ref, d_ref, y_ref, st_ref,
                       x_scr, ss_scr, s_scr, *, lc, nb, ncl):
    tt = pl.program_id(1)

    @pl.when(tt == 0)
    def _():
        s_scr[...] = jnp.zeros_like(s_scr)

    def piece(n, j):
        return u_ref[n, pl.ds(j, ncl, stride=lc), :]

    a = jnp.concatenate(
        [jnp.concatenate([piece(n, j) for j in range(lc)], axis=1) for n in range(nb)], axis=0).astype(BF16)
    x = jnp.dot(a, w_ref[0], preferred_element_type=F32)
    nq = STATE_LANES // LANES
    for qq in range(2 * nq):
        x_scr[qq] = x[:, qq * LANES:(qq + 1) * LANES]
    l_re = [jnp.broadcast_to(lam_ref[0, :, qq * LANES:(qq + 1) * LANES], (nb, LANES)) for qq in range(nq)]
    l_im = [jnp.broadcast_to(lam_ref[0, :, (nq + qq) * LANES:(nq + qq + 1) * LANES], (nb, LANES)) for qq in range(nq)]

    def body(c, carry):
        rows = pl.ds(c, nb, stride=ncl)
        new = []
        for qq in range(nq):
            s_re, s_im = carry[qq], carry[nq + qq]
            ss_scr[qq, rows, :] = s_re
            ss_scr[nq + qq, rows, :] = s_im
            new.append(_complex_step(s_re, s_im, l_re[qq], l_im[qq], x_scr[qq, rows, :], x_scr[nq + qq, rows, :]))
        return tuple(r for r, _ in new) + tuple(i for _, i in new)

    s_fin = lax.fori_loop(0, ncl, body, tuple(s_scr[qq] for qq in range(2 * nq)), unroll=4)
    for qq in range(2 * nq):
        s_scr[qq] = s_fin[qq]
    st_ref[0] = jnp.concatenate(s_fin, axis=1)
    s_start = jnp.concatenate([ss_scr[qq] for qq in range(2 * nq)], axis=1).astype(BF16)
    y = (jnp.dot(a, m_ref[0], preferred_element_type=F32)
         + jnp.dot(s_start, v_ref[0], preferred_element_type=F32))
    d = d_ref[...]
    for n in range(nb):
        for j in range(lc):
            y_ref[n, pl.ds(j, ncl, stride=lc), :] = (
                y[n * ncl:(n + 1) * ncl, j * LANES:(j + 1) * LANES] + d * piece(n, j))


def _ssm_prompt(u3, m, w, v, lam, d, lc, tl):
    nb, seq, _ = u3.shape
    ncl = tl // lc
    nslab = 2 * STATE_LANES // LANES
    kern = functools.partial(_ssm_prompt_kernel, lc=lc, nb=nb, ncl=ncl)
    wspec = lambda a: pl.BlockSpec((1,) + a.shape[1:], lambda b, t: (b, 0, 0))
    return pl.pallas_call(
        kern,
        grid=(N_LANE_BLOCKS, seq // tl),
        in_specs=[pl.BlockSpec((nb, tl, LANES), lambda b, t: (0, t, b)), wspec(m), wspec(w), wspec(v), wspec(lam),
                  pl.BlockSpec((1, LANES), lambda b, t: (0, b))],
        out_specs=[pl.BlockSpec((nb, tl, LANES), lambda b, t: (0, t, b)),
                   pl.BlockSpec((1, nb, 2 * STATE_LANES), lambda b, t: (b, 0, 0))],
        out_shape=[jax.ShapeDtypeStruct(u3.shape, F32),
                   jax.ShapeDtypeStruct((N_LANE_BLOCKS, nb, 2 * STATE_LANES), F32)],
        scratch_shapes=[pltpu.VMEM((nslab, nb * ncl, LANES), F32), pltpu.VMEM((nslab, nb * ncl, LANES), F32),
                        pltpu.VMEM((nslab, nb, LANES), F32)],
        compiler_params=_params(("arbitrary", "arbitrary")),
        name="ssm_prompt",
    )(u3, m, w, v, lam, d)


def _ssm_sample_kernel(u_ref, sre_ref, sim_ref, m_ref, w_ref, v_ref, lam_ref, d_ref,
                       y_ref, ore_ref, oim_ref, *, lc, ns):
    def piece(j):
        return u_ref[pl.ds(j, ns, stride=lc), :]

    a = jnp.concatenate([piece(j) for j in range(lc)], axis=1).astype(BF16)
    s_re, s_im = sre_ref[...], sim_ref[...]
    x = jnp.dot(a, w_ref[0], preferred_element_type=F32)
    e_re, e_im = _complex_step(s_re, s_im, lam_ref[0, :, :STATE_LANES], lam_ref[0, :, STATE_LANES:],
                               x[:, :STATE_LANES], x[:, STATE_LANES:])
    ore_ref[...] = e_re
    oim_ref[...] = e_im
    s0 = jnp.concatenate([s_re, s_im], axis=1).astype(BF16)
    y = (jnp.dot(a, m_ref[0], preferred_element_type=F32) + jnp.dot(s0, v_ref[0], preferred_element_type=F32))
    d = d_ref[...]
    for j in range(lc):
        y_ref[pl.ds(j, ns, stride=lc), :] = y[:, j * LANES:(j + 1) * LANES] + d * piece(j)


def _ssm_sample(u, s_re, s_im, m, w, v, lam, d, lc):
    t = u.shape[0]
    ns = t // lc
    kern = functools.partial(_ssm_sample_kernel, lc=lc, ns=ns)
    assert m.shape[1] == 2 * lc * LANES
    n = lc * LANES
    col = lambda rows, width: pl.BlockSpec((rows, width), lambda b: (0, b))
    return pl.pallas_call(
        kern,
        grid=(N_LANE_BLOCKS,),
        in_specs=[col(t, LANES), col(ns, STATE_LANES), col(ns, STATE_LANES),
                  pl.BlockSpec((1, n, n), lambda b: (b, 0, 0)),
                  pl.BlockSpec((1, n, 2 * STATE_LANES), lambda b: (b, 1, 0)),
                  pl.BlockSpec((1, 2 * STATE_LANES, n), lambda b: (b, 0, 0)),
                  pl.BlockSpec((1, 1, 2 * STATE_LANES), lambda b: (b, 0, 0)), col(1, LANES)],
        out_specs=[col(t, LANES), col(ns, STATE_LANES), col(ns, STATE_LANES)],
        out_shape=[jax.ShapeDtypeStruct(u.shape, F32), jax.ShapeDtypeStruct(s_re.shape, F32),
                   jax.ShapeDtypeStruct(s_im.shape, F32)],
        compiler_params=_params(("arbitrary",)),
        name="ssm_sample",
    )(u, s_re, s_im, m, w, v, lam, d)


def _ssm_discretise(a_re, a_im, log_step):
    dt = jnp.exp(log_step)
    mag = jnp.exp(a_re * dt)
    l_re, l_im = mag * jnp.cos(a_im * dt), mag * jnp.sin(a_im * dt)
    den = a_re * a_re + a_im * a_im
    n_re, n_im = l_re - 1.0, l_im
    return l_re, l_im, (n_re * a_re + n_im * a_im) / den, (n_im * a_re - n_re * a_im) / den


def _complex_powers(l_re, l_im, n):
    p_re, p_im = [jnp.ones_like(l_re)], [jnp.zeros_like(l_re)]
    for _ in range(n):
        p_re, p_im = p_re + [p_re[-1] * l_re - p_im[-1] * l_im], p_im + [p_re[-1] * l_im + p_im[-1] * l_re]
    return p_re, p_im


def _ssm_tables_kernel(ac_re_ref, ac_im_ref, lsc_ref, ar_re_ref, ar_im_ref, lsr_ref, b_re_ref, b_im_ref,
                       c_re_ref, c_im_ref, m_ref, w_ref, v_ref, lam_ref, lam_half_ref, *, lc):
    hi = lax.Precision.HIGHEST
    l_re, l_im, cf_re, cf_im = _ssm_discretise(ac_re_ref[0], ac_im_ref[0], lsc_ref[0])
    b_re, b_im, c_re, c_im = b_re_ref[0], b_im_ref[0], c_re_ref[0], c_im_ref[0]
    bb_re = cf_re * b_re - cf_im * b_im
    bb_im = cf_re * b_im + cf_im * b_re
    p_re, p_im = _complex_powers(l_re, l_im, lc)
    lag = []
    for k in range(lc):
        et_re = (p_re[k] * bb_re - p_im[k] * bb_im).T
        et_im = (p_re[k] * bb_im + p_im[k] * bb_re).T
        j = lc - 1 - k
        w_ref[0, j * LANES:(j + 1) * LANES, :STATE_LANES] = et_re.astype(BF16)
        w_ref[0, j * LANES:(j + 1) * LANES, STATE_LANES:] = et_im.astype(BF16)
        lag.append((jnp.dot(et_re, c_re, precision=hi, preferred_element_type=F32)
                    - jnp.dot(et_im, c_im, precision=hi, preferred_element_type=F32)).astype(BF16))
        v_ref[0, :STATE_LANES, k * LANES:(k + 1) * LANES] = (p_re[k + 1] * c_re - p_im[k + 1] * c_im).astype(BF16)
        v_ref[0, STATE_LANES:, k * LANES:(k + 1) * LANES] = (-(p_im[k + 1] * c_re + p_re[k + 1] * c_im)).astype(BF16)
    zero = jnp.zeros((LANES, LANES), BF16)
    for j in range(lc):
        for jj in range(lc):
            m_ref[0, j * LANES:(j + 1) * LANES, jj * LANES:(jj + 1) * LANES] = lag[jj - j] if jj >= j else zero
    r_re, r_im, _, _ = _ssm_discretise(ar_re_ref[0], ar_im_ref[0], lsr_ref[0])
    q_re, q_im = _complex_powers(r_re, r_im, lc)
    lam_ref[0] = jnp.concatenate([q_re[lc], q_im[lc]], axis=1)
    lam_half_ref[0] = jnp.concatenate([q_re[lc // 2], q_im[lc // 2]], axis=1)


def _ssm_tables(a_re, a_im, log_step, b_re, b_im, c_re, c_im, lc):
    g, p, h = b_re.shape
    nbk, r = N_LANE_BLOCKS, GROUPS_PER_LANE_BLOCK
    ls = jnp.broadcast_to(log_step[:, None], (g, p))
    cols = [x.reshape(nbk, STATE_LANES, 1) for x in (a_re, a_im, ls)]
    rows = [x.reshape(nbk, 1, STATE_LANES) for x in (a_re, a_im, ls)]
    eye = jnp.eye(r, dtype=F32)[None, :, None, :, None]

    def block_diag(x):
        return (x[:, :, :, None, :] * eye).reshape(nbk, STATE_LANES, LANES)

    mats = [block_diag(b_re.reshape(nbk, r, p, h)), block_diag(b_im.reshape(nbk, r, p, h)),
            block_diag(c_re.reshape(nbk, r, h, p).transpose(0, 1, 3, 2)),
            block_diag(c_im.reshape(nbk, r, h, p).transpose(0, 1, 3, 2))]
    spec = lambda shape: pl.BlockSpec((1,) + shape, lambda b: (b, 0, 0))
    n = lc * LANES
    return pl.pallas_call(
        functools.partial(_ssm_tables_kernel, lc=lc),
        grid=(nbk,),
        in_specs=[spec((STATE_LANES, 1))] * 3 + [spec((1, STATE_LANES))] * 3 + [spec((STATE_LANES, LANES))] * 4,
        out_specs=[spec((n, n)), spec((n, 2 * STATE_LANES)), spec((2 * STATE_LANES, n)),
                   spec((1, 2 * STATE_LANES)), spec((1, 2 * STATE_LANES))],
        out_shape=[jax.ShapeDtypeStruct((nbk, n, n), BF16), jax.ShapeDtypeStruct((nbk, n, 2 * STATE_LANES), BF16),
                   jax.ShapeDtypeStruct((nbk, 2 * STATE_LANES, n), BF16),
                   jax.ShapeDtypeStruct((nbk, 1, 2 * STATE_LANES), F32),
                   jax.ShapeDtypeStruct((nbk, 1, 2 * STATE_LANES), F32)],
        compiler_params=_params(("arbitrary",)),
        name="ssm_tables",
    )(*cols, *rows, *mats)


def _nt_dot(a, b):
    return lax.dot_general(a, b, (((1,), (1,)), ((), ())), preferred_element_type=F32)


def _swa_prompt_kernel(sink_ref, q_ref, kp_ref, kc_ref, vp_ref, vc_ref, mstd_ref, mfirst_ref, g_ref, o_ref, *, nblk):
    first_tile = pl.program_id(1) == 0
    low_half = lax.broadcasted_iota(jnp.int32, (1, LANES), 1) < HEAD_DIM

    def head_variants(prev_ref, cur_ref):
        x = jnp.concatenate([prev_ref[...], cur_ref[...]], axis=0)
        xr = pltpu.roll(x, HEAD_DIM, 1)
        zero = jnp.zeros_like(x)
        pick = lambda lo, hi: jnp.where(low_half, lo, hi).astype(BF16)
        return [[pick(x, zero), pick(zero, xr)], [pick(xr, zero), pick(zero, x)]]

    kvar = head_variants(kp_ref, kc_ref)
    vvar = head_variants(vp_ref, vc_ref)
    gain = g_ref[...]
    pair = GQA_GROUP // 2
    for j in range(nblk):
        keys = slice(j * WINDOW, (j + 2) * WINDOW)
        vmask = mstd_ref[...]
        if j == 0:
            vmask = jnp.where(first_tile, mfirst_ref[...], vmask)
        blocks = []
        for kh in range(N_KV_HEADS):
            qs = jnp.concatenate([q_ref[j * WINDOW:(j + 1) * WINDOW, (kh * pair + a) * LANES:(kh * pair + a + 1) * LANES]
                                  for a in range(pair)], axis=0)
            acc = None
            for par in range(2):
                sink = jnp.concatenate(
                    [jnp.full((WINDOW, 1), sink_ref[(kh * pair + a) * 2 + par], F32) for a in range(pair)], axis=0)
                sc = jnp.where(vmask > 0.0, _nt_dot(qs, kvar[kh][par][keys]), NEG_INF)
                m = jnp.maximum(jnp.max(sc, axis=-1, keepdims=True), sink)
                e = jnp.exp(sc - m)
                den = jnp.sum(e, axis=-1, keepdims=True) + jnp.exp(sink - m)
                o = jnp.dot(e.astype(BF16), vvar[kh][par][keys], preferred_element_type=F32) * (1.0 / den)
                acc = o if acc is None else acc + o
            blocks += [acc[a * WINDOW:(a + 1) * WINDOW] for a in range(pair)]
        o_ref[j * WINDOW:(j + 1) * WINDOW, :] = _rms(jnp.concatenate(blocks, axis=1), gain).astype(BF16)


def _swa_masks():
    rows = (GQA_GROUP // 2) * WINDOW
    qi = jnp.arange(rows)[:, None] % WINDOW + WINDOW
    kj = jnp.arange(2 * WINDOW)[None, :]
    diff = qi - kj
    std = (diff >= 0) & (diff <= WINDOW)
    return std.astype(F32), (std & (kj >= WINDOW)).astype(F32)


def _swa_prompt(q, k, v, sinks, out_g, n, seq, tq):
    nt, nblk = seq // tq, tq // WINDOW
    mstd, mfirst = _swa_masks()
    cur = lambda w: pl.BlockSpec((tq, w), lambda b, i: (b * nt + i, 0))
    prev = lambda w: pl.BlockSpec((WINDOW, w), lambda b, i: (jnp.maximum((b * nt + i) * nblk - 1, b * nt * nblk), 0))
    return pl.pallas_call(
        functools.partial(_swa_prompt_kernel, nblk=nblk),
        grid=(n, nt),
        in_specs=[pl.BlockSpec(memory_space=pltpu.SMEM), cur(Q_WIDTH), prev(KV_WIDTH), cur(KV_WIDTH),
                  prev(KV_WIDTH), cur(KV_WIDTH), _const_spec(mstd.shape), _const_spec(mfirst.shape),
                  _const_spec((1, Q_WIDTH))],
        out_specs=cur(Q_WIDTH),
        out_shape=jax.ShapeDtypeStruct((n * seq, Q_WIDTH), BF16),
        compiler_params=_params(("arbitrary", "arbitrary")),
        name="swa_prompt",
    )(sinks, q, k, k, v, v, mstd, mfirst, out_g)


def _swa_sample_kernel(sink_ref, q_ref, kn_ref, vn_ref, ck_ref, cv_ref, g_ref, o_ref, nk_ref, nv_ref, *, ns, t):
    rows = GQA_GROUP * t
    tok = lax.broadcasted_iota(jnp.int32, (rows, WINDOW), 0) % t
    valid_c = lax.broadcasted_iota(jnp.int32, (rows, WINDOW), 1) >= tok
    tok_n = lax.broadcasted_iota(jnp.int32, (rows, t), 0) % t
    valid_n = lax.broadcasted_iota(jnp.int32, (rows, t), 1) <= tok_n
    gain = g_ref[...]

    def body(s, _):
        q = q_ref[s].astype(F32)
        kn, vn = kn_ref[s], vn_ref[s]
        ck, cv = ck_ref[s], cv_ref[s]
        nk_ref[s, :WINDOW - t, :] = ck[t:]
        nk_ref[s, WINDOW - t:, :] = kn
        nv_ref[s, :WINDOW - t, :] = cv[t:]
        nv_ref[s, WINDOW - t:, :] = vn
        outs = []
        for kh in range(N_KV_HEADS):
            hs = slice(kh * HEAD_DIM, (kh + 1) * HEAD_DIM)
            q4 = jnp.concatenate([q[:, (kh * GQA_GROUP + g) * HEAD_DIM:(kh * GQA_GROUP + g + 1) * HEAD_DIM]
                                  for g in range(GQA_GROUP)], axis=0).astype(BF16)
            sc_c = jnp.where(valid_c, _nt_dot(q4, ck[:, hs].astype(BF16)), NEG_INF)
            sc_n = jnp.where(valid_n, _nt_dot(q4, kn[:, hs].astype(BF16)), NEG_INF)
            sink = jnp.concatenate([jnp.full((t, 1), sink_ref[kh * GQA_GROUP + g], F32) for g in range(GQA_GROUP)],
                                   axis=0)
            m = jnp.maximum(jnp.maximum(jnp.max(sc_c, axis=-1, keepdims=True),
                                        jnp.max(sc_n, axis=-1, keepdims=True)), sink)
            e_c, e_n = jnp.exp(sc_c - m), jnp.exp(sc_n - m)
            den = jnp.sum(e_c, axis=-1, keepdims=True) + jnp.sum(e_n, axis=-1, keepdims=True) + jnp.exp(sink - m)
            o4 = (jnp.dot((e_c / den).astype(BF16), cv[:, hs].astype(BF16), preferred_element_type=F32)
                  + jnp.dot((e_n / den).astype(BF16), vn[:, hs].astype(BF16), preferred_element_type=F32))
            outs += [o4[g * t:(g + 1) * t] for g in range(GQA_GROUP)]
        o = jnp.concatenate(outs, axis=1)
        o_ref[s] = _rms(o, gain).astype(BF16)
        return 0

    lax.fori_loop(0, ns, body, 0)


def _swa_sample(q3, kn3, vn3, ck, cv, sinks, out_g, sb):
    ns, t, _ = q3.shape
    blk = lambda a: pl.BlockSpec((sb,) + a.shape[1:], lambda i: (i, 0, 0))
    kern = functools.partial(_swa_sample_kernel, ns=sb, t=t)
    return pl.pallas_call(
        kern,
        grid=(ns // sb,),
        in_specs=[pl.BlockSpec(memory_space=pltpu.SMEM), blk(q3), blk(kn3), blk(vn3), blk(ck), blk(cv),
                  pl.BlockSpec((1, Q_WIDTH), lambda i: (0, 0))],
        out_specs=[blk(q3), blk(ck), blk(cv)],
        out_shape=[jax.ShapeDtypeStruct(q3.shape, BF16), jax.ShapeDtypeStruct(ck.shape, F32),
                   jax.ShapeDtypeStruct(cv.shape, F32)],
        compiler_params=_params(("arbitrary",)),
        name="swa_sample",
    )(sinks, q3, kn3, vn3, ck, cv, out_g)


def _gelu_tanh(x):
    return 0.5 * x * (1.0 + jnp.tanh(math.sqrt(2.0 / math.pi) * (x + 0.044715 * (x * x * x))))


def _merge_tile(h, y, at, wglu_ref, bglu, sg, wout_ref, post_g, xa_g, wq_ref):
    g = _gelu_tanh(y)
    lin = jnp.dot(g.astype(BF16), wglu_ref[...], preferred_element_type=F32) + bglu
    y_ssm = g * (1.0 / (1.0 + jnp.exp(-lin)))
    ssm_n = _rms(y_ssm, sg).astype(BF16)
    mixed = (jnp.dot(ssm_n, wout_ref[:SSM_WIDTH, :], preferred_element_type=F32)
             + jnp.dot(at, wout_ref[SSM_WIDTH:, :], preferred_element_type=F32))
    h2 = h + _rms(mixed, post_g)
    qm = jnp.dot(_rms(h2, xa_g).astype(BF16), wq_ref[...], preferred_element_type=F32)
    return h2, (qm * (MEM_HEAD_DIM ** -0.5)).astype(BF16)


def _merge_kernel(h_ref, y_ref, at_ref, wglu_ref, bglu_ref, sg_ref, wout_ref, post_g_ref, xa_g_ref, wq_ref,
                  h2_ref, qm_ref):
    h2_ref[...], qm_ref[...] = _merge_tile(h_ref[...], y_ref[...], at_ref[...], wglu_ref, bglu_ref[...], sg_ref[...],
                                           wout_ref, post_g_ref[...], xa_g_ref[...], wq_ref)


def _merge_specs(wglu, wout, wq):
    return [_const_spec(wglu.shape), _const_spec((1, SSM_WIDTH)), _const_spec((1, SSM_WIDTH)), _const_spec(wout.shape),
            _const_spec((1, D_MODEL)), _const_spec((1, D_MODEL)), _const_spec(wq.shape)]


def _merge(h, y, at, merge_w, tm):
    t = h.shape[0]
    row = lambda w: pl.BlockSpec((tm, w), lambda i: (i, 0))
    return pl.pallas_call(
        _merge_kernel,
        grid=(t // tm,),
        in_specs=[row(D_MODEL), row(SSM_WIDTH), row(Q_WIDTH)] + _merge_specs(merge_w[0], merge_w[3], merge_w[6]),
        out_specs=[row(D_MODEL), row(D_MODEL)],
        out_shape=[jax.ShapeDtypeStruct((t, D_MODEL), F32), jax.ShapeDtypeStruct((t, D_MODEL), BF16)],
        compiler_params=_params(("arbitrary",)),
        name="merge",
    )(h, y, at, *merge_w)


def _mem_attn_sample_kernel(q_ref, k_ref, v_ref, o_ref, *, gb, t):
    rows = t * MEM_HEADS
    halves = MEM_HEAD_DIM // LANES
    kv_rows = N_MEM * halves * MEM_HEADS
    period = halves * MEM_HEADS
    lane = lax.broadcasted_iota(jnp.int32, (rows, kv_rows), 1) % period
    head = lax.broadcasted_iota(jnp.int32, (rows, kv_rows), 0) % MEM_HEADS
    in_half = [lane == head + hf * MEM_HEADS for hf in range(halves)]
    for b in range(gb):
        kb = k_ref[b].astype(BF16)
        vb = v_ref[b].astype(BF16)
        part = _nt_dot(q_ref[b], kb)
        sc = jnp.where(in_half[0], part[:rows], 0.0)
        for hf in range(1, halves):
            sc = sc + pltpu.roll(jnp.where(in_half[hf], part[hf * rows:(hf + 1) * rows], 0.0),
                                 kv_rows - hf * MEM_HEADS, 1)
        sc = jnp.where(in_half[0], sc, -jnp.inf)
        e = jnp.exp(sc - jnp.max(sc, axis=-1, keepdims=True))
        inv = 1.0 / jnp.sum(e, axis=-1, keepdims=True)
        e_all = jnp.concatenate([e] + [pltpu.roll(e, hf * MEM_HEADS, 1) for hf in range(1, halves)], axis=0)
        o = jnp.dot(e_all.astype(BF16), vb, preferred_element_type=F32)
        o_ref[b] = (o * jnp.concatenate([inv] * halves, axis=0)).astype(BF16)


def _mem_attn_sample(qm, cache_k, cache_v, n_s, t_s, gb):
    halves = MEM_HEAD_DIM // LANES
    rows = halves * t_s * MEM_HEADS
    kv_rows = N_MEM * halves * MEM_HEADS

    def stored_rows(c):
        c = c.reshape(n_s, N_MEM, MEM_HEADS, halves, LANES).transpose(0, 1, 3, 2, 4)
        return c.reshape(n_s, kv_rows, LANES)

    q = qm.reshape(n_s, t_s, MEM_HEADS, halves, LANES).transpose(0, 3, 1, 2, 4).reshape(n_s, rows, LANES)
    blk = lambda r: pl.BlockSpec((gb, r, LANES), lambda i: (i, 0, 0))
    o = pl.pallas_call(
        functools.partial(_mem_attn_sample_kernel, gb=gb, t=t_s),
        grid=(n_s // gb,),
        in_specs=[blk(rows), blk(kv_rows), blk(kv_rows)],
        out_specs=blk(rows),
        out_shape=jax.ShapeDtypeStruct((n_s, rows, LANES), BF16),
        compiler_params=_params(("arbitrary",)),
        name="mem_attn_sample",
    )(q, stored_rows(cache_k), stored_rows(cache_v))
    o = o.reshape(n_s, halves, t_s, MEM_HEADS, LANES).transpose(0, 2, 3, 1, 4)
    return o.reshape(n_s * t_s, D_MODEL)


def _mem_heads(q, k, v):
    outs = []
    for hh in range(MEM_HEADS):
        hs = slice(hh * MEM_HEAD_DIM, (hh + 1) * MEM_HEAD_DIM)
        sc = _nt_dot(q[:, hs], k[:, hs])
        e = jnp.exp(sc - jnp.max(sc, axis=-1, keepdims=True))
        inv = 1.0 / jnp.sum(e, axis=-1, keepdims=True)
        outs.append((jnp.dot(e.astype(BF16), v[:, hs], preferred_element_type=F32) * inv).astype(BF16))
    return jnp.concatenate(outs, axis=1)


def _ffn_out_tile(h2, om, wo_ref, xa_post, pre_g, wg_ref, wu_ref, wd_ref, post_g, act_ref):
    c = jnp.dot(om, wo_ref[...], preferred_element_type=F32)
    h3 = h2 + _rms(c, xa_post)
    return _ffn_tile(h3, pre_g, wg_ref, wu_ref, wd_ref, post_g, act_ref)


def _ffn_out_specs(wo, ffn_w):
    return [_const_spec(wo.shape), _const_spec((1, D_MODEL)), _const_spec((1, D_MODEL)),
            *[_const_spec(w.shape) for w in ffn_w], _const_spec((1, D_MODEL))]


def _ffn_out_kernel(h_ref, o_ref, wo_ref, xa_post_ref, pre_g_ref, wg_ref, wu_ref, wd_ref, post_g_ref, out_ref, act_ref):
    out_ref[...] = _ffn_out_tile(h_ref[...], o_ref[...], wo_ref, xa_post_ref[...], pre_g_ref[...], wg_ref, wu_ref,
                                 wd_ref, post_g_ref[...], act_ref)


def _ffn_out(h, o, out_w, tm):
    t = h.shape[0]
    row = pl.BlockSpec((tm, D_MODEL), lambda i: (i, 0))
    return pl.pallas_call(
        _ffn_out_kernel,
        grid=(t // tm,),
        in_specs=[row, row] + _ffn_out_specs(out_w[0], out_w[3:6]),
        out_specs=row,
        out_shape=jax.ShapeDtypeStruct((t, D_MODEL), F32),
        scratch_shapes=[pltpu.VMEM((tm, D_FF), BF16)],
        compiler_params=_params(("arbitrary",)),
        name="ffn_out",
    )(h, o, *out_w)


def _post_kernel(h_ref, y_ref, at_ref, k_ref, v_ref,
                 wglu_ref, bglu_ref, sg_ref, wout_ref, post_g_ref, xa_g_ref, wq_ref,
                 wo_ref, xa_post_ref, pre_g_ref, wg_ref, wu_ref, wd_ref, ffn_post_ref, out_ref, act_ref):
    h2, qm = _merge_tile(h_ref[...], y_ref[...], at_ref[...], wglu_ref, bglu_ref[...], sg_ref[...],
                         wout_ref, post_g_ref[...], xa_g_ref[...], wq_ref)
    om = _mem_heads(qm, k_ref[0], v_ref[0])
    out_ref[...] = _ffn_out_tile(h2, om, wo_ref, xa_post_ref[...], pre_g_ref[...], wg_ref, wu_ref, wd_ref,
                                 ffn_post_ref[...], act_ref)


def _post(h, y, at, k3, v3, merge_w, out_w, tm):
    t = h.shape[0]
    tiles_per_batch = t // k3.shape[0] // tm
    row = lambda w: pl.BlockSpec((tm, w), lambda i: (i, 0))
    kv = pl.BlockSpec((1, N_MEM, D_MODEL), lambda i: (i // tiles_per_batch, 0, 0))
    return pl.pallas_call(
        _post_kernel,
        grid=(t // tm,),
        in_specs=([row(D_MODEL), row(SSM_WIDTH), row(Q_WIDTH), kv, kv]
                  + _merge_specs(merge_w[0], merge_w[3], merge_w[6]) + _ffn_out_specs(out_w[0], out_w[3:6])),
        out_specs=row(D_MODEL),
        out_shape=jax.ShapeDtypeStruct((t, D_MODEL), F32),
        scratch_shapes=[pltpu.VMEM((tm, D_FF), BF16)],
        compiler_params=_params(("arbitrary",)),
        name="post",
    )(h, y, at, k3, v3, *merge_w, *out_w)


def _mem_kv_kernel(m_ref, g_ref, wkv_ref, k_ref, v_ref, kb_ref, vb_ref):
    kv = jnp.dot(_rms(m_ref[...], g_ref[...]).astype(BF16), wkv_ref[...], preferred_element_type=F32)
    k_ref[...] = kv[:, :D_MODEL]
    v_ref[...] = kv[:, D_MODEL:]
    kb_ref[...] = kv[:, :D_MODEL].astype(BF16)
    vb_ref[...] = kv[:, D_MODEL:].astype(BF16)


def _mem_kv(mem, g, wkv, tm):
    t = mem.shape[0]
    row = pl.BlockSpec((tm, D_MODEL), lambda i: (i, 0))
    return pl.pallas_call(
        _mem_kv_kernel,
        grid=(t // tm,),
        in_specs=[row, _const_spec((1, D_MODEL)), _const_spec(wkv.shape)],
        out_specs=[row] * 4,
        out_shape=[jax.ShapeDtypeStruct((t, D_MODEL), F32)] * 2 + [jax.ShapeDtypeStruct((t, D_MODEL), BF16)] * 2,
        compiler_params=_params(("arbitrary",)),
        name="mem_kv",
    )(mem, g, wkv)


def _rope_tables(pos):
    half = ROPE_DIM // 2
    inv = ROPE_THETA ** (-jnp.arange(half, dtype=F32) * (2.0 / ROPE_DIM))
    ang = pos.astype(F32)[:, None] * inv[None, :]
    cos, sin = jnp.cos(ang), jnp.sin(ang)
    n = pos.shape[0]
    pad = jnp.zeros((n, HEAD_DIM - ROPE_DIM), F32)
    zero = jnp.zeros((n, half), F32)
    cos_h = jnp.concatenate([cos, cos, pad + 1.0], axis=1)
    lo_h = jnp.concatenate([-sin, zero, pad], axis=1)
    hi_h = jnp.concatenate([zero, sin, pad], axis=1)
    rep = LANES // HEAD_DIM
    return tuple(jnp.tile(a, (1, rep)) for a in (cos_h, lo_h, hi_h))


def _ffn_weights(w_gate, w_up, w_down):
    return w_gate.astype(BF16), w_up.astype(BF16), w_down.astype(BF16)


def _lane_block_states(st, n):
    st = st.reshape(N_LANE_BLOCKS, n, 2, GROUPS_PER_LANE_BLOCK, SSM_STATE).transpose(2, 1, 0, 3, 4)
    st = st.reshape(2, n, N_SSM_GROUPS, SSM_STATE)
    return st[0], st[1]


def kernel(x_prompt, x_sample, state_ssm_re, state_ssm_im, cache_swa_k, cache_swa_v, cache_mem_k, cache_mem_v, mem_prompt, ffn1_pre_g, ffn1_w_gate, ffn1_w_up, ffn1_w_down, ffn1_post_g, mix_pre_g, w_in, ssm_a_re, ssm_a_im, ssm_log_step, ssm_b_re, ssm_b_im, ssm_c_re, ssm_c_im, ssm_d, ssm_w_glu, ssm_b_glu, attn_sinks, ssm_out_g, attn_out_g, w_out, mix_post_g, mem_norm_g, w_mem_q, w_mem_k, w_mem_v, w_mem_o, xa_pre_g, xa_post_g, ffn2_pre_g, ffn2_w_gate, ffn2_w_up, ffn2_w_down, ffn2_post_g):
    n_p, s_p, _ = x_prompt.shape
    n_s, t_s, _ = x_sample.shape
    tm = 512
    row = lambda a: a.reshape(1, -1).astype(F32)

    ffn1_w = _ffn_weights(ffn1_w_gate, ffn1_w_up, ffn1_w_down)
    win = w_in.astype(BF16)
    merge_w = (ssm_w_glu.astype(BF16), row(ssm_b_glu), row(ssm_out_g), w_out.astype(BF16), row(mix_post_g),
               row(xa_pre_g), w_mem_q.astype(BF16))
    out_w = (w_mem_o.astype(BF16), row(xa_post_g), row(ffn2_pre_g),
             *_ffn_weights(ffn2_w_gate, ffn2_w_up, ffn2_w_down), row(ffn2_post_g))
    wkv = jnp.concatenate([w_mem_k, w_mem_v], axis=1).astype(BF16)
    d_row = row(ssm_d)
    ssm_args = (ssm_a_re.astype(F32), ssm_a_im.astype(F32), ssm_log_step.astype(F32), ssm_b_re.astype(F32),
                ssm_b_im.astype(F32), ssm_c_re.astype(F32), ssm_c_im.astype(F32))
    sinks = attn_sinks.astype(F32)

    pm_k, pm_v, pm_kb, pm_vb = _mem_kv(mem_prompt.reshape(n_p * N_MEM, D_MODEL), row(mem_norm_g), wkv, N_MEM)

    def tokenwise_in(x2, pos_tab):
        return _ffn_in(x2, row(ffn1_pre_g), ffn1_w, row(ffn1_post_g), row(mix_pre_g), win, *pos_tab, tm)

    lc_p = 2 * t_s
    ssm_m, ssm_w, ssm_v, lam_p, lam_s = _ssm_tables(*ssm_args, lc_p)
    h1, u, q, k, v = tokenwise_in(x_prompt.reshape(n_p * s_p, D_MODEL), _rope_tables(jnp.arange(s_p, dtype=jnp.int32)))
    y3, st_p = _ssm_prompt(u.reshape(n_p, s_p, SSM_WIDTH), ssm_m, ssm_w, ssm_v, lam_p, d_row, lc_p, 1024)
    at = _swa_prompt(q, k, v, sinks, row(attn_out_g), n_p, s_p, 512)
    y_prompt = _post(h1, y3.reshape(n_p * s_p, SSM_WIDTH), at, pm_kb.reshape(n_p, N_MEM, D_MODEL),
                     pm_vb.reshape(n_p, N_MEM, D_MODEL), merge_w, out_w, tm).reshape(n_p, s_p, D_MODEL)
    p_sre, p_sim = _lane_block_states(st_p, n_p)
    p_wk = k.reshape(n_p, s_p, N_KV_HEADS, HEAD_DIM)[:, -WINDOW:]
    p_wv = v.reshape(n_p, s_p, N_KV_HEADS, HEAD_DIM)[:, -WINDOW:]

    pos_s = jnp.tile(PAST_LEN + jnp.arange(t_s, dtype=jnp.int32), n_s)
    h1s, us, qs, ks, vs = tokenwise_in(x_sample.reshape(n_s * t_s, D_MODEL), _rope_tables(pos_s))
    ys, s_sre, s_sim = _ssm_sample(us, state_ssm_re.reshape(n_s, -1).astype(F32), state_ssm_im.reshape(n_s, -1).astype(F32),
                                   ssm_m, ssm_w, ssm_v, lam_s, d_row, t_s)
    win_len = cache_swa_k.shape[1]
    ats, s_wk, s_wv = _swa_sample(qs.reshape(n_s, t_s, Q_WIDTH), ks.reshape(n_s, t_s, KV_WIDTH), vs.reshape(n_s, t_s, KV_WIDTH),
                                  cache_swa_k.reshape(n_s, win_len, KV_WIDTH), cache_swa_v.reshape(n_s, win_len, KV_WIDTH),
                                  sinks, row(attn_out_g), 16)
    h2s, qms = _merge(h1s, ys, ats.reshape(n_s * t_s, Q_WIDTH), merge_w, tm)
    oms = _mem_attn_sample(qms, cache_mem_k, cache_mem_v, n_s, t_s, 4)
    y_sample = _ffn_out(h2s, oms, out_w, tm).reshape(n_s, t_s, D_MODEL)

    return (y_prompt, y_sample, p_sre, p_sim, p_wk, p_wv,
            pm_k.reshape(n_p, N_MEM, MEM_HEADS, MEM_HEAD_DIM), pm_v.reshape(n_p, N_MEM, MEM_HEADS, MEM_HEAD_DIM),
            s_sre.reshape(n_s, N_SSM_GROUPS, SSM_STATE), s_sim.reshape(n_s, N_SSM_GROUPS, SSM_STATE),
            s_wk.reshape(n_s, win_len, N_KV_HEADS, HEAD_DIM), s_wv.reshape(n_s, win_len, N_KV_HEADS, HEAD_DIM))
```

```python
import functools
import math

import jax
import jax.numpy as jnp
from jax import lax
from jax.experimental import pallas as pl
from jax.experimental.pallas import tpu as pltpu

F32 = jnp.float32
BF16 = jnp.bfloat16

D_MODEL = 1024
PAST_LEN = 16384
SSM_WIDTH = 512
SSM_GROUP = 16
N_SSM_GROUPS = 32
SSM_STATE = 64
HEAD_DIM = 64
N_HEADS = 8
N_KV_HEADS = 2
GQA_GROUP = 4
Q_WIDTH = 512
KV_WIDTH = 128
WINDOW = 128
ROPE_THETA = 500000.0
ROPE_DIM = 16
N_MEM = 256
MEM_HEADS = 4
MEM_HEAD_DIM = 256
D_FF = 2816
RMS_EPS = 1e-6
IN_WIDTH = SSM_WIDTH + Q_WIDTH + 2 * KV_WIDTH
NEG_INF = -1e30

LANES = 128
FF_CHUNK = 256
N_FF_CHUNKS = D_FF // FF_CHUNK
GROUPS_PER_LANE_BLOCK = LANES // SSM_GROUP
N_LANE_BLOCKS = SSM_WIDTH // LANES
STATE_LANES = GROUPS_PER_LANE_BLOCK * SSM_STATE
VMEM_LIMIT = 56 * 1024 * 1024


def _rms(x, g):
    return x * lax.rsqrt(jnp.mean(x * x, axis=-1, keepdims=True) + RMS_EPS) * g


def _const_spec(shape):
    nd = len(shape)
    return pl.BlockSpec(shape, lambda *_: (0,) * nd, pipeline_mode=pl.Buffered(1))


def _params(sem):
    return pltpu.CompilerParams(dimension_semantics=sem, vmem_limit_bytes=VMEM_LIMIT)


def _ffn_tile(x, pre_g, wg_ref, wu_ref, wd_ref, post_g, act_ref):
    xn = _rms(x, pre_g).astype(BF16)
    for c in range(N_FF_CHUNKS):
        cols = slice(c * FF_CHUNK, (c + 1) * FF_CHUNK)
        gate = jnp.dot(xn, wg_ref[:, cols], preferred_element_type=F32)
        up = jnp.dot(xn, wu_ref[:, cols], preferred_element_type=F32)
        act = gate * (1.0 / (1.0 + jnp.exp(-gate))) * up
        act_ref[:, cols] = act.astype(BF16)
    down = jnp.dot(act_ref[...], wd_ref[...], preferred_element_type=F32)
    return x + 0.5 * _rms(down, post_g)


def _rope(x, cos, sin_lo, sin_hi):
    w = x.shape[1]
    half = ROPE_DIM // 2
    return (x * cos + pltpu.roll(x, w - half, 1) * sin_lo + pltpu.roll(x, half, 1) * sin_hi)


def _ffn_in_kernel(x_ref, pre_g_ref, wg_ref, wu_ref, wd_ref, post_g_ref, mix_g_ref, win_ref,
                   cos_ref, slo_ref, shi_ref,
                   h_ref, u_ref, q_ref, k_ref, v_ref, act_ref):
    h = _ffn_tile(x_ref[...], pre_g_ref[...], wg_ref, wu_ref, wd_ref, post_g_ref[...], act_ref)
    h_ref[...] = h
    z = jnp.dot(_rms(h, mix_g_ref[...]).astype(BF16), win_ref[...], preferred_element_type=F32)
    u_ref[...] = z[:, :SSM_WIDTH]
    o1 = SSM_WIDTH + Q_WIDTH
    cos, slo, shi = cos_ref[...], slo_ref[...], shi_ref[...]
    rep = Q_WIDTH // LANES
    q = _rope(z[:, SSM_WIDTH:o1], jnp.tile(cos, (1, rep)), jnp.tile(slo, (1, rep)), jnp.tile(shi, (1, rep)))
    q_ref[...] = (q * (HEAD_DIM ** -0.5)).astype(BF16)
    k_ref[...] = _rope(z[:, o1:o1 + KV_WIDTH], cos, slo, shi)
    v_ref[...] = z[:, o1 + KV_WIDTH:]


def _ffn_in(x, pre_g, ffn_w, post_g, mix_g, win, cos, slo, shi, tm):
    t = x.shape[0]
    n_pos_tiles = cos.shape[0] // tm
    row = lambda w: pl.BlockSpec((tm, w), lambda i: (i, 0))
    tab = pl.BlockSpec((tm, LANES), lambda i: (i % n_pos_tiles, 0))
    return pl.pallas_call(
        _ffn_in_kernel,
        grid=(t // tm,),
        in_specs=[row(D_MODEL), _const_spec((1, D_MODEL)), *[_const_spec(w.shape) for w in ffn_w],
                  _const_spec((1, D_MODEL)), _const_spec((1, D_MODEL)), _const_spec(win.shape),
                  tab, tab, tab],
        out_specs=[row(D_MODEL), row(SSM_WIDTH), row(Q_WIDTH), row(KV_WIDTH), row(KV_WIDTH)],
        out_shape=[jax.ShapeDtypeStruct((t, D_MODEL), F32), jax.ShapeDtypeStruct((t, SSM_WIDTH), F32),
                   jax.ShapeDtypeStruct((t, Q_WIDTH), BF16), jax.ShapeDtypeStruct((t, KV_WIDTH), F32),
                   jax.ShapeDtypeStruct((t, KV_WIDTH), F32)],
        scratch_shapes=[pltpu.VMEM((tm, D_FF), BF16)],
        compiler_params=_params(("arbitrary",)),
        name="ffn_in",
    )(x, pre_g, *ffn_w, post_g, mix_g, win, cos, slo, shi)


def _complex_step(s_re, s_im, l_re, l_im, x_re, x_im):
    return l_re * s_re - l_im * s_im + x_re, l_re * s_im + l_im * s_re + x_im


def _ssm_prompt_kernel(u_ref, m_ref, w_ref, v_ref, lam_ref, d_ref, y_ref, st_ref,
                       x_scr, ss_scr, s_scr, *, lc, nb, ncl):
    tt = pl.program_id(1)

    @pl.when(tt == 0)
    def _():
        s_scr[...] = jnp.zeros_like(s_scr)

    def piece(n, j):
        return u_ref[n, pl.ds(j, ncl, stride=lc), :]

    a = jnp.concatenate(
        [jnp.concatenate([piece(n, j) for j in range(lc)], axis=1) for n in range(nb)], axis=0).astype(BF16)
    x = jnp.dot(a, w_ref[0], preferred_element_type=F32)
    nq = STATE_LANES // LANES
    for qq in range(2 * nq):
        x_scr[qq] = x[:, qq * LANES:(qq + 1) * LANES]
    l_re = [jnp.broadcast_to(lam_ref[0, :, qq * LANES:(qq + 1) * LANES], (nb, LANES)) for qq in range(nq)]
    l_im = [jnp.broadcast_to(lam_ref[0, :, (nq + qq) * LANES:(nq + qq + 1) * LANES], (nb, LANES)) for qq in range(nq)]

    def body(c, carry):
        rows = pl.ds(c, nb, stride=ncl)
        new = []
        for qq in range(nq):
            s_re, s_im = carry[qq], carry[nq + qq]
            ss_scr[qq, rows, :] = s_re
            ss_scr[nq + qq, rows, :] = s_im
            new.append(_complex_step(s_re, s_im, l_re[qq], l_im[qq], x_scr[qq, rows, :], x_scr[nq + qq, rows, :]))
        return tuple(r for r, _ in new) + tuple(i for _, i in new)

    s_fin = lax.fori_loop(0, ncl, body, tuple(s_scr[qq] for qq in range(2 * nq)), unroll=4)
    for qq in range(2 * nq):
        s_scr[qq] = s_fin[qq]
    st_ref[0] = jnp.concatenate(s_fin, axis=1)
    s_start = jnp.concatenate([ss_scr[qq] for qq in range(2 * nq)], axis=1).astype(BF16)
    y = (jnp.dot(a, m_ref[0], preferred_element_type=F32)
         + jnp.dot(s_start, v_ref[0], preferred_element_type=F32))
    d = d_ref[...]
    for n in range(nb):
        for j in range(lc):
            y_ref[n, pl.ds(j, ncl, stride=lc), :] = (
                y[n * ncl:(n + 1) * ncl, j * LANES:(j + 1) * LANES] + d * piece(n, j))


def _ssm_prompt(u3, m, w, v, lam, d, lc, tl):
    nb, seq, _ = u3.shape
    ncl = tl // lc
    nslab = 2 * STATE_LANES // LANES
    kern = functools.partial(_ssm_prompt_kernel, lc=lc, nb=nb, ncl=ncl)
    wspec = lambda a: pl.BlockSpec((1,) + a.shape[1:], lambda b, t: (b, 0, 0))
    return pl.pallas_call(
        kern,
        grid=(N_LANE_BLOCKS, seq // tl),
        in_specs=[pl.BlockSpec((nb, tl, LANES), lambda b, t: (0, t, b)), wspec(m), wspec(w), wspec(v), wspec(lam),
                  pl.BlockSpec((1, LANES), lambda b, t: (0, b))],
        out_specs=[pl.BlockSpec((nb, tl, LANES), lambda b, t: (0, t, b)),
                   pl.BlockSpec((1, nb, 2 * STATE_LANES), lambda b, t: (b, 0, 0))],
        out_shape=[jax.ShapeDtypeStruct(u3.shape, F32),
                   jax.ShapeDtypeStruct((N_LANE_BLOCKS, nb, 2 * STATE_LANES), F32)],
        scratch_shapes=[pltpu.VMEM((nslab, nb * ncl, LANES), F32), pltpu.VMEM((nslab, nb * ncl, LANES), F32),
                        pltpu.VMEM((nslab, nb, LANES), F32)],
        compiler_params=_params(("arbitrary", "arbitrary")),
        name="ssm_prompt",
    )(u3, m, w, v, lam, d)


def _ssm_sample_kernel(u_ref, sre_ref, sim_ref, m_ref, w_ref, v_ref, lam_ref, d_ref,
                       y_ref, ore_ref, oim_ref, *, lc, ns):
    def piece(j):
        return u_ref[pl.ds(j, ns, stride=lc), :]

    a = jnp.concatenate([piece(j) for j in range(lc)], axis=1).astype(BF16)
    s_re, s_im = sre_ref[...], sim_ref[...]
    x = jnp.dot(a, w_ref[0], preferred_element_type=F32)
    e_re, e_im = _complex_step(s_re, s_im, lam_ref[0, :, :STATE_LANES], lam_ref[0, :, STATE_LANES:],
                               x[:, :STATE_LANES], x[:, STATE_LANES:])
    ore_ref[...] = e_re
    oim_ref[...] = e_im
    s0 = jnp.concatenate([s_re, s_im], axis=1).astype(BF16)
    y = (jnp.dot(a, m_ref[0], preferred_element_type=F32) + jnp.dot(s0, v_ref[0], preferred_element_type=F32))
    d = d_ref[...]
    for j in range(lc):
        y_ref[pl.ds(j, ns, stride=lc), :] = y[:, j * LANES:(j + 1) * LANES] + d * piece(j)


def _ssm_sample(u, s_re, s_im, m, w, v, lam, d, lc):
    t = u.shape[0]
    ns = t // lc
    kern = functools.partial(_ssm_sample_kernel, lc=lc, ns=ns)
    assert m.shape[1] == 2 * lc * LANES
    n = lc * LANES
    col = lambda rows, width: pl.BlockSpec((rows, width), lambda b: (0, b))
    return pl.pallas_call(
        kern,
        grid=(N_LANE_BLOCKS,),
        in_specs=[col(t, LANES), col(ns, STATE_LANES), col(ns, STATE_LANES),
                  pl.BlockSpec((1, n, n), lambda b: (b, 0, 0)),
                  pl.BlockSpec((1, n, 2 * STATE_LANES), lambda b: (b, 1, 0)),
                  pl.BlockSpec((1, 2 * STATE_LANES, n), lambda b: (b, 0, 0)),
                  pl.BlockSpec((1, 1, 2 * STATE_LANES), lambda b: (b, 0, 0)), col(1, LANES)],
        out_specs=[col(t, LANES), col(ns, STATE_LANES), col(ns, STATE_LANES)],
        out_shape=[jax.ShapeDtypeStruct(u.shape, F32), jax.ShapeDtypeStruct(s_re.shape, F32),
                   jax.ShapeDtypeStruct(s_im.shape, F32)],
        compiler_params=_params(("arbitrary",)),
        name="ssm_sample",
    )(u, s_re, s_im, m, w, v, lam, d)


def _ssm_discretise(a_re, a_im, log_step):
    dt = jnp.exp(log_step)
    mag = jnp.exp(a_re * dt)
    l_re, l_im = mag * jnp.cos(a_im * dt), mag * jnp.sin(a_im * dt)
    den = a_re * a_re + a_im * a_im
    n_re, n_im = l_re - 1.0, l_im
    return l_re, l_im, (n_re * a_re + n_im * a_im) / den, (n_im * a_re - n_re * a_im) / den


def _complex_powers(l_re, l_im, n):
    p_re, p_im = [jnp.ones_like(l_re)], [jnp.zeros_like(l_re)]
    for _ in range(n):
        p_re, p_im = p_re + [p_re[-1] * l_re - p_im[-1] * l_im], p_im + [p_re[-1] * l_im + p_im[-1] * l_re]
    return p_re, p_im


def _ssm_tables_kernel(ac_re_ref, ac_im_ref, lsc_ref, ar_re_ref, ar_im_ref, lsr_ref, b_re_ref, b_im_ref,
                       c_re_ref, c_im_ref, m_ref, w_ref, v_ref, lam_ref, lam_half_ref, *, lc):
    hi = lax.Precision.HIGHEST
    l_re, l_im, cf_re, cf_im = _ssm_discretise(ac_re_ref[0], ac_im_ref[0], lsc_ref[0])
    b_re, b_im, c_re, c_im = b_re_ref[0], b_im_ref[0], c_re_ref[0], c_im_ref[0]
    bb_re = cf_re * b_re - cf_im * b_im
    bb_im = cf_re * b_im + cf_im * b_re
    p_re, p_im = _complex_powers(l_re, l_im, lc)
    lag = []
    for k in range(lc):
        et_re = (p_re[k] * bb_re - p_im[k] * bb_im).T
        et_im = (p_re[k] * bb_im + p_im[k] * bb_re).T
        j = lc - 1 - k
        w_ref[0, j * LANES:(j + 1) * LANES, :STATE_LANES] = et_re.astype(BF16)
        w_ref[0, j * LANES:(j + 1) * LANES, STATE_LANES:] = et_im.astype(BF16)
        lag.append((jnp.dot(et_re, c_re, precision=hi, preferred_element_type=F32)
                    - jnp.dot(et_im, c_im, precision=hi, preferred_element_type=F32)).astype(BF16))
        v_ref[0, :STATE_LANES, k * LANES:(k + 1) * LANES] = (p_re[k + 1] * c_re - p_im[k + 1] * c_im).astype(BF16)
        v_ref[0, STATE_LANES:, k * LANES:(k + 1) * LANES] = (-(p_im[k + 1] * c_re + p_re[k + 1] * c_im)).astype(BF16)
    zero = jnp.zeros((LANES, LANES), BF16)
    for j in range(lc):
        for jj in range(lc):
            m_ref[0, j * LANES:(j + 1) * LANES, jj * LANES:(jj + 1) * LANES] = lag[jj - j] if jj >= j else zero
    r_re, r_im, _, _ = _ssm_discretise(ar_re_ref[0], ar_im_ref[0], lsr_ref[0])
    q_re, q_im = _complex_powers(r_re, r_im, lc)
    lam_ref[0] = jnp.concatenate([q_re[lc], q_im[lc]], axis=1)
    lam_half_ref[0] = jnp.concatenate([q_re[lc // 2], q_im[lc // 2]], axis=1)


def _ssm_tables(a_re, a_im, log_step, b_re, b_im, c_re, c_im, lc):
    g, p, h = b_re.shape
    nbk, r = N_LANE_BLOCKS, GROUPS_PER_LANE_BLOCK
    ls = jnp.broadcast_to(log_step[:, None], (g, p))
    cols = [x.reshape(nbk, STATE_LANES, 1) for x in (a_re, a_im, ls)]
    rows = [x.reshape(nbk, 1, STATE_LANES) for x in (a_re, a_im, ls)]
    eye = jnp.eye(r, dtype=F32)[None, :, None, :, None]

    def block_diag(x):
        return (x[:, :, :, None, :] * eye).reshape(nbk, STATE_LANES, LANES)

    mats = [block_diag(b_re.reshape(nbk, r, p, h)), block_diag(b_im.reshape(nbk, r, p, h)),
            block_diag(c_re.reshape(nbk, r, h, p).transpose(0, 1, 3, 2)),
            block_diag(c_im.reshape(nbk, r, h, p).transpose(0, 1, 3, 2))]
    spec = lambda shape: pl.BlockSpec((1,) + shape, lambda b: (b, 0, 0))
    n = lc * LANES
    return pl.pallas_call(
        functools.partial(_ssm_tables_kernel, lc=lc),
        grid=(nbk,),
        in_specs=[spec((STATE_LANES, 1))] * 3 + [spec((1, STATE_LANES))] * 3 + [spec((STATE_LANES, LANES))] * 4,
        out_specs=[spec((n, n)), spec((n, 2 * STATE_LANES)), spec((2 * STATE_LANES, n)),
                   spec((1, 2 * STATE_LANES)), spec((1, 2 * STATE_LANES))],
        out_shape=[jax.ShapeDtypeStruct((nbk, n, n), BF16), jax.ShapeDtypeStruct((nbk, n, 2 * STATE_LANES), BF16),
                   jax.ShapeDtypeStruct((nbk, 2 * STATE_LANES, n), BF16),
                   jax.ShapeDtypeStruct((nbk, 1, 2 * STATE_LANES), F32),
                   jax.ShapeDtypeStruct((nbk, 1, 2 * STATE_LANES), F32)],
        compiler_params=_params(("arbitrary",)),
        name="ssm_tables",
    )(*cols, *rows, *mats)


def _nt_dot(a, b):
    return lax.dot_general(a, b, (((1,), (1,)), ((), ())), preferred_element_type=F32)


def _swa_prompt_kernel(sink_ref, q_ref, kp_ref, kc_ref, vp_ref, vc_ref, mstd_ref, mfirst_ref, g_ref, o_ref, *, nblk):
    first_tile = pl.program_id(1) == 0
    low_half = lax.broadcasted_iota(jnp.int32, (1, LANES), 1) < HEAD_DIM

    def head_variants(prev_ref, cur_ref):
        x = jnp.concatenate([prev_ref[...], cur_ref[...]], axis=0)
        xr = pltpu.roll(x, HEAD_DIM, 1)
        zero = jnp.zeros_like(x)
        pick = lambda lo, hi: jnp.where(low_half, lo, hi).astype(BF16)
        return [[pick(x, zero), pick(zero, xr)], [pick(xr, zero), pick(zero, x)]]

    kvar = head_variants(kp_ref, kc_ref)
    vvar = head_variants(vp_ref, vc_ref)
    gain = g_ref[...]
    pair = GQA_GROUP // 2
    keys = lambda j: slice(j * WINDOW, (j + 2) * WINDOW)
    sinks = [[jnp.concatenate([jnp.full((WINDOW, 1), sink_ref[(kh * pair + a) * 2 + par], F32) for a in range(pair)],
                              axis=0) for par in range(2)] for kh in range(N_KV_HEADS)]
    chains = [(j, kh, par) for j in range(nblk) for kh in range(N_KV_HEADS) for par in range(2)]

    def scores(j, kh, par):
        qs = jnp.concatenate([q_ref[j * WINDOW:(j + 1) * WINDOW, (kh * pair + a) * LANES:(kh * pair + a + 1) * LANES]
                              for a in range(pair)], axis=0)
        vmask = mstd_ref[...]
        if j == 0:
            vmask = jnp.where(first_tile, mfirst_ref[...], vmask)
        return jnp.where(vmask > 0.0, _nt_dot(qs, kvar[kh][par][keys(j)]), NEG_INF)

    def softmax(kh, par, sc):
        m = jnp.maximum(jnp.max(sc, axis=-1, keepdims=True), sinks[kh][par])
        e = jnp.exp(sc - m)
        return e.astype(BF16), 1.0 / (jnp.sum(e, axis=-1, keepdims=True) + jnp.exp(sinks[kh][par] - m))

    sc = [scores(*c) for c in chains]
    pr = [softmax(kh, par, x) for (j, kh, par), x in zip(chains, sc)]
    out = [jnp.dot(e, vvar[kh][par][keys(j)], preferred_element_type=F32) * inv
           for (j, kh, par), (e, inv) in zip(chains, pr)]
    for j in range(nblk):
        blocks = []
        for kh in range(N_KV_HEADS):
            i = (j * N_KV_HEADS + kh) * 2
            acc = out[i] + out[i + 1]
            blocks += [acc[a * WINDOW:(a + 1) * WINDOW] for a in range(pair)]
        o_ref[j * WINDOW:(j + 1) * WINDOW, :] = _rms(jnp.concatenate(blocks, axis=1), gain).astype(BF16)


def _swa_masks():
    rows = (GQA_GROUP // 2) * WINDOW
    qi = jnp.arange(rows)[:, None] % WINDOW + WINDOW
    kj = jnp.arange(2 * WINDOW)[None, :]
    diff = qi - kj
    std = (diff >= 0) & (diff <= WINDOW)
    return std.astype(F32), (std & (kj >= WINDOW)).astype(F32)


def _swa_prompt(q, k, v, sinks, out_g, n, seq, tq):
    nt, nblk = seq // tq, tq // WINDOW
    mstd, mfirst = _swa_masks()
    cur = lambda w: pl.BlockSpec((tq, w), lambda b, i: (b * nt + i, 0))
    prev = lambda w: pl.BlockSpec((WINDOW, w), lambda b, i: (jnp.maximum((b * nt + i) * nblk - 1, b * nt * nblk), 0))
    return pl.pallas_call(
        functools.partial(_swa_prompt_kernel, nblk=nblk),
        grid=(n, nt),
        in_specs=[pl.BlockSpec(memory_space=pltpu.SMEM), cur(Q_WIDTH), prev(KV_WIDTH), cur(KV_WIDTH),
                  prev(KV_WIDTH), cur(KV_WIDTH), _const_spec(mstd.shape), _const_spec(mfirst.shape),
                  _const_spec((1, Q_WIDTH))],
        out_specs=cur(Q_WIDTH),
        out_shape=jax.ShapeDtypeStruct((n * seq, Q_WIDTH), BF16),
        compiler_params=_params(("arbitrary", "arbitrary")),
        name="swa_prompt",
    )(sinks, q, k, k, v, v, mstd, mfirst, out_g)


def _swa_sample_kernel(sink_ref, q_ref, kn_ref, vn_ref, ck_ref, cv_ref, g_ref, o_ref, nk_ref, nv_ref, *, ns, t):
    rows = GQA_GROUP * t
    tok = lax.broadcasted_iota(jnp.int32, (rows, WINDOW), 0) % t
    valid_c = lax.broadcasted_iota(jnp.int32, (rows, WINDOW), 1) >= tok
    tok_n = lax.broadcasted_iota(jnp.int32, (rows, t), 0) % t
    valid_n = lax.broadcasted_iota(jnp.int32, (rows, t), 1) <= tok_n
    gain = g_ref[...]

    sinks = [jnp.concatenate([jnp.full((t, 1), sink_ref[kh * GQA_GROUP + g], F32) for g in range(GQA_GROUP)], axis=0)
             for kh in range(N_KV_HEADS)]
    heads = [(s, kh) for s in range(ns) for kh in range(N_KV_HEADS)]
    hs = lambda kh: slice(kh * HEAD_DIM, (kh + 1) * HEAD_DIM)

    for s in range(ns):
        nk_ref[s, :WINDOW - t, :] = ck_ref[s, t:, :]
        nk_ref[s, WINDOW - t:, :] = kn_ref[s]
        nv_ref[s, :WINDOW - t, :] = cv_ref[s, t:, :]
        nv_ref[s, WINDOW - t:, :] = vn_ref[s]

    def scores(s, kh):
        q = q_ref[s].astype(F32)
        q4 = jnp.concatenate([q[:, (kh * GQA_GROUP + g) * HEAD_DIM:(kh * GQA_GROUP + g + 1) * HEAD_DIM]
                              for g in range(GQA_GROUP)], axis=0).astype(BF16)
        sc_c = jnp.where(valid_c, _nt_dot(q4, ck_ref[s, :, hs(kh)].astype(BF16)), NEG_INF)
        sc_n = jnp.where(valid_n, _nt_dot(q4, kn_ref[s, :, hs(kh)].astype(BF16)), NEG_INF)
        return sc_c, sc_n

    def softmax(kh, sc_c, sc_n):
        m = jnp.maximum(jnp.maximum(jnp.max(sc_c, axis=-1, keepdims=True),
                                    jnp.max(sc_n, axis=-1, keepdims=True)), sinks[kh])
        e_c, e_n = jnp.exp(sc_c - m), jnp.exp(sc_n - m)
        inv = 1.0 / (jnp.sum(e_c, axis=-1, keepdims=True) + jnp.sum(e_n, axis=-1, keepdims=True)
                     + jnp.exp(sinks[kh] - m))
        return (e_c * inv).astype(BF16), (e_n * inv).astype(BF16)

    def values(s, kh, p_c, p_n):
        return (jnp.dot(p_c, cv_ref[s, :, hs(kh)].astype(BF16), preferred_element_type=F32)
                + jnp.dot(p_n, vn_ref[s, :, hs(kh)].astype(BF16), preferred_element_type=F32))

    sc = [scores(s, kh) for s, kh in heads]
    pr = [softmax(kh, *x) for (s, kh), x in zip(heads, sc)]
    o4 = [values(s, kh, *x) for (s, kh), x in zip(heads, pr)]
    for s in range(ns):
        o = jnp.concatenate([o4[s * N_KV_HEADS + kh][g * t:(g + 1) * t]
                             for kh in range(N_KV_HEADS) for g in range(GQA_GROUP)], axis=1)
        o_ref[s] = _rms(o, gain).astype(BF16)


def _swa_sample(q3, kn3, vn3, ck, cv, sinks, out_g, sb):
    ns, t, _ = q3.shape
    blk = lambda a: pl.BlockSpec((sb,) + a.shape[1:], lambda i: (i, 0, 0))
    kern = functools.partial(_swa_sample_kernel, ns=sb, t=t)
    return pl.pallas_call(
        kern,
        grid=(ns // sb,),
        in_specs=[pl.BlockSpec(memory_space=pltpu.SMEM), blk(q3), blk(kn3), blk(vn3), blk(ck), blk(cv),
                  pl.BlockSpec((1, Q_WIDTH), lambda i: (0, 0))],
        out_specs=[blk(q3), blk(ck), blk(cv)],
        out_shape=[jax.ShapeDtypeStruct(q3.shape, BF16), jax.ShapeDtypeStruct(ck.shape, F32),
                   jax.ShapeDtypeStruct(cv.shape, F32)],
        compiler_params=_params(("arbitrary",)),
        name="swa_sample",
    )(sinks, q3, kn3, vn3, ck, cv, out_g)


def _gelu_tanh(x):
    return 0.5 * x * (1.0 + jnp.tanh(math.sqrt(2.0 / math.pi) * (x + 0.044715 * (x * x * x))))


def _merge_tile(h, y, at, wglu_ref, bglu, sg, wout_ref, post_g, xa_g, wq_ref):
    g = _gelu_tanh(y)
    lin = jnp.dot(g.astype(BF16), wglu_ref[...], preferred_element_type=F32) + bglu
    y_ssm = g * (1.0 / (1.0 + jnp.exp(-lin)))
    ssm_n = _rms(y_ssm, sg).astype(BF16)
    mixed = (jnp.dot(ssm_n, wout_ref[:SSM_WIDTH, :], preferred_element_type=F32)
             + jnp.dot(at, wout_ref[SSM_WIDTH:, :], preferred_element_type=F32))
    h2 = h + _rms(mixed, post_g)
    qm = jnp.dot(_rms(h2, xa_g).astype(BF16), wq_ref[...], preferred_element_type=F32)
    return h2, (qm * (MEM_HEAD_DIM ** -0.5)).astype(BF16)


def _merge_kernel(h_ref, y_ref, at_ref, wglu_ref, bglu_ref, sg_ref, wout_ref, post_g_ref, xa_g_ref, wq_ref,
                  h2_ref, qm_ref):
    h2_ref[...], qm_ref[...] = _merge_tile(h_ref[...], y_ref[...], at_ref[...], wglu_ref, bglu_ref[...], sg_ref[...],
                                           wout_ref, post_g_ref[...], xa_g_ref[...], wq_ref)


def _merge_specs(wglu, wout, wq):
    return [_const_spec(wglu.shape), _const_spec((1, SSM_WIDTH)), _const_spec((1, SSM_WIDTH)), _const_spec(wout.shape),
            _const_spec((1, D_MODEL)), _const_spec((1, D_MODEL)), _const_spec(wq.shape)]


def _merge(h, y, at, merge_w, tm):
    t = h.shape[0]
    row = lambda w: pl.BlockSpec((tm, w), lambda i: (i, 0))
    return pl.pallas_call(
        _merge_kernel,
        grid=(t // tm,),
        in_specs=[row(D_MODEL), row(SSM_WIDTH), row(Q_WIDTH)] + _merge_specs(merge_w[0], merge_w[3], merge_w[6]),
        out_specs=[row(D_MODEL), row(D_MODEL)],
        out_shape=[jax.ShapeDtypeStruct((t, D_MODEL), F32), jax.ShapeDtypeStruct((t, D_MODEL), BF16)],
        compiler_params=_params(("arbitrary",)),
        name="merge",
    )(h, y, at, *merge_w)


def _mem_attn_sample_kernel(q_ref, k_ref, v_ref, o_ref, *, gb, t):
    rows = t * MEM_HEADS
    halves = MEM_HEAD_DIM // LANES
    kv_rows = N_MEM * halves * MEM_HEADS
    period = halves * MEM_HEADS
    lane = lax.broadcasted_iota(jnp.int32, (rows, kv_rows), 1) % period
    head = lax.broadcasted_iota(jnp.int32, (rows, kv_rows), 0) % MEM_HEADS
    in_half = [lane == head + hf * MEM_HEADS for hf in range(halves)]
    for b in range(gb):
        kb = k_ref[b].astype(BF16)
        vb = v_ref[b].astype(BF16)
        part = _nt_dot(q_ref[b], kb)
        sc = jnp.where(in_half[0], part[:rows], 0.0)
        for hf in range(1, halves):
            sc = sc + pltpu.roll(jnp.where(in_half[hf], part[hf * rows:(hf + 1) * rows], 0.0),
                                 kv_rows - hf * MEM_HEADS, 1)
        sc = jnp.where(in_half[0], sc, -jnp.inf)
        e = jnp.exp(sc - jnp.max(sc, axis=-1, keepdims=True))
        inv = 1.0 / jnp.sum(e, axis=-1, keepdims=True)
        e_all = jnp.concatenate([e] + [pltpu.roll(e, hf * MEM_HEADS, 1) for hf in range(1, halves)], axis=0)
        o = jnp.dot(e_all.astype(BF16), vb, preferred_element_type=F32)
        o_ref[b] = (o * jnp.concatenate([inv] * halves, axis=0)).astype(BF16)


def _mem_attn_sample(qm, cache_k, cache_v, n_s, t_s, gb):
    halves = MEM_HEAD_DIM // LANES
    rows = halves * t_s * MEM_HEADS
    kv_rows = N_MEM * halves * MEM_HEADS

    def stored_rows(c):
        c = c.reshape(n_s, N_MEM, MEM_HEADS, halves, LANES).transpose(0, 1, 3, 2, 4)
        return c.reshape(n_s, kv_rows, LANES)

    q = qm.reshape(n_s, t_s, MEM_HEADS, halves, LANES).transpose(0, 3, 1, 2, 4).reshape(n_s, rows, LANES)
    blk = lambda r: pl.BlockSpec((gb, r, LANES), lambda i: (i, 0, 0))
    o = pl.pallas_call(
        functools.partial(_mem_attn_sample_kernel, gb=gb, t=t_s),
        grid=(n_s // gb,),
        in_specs=[blk(rows), blk(kv_rows), blk(kv_rows)],
        out_specs=blk(rows),
        out_shape=jax.ShapeDtypeStruct((n_s, rows, LANES), BF16),
        compiler_params=_params(("arbitrary",)),
        name="mem_attn_sample",
    )(q, stored_rows(cache_k), stored_rows(cache_v))
    o = o.reshape(n_s, halves, t_s, MEM_HEADS, LANES).transpose(0, 2, 3, 1, 4)
    return o.reshape(n_s * t_s, D_MODEL)


def _mem_heads(q, k, v):
    outs = []
    for hh in range(MEM_HEADS):
        hs = slice(hh * MEM_HEAD_DIM, (hh + 1) * MEM_HEAD_DIM)
        sc = _nt_dot(q[:, hs], k[:, hs])
        e = jnp.exp(sc - jnp.max(sc, axis=-1, keepdims=True))
        inv = 1.0 / jnp.sum(e, axis=-1, keepdims=True)
        outs.append((jnp.dot(e.astype(BF16), v[:, hs], preferred_element_type=F32) * inv).astype(BF16))
    return jnp.concatenate(outs, axis=1)


def _ffn_out_tile(h2, om, wo_ref, xa_post, pre_g, wg_ref, wu_ref, wd_ref, post_g, act_ref):
    c = jnp.dot(om, wo_ref[...], preferred_element_type=F32)
    h3 = h2 + _rms(c, xa_post)
    return _ffn_tile(h3, pre_g, wg_ref, wu_ref, wd_ref, post_g, act_ref)


def _ffn_out_specs(wo, ffn_w):
    return [_const_spec(wo.shape), _const_spec((1, D_MODEL)), _const_spec((1, D_MODEL)),
            *[_const_spec(w.shape) for w in ffn_w], _const_spec((1, D_MODEL))]


def _ffn_out_kernel(h_ref, o_ref, wo_ref, xa_post_ref, pre_g_ref, wg_ref, wu_ref, wd_ref, post_g_ref, out_ref, act_ref):
    out_ref[...] = _ffn_out_tile(h_ref[...], o_ref[...], wo_ref, xa_post_ref[...], pre_g_ref[...], wg_ref, wu_ref,
                                 wd_ref, post_g_ref[...], act_ref)


def _ffn_out(h, o, out_w, tm):
    t = h.shape[0]
    row = pl.BlockSpec((tm, D_MODEL), lambda i: (i, 0))
    return pl.pallas_call(
        _ffn_out_kernel,
        grid=(t // tm,),
        in_specs=[row, row] + _ffn_out_specs(out_w[0], out_w[3:6]),
        out_specs=row,
        out_shape=jax.ShapeDtypeStruct((t, D_MODEL), F32),
        scratch_shapes=[pltpu.VMEM((tm, D_FF), BF16)],
        compiler_params=_params(("arbitrary",)),
        name="ffn_out",
    )(h, o, *out_w)


def _post_kernel(h_ref, y_ref, at_ref, k_ref, v_ref,
                 wglu_ref, bglu_ref, sg_ref, wout_ref, post_g_ref, xa_g_ref, wq_ref,
                 wo_ref, xa_post_ref, pre_g_ref, wg_ref, wu_ref, wd_ref, ffn_post_ref, out_ref, act_ref):
    h2, qm = _merge_tile(h_ref[...], y_ref[...], at_ref[...], wglu_ref, bglu_ref[...], sg_ref[...],
                         wout_ref, post_g_ref[...], xa_g_ref[...], wq_ref)
    om = _mem_heads(qm, k_ref[0], v_ref[0])
    out_ref[...] = _ffn_out_tile(h2, om, wo_ref, xa_post_ref[...], pre_g_ref[...], wg_ref, wu_ref, wd_ref,
                                 ffn_post_ref[...], act_ref)


def _post(h, y, at, k3, v3, merge_w, out_w, tm):
    t = h.shape[0]
    tiles_per_batch = t // k3.shape[0] // tm
    row = lambda w: pl.BlockSpec((tm, w), lambda i: (i, 0))
    kv = pl.BlockSpec((1, N_MEM, D_MODEL), lambda i: (i // tiles_per_batch, 0, 0))
    return pl.pallas_call(
        _post_kernel,
        grid=(t // tm,),
        in_specs=([row(D_MODEL), row(SSM_WIDTH), row(Q_WIDTH), kv, kv]
                  + _merge_specs(merge_w[0], merge_w[3], merge_w[6]) + _ffn_out_specs(out_w[0], out_w[3:6])),
        out_specs=row(D_MODEL),
        out_shape=jax.ShapeDtypeStruct((t, D_MODEL), F32),
        scratch_shapes=[pltpu.VMEM((tm, D_FF), BF16)],
        compiler_params=_params(("arbitrary",)),
        name="post",
    )(h, y, at, k3, v3, *merge_w, *out_w)


def _mem_kv_kernel(m_ref, g_ref, wkv_ref, k_ref, v_ref, kb_ref, vb_ref):
    kv = jnp.dot(_rms(m_ref[...], g_ref[...]).astype(BF16), wkv_ref[...], preferred_element_type=F32)
    k_ref[...] = kv[:, :D_MODEL]
    v_ref[...] = kv[:, D_MODEL:]
    kb_ref[...] = kv[:, :D_MODEL].astype(BF16)
    vb_ref[...] = kv[:, D_MODEL:].astype(BF16)


def _mem_kv(mem, g, wkv, tm):
    t = mem.shape[0]
    row = pl.BlockSpec((tm, D_MODEL), lambda i: (i, 0))
    return pl.pallas_call(
        _mem_kv_kernel,
        grid=(t // tm,),
        in_specs=[row, _const_spec((1, D_MODEL)), _const_spec(wkv.shape)],
        out_specs=[row] * 4,
        out_shape=[jax.ShapeDtypeStruct((t, D_MODEL), F32)] * 2 + [jax.ShapeDtypeStruct((t, D_MODEL), BF16)] * 2,
        compiler_params=_params(("arbitrary",)),
        name="mem_kv",
    )(mem, g, wkv)


def _rope_tables(pos):
    half = ROPE_DIM // 2
    inv = ROPE_THETA ** (-jnp.arange(half, dtype=F32) * (2.0 / ROPE_DIM))
    ang = pos.astype(F32)[:, None] * inv[None, :]
    cos, sin = jnp.cos(ang), jnp.sin(ang)
    n = pos.shape[0]
    pad = jnp.zeros((n, HEAD_DIM - ROPE_DIM), F32)
    zero = jnp.zeros((n, half), F32)
    cos_h = jnp.concatenate([cos, cos, pad + 1.0], axis=1)
    lo_h = jnp.concatenate([-sin, zero, pad], axis=1)
    hi_h = jnp.concatenate([zero, sin, pad], axis=1)
    rep = LANES // HEAD_DIM
    return tuple(jnp.tile(a, (1, rep)) for a in (cos_h, lo_h, hi_h))


def _ffn_weights(w_gate, w_up, w_down):
    return w_gate.astype(BF16), w_up.astype(BF16), w_down.astype(BF16)


def _lane_block_states(st, n):
    st = st.reshape(N_LANE_BLOCKS, n, 2, GROUPS_PER_LANE_BLOCK, SSM_STATE).transpose(2, 1, 0, 3, 4)
    st = st.reshape(2, n, N_SSM_GROUPS, SSM_STATE)
    return st[0], st[1]


def kernel(x_prompt, x_sample, state_ssm_re, state_ssm_im, cache_swa_k, cache_swa_v, cache_mem_k, cache_mem_v, mem_prompt, ffn1_pre_g, ffn1_w_gate, ffn1_w_up, ffn1_w_down, ffn1_post_g, mix_pre_g, w_in, ssm_a_re, ssm_a_im, ssm_log_step, ssm_b_re, ssm_b_im, ssm_c_re, ssm_c_im, ssm_d, ssm_w_glu, ssm_b_glu, attn_sinks, ssm_out_g, attn_out_g, w_out, mix_post_g, mem_norm_g, w_mem_q, w_mem_k, w_mem_v, w_mem_o, xa_pre_g, xa_post_g, ffn2_pre_g, ffn2_w_gate, ffn2_w_up, ffn2_w_down, ffn2_post_g):
    n_p, s_p, _ = x_prompt.shape
    n_s, t_s, _ = x_sample.shape
    tm = 512
    row = lambda a: a.reshape(1, -1).astype(F32)

    ffn1_w = _ffn_weights(ffn1_w_gate, ffn1_w_up, ffn1_w_down)
    win = w_in.astype(BF16)
    merge_w = (ssm_w_glu.astype(BF16), row(ssm_b_glu), row(ssm_out_g), w_out.astype(BF16), row(mix_post_g),
               row(xa_pre_g), w_mem_q.astype(BF16))
    out_w = (w_mem_o.astype(BF16), row(xa_post_g), row(ffn2_pre_g),
             *_ffn_weights(ffn2_w_gate, ffn2_w_up, ffn2_w_down), row(ffn2_post_g))
    wkv = jnp.concatenate([w_mem_k, w_mem_v], axis=1).astype(BF16)
    d_row = row(ssm_d)
    ssm_args = (ssm_a_re.astype(F32), ssm_a_im.astype(F32), ssm_log_step.astype(F32), ssm_b_re.astype(F32),
                ssm_b_im.astype(F32), ssm_c_re.astype(F32), ssm_c_im.astype(F32))
    sinks = attn_sinks.astype(F32)

    pm_k, pm_v, pm_kb, pm_vb = _mem_kv(mem_prompt.reshape(n_p * N_MEM, D_MODEL), row(mem_norm_g), wkv, N_MEM)

    def tokenwise_in(x2, pos_tab):
        return _ffn_in(x2, row(ffn1_pre_g), ffn1_w, row(ffn1_post_g), row(mix_pre_g), win, *pos_tab, tm)

    lc_p = 2 * t_s
    ssm_m, ssm_w, ssm_v, lam_p, lam_s = _ssm_tables(*ssm_args, lc_p)
    h1, u, q, k, v = tokenwise_in(x_prompt.reshape(n_p * s_p, D_MODEL), _rope_tables(jnp.arange(s_p, dtype=jnp.int32)))
    y3, st_p = _ssm_prompt(u.reshape(n_p, s_p, SSM_WIDTH), ssm_m, ssm_w, ssm_v, lam_p, d_row, lc_p, 1024)
    at = _swa_prompt(q, k, v, sinks, row(attn_out_g), n_p, s_p, 512)
    y_prompt = _post(h1, y3.reshape(n_p * s_p, SSM_WIDTH), at, pm_kb.reshape(n_p, N_MEM, D_MODEL),
                     pm_vb.reshape(n_p, N_MEM, D_MODEL), merge_w, out_w, tm).reshape(n_p, s_p, D_MODEL)
    p_sre, p_sim = _lane_block_states(st_p, n_p)
    p_wk = k.reshape(n_p, s_p, N_KV_HEADS, HEAD_DIM)[:, -WINDOW:]
    p_wv = v.reshape(n_p, s_p, N_KV_HEADS, HEAD_DIM)[:, -WINDOW:]

    pos_s = jnp.tile(PAST_LEN + jnp.arange(t_s, dtype=jnp.int32), n_s)
    h1s, us, qs, ks, vs = tokenwise_in(x_sample.reshape(n_s * t_s, D_MODEL), _rope_tables(pos_s))
    ys, s_sre, s_sim = _ssm_sample(us, state_ssm_re.reshape(n_s, -1).astype(F32), state_ssm_im.reshape(n_s, -1).astype(F32),
                                   ssm_m, ssm_w, ssm_v, lam_s, d_row, t_s)
    win_len = cache_swa_k.shape[1]
    ats, s_wk, s_wv = _swa_sample(qs.reshape(n_s, t_s, Q_WIDTH), ks.reshape(n_s, t_s, KV_WIDTH), vs.reshape(n_s, t_s, KV_WIDTH),
                                  cache_swa_k.reshape(n_s, win_len, KV_WIDTH), cache_swa_v.reshape(n_s, win_len, KV_WIDTH),
                                  sinks, row(attn_out_g), 16)
    h2s, qms = _merge(h1s, ys, ats.reshape(n_s * t_s, Q_WIDTH), merge_w, tm)
    oms = _mem_attn_sample(qms, cache_mem_k, cache_mem_v, n_s, t_s, 4)
    y_sample = _ffn_out(h2s, oms, out_w, tm).reshape(n_s, t_s, D_MODEL)

    return (y_prompt, y_sample, p_sre, p_sim, p_wk, p_wv,
            pm_k.reshape(n_p, N_MEM, MEM_HEADS, MEM_HEAD_DIM), pm_v.reshape(n_p, N_MEM, MEM_HEADS, MEM_HEAD_DIM),
            s_sre.reshape(n_s, N_SSM_GROUPS, SSM_STATE), s_sim.reshape(n_s, N_SSM_GROUPS, SSM_STATE),
            s_wk.reshape(n_s, win_len, N_KV_HEADS, HEAD_DIM), s_wv.reshape(n_s, win_len, N_KV_HEADS, HEAD_DIM))
```

```python
import functools
import math

import jax
import jax.numpy as jnp
from jax import lax
from jax.experimental import pallas as pl
from jax.experimental.pallas import tpu as pltpu

F32 = jnp.float32
BF16 = jnp.bfloat16

D_MODEL = 1024
PAST_LEN = 16384
SSM_WIDTH = 512
SSM_GROUP = 16
N_SSM_GROUPS = 32
SSM_STATE = 64
HEAD_DIM = 64
N_HEADS = 8
N_KV_HEADS = 2
GQA_GROUP = 4
Q_WIDTH = 512
KV_WIDTH = 128
WINDOW = 128
ROPE_THETA = 500000.0
ROPE_DIM = 16
N_MEM = 256
MEM_HEADS = 4
MEM_HEAD_DIM = 256
D_FF = 2816
RMS_EPS = 1e-6
IN_WIDTH = SSM_WIDTH + Q_WIDTH + 2 * KV_WIDTH
NEG_INF = -1e30

LANES = 128
FF_CHUNK = 256
N_FF_CHUNKS = D_FF // FF_CHUNK
GROUPS_PER_LANE_BLOCK = LANES // SSM_GROUP
N_LANE_BLOCKS = SSM_WIDTH // LANES
STATE_LANES = GROUPS_PER_LANE_BLOCK * SSM_STATE
VMEM_LIMIT = 56 * 1024 * 1024


def _rms(x, g):
    return x * lax.rsqrt(jnp.mean(x * x, axis=-1, keepdims=True) + RMS_EPS) * g


def _const_spec(shape):
    nd = len(shape)
    return pl.BlockSpec(shape, lambda *_: (0,) * nd, pipeline_mode=pl.Buffered(1))


def _params(sem):
    return pltpu.CompilerParams(dimension_semantics=sem, vmem_limit_bytes=VMEM_LIMIT)


def _ffn_tile(x, pre_g, wg_ref, wu_ref, wd_ref, post_g, act_ref):
    xn = _rms(x, pre_g).astype(BF16)
    for c in range(N_FF_CHUNKS):
        cols = slice(c * FF_CHUNK, (c + 1) * FF_CHUNK)
        gate = jnp.dot(xn, wg_ref[:, cols], preferred_element_type=F32)
        up = jnp.dot(xn, wu_ref[:, cols], preferred_element_type=F32)
        act = gate * (1.0 / (1.0 + jnp.exp(-gate))) * up
        act_ref[:, cols] = act.astype(BF16)
    down = jnp.dot(act_ref[...], wd_ref[...], preferred_element_type=F32)
    return x + 0.5 * _rms(down, post_g)


def _rope(x, cos, sin_lo, sin_hi):
    w = x.shape[1]
    half = ROPE_DIM // 2
    return (x * cos + pltpu.roll(x, w - half, 1) * sin_lo + pltpu.roll(x, half, 1) * sin_hi)


def _ffn_in_kernel(x_ref, pre_g_ref, wg_ref, wu_ref, wd_ref, post_g_ref, mix_g_ref, win_ref,
                   cos_ref, slo_ref, shi_ref,
                   h_ref, u_ref, q_ref, k_ref, v_ref, act_ref):
    h = _ffn_tile(x_ref[...], pre_g_ref[...], wg_ref, wu_ref, wd_ref, post_g_ref[...], act_ref)
    h_ref[...] = h
    z = jnp.dot(_rms(h, mix_g_ref[...]).astype(BF16), win_ref[...], preferred_element_type=F32)
    u_ref[...] = z[:, :SSM_WIDTH]
    o1 = SSM_WIDTH + Q_WIDTH
    cos, slo, shi = cos_ref[...], slo_ref[...], shi_ref[...]
    rep = Q_WIDTH // LANES
    q = _rope(z[:, SSM_WIDTH:o1], jnp.tile(cos, (1, rep)), jnp.tile(slo, (1, rep)), jnp.tile(shi, (1, rep)))
    q_ref[...] = (q * (HEAD_DIM ** -0.5)).astype(BF16)
    k_ref[...] = _rope(z[:, o1:o1 + KV_WIDTH], cos, slo, shi)
    v_ref[...] = z[:, o1 + KV_WIDTH:]


def _ffn_in_t_kernel(x_ref, pre_g_ref, wg_ref, wu_ref, wd_ref, post_g_ref, mix_g_ref, wuk_ref, wqv_t_ref,
                     cos_ref, slo_ref, shi_ref, cos_t_ref, sin_t_ref,
                     h_ref, u_ref, k_ref, qt_ref, vt_ref, act_ref):
    h = _ffn_tile(x_ref[...], pre_g_ref[...], wg_ref, wu_ref, wd_ref, post_g_ref[...], act_ref)
    h_ref[...] = h
    hn = _rms(h, mix_g_ref[...]).astype(BF16)
    z = jnp.dot(hn, wuk_ref[...], preferred_element_type=F32)
    u_ref[...] = z[:, :SSM_WIDTH]
    k_ref[...] = _rope(z[:, SSM_WIDTH:], cos_ref[...], slo_ref[...], shi_ref[...])
    zt = _nt_dot(wqv_t_ref[...], hn)
    vt_ref[...] = zt[Q_WIDTH:]
    tm = zt.shape[1]
    half = ROPE_DIM // 2
    q3 = zt[:Q_WIDTH].reshape(N_HEADS, HEAD_DIM, tm)
    x1, x2 = q3[:, :half], q3[:, half:ROPE_DIM]
    cos, sin = cos_t_ref[...][None], sin_t_ref[...][None]
    q3 = jnp.concatenate([x1 * cos - x2 * sin, x2 * cos + x1 * sin, q3[:, ROPE_DIM:]], axis=1)
    qt_ref[...] = (q3.reshape(Q_WIDTH, tm) * (HEAD_DIM ** -0.5)).astype(BF16)


def _ffn_in_t(x, pre_g, ffn_w, post_g, mix_g, wuk, wqv_t, cos, slo, shi, cos_t, sin_t, tm):
    t = x.shape[0]
    n_pos_tiles = cos.shape[0] // tm
    half = ROPE_DIM // 2
    row = lambda w: pl.BlockSpec((tm, w), lambda i: (i, 0))
    col = lambda r: pl.BlockSpec((r, tm), lambda i: (0, i))
    tab = pl.BlockSpec((tm, LANES), lambda i: (i % n_pos_tiles, 0))
    tab_t = pl.BlockSpec((half, tm), lambda i: (0, i % n_pos_tiles))
    return pl.pallas_call(
        _ffn_in_t_kernel,
        grid=(t // tm,),
        in_specs=[row(D_MODEL), _const_spec((1, D_MODEL)), *[_const_spec(w.shape) for w in ffn_w],
                  _const_spec((1, D_MODEL)), _const_spec((1, D_MODEL)), _const_spec(wuk.shape),
                  _const_spec(wqv_t.shape), tab, tab, tab, tab_t, tab_t],
        out_specs=[row(D_MODEL), row(SSM_WIDTH), row(KV_WIDTH), col(Q_WIDTH), col(KV_WIDTH)],
        out_shape=[jax.ShapeDtypeStruct((t, D_MODEL), F32), jax.ShapeDtypeStruct((t, SSM_WIDTH), F32),
                   jax.ShapeDtypeStruct((t, KV_WIDTH), F32), jax.ShapeDtypeStruct((Q_WIDTH, t), BF16),
                   jax.ShapeDtypeStruct((KV_WIDTH, t), F32)],
        scratch_shapes=[pltpu.VMEM((tm, D_FF), BF16)],
        compiler_params=_params(("arbitrary",)),
        name="ffn_in_t",
    )(x, pre_g, *ffn_w, post_g, mix_g, wuk, wqv_t, cos, slo, shi, cos_t, sin_t)


def _ffn_in(x, pre_g, ffn_w, post_g, mix_g, win, cos, slo, shi, tm):
    t = x.shape[0]
    n_pos_tiles = cos.shape[0] // tm
    row = lambda w: pl.BlockSpec((tm, w), lambda i: (i, 0))
    tab = pl.BlockSpec((tm, LANES), lambda i: (i % n_pos_tiles, 0))
    return pl.pallas_call(
        _ffn_in_kernel,
        grid=(t // tm,),
        in_specs=[row(D_MODEL), _const_spec((1, D_MODEL)), *[_const_spec(w.shape) for w in ffn_w],
                  _const_spec((1, D_MODEL)), _const_spec((1, D_MODEL)), _const_spec(win.shape),
                  tab, tab, tab],
        out_specs=[row(D_MODEL), row(SSM_WIDTH), row(Q_WIDTH), row(KV_WIDTH), row(KV_WIDTH)],
        out_shape=[jax.ShapeDtypeStruct((t, D_MODEL), F32), jax.ShapeDtypeStruct((t, SSM_WIDTH), F32),
                   jax.ShapeDtypeStruct((t, Q_WIDTH), BF16), jax.ShapeDtypeStruct((t, KV_WIDTH), F32),
                   jax.ShapeDtypeStruct((t, KV_WIDTH), F32)],
        scratch_shapes=[pltpu.VMEM((tm, D_FF), BF16)],
        compiler_params=_params(("arbitrary",)),
        name="ffn_in",
    )(x, pre_g, *ffn_w, post_g, mix_g, win, cos, slo, shi)


def _complex_step(s_re, s_im, l_re, l_im, x_re, x_im):
    return l_re * s_re - l_im * s_im + x_re, l_re * s_im + l_im * s_re + x_im


def _ssm_prompt_kernel(u_ref, m_ref, w_ref, v_ref, lam_ref, d_ref, y_ref, st_ref,
                       x_scr, ss_scr, s_scr, *, lc, nb, ncl):
    tt = pl.program_id(1)

    @pl.when(tt == 0)
    def _():
        s_scr[...] = jnp.zeros_like(s_scr)

    def piece(n, j):
        return u_ref[n, pl.ds(j, ncl, stride=lc), :]

    a = jnp.concatenate(
        [jnp.concatenate([piece(n, j) for j in range(lc)], axis=1) for n in range(nb)], axis=0).astype(BF16)
    x = jnp.dot(a, w_ref[0], preferred_element_type=F32)
    nq = STATE_LANES // LANES
    for qq in range(2 * nq):
        x_scr[qq] = x[:, qq * LANES:(qq + 1) * LANES]
    l_re = [jnp.broadcast_to(lam_ref[0, :, qq * LANES:(qq + 1) * LANES], (nb, LANES)) for qq in range(nq)]
    l_im = [jnp.broadcast_to(lam_ref[0, :, (nq + qq) * LANES:(nq + qq + 1) * LANES], (nb, LANES)) for qq in range(nq)]

    def body(c, carry):
        rows = pl.ds(c, nb, stride=ncl)
        new = []
        for qq in range(nq):
            s_re, s_im = carry[qq], carry[nq + qq]
            ss_scr[qq, rows, :] = s_re
            ss_scr[nq + qq, rows, :] = s_im
            new.append(_complex_step(s_re, s_im, l_re[qq], l_im[qq], x_scr[qq, rows, :], x_scr[nq + qq, rows, :]))
        return tuple(r for r, _ in new) + tuple(i for _, i in new)

    s_fin = lax.fori_loop(0, ncl, body, tuple(s_scr[qq] for qq in range(2 * nq)), unroll=4)
    for qq in range(2 * nq):
        s_scr[qq] = s_fin[qq]
    st_ref[0] = jnp.concatenate(s_fin, axis=1)
    s_start = jnp.concatenate([ss_scr[qq] for qq in range(2 * nq)], axis=1).astype(BF16)
    y = (jnp.dot(a, m_ref[0], preferred_element_type=F32)
         + jnp.dot(s_start, v_ref[0], preferred_element_type=F32))
    d = d_ref[...]
    for n in range(nb):
        for j in range(lc):
            y_ref[n, pl.ds(j, ncl, stride=lc), :] = (
                y[n * ncl:(n + 1) * ncl, j * LANES:(j + 1) * LANES] + d * piece(n, j))


def _ssm_prompt(u3, m, w, v, lam, d, lc, tl):
    nb, seq, _ = u3.shape
    ncl = tl // lc
    nslab = 2 * STATE_LANES // LANES
    kern = functools.partial(_ssm_prompt_kernel, lc=lc, nb=nb, ncl=ncl)
    wspec = lambda a: pl.BlockSpec((1,) + a.shape[1:], lambda b, t: (b, 0, 0))
    return pl.pallas_call(
        kern,
        grid=(N_LANE_BLOCKS, seq // tl),
        in_specs=[pl.BlockSpec((nb, tl, LANES), lambda b, t: (0, t, b)), wspec(m), wspec(w), wspec(v), wspec(lam),
                  pl.BlockSpec((1, LANES), lambda b, t: (0, b))],
        out_specs=[pl.BlockSpec((nb, tl, LANES), lambda b, t: (0, t, b)),
                   pl.BlockSpec((1, nb, 2 * STATE_LANES), lambda b, t: (b, 0, 0))],
        out_shape=[jax.ShapeDtypeStruct(u3.shape, F32),
                   jax.ShapeDtypeStruct((N_LANE_BLOCKS, nb, 2 * STATE_LANES), F32)],
        scratch_shapes=[pltpu.VMEM((nslab, nb * ncl, LANES), F32), pltpu.VMEM((nslab, nb * ncl, LANES), F32),
                        pltpu.VMEM((nslab, nb, LANES), F32)],
        compiler_params=_params(("arbitrary", "arbitrary")),
        name="ssm_prompt",
    )(u3, m, w, v, lam, d)


def _ssm_sample_kernel(u_ref, sre_ref, sim_ref, m_ref, w_ref, v_ref, lam_ref, d_ref,
                       y_ref, ore_ref, oim_ref, *, lc, ns):
    def piece(j):
        return u_ref[pl.ds(j, ns, stride=lc), :]

    a = jnp.concatenate([piece(j) for j in range(lc)], axis=1).astype(BF16)
    s_re, s_im = sre_ref[...], sim_ref[...]
    x = jnp.dot(a, w_ref[0], preferred_element_type=F32)
    e_re, e_im = _complex_step(s_re, s_im, lam_ref[0, :, :STATE_LANES], lam_ref[0, :, STATE_LANES:],
                               x[:, :STATE_LANES], x[:, STATE_LANES:])
    ore_ref[...] = e_re
    oim_ref[...] = e_im
    s0 = jnp.concatenate([s_re, s_im], axis=1).astype(BF16)
    y = (jnp.dot(a, m_ref[0], preferred_element_type=F32) + jnp.dot(s0, v_ref[0], preferred_element_type=F32))
    d = d_ref[...]
    for j in range(lc):
        y_ref[pl.ds(j, ns, stride=lc), :] = y[:, j * LANES:(j + 1) * LANES] + d * piece(j)


def _ssm_sample(u, s_re, s_im, m, w, v, lam, d, lc):
    t = u.shape[0]
    ns = t // lc
    kern = functools.partial(_ssm_sample_kernel, lc=lc, ns=ns)
    assert m.shape[1] == 2 * lc * LANES
    n = lc * LANES
    col = lambda rows, width: pl.BlockSpec((rows, width), lambda b: (0, b))
    return pl.pallas_call(
        kern,
        grid=(N_LANE_BLOCKS,),
        in_specs=[col(t, LANES), col(ns, STATE_LANES), col(ns, STATE_LANES),
                  pl.BlockSpec((1, n, n), lambda b: (b, 0, 0)),
                  pl.BlockSpec((1, n, 2 * STATE_LANES), lambda b: (b, 1, 0)),
                  pl.BlockSpec((1, 2 * STATE_LANES, n), lambda b: (b, 0, 0)),
                  pl.BlockSpec((1, 1, 2 * STATE_LANES), lambda b: (b, 0, 0)), col(1, LANES)],
        out_specs=[col(t, LANES), col(ns, STATE_LANES), col(ns, STATE_LANES)],
        out_shape=[jax.ShapeDtypeStruct(u.shape, F32), jax.ShapeDtypeStruct(s_re.shape, F32),
                   jax.ShapeDtypeStruct(s_im.shape, F32)],
        compiler_params=_params(("arbitrary",)),
        name="ssm_sample",
    )(u, s_re, s_im, m, w, v, lam, d)


def _ssm_discretise(a_re, a_im, log_step):
    dt = jnp.exp(log_step)
    mag = jnp.exp(a_re * dt)
    l_re, l_im = mag * jnp.cos(a_im * dt), mag * jnp.sin(a_im * dt)
    den = a_re * a_re + a_im * a_im
    n_re, n_im = l_re - 1.0, l_im
    return l_re, l_im, (n_re * a_re + n_im * a_im) / den, (n_im * a_re - n_re * a_im) / den


def _complex_powers(l_re, l_im, n):
    p_re, p_im = [jnp.ones_like(l_re)], [jnp.zeros_like(l_re)]
    for _ in range(n):
        p_re, p_im = p_re + [p_re[-1] * l_re - p_im[-1] * l_im], p_im + [p_re[-1] * l_im + p_im[-1] * l_re]
    return p_re, p_im


def _ssm_tables_kernel(ac_re_ref, ac_im_ref, lsc_ref, ar_re_ref, ar_im_ref, lsr_ref, b_re_ref, b_im_ref,
                       c_re_ref, c_im_ref, m_ref, w_ref, v_ref, lam_ref, lam_half_ref, *, lc):
    hi = lax.Precision.HIGHEST
    l_re, l_im, cf_re, cf_im = _ssm_discretise(ac_re_ref[0], ac_im_ref[0], lsc_ref[0])
    b_re, b_im, c_re, c_im = b_re_ref[0], b_im_ref[0], c_re_ref[0], c_im_ref[0]
    bb_re = cf_re * b_re - cf_im * b_im
    bb_im = cf_re * b_im + cf_im * b_re
    p_re, p_im = _complex_powers(l_re, l_im, lc)
    lag = []
    for k in range(lc):
        et_re = (p_re[k] * bb_re - p_im[k] * bb_im).T
        et_im = (p_re[k] * bb_im + p_im[k] * bb_re).T
        j = lc - 1 - k
        w_ref[0, j * LANES:(j + 1) * LANES, :STATE_LANES] = et_re.astype(BF16)
        w_ref[0, j * LANES:(j + 1) * LANES, STATE_LANES:] = et_im.astype(BF16)
        lag.append((jnp.dot(et_re, c_re, precision=hi, preferred_element_type=F32)
                    - jnp.dot(et_im, c_im, precision=hi, preferred_element_type=F32)).astype(BF16))
        v_ref[0, :STATE_LANES, k * LANES:(k + 1) * LANES] = (p_re[k + 1] * c_re - p_im[k + 1] * c_im).astype(BF16)
        v_ref[0, STATE_LANES:, k * LANES:(k + 1) * LANES] = (-(p_im[k + 1] * c_re + p_re[k + 1] * c_im)).astype(BF16)
    zero = jnp.zeros((LANES, LANES), BF16)
    for j in range(lc):
        for jj in range(lc):
            m_ref[0, j * LANES:(j + 1) * LANES, jj * LANES:(jj + 1) * LANES] = lag[jj - j] if jj >= j else zero
    r_re, r_im, _, _ = _ssm_discretise(ar_re_ref[0], ar_im_ref[0], lsr_ref[0])
    q_re, q_im = _complex_powers(r_re, r_im, lc)
    lam_ref[0] = jnp.concatenate([q_re[lc], q_im[lc]], axis=1)
    lam_half_ref[0] = jnp.concatenate([q_re[lc // 2], q_im[lc // 2]], axis=1)


def _ssm_tables(a_re, a_im, log_step, b_re, b_im, c_re, c_im, lc):
    g, p, h = b_re.shape
    nbk, r = N_LANE_BLOCKS, GROUPS_PER_LANE_BLOCK
    ls = jnp.broadcast_to(log_step[:, None], (g, p))
    cols = [x.reshape(nbk, STATE_LANES, 1) for x in (a_re, a_im, ls)]
    rows = [x.reshape(nbk, 1, STATE_LANES) for x in (a_re, a_im, ls)]
    eye = jnp.eye(r, dtype=F32)[None, :, None, :, None]

    def block_diag(x):
        return (x[:, :, :, None, :] * eye).reshape(nbk, STATE_LANES, LANES)

    mats = [block_diag(b_re.reshape(nbk, r, p, h)), block_diag(b_im.reshape(nbk, r, p, h)),
            block_diag(c_re.reshape(nbk, r, h, p).transpose(0, 1, 3, 2)),
            block_diag(c_im.reshape(nbk, r, h, p).transpose(0, 1, 3, 2))]
    spec = lambda shape: pl.BlockSpec((1,) + shape, lambda b: (b, 0, 0))
    n = lc * LANES
    return pl.pallas_call(
        functools.partial(_ssm_tables_kernel, lc=lc),
        grid=(nbk,),
        in_specs=[spec((STATE_LANES, 1))] * 3 + [spec((1, STATE_LANES))] * 3 + [spec((STATE_LANES, LANES))] * 4,
        out_specs=[spec((n, n)), spec((n, 2 * STATE_LANES)), spec((2 * STATE_LANES, n)),
                   spec((1, 2 * STATE_LANES)), spec((1, 2 * STATE_LANES))],
        out_shape=[jax.ShapeDtypeStruct((nbk, n, n), BF16), jax.ShapeDtypeStruct((nbk, n, 2 * STATE_LANES), BF16),
                   jax.ShapeDtypeStruct((nbk, 2 * STATE_LANES, n), BF16),
                   jax.ShapeDtypeStruct((nbk, 1, 2 * STATE_LANES), F32),
                   jax.ShapeDtypeStruct((nbk, 1, 2 * STATE_LANES), F32)],
        compiler_params=_params(("arbitrary",)),
        name="ssm_tables",
    )(*cols, *rows, *mats)


def _nt_dot(a, b):
    return lax.dot_general(a, b, (((1,), (1,)), ((), ())), preferred_element_type=F32)


def _swa_prompt_kernel(sink_ref, qt_ref, kp_ref, kc_ref, vtp_ref, vtc_ref, mstd_ref, mfirst_ref, g_ref, ot_ref, *, nblk):
    first_tile = pl.program_id(1) == 0
    kcat = jnp.concatenate([kp_ref[...], kc_ref[...]], axis=0).astype(BF16)
    vcat_t = jnp.concatenate([vtp_ref[...], vtc_ref[...]], axis=1).astype(BF16)
    keys = lambda j: slice(j * WINDOW, (j + 2) * WINDOW)
    zeros = jnp.zeros((HEAD_DIM, WINDOW), BF16)
    sinks = [jnp.concatenate([jnp.full((1, WINDOW), sink_ref[kh * GQA_GROUP + g], F32) for g in range(GQA_GROUP)],
                             axis=1) for kh in range(N_KV_HEADS)]
    chains = [(j, kh) for j in range(nblk) for kh in range(N_KV_HEADS)]

    def scores(j, kh):
        def rhs(g):
            h = kh * GQA_GROUP + g
            q = qt_ref[h * HEAD_DIM:(h + 1) * HEAD_DIM, j * WINDOW:(j + 1) * WINDOW]
            return jnp.concatenate([q, zeros] if kh == 0 else [zeros, q], axis=0)
        r = jnp.concatenate([rhs(g) for g in range(GQA_GROUP)], axis=1)
        vmask = mstd_ref[...]
        if j == 0:
            vmask = jnp.where(first_tile, mfirst_ref[...], vmask)
        valid = jnp.tile(vmask, (1, GQA_GROUP)) > 0.0
        return jnp.where(valid, jnp.dot(kcat[keys(j)], r, preferred_element_type=F32), NEG_INF)

    def softmax(kh, sc):
        m = jnp.maximum(jnp.max(sc, axis=0, keepdims=True), sinks[kh])
        e = jnp.exp(sc - m)
        return e.astype(BF16), 1.0 / (jnp.sum(e, axis=0, keepdims=True) + jnp.exp(sinks[kh] - m))

    sc = [scores(*c) for c in chains]
    pr = [softmax(kh, x) for (j, kh), x in zip(chains, sc)]
    out = [jnp.dot(vcat_t[kh * HEAD_DIM:(kh + 1) * HEAD_DIM, keys(j)], e, preferred_element_type=F32) * inv
           for (j, kh), (e, inv) in zip(chains, pr)]
    gain = g_ref[...]
    for j in range(nblk):
        o = jnp.concatenate([out[j * N_KV_HEADS + kh][:, g * WINDOW:(g + 1) * WINDOW]
                             for kh in range(N_KV_HEADS) for g in range(GQA_GROUP)], axis=0)
        scale = lax.rsqrt(jnp.mean(o * o, axis=0, keepdims=True) + RMS_EPS)
        ot_ref[:, j * WINDOW:(j + 1) * WINDOW] = (o * scale * gain).astype(BF16)


def _swa_masks():
    kj = jnp.arange(2 * WINDOW)[:, None]
    diff = jnp.arange(WINDOW)[None, :] + WINDOW - kj
    std = (diff >= 0) & (diff <= WINDOW)
    return std.astype(F32), (std & (kj >= WINDOW)).astype(F32)


def _swa_prompt(qt, k, vt, sinks, out_g, n, seq, tq):
    nt, nblk = seq // tq, tq // WINDOW
    mstd, mfirst = _swa_masks()
    gain = jnp.broadcast_to(out_g.reshape(Q_WIDTH, 1), (Q_WIDTH, WINDOW))
    prev_blk = lambda b, i: jnp.maximum((b * nt + i) * nblk - 1, b * nt * nblk)
    cur_t = lambda r: pl.BlockSpec((r, tq), lambda b, i: (0, b * nt + i))
    return pl.pallas_call(
        functools.partial(_swa_prompt_kernel, nblk=nblk),
        grid=(n, nt),
        in_specs=[pl.BlockSpec(memory_space=pltpu.SMEM), cur_t(Q_WIDTH),
                  pl.BlockSpec((WINDOW, KV_WIDTH), lambda b, i: (prev_blk(b, i), 0)),
                  pl.BlockSpec((tq, KV_WIDTH), lambda b, i: (b * nt + i, 0)),
                  pl.BlockSpec((KV_WIDTH, WINDOW), lambda b, i: (0, prev_blk(b, i))), cur_t(KV_WIDTH),
                  _const_spec(mstd.shape), _const_spec(mfirst.shape), _const_spec(gain.shape)],
        out_specs=cur_t(Q_WIDTH),
        out_shape=jax.ShapeDtypeStruct((Q_WIDTH, n * seq), BF16),
        compiler_params=_params(("arbitrary", "arbitrary")),
        name="swa_prompt",
    )(sinks, qt, k, k, vt, vt, mstd, mfirst, gain)


def _swa_sample_kernel(sink_ref, q_ref, kn_ref, vn_ref, ck_ref, cv_ref, g_ref, o_ref, nk_ref, nv_ref, *, ns, t):
    rows = GQA_GROUP * t
    tok = lax.broadcasted_iota(jnp.int32, (rows, WINDOW), 0) % t
    valid_c = lax.broadcasted_iota(jnp.int32, (rows, WINDOW), 1) >= tok
    tok_n = lax.broadcasted_iota(jnp.int32, (rows, t), 0) % t
    valid_n = lax.broadcasted_iota(jnp.int32, (rows, t), 1) <= tok_n
    gain = g_ref[...]

    sinks = [jnp.concatenate([jnp.full((t, 1), sink_ref[kh * GQA_GROUP + g], F32) for g in range(GQA_GROUP)], axis=0)
             for kh in range(N_KV_HEADS)]
    heads = [(s, kh) for s in range(ns) for kh in range(N_KV_HEADS)]
    hs = lambda kh: slice(kh * HEAD_DIM, (kh + 1) * HEAD_DIM)

    for s in range(ns):
        nk_ref[s, :WINDOW - t, :] = ck_ref[s, t:, :]
        nk_ref[s, WINDOW - t:, :] = kn_ref[s]
        nv_ref[s, :WINDOW - t, :] = cv_ref[s, t:, :]
        nv_ref[s, WINDOW - t:, :] = vn_ref[s]

    def scores(s, kh):
        q = q_ref[s].astype(F32)
        q4 = jnp.concatenate([q[:, (kh * GQA_GROUP + g) * HEAD_DIM:(kh * GQA_GROUP + g + 1) * HEAD_DIM]
                              for g in range(GQA_GROUP)], axis=0).astype(BF16)
        sc_c = jnp.where(valid_c, _nt_dot(q4, ck_ref[s, :, hs(kh)].astype(BF16)), NEG_INF)
        sc_n = jnp.where(valid_n, _nt_dot(q4, kn_ref[s, :, hs(kh)].astype(BF16)), NEG_INF)
        return sc_c, sc_n

    def softmax(kh, sc_c, sc_n):
        m = jnp.maximum(jnp.maximum(jnp.max(sc_c, axis=-1, keepdims=True),
                                    jnp.max(sc_n, axis=-1, keepdims=True)), sinks[kh])
        e_c, e_n = jnp.exp(sc_c - m), jnp.exp(sc_n - m)
        inv = 1.0 / (jnp.sum(e_c, axis=-1, keepdims=True) + jnp.sum(e_n, axis=-1, keepdims=True)
                     + jnp.exp(sinks[kh] - m))
        return (e_c * inv).astype(BF16), (e_n * inv).astype(BF16)

    def values(s, kh, p_c, p_n):
        return (jnp.dot(p_c, cv_ref[s, :, hs(kh)].astype(BF16), preferred_element_type=F32)
                + jnp.dot(p_n, vn_ref[s, :, hs(kh)].astype(BF16), preferred_element_type=F32))

    sc = [scores(s, kh) for s, kh in heads]
    pr = [softmax(kh, *x) for (s, kh), x in zip(heads, sc)]
    o4 = [values(s, kh, *x) for (s, kh), x in zip(heads, pr)]
    for s in range(ns):
        o = jnp.concatenate([o4[s * N_KV_HEADS + kh][g * t:(g + 1) * t]
                             for kh in range(N_KV_HEADS) for g in range(GQA_GROUP)], axis=1)
        o_ref[s] = _rms(o, gain).astype(BF16)


def _swa_sample(q3, kn3, vn3, ck, cv, sinks, out_g, sb):
    ns, t, _ = q3.shape
    blk = lambda a: pl.BlockSpec((sb,) + a.shape[1:], lambda i: (i, 0, 0))
    kern = functools.partial(_swa_sample_kernel, ns=sb, t=t)
    return pl.pallas_call(
        kern,
        grid=(ns // sb,),
        in_specs=[pl.BlockSpec(memory_space=pltpu.SMEM), blk(q3), blk(kn3), blk(vn3), blk(ck), blk(cv),
                  pl.BlockSpec((1, Q_WIDTH), lambda i: (0, 0))],
        out_specs=[blk(q3), blk(ck), blk(cv)],
        out_shape=[jax.ShapeDtypeStruct(q3.shape, BF16), jax.ShapeDtypeStruct(ck.shape, F32),
                   jax.ShapeDtypeStruct(cv.shape, F32)],
        compiler_params=_params(("arbitrary",)),
        name="swa_sample",
    )(sinks, q3, kn3, vn3, ck, cv, out_g)


def _gelu_tanh(x):
    return 0.5 * x * (1.0 + jnp.tanh(math.sqrt(2.0 / math.pi) * (x + 0.044715 * (x * x * x))))


def _merge_tile(h, y, at, wglu_ref, bglu, sg, wout_ref, post_g, xa_g, wq_ref, at_transposed=False):
    g = _gelu_tanh(y)
    lin = jnp.dot(g.astype(BF16), wglu_ref[...], preferred_element_type=F32) + bglu
    y_ssm = g * (1.0 / (1.0 + jnp.exp(-lin)))
    ssm_n = _rms(y_ssm, sg).astype(BF16)
    at_dims = (((0,), (0,)), ((), ())) if at_transposed else (((1,), (0,)), ((), ()))
    mixed = (jnp.dot(ssm_n, wout_ref[:SSM_WIDTH, :], preferred_element_type=F32)
             + lax.dot_general(at, wout_ref[SSM_WIDTH:, :], at_dims, preferred_element_type=F32))
    h2 = h + _rms(mixed, post_g)
    qm = jnp.dot(_rms(h2, xa_g).astype(BF16), wq_ref[...], preferred_element_type=F32)
    return h2, (qm * (MEM_HEAD_DIM ** -0.5)).astype(BF16)


def _merge_kernel(h_ref, y_ref, at_ref, wglu_ref, bglu_ref, sg_ref, wout_ref, post_g_ref, xa_g_ref, wq_ref,
                  h2_ref, qm_ref):
    h2_ref[...], qm_ref[...] = _merge_tile(h_ref[...], y_ref[...], at_ref[...], wglu_ref, bglu_ref[...], sg_ref[...],
                                           wout_ref, post_g_ref[...], xa_g_ref[...], wq_ref)


def _merge_specs(wglu, wout, wq):
    return [_const_spec(wglu.shape), _const_spec((1, SSM_WIDTH)), _const_spec((1, SSM_WIDTH)), _const_spec(wout.shape),
            _const_spec((1, D_MODEL)), _const_spec((1, D_MODEL)), _const_spec(wq.shape)]


def _merge(h, y, at, merge_w, tm):
    t = h.shape[0]
    row = lambda w: pl.BlockSpec((tm, w), lambda i: (i, 0))
    return pl.pallas_call(
        _merge_kernel,
        grid=(t // tm,),
        in_specs=[row(D_MODEL), row(SSM_WIDTH), row(Q_WIDTH)] + _merge_specs(merge_w[0], merge_w[3], merge_w[6]),
        out_specs=[row(D_MODEL), row(D_MODEL)],
        out_shape=[jax.ShapeDtypeStruct((t, D_MODEL), F32), jax.ShapeDtypeStruct((t, D_MODEL), BF16)],
        compiler_params=_params(("arbitrary",)),
        name="merge",
    )(h, y, at, *merge_w)


def _mem_attn_sample_kernel(q_ref, k_ref, v_ref, o_ref, *, gb, t):
    rows = t * MEM_HEADS
    halves = MEM_HEAD_DIM // LANES
    kv_rows = N_MEM * halves * MEM_HEADS
    period = halves * MEM_HEADS
    lane = lax.broadcasted_iota(jnp.int32, (rows, kv_rows), 1) % period
    head = lax.broadcasted_iota(jnp.int32, (rows, kv_rows), 0) % MEM_HEADS
    in_half = [lane == head + hf * MEM_HEADS for hf in range(halves)]
    for b in range(gb):
        kb = k_ref[b].astype(BF16)
        vb = v_ref[b].astype(BF16)
        part = _nt_dot(q_ref[b], kb)
        sc = jnp.where(in_half[0], part[:rows], 0.0)
        for hf in range(1, halves):
            sc = sc + pltpu.roll(jnp.where(in_half[hf], part[hf * rows:(hf + 1) * rows], 0.0),
                                 kv_rows - hf * MEM_HEADS, 1)
        sc = jnp.where(in_half[0], sc, -jnp.inf)
        e = jnp.exp(sc - jnp.max(sc, axis=-1, keepdims=True))
        inv = 1.0 / jnp.sum(e, axis=-1, keepdims=True)
        e_all = jnp.concatenate([e] + [pltpu.roll(e, hf * MEM_HEADS, 1) for hf in range(1, halves)], axis=0)
        o = jnp.dot(e_all.astype(BF16), vb, preferred_element_type=F32)
        o_ref[b] = (o * jnp.concatenate([inv] * halves, axis=0)).astype(BF16)


def _mem_attn_sample(qm, cache_k, cache_v, n_s, t_s, gb):
    halves = MEM_HEAD_DIM // LANES
    rows = halves * t_s * MEM_HEADS
    kv_rows = N_MEM * halves * MEM_HEADS

    def stored_rows(c):
        c = c.reshape(n_s, N_MEM, MEM_HEADS, halves, LANES).transpose(0, 1, 3, 2, 4)
        return c.reshape(n_s, kv_rows, LANES)

    q = qm.reshape(n_s, t_s, MEM_HEADS, halves, LANES).transpose(0, 3, 1, 2, 4).reshape(n_s, rows, LANES)
    blk = lambda r: pl.BlockSpec((gb, r, LANES), lambda i: (i, 0, 0))
    o = pl.pallas_call(
        functools.partial(_mem_attn_sample_kernel, gb=gb, t=t_s),
        grid=(n_s // gb,),
        in_specs=[blk(rows), blk(kv_rows), blk(kv_rows)],
        out_specs=blk(rows),
        out_shape=jax.ShapeDtypeStruct((n_s, rows, LANES), BF16),
        compiler_params=_params(("arbitrary",)),
        name="mem_attn_sample",
    )(q, stored_rows(cache_k), stored_rows(cache_v))
    o = o.reshape(n_s, halves, t_s, MEM_HEADS, LANES).transpose(0, 2, 3, 1, 4)
    return o.reshape(n_s * t_s, D_MODEL)


def _mem_heads(q, k, v):
    outs = []
    for hh in range(MEM_HEADS):
        hs = slice(hh * MEM_HEAD_DIM, (hh + 1) * MEM_HEAD_DIM)
        sc = _nt_dot(q[:, hs], k[:, hs])
        e = jnp.exp(sc - jnp.max(sc, axis=-1, keepdims=True))
        inv = 1.0 / jnp.sum(e, axis=-1, keepdims=True)
        outs.append((jnp.dot(e.astype(BF16), v[:, hs], preferred_element_type=F32) * inv).astype(BF16))
    return jnp.concatenate(outs, axis=1)


def _ffn_out_tile(h2, om, wo_ref, xa_post, pre_g, wg_ref, wu_ref, wd_ref, post_g, act_ref):
    c = jnp.dot(om, wo_ref[...], preferred_element_type=F32)
    h3 = h2 + _rms(c, xa_post)
    return _ffn_tile(h3, pre_g, wg_ref, wu_ref, wd_ref, post_g, act_ref)


def _ffn_out_specs(wo, ffn_w):
    return [_const_spec(wo.shape), _const_spec((1, D_MODEL)), _const_spec((1, D_MODEL)),
            *[_const_spec(w.shape) for w in ffn_w], _const_spec((1, D_MODEL))]


def _ffn_out_kernel(h_ref, o_ref, wo_ref, xa_post_ref, pre_g_ref, wg_ref, wu_ref, wd_ref, post_g_ref, out_ref, act_ref):
    out_ref[...] = _ffn_out_tile(h_ref[...], o_ref[...], wo_ref, xa_post_ref[...], pre_g_ref[...], wg_ref, wu_ref,
                                 wd_ref, post_g_ref[...], act_ref)


def _ffn_out(h, o, out_w, tm):
    t = h.shape[0]
    row = pl.BlockSpec((tm, D_MODEL), lambda i: (i, 0))
    return pl.pallas_call(
        _ffn_out_kernel,
        grid=(t // tm,),
        in_specs=[row, row] + _ffn_out_specs(out_w[0], out_w[3:6]),
        out_specs=row,
        out_shape=jax.ShapeDtypeStruct((t, D_MODEL), F32),
        scratch_shapes=[pltpu.VMEM((tm, D_FF), BF16)],
        compiler_params=_params(("arbitrary",)),
        name="ffn_out",
    )(h, o, *out_w)


def _post_kernel(h_ref, y_ref, at_ref, k_ref, v_ref,
                 wglu_ref, bglu_ref, sg_ref, wout_ref, post_g_ref, xa_g_ref, wq_ref,
                 wo_ref, xa_post_ref, pre_g_ref, wg_ref, wu_ref, wd_ref, ffn_post_ref, out_ref, act_ref):
    h2, qm = _merge_tile(h_ref[...], y_ref[...], at_ref[...], wglu_ref, bglu_ref[...], sg_ref[...],
                         wout_ref, post_g_ref[...], xa_g_ref[...], wq_ref, at_transposed=True)
    om = _mem_heads(qm, k_ref[0], v_ref[0])
    out_ref[...] = _ffn_out_tile(h2, om, wo_ref, xa_post_ref[...], pre_g_ref[...], wg_ref, wu_ref, wd_ref,
                                 ffn_post_ref[...], act_ref)


def _post(h, y, at, k3, v3, merge_w, out_w, tm):
    t = h.shape[0]
    tiles_per_batch = t // k3.shape[0] // tm
    row = lambda w: pl.BlockSpec((tm, w), lambda i: (i, 0))
    kv = pl.BlockSpec((1, N_MEM, D_MODEL), lambda i: (i // tiles_per_batch, 0, 0))
    return pl.pallas_call(
        _post_kernel,
        grid=(t // tm,),
        in_specs=([row(D_MODEL), row(SSM_WIDTH), pl.BlockSpec((Q_WIDTH, tm), lambda i: (0, i)), kv, kv]
                  + _merge_specs(merge_w[0], merge_w[3], merge_w[6]) + _ffn_out_specs(out_w[0], out_w[3:6])),
        out_specs=row(D_MODEL),
        out_shape=jax.ShapeDtypeStruct((t, D_MODEL), F32),
        scratch_shapes=[pltpu.VMEM((tm, D_FF), BF16)],
        compiler_params=_params(("arbitrary",)),
        name="post",
    )(h, y, at, k3, v3, *merge_w, *out_w)


def _mem_kv_kernel(m_ref, g_ref, wkv_ref, k_ref, v_ref, kb_ref, vb_ref):
    kv = jnp.dot(_rms(m_ref[...], g_ref[...]).astype(BF16), wkv_ref[...], preferred_element_type=F32)
    k_ref[...] = kv[:, :D_MODEL]
    v_ref[...] = kv[:, D_MODEL:]
    kb_ref[...] = kv[:, :D_MODEL].astype(BF16)
    vb_ref[...] = kv[:, D_MODEL:].astype(BF16)


def _mem_kv(mem, g, wkv, tm):
    t = mem.shape[0]
    row = pl.BlockSpec((tm, D_MODEL), lambda i: (i, 0))
    return pl.pallas_call(
        _mem_kv_kernel,
        grid=(t // tm,),
        in_specs=[row, _const_spec((1, D_MODEL)), _const_spec(wkv.shape)],
        out_specs=[row] * 4,
        out_shape=[jax.ShapeDtypeStruct((t, D_MODEL), F32)] * 2 + [jax.ShapeDtypeStruct((t, D_MODEL), BF16)] * 2,
        compiler_params=_params(("arbitrary",)),
        name="mem_kv",
    )(mem, g, wkv)


def _rope_tables(pos):
    half = ROPE_DIM // 2
    inv = ROPE_THETA ** (-jnp.arange(half, dtype=F32) * (2.0 / ROPE_DIM))
    ang = pos.astype(F32)[:, None] * inv[None, :]
    cos, sin = jnp.cos(ang), jnp.sin(ang)
    n = pos.shape[0]
    pad = jnp.zeros((n, HEAD_DIM - ROPE_DIM), F32)
    zero = jnp.zeros((n, half), F32)
    cos_h = jnp.concatenate([cos, cos, pad + 1.0], axis=1)
    lo_h = jnp.concatenate([-sin, zero, pad], axis=1)
    hi_h = jnp.concatenate([zero, sin, pad], axis=1)
    rep = LANES // HEAD_DIM
    return tuple(jnp.tile(a, (1, rep)) for a in (cos_h, lo_h, hi_h)), (cos.T, sin.T)


def _ffn_weights(w_gate, w_up, w_down):
    return w_gate.astype(BF16), w_up.astype(BF16), w_down.astype(BF16)


def _lane_block_states(st, n):
    st = st.reshape(N_LANE_BLOCKS, n, 2, GROUPS_PER_LANE_BLOCK, SSM_STATE).transpose(2, 1, 0, 3, 4)
    st = st.reshape(2, n, N_SSM_GROUPS, SSM_STATE)
    return st[0], st[1]


def kernel(x_prompt, x_sample, state_ssm_re, state_ssm_im, cache_swa_k, cache_swa_v, cache_mem_k, cache_mem_v, mem_prompt, ffn1_pre_g, ffn1_w_gate, ffn1_w_up, ffn1_w_down, ffn1_post_g, mix_pre_g, w_in, ssm_a_re, ssm_a_im, ssm_log_step, ssm_b_re, ssm_b_im, ssm_c_re, ssm_c_im, ssm_d, ssm_w_glu, ssm_b_glu, attn_sinks, ssm_out_g, attn_out_g, w_out, mix_post_g, mem_norm_g, w_mem_q, w_mem_k, w_mem_v, w_mem_o, xa_pre_g, xa_post_g, ffn2_pre_g, ffn2_w_gate, ffn2_w_up, ffn2_w_down, ffn2_post_g):
    n_p, s_p, _ = x_prompt.shape
    n_s, t_s, _ = x_sample.shape
    tm = 512
    row = lambda a: a.reshape(1, -1).astype(F32)

    ffn1_w = _ffn_weights(ffn1_w_gate, ffn1_w_up, ffn1_w_down)
    win = w_in.astype(BF16)
    merge_w = (ssm_w_glu.astype(BF16), row(ssm_b_glu), row(ssm_out_g), w_out.astype(BF16), row(mix_post_g),
               row(xa_pre_g), w_mem_q.astype(BF16))
    out_w = (w_mem_o.astype(BF16), row(xa_post_g), row(ffn2_pre_g),
             *_ffn_weights(ffn2_w_gate, ffn2_w_up, ffn2_w_down), row(ffn2_post_g))
    wkv = jnp.concatenate([w_mem_k, w_mem_v], axis=1).astype(BF16)
    d_row = row(ssm_d)
    ssm_args = (ssm_a_re.astype(F32), ssm_a_im.astype(F32), ssm_log_step.astype(F32), ssm_b_re.astype(F32),
                ssm_b_im.astype(F32), ssm_c_re.astype(F32), ssm_c_im.astype(F32))
    sinks = attn_sinks.astype(F32)

    pm_k, pm_v, pm_kb, pm_vb = _mem_kv(mem_prompt.reshape(n_p * N_MEM, D_MODEL), row(mem_norm_g), wkv, N_MEM)

    def tokenwise_in(x2, pos_tab):
        return _ffn_in(x2, row(ffn1_pre_g), ffn1_w, row(ffn1_post_g), row(mix_pre_g), win, *pos_tab, tm)

    lc_p = 2 * t_s
    ssm_m, ssm_w, ssm_v, lam_p, lam_s = _ssm_tables(*ssm_args, lc_p)
    o1 = SSM_WIDTH + Q_WIDTH
    wuk = jnp.concatenate([win[:, :SSM_WIDTH], win[:, o1:o1 + KV_WIDTH]], axis=1)
    wqv_t = jnp.concatenate([win[:, SSM_WIDTH:o1], win[:, o1 + KV_WIDTH:]], axis=1).T
    tab_p, tab_p_t = _rope_tables(jnp.arange(s_p, dtype=jnp.int32))
    h1, u, k, qt, vt = _ffn_in_t(x_prompt.reshape(n_p * s_p, D_MODEL), row(ffn1_pre_g), ffn1_w, row(ffn1_post_g),
                                 row(mix_pre_g), wuk, wqv_t, *tab_p, *tab_p_t, tm)
    y3, st_p = _ssm_prompt(u.reshape(n_p, s_p, SSM_WIDTH), ssm_m, ssm_w, ssm_v, lam_p, d_row, lc_p, 1024)
    at = _swa_prompt(qt, k, vt, sinks, attn_out_g.astype(F32), n_p, s_p, 512)
    y_prompt = _post(h1, y3.reshape(n_p * s_p, SSM_WIDTH), at, pm_kb.reshape(n_p, N_MEM, D_MODEL),
                     pm_vb.reshape(n_p, N_MEM, D_MODEL), merge_w, out_w, tm).reshape(n_p, s_p, D_MODEL)
    p_sre, p_sim = _lane_block_states(st_p, n_p)
    p_wk = k.reshape(n_p, s_p, N_KV_HEADS, HEAD_DIM)[:, -WINDOW:]
    p_wv = vt.reshape(N_KV_HEADS, HEAD_DIM, n_p, s_p)[..., -WINDOW:].transpose(2, 3, 0, 1)

    pos_s = jnp.tile(PAST_LEN + jnp.arange(t_s, dtype=jnp.int32), n_s)
    h1s, us, qs, ks, vs = tokenwise_in(x_sample.reshape(n_s * t_s, D_MODEL), _rope_tables(pos_s)[0])
    ys, s_sre, s_sim = _ssm_sample(us, state_ssm_re.reshape(n_s, -1).astype(F32), state_ssm_im.reshape(n_s, -1).astype(F32),
                                   ssm_m, ssm_w, ssm_v, lam_s, d_row, t_s)
    win_len = cache_swa_k.shape[1]
    ats, s_wk, s_wv = _swa_sample(qs.reshape(n_s, t_s, Q_WIDTH), ks.reshape(n_s, t_s, KV_WIDTH), vs.reshape(n_s, t_s, KV_WIDTH),
                                  cache_swa_k.reshape(n_s, win_len, KV_WIDTH), cache_swa_v.reshape(n_s, win_len, KV_WIDTH),
                                  sinks, row(attn_out_g), 16)
    h2s, qms = _merge(h1s, ys, ats.reshape(n_s * t_s, Q_WIDTH), merge_w, tm)
    oms = _mem_attn_sample(qms, cache_mem_k, cache_mem_v, n_s, t_s, 4)
    y_sample = _ffn_out(h2s, oms, out_w, tm).reshape(n_s, t_s, D_MODEL)

    return (y_prompt, y_sample, p_sre, p_sim, p_wk, p_wv,
            pm_k.reshape(n_p, N_MEM, MEM_HEADS, MEM_HEAD_DIM), pm_v.reshape(n_p, N_MEM, MEM_HEADS, MEM_HEAD_DIM),
            s_sre.reshape(n_s, N_SSM_GROUPS, SSM_STATE), s_sim.reshape(n_s, N_SSM_GROUPS, SSM_STATE),
            s_wk.reshape(n_s, win_len, N_KV_HEADS, HEAD_DIM), s_wv.reshape(n_s, win_len, N_KV_HEADS, HEAD_DIM))
```

```python
import functools
import math

import jax
import jax.numpy as jnp
from jax import lax
from jax.experimental import pallas as pl
from jax.experimental.pallas import tpu as pltpu

F32 = jnp.float32
BF16 = jnp.bfloat16

D_MODEL = 1024
PAST_LEN = 16384
SSM_WIDTH = 512
SSM_GROUP = 16
N_SSM_GROUPS = 32
SSM_STATE = 64
HEAD_DIM = 64
N_HEADS = 8
N_KV_HEADS = 2
GQA_GROUP = 4
Q_WIDTH = 512
KV_WIDTH = 128
WINDOW = 128
ROPE_THETA = 500000.0
ROPE_DIM = 16
N_MEM = 256
MEM_HEADS = 4
MEM_HEAD_DIM = 256
D_FF = 2816
RMS_EPS = 1e-6
IN_WIDTH = SSM_WIDTH + Q_WIDTH + 2 * KV_WIDTH
NEG_INF = -1e30

LANES = 128
FF_CHUNK = 256
N_FF_CHUNKS = D_FF // FF_CHUNK
GROUPS_PER_LANE_BLOCK = LANES // SSM_GROUP
N_LANE_BLOCKS = SSM_WIDTH // LANES
STATE_LANES = GROUPS_PER_LANE_BLOCK * SSM_STATE
VMEM_LIMIT = 56 * 1024 * 1024


def _rms(x, g):
    return x * lax.rsqrt(jnp.mean(x * x, axis=-1, keepdims=True) + RMS_EPS) * g


def _const_spec(shape):
    nd = len(shape)
    return pl.BlockSpec(shape, lambda *_: (0,) * nd, pipeline_mode=pl.Buffered(1))


def _params(sem):
    return pltpu.CompilerParams(dimension_semantics=sem, vmem_limit_bytes=VMEM_LIMIT)


def _ffn_tile(x, pre_g, wg_ref, wu_ref, wd_ref, post_g, act_ref):
    xn = _rms(x, pre_g).astype(BF16)
    for c in range(N_FF_CHUNKS):
        cols = slice(c * FF_CHUNK, (c + 1) * FF_CHUNK)
        gate = jnp.dot(xn, wg_ref[:, cols], preferred_element_type=F32)
        up = jnp.dot(xn, wu_ref[:, cols], preferred_element_type=F32)
        act = gate * (1.0 / (1.0 + jnp.exp(-gate))) * up
        act_ref[:, cols] = act.astype(BF16)
    down = jnp.dot(act_ref[...], wd_ref[...], preferred_element_type=F32)
    return x + 0.5 * _rms(down, post_g)


def _rope(x, cos, sin_lo, sin_hi):
    w = x.shape[1]
    half = ROPE_DIM // 2
    return (x * cos + pltpu.roll(x, w - half, 1) * sin_lo + pltpu.roll(x, half, 1) * sin_hi)


def _ffn_in_kernel(x_ref, pre_g_ref, wg_ref, wu_ref, wd_ref, post_g_ref, mix_g_ref, win_ref,
                   cos_ref, slo_ref, shi_ref,
                   h_ref, u_ref, q_ref, k_ref, v_ref, act_ref):
    h = _ffn_tile(x_ref[...], pre_g_ref[...], wg_ref, wu_ref, wd_ref, post_g_ref[...], act_ref)
    h_ref[...] = h
    z = jnp.dot(_rms(h, mix_g_ref[...]).astype(BF16), win_ref[...], preferred_element_type=F32)
    u_ref[...] = z[:, :SSM_WIDTH]
    o1 = SSM_WIDTH + Q_WIDTH
    cos, slo, shi = cos_ref[...], slo_ref[...], shi_ref[...]
    rep = Q_WIDTH // LANES
    q = _rope(z[:, SSM_WIDTH:o1], jnp.tile(cos, (1, rep)), jnp.tile(slo, (1, rep)), jnp.tile(shi, (1, rep)))
    q_ref[...] = (q * (HEAD_DIM ** -0.5)).astype(BF16)
    k_ref[...] = _rope(z[:, o1:o1 + KV_WIDTH], cos, slo, shi)
    v_ref[...] = z[:, o1 + KV_WIDTH:]


def _ffn_in_t_kernel(x_ref, pre_g_ref, wg_ref, wu_ref, wd_ref, post_g_ref, mix_g_ref, wuk_ref, wqv_t_ref,
                     cos_ref, slo_ref, shi_ref, cos_t_ref, sin_t_ref,
                     h_ref, u_ref, k_ref, qt_ref, vt_ref, act_ref):
    h = _ffn_tile(x_ref[...], pre_g_ref[...], wg_ref, wu_ref, wd_ref, post_g_ref[...], act_ref)
    h_ref[...] = h
    hn = _rms(h, mix_g_ref[...]).astype(BF16)
    z = jnp.dot(hn, wuk_ref[...], preferred_element_type=F32)
    for b in range(N_LANE_BLOCKS):
        u_ref[b] = z[:, b * LANES:(b + 1) * LANES]
    k_ref[...] = _rope(z[:, SSM_WIDTH:], cos_ref[...], slo_ref[...], shi_ref[...])
    zt = _nt_dot(wqv_t_ref[...], hn)
    vt_ref[...] = zt[Q_WIDTH:]
    tm = zt.shape[1]
    half = ROPE_DIM // 2
    q3 = zt[:Q_WIDTH].reshape(N_HEADS, HEAD_DIM, tm)
    x1, x2 = q3[:, :half], q3[:, half:ROPE_DIM]
    cos, sin = cos_t_ref[...][None], sin_t_ref[...][None]
    q3 = jnp.concatenate([x1 * cos - x2 * sin, x2 * cos + x1 * sin, q3[:, ROPE_DIM:]], axis=1)
    qt_ref[...] = (q3.reshape(Q_WIDTH, tm) * (HEAD_DIM ** -0.5)).astype(BF16)


def _ffn_in_t(x, pre_g, ffn_w, post_g, mix_g, wuk, wqv_t, cos, slo, shi, cos_t, sin_t, tm):
    t = x.shape[0]
    n_pos_tiles = cos.shape[0] // tm
    half = ROPE_DIM // 2
    row = lambda w: pl.BlockSpec((tm, w), lambda i: (i, 0))
    col = lambda r: pl.BlockSpec((r, tm), lambda i: (0, i))
    tab = pl.BlockSpec((tm, LANES), lambda i: (i % n_pos_tiles, 0))
    tab_t = pl.BlockSpec((half, tm), lambda i: (0, i % n_pos_tiles))
    return pl.pallas_call(
        _ffn_in_t_kernel,
        grid=(t // tm,),
        in_specs=[row(D_MODEL), _const_spec((1, D_MODEL)), *[_const_spec(w.shape) for w in ffn_w],
                  _const_spec((1, D_MODEL)), _const_spec((1, D_MODEL)), _const_spec(wuk.shape),
                  _const_spec(wqv_t.shape), tab, tab, tab, tab_t, tab_t],
        out_specs=[row(D_MODEL), pl.BlockSpec((N_LANE_BLOCKS, tm, LANES), lambda i: (0, i, 0)), row(KV_WIDTH),
                   col(Q_WIDTH), col(KV_WIDTH)],
        out_shape=[jax.ShapeDtypeStruct((t, D_MODEL), F32), jax.ShapeDtypeStruct((N_LANE_BLOCKS, t, LANES), F32),
                   jax.ShapeDtypeStruct((t, KV_WIDTH), F32), jax.ShapeDtypeStruct((Q_WIDTH, t), BF16),
                   jax.ShapeDtypeStruct((KV_WIDTH, t), F32)],
        scratch_shapes=[pltpu.VMEM((tm, D_FF), BF16)],
        compiler_params=_params(("arbitrary",)),
        name="ffn_in_t",
    )(x, pre_g, *ffn_w, post_g, mix_g, wuk, wqv_t, cos, slo, shi, cos_t, sin_t)


def _ffn_in(x, pre_g, ffn_w, post_g, mix_g, win, cos, slo, shi, tm):
    t = x.shape[0]
    n_pos_tiles = cos.shape[0] // tm
    row = lambda w: pl.BlockSpec((tm, w), lambda i: (i, 0))
    tab = pl.BlockSpec((tm, LANES), lambda i: (i % n_pos_tiles, 0))
    return pl.pallas_call(
        _ffn_in_kernel,
        grid=(t // tm,),
        in_specs=[row(D_MODEL), _const_spec((1, D_MODEL)), *[_const_spec(w.shape) for w in ffn_w],
                  _const_spec((1, D_MODEL)), _const_spec((1, D_MODEL)), _const_spec(win.shape),
                  tab, tab, tab],
        out_specs=[row(D_MODEL), row(SSM_WIDTH), row(Q_WIDTH), row(KV_WIDTH), row(KV_WIDTH)],
        out_shape=[jax.ShapeDtypeStruct((t, D_MODEL), F32), jax.ShapeDtypeStruct((t, SSM_WIDTH), F32),
                   jax.ShapeDtypeStruct((t, Q_WIDTH), BF16), jax.ShapeDtypeStruct((t, KV_WIDTH), F32),
                   jax.ShapeDtypeStruct((t, KV_WIDTH), F32)],
        scratch_shapes=[pltpu.VMEM((tm, D_FF), BF16)],
        compiler_params=_params(("arbitrary",)),
        name="ffn_in",
    )(x, pre_g, *ffn_w, post_g, mix_g, win, cos, slo, shi)


def _complex_step(s_re, s_im, l_re, l_im, x_re, x_im):
    return l_re * s_re - l_im * s_im + x_re, l_re * s_im + l_im * s_re + x_im


def _ssm_prompt_kernel(u_ref, m_ref, w_ref, v_ref, lam_ref, d_ref, y_ref, st_ref,
                       x_scr, ss_scr, s_scr, *, lc, nb, ncl, nlb):
    tt = pl.program_id(1)
    nrow = nlb * nb
    nq = STATE_LANES // LANES

    @pl.when(tt == 0)
    def _():
        s_scr[...] = jnp.zeros_like(s_scr)

    def piece(b, n, j):
        return u_ref[b, n, pl.ds(j, ncl, stride=lc), :]

    a = []
    for b in range(nlb):
        a.append(jnp.concatenate(
            [jnp.concatenate([piece(b, n, j) for j in range(lc)], axis=1) for n in range(nb)], axis=0).astype(BF16))
        x = jnp.dot(a[b], w_ref[b], preferred_element_type=F32)
        for qq in range(2 * nq):
            for n in range(nb):
                x_scr[qq, pl.ds(b * nb + n, ncl, stride=nrow), :] = x[n * ncl:(n + 1) * ncl, qq * LANES:(qq + 1) * LANES]
    lam_rows = lambda qq: jnp.concatenate(
        [jnp.broadcast_to(lam_ref[b, :, qq * LANES:(qq + 1) * LANES], (nb, LANES)) for b in range(nlb)], axis=0)
    l_re = [lam_rows(qq) for qq in range(nq)]
    l_im = [lam_rows(nq + qq) for qq in range(nq)]

    s = [s_scr[qq] for qq in range(2 * nq)]
    for c in range(ncl):
        rows = slice(c * nrow, (c + 1) * nrow)
        for qq in range(nq):
            ss_scr[qq, rows, :] = s[qq]
            ss_scr[nq + qq, rows, :] = s[nq + qq]
            s[qq], s[nq + qq] = _complex_step(s[qq], s[nq + qq], l_re[qq], l_im[qq],
                                              x_scr[qq, rows, :], x_scr[nq + qq, rows, :])
    for qq in range(2 * nq):
        s_scr[qq] = s[qq]
    for b in range(nlb):
        st_ref[b] = jnp.concatenate([s[qq][b * nb:(b + 1) * nb] for qq in range(2 * nq)], axis=1)
        s_start = jnp.concatenate(
            [jnp.concatenate([ss_scr[qq, pl.ds(b * nb + n, ncl, stride=nrow), :] for n in range(nb)], axis=0)
             for qq in range(2 * nq)], axis=1).astype(BF16)
        y = (jnp.dot(a[b], m_ref[b], preferred_element_type=F32)
             + jnp.dot(s_start, v_ref[b], preferred_element_type=F32))
        d = d_ref[b]
        for n in range(nb):
            for j in range(lc):
                y_ref[b, n, pl.ds(j, ncl, stride=lc), :] = (
                    y[n * ncl:(n + 1) * ncl, j * LANES:(j + 1) * LANES] + d * piece(b, n, j))


def _ssm_prompt(u4, m, w, v, lam, d, lc, tl, nlb):
    _, nb, seq, _ = u4.shape
    ncl = tl // lc
    nslab = 2 * STATE_LANES // LANES
    kern = functools.partial(_ssm_prompt_kernel, lc=lc, nb=nb, ncl=ncl, nlb=nlb)
    wspec = lambda a: pl.BlockSpec((nlb,) + a.shape[1:], lambda p, t: (p, 0, 0), pipeline_mode=pl.Buffered(1))
    io = pl.BlockSpec((nlb, nb, tl, LANES), lambda p, t: (p, 0, t, 0))
    return pl.pallas_call(
        kern,
        grid=(N_LANE_BLOCKS // nlb, seq // tl),
        in_specs=[io, wspec(m), wspec(w), wspec(v), wspec(lam), wspec(d)],
        out_specs=[io, pl.BlockSpec((nlb, nb, 2 * STATE_LANES), lambda p, t: (p, 0, 0))],
        out_shape=[jax.ShapeDtypeStruct(u4.shape, F32),
                   jax.ShapeDtypeStruct((N_LANE_BLOCKS, nb, 2 * STATE_LANES), F32)],
        scratch_shapes=[pltpu.VMEM((nslab, nlb * nb * ncl, LANES), F32),
                        pltpu.VMEM((nslab, nlb * nb * ncl, LANES), F32),
                        pltpu.VMEM((nslab, nlb * nb, LANES), F32)],
        compiler_params=_params(("arbitrary", "arbitrary")),
        name="ssm_prompt",
    )(u4, m, w, v, lam, d)


def _ssm_sample_kernel(u_ref, sre_ref, sim_ref, m_ref, w_ref, v_ref, lam_ref, d_ref,
                       y_ref, ore_ref, oim_ref, *, lc, ns):
    def piece(j):
        return u_ref[pl.ds(j, ns, stride=lc), :]

    a = jnp.concatenate([piece(j) for j in range(lc)], axis=1).astype(BF16)
    s_re, s_im = sre_ref[...], sim_ref[...]
    x = jnp.dot(a, w_ref[0], preferred_element_type=F32)
    e_re, e_im = _complex_step(s_re, s_im, lam_ref[0, :, :STATE_LANES], lam_ref[0, :, STATE_LANES:],
                               x[:, :STATE_LANES], x[:, STATE_LANES:])
    ore_ref[...] = e_re
    oim_ref[...] = e_im
    s0 = jnp.concatenate([s_re, s_im], axis=1).astype(BF16)
    y = (jnp.dot(a, m_ref[0], preferred_element_type=F32) + jnp.dot(s0, v_ref[0], preferred_element_type=F32))
    d = d_ref[...]
    for j in range(lc):
        y_ref[pl.ds(j, ns, stride=lc), :] = y[:, j * LANES:(j + 1) * LANES] + d * piece(j)


def _ssm_sample(u, s_re, s_im, m, w, v, lam, d, lc):
    t = u.shape[0]
    ns = t // lc
    kern = functools.partial(_ssm_sample_kernel, lc=lc, ns=ns)
    assert m.shape[1] == 2 * lc * LANES
    n = lc * LANES
    col = lambda rows, width: pl.BlockSpec((rows, width), lambda b: (0, b))
    return pl.pallas_call(
        kern,
        grid=(N_LANE_BLOCKS,),
        in_specs=[col(t, LANES), col(ns, STATE_LANES), col(ns, STATE_LANES),
                  pl.BlockSpec((1, n, n), lambda b: (b, 0, 0)),
                  pl.BlockSpec((1, n, 2 * STATE_LANES), lambda b: (b, 1, 0)),
                  pl.BlockSpec((1, 2 * STATE_LANES, n), lambda b: (b, 0, 0)),
                  pl.BlockSpec((1, 1, 2 * STATE_LANES), lambda b: (b, 0, 0)), col(1, LANES)],
        out_specs=[col(t, LANES), col(ns, STATE_LANES), col(ns, STATE_LANES)],
        out_shape=[jax.ShapeDtypeStruct(u.shape, F32), jax.ShapeDtypeStruct(s_re.shape, F32),
                   jax.ShapeDtypeStruct(s_im.shape, F32)],
        compiler_params=_params(("arbitrary",)),
        name="ssm_sample",
    )(u, s_re, s_im, m, w, v, lam, d)


def _ssm_discretise(a_re, a_im, log_step):
    dt = jnp.exp(log_step)
    mag = jnp.exp(a_re * dt)
    l_re, l_im = mag * jnp.cos(a_im * dt), mag * jnp.sin(a_im * dt)
    den = a_re * a_re + a_im * a_im
    n_re, n_im = l_re - 1.0, l_im
    return l_re, l_im, (n_re * a_re + n_im * a_im) / den, (n_im * a_re - n_re * a_im) / den


def _complex_powers(l_re, l_im, n):
    p_re, p_im = [jnp.ones_like(l_re)], [jnp.zeros_like(l_re)]
    for _ in range(n):
        p_re, p_im = p_re + [p_re[-1] * l_re - p_im[-1] * l_im], p_im + [p_re[-1] * l_im + p_im[-1] * l_re]
    return p_re, p_im


def _ssm_tables_kernel(ac_re_ref, ac_im_ref, lsc_ref, ar_re_ref, ar_im_ref, lsr_ref, b_re_ref, b_im_ref,
                       c_re_ref, c_im_ref, m_ref, w_ref, v_ref, lam_ref, lam_half_ref, *, lc):
    hi = lax.Precision.HIGHEST
    l_re, l_im, cf_re, cf_im = _ssm_discretise(ac_re_ref[0], ac_im_ref[0], lsc_ref[0])
    b_re, b_im, c_re, c_im = b_re_ref[0], b_im_ref[0], c_re_ref[0], c_im_ref[0]
    bb_re = cf_re * b_re - cf_im * b_im
    bb_im = cf_re * b_im + cf_im * b_re
    p_re, p_im = _complex_powers(l_re, l_im, lc)
    lag = []
    for k in range(lc):
        et_re = (p_re[k] * bb_re - p_im[k] * bb_im).T
        et_im = (p_re[k] * bb_im + p_im[k] * bb_re).T
        j = lc - 1 - k
        w_ref[0, j * LANES:(j + 1) * LANES, :STATE_LANES] = et_re.astype(BF16)
        w_ref[0, j * LANES:(j + 1) * LANES, STATE_LANES:] = et_im.astype(BF16)
        lag.append((jnp.dot(et_re, c_re, precision=hi, preferred_element_type=F32)
                    - jnp.dot(et_im, c_im, precision=hi, preferred_element_type=F32)).astype(BF16))
        v_ref[0, :STATE_LANES, k * LANES:(k + 1) * LANES] = (p_re[k + 1] * c_re - p_im[k + 1] * c_im).astype(BF16)
        v_ref[0, STATE_LANES:, k * LANES:(k + 1) * LANES] = (-(p_im[k + 1] * c_re + p_re[k + 1] * c_im)).astype(BF16)
    zero = jnp.zeros((LANES, LANES), BF16)
    for j in range(lc):
        for jj in range(lc):
            m_ref[0, j * LANES:(j + 1) * LANES, jj * LANES:(jj + 1) * LANES] = lag[jj - j] if jj >= j else zero
    r_re, r_im, _, _ = _ssm_discretise(ar_re_ref[0], ar_im_ref[0], lsr_ref[0])
    q_re, q_im = _complex_powers(r_re, r_im, lc)
    lam_ref[0] = jnp.concatenate([q_re[lc], q_im[lc]], axis=1)
    lam_half_ref[0] = jnp.concatenate([q_re[lc // 2], q_im[lc // 2]], axis=1)


def _ssm_tables(a_re, a_im, log_step, b_re, b_im, c_re, c_im, lc):
    g, p, h = b_re.shape
    nbk, r = N_LANE_BLOCKS, GROUPS_PER_LANE_BLOCK
    ls = jnp.broadcast_to(log_step[:, None], (g, p))
    cols = [x.reshape(nbk, STATE_LANES, 1) for x in (a_re, a_im, ls)]
    rows = [x.reshape(nbk, 1, STATE_LANES) for x in (a_re, a_im, ls)]
    eye = jnp.eye(r, dtype=F32)[None, :, None, :, None]

    def block_diag(x):
        return (x[:, :, :, None, :] * eye).reshape(nbk, STATE_LANES, LANES)

    mats = [block_diag(b_re.reshape(nbk, r, p, h)), block_diag(b_im.reshape(nbk, r, p, h)),
            block_diag(c_re.reshape(nbk, r, h, p).transpose(0, 1, 3, 2)),
            block_diag(c_im.reshape(nbk, r, h, p).transpose(0, 1, 3, 2))]
    spec = lambda shape: pl.BlockSpec((1,) + shape, lambda b: (b, 0, 0))
    n = lc * LANES
    return pl.pallas_call(
        functools.partial(_ssm_tables_kernel, lc=lc),
        grid=(nbk,),
        in_specs=[spec((STATE_LANES, 1))] * 3 + [spec((1, STATE_LANES))] * 3 + [spec((STATE_LANES, LANES))] * 4,
        out_specs=[spec((n, n)), spec((n, 2 * STATE_LANES)), spec((2 * STATE_LANES, n)),
                   spec((1, 2 * STATE_LANES)), spec((1, 2 * STATE_LANES))],
        out_shape=[jax.ShapeDtypeStruct((nbk, n, n), BF16), jax.ShapeDtypeStruct((nbk, n, 2 * STATE_LANES), BF16),
                   jax.ShapeDtypeStruct((nbk, 2 * STATE_LANES, n), BF16),
                   jax.ShapeDtypeStruct((nbk, 1, 2 * STATE_LANES), F32),
                   jax.ShapeDtypeStruct((nbk, 1, 2 * STATE_LANES), F32)],
        compiler_params=_params(("arbitrary",)),
        name="ssm_tables",
    )(*cols, *rows, *mats)


def _nt_dot(a, b):
    return lax.dot_general(a, b, (((1,), (1,)), ((), ())), preferred_element_type=F32)


def _swa_prompt_kernel(sink_ref, qt_ref, kp_ref, kc_ref, vtp_ref, vtc_ref, mstd_ref, mfirst_ref, g_ref, ot_ref, *, nblk):
    first_tile = pl.program_id(1) == 0
    kcat = jnp.concatenate([kp_ref[...], kc_ref[...]], axis=0).astype(BF16)
    vcat_t = jnp.concatenate([vtp_ref[...], vtc_ref[...]], axis=1).astype(BF16)
    keys = lambda j: slice(j * WINDOW, (j + 2) * WINDOW)
    zeros = jnp.zeros((HEAD_DIM, WINDOW), BF16)
    sinks = [jnp.concatenate([jnp.full((1, WINDOW), sink_ref[kh * GQA_GROUP + g], F32) for g in range(GQA_GROUP)],
                             axis=1) for kh in range(N_KV_HEADS)]
    chains = [(j, kh) for j in range(nblk) for kh in range(N_KV_HEADS)]

    def scores(j, kh):
        def rhs(g):
            h = kh * GQA_GROUP + g
            q = qt_ref[h * HEAD_DIM:(h + 1) * HEAD_DIM, j * WINDOW:(j + 1) * WINDOW]
            return jnp.concatenate([q, zeros] if kh == 0 else [zeros, q], axis=0)
        r = jnp.concatenate([rhs(g) for g in range(GQA_GROUP)], axis=1)
        vmask = mstd_ref[...]
        if j == 0:
            vmask = jnp.where(first_tile, mfirst_ref[...], vmask)
        valid = jnp.tile(vmask, (1, GQA_GROUP)) > 0.0
        return jnp.where(valid, jnp.dot(kcat[keys(j)], r, preferred_element_type=F32), NEG_INF)

    def softmax(kh, sc):
        m = jnp.maximum(jnp.max(sc, axis=0, keepdims=True), sinks[kh])
        e = jnp.exp(sc - m)
        return e.astype(BF16), 1.0 / (jnp.sum(e, axis=0, keepdims=True) + jnp.exp(sinks[kh] - m))

    sc = [scores(*c) for c in chains]
    pr = [softmax(kh, x) for (j, kh), x in zip(chains, sc)]
    out = [jnp.dot(vcat_t[kh * HEAD_DIM:(kh + 1) * HEAD_DIM, keys(j)], e, preferred_element_type=F32) * inv
           for (j, kh), (e, inv) in zip(chains, pr)]
    gain = g_ref[...]
    for j in range(nblk):
        o = jnp.concatenate([out[j * N_KV_HEADS + kh][:, g * WINDOW:(g + 1) * WINDOW]
                             for kh in range(N_KV_HEADS) for g in range(GQA_GROUP)], axis=0)
        scale = lax.rsqrt(jnp.mean(o * o, axis=0, keepdims=True) + RMS_EPS)
        ot_ref[:, j * WINDOW:(j + 1) * WINDOW] = (o * scale * gain).astype(BF16)


def _swa_masks():
    kj = jnp.arange(2 * WINDOW)[:, None]
    diff = jnp.arange(WINDOW)[None, :] + WINDOW - kj
    std = (diff >= 0) & (diff <= WINDOW)
    return std.astype(F32), (std & (kj >= WINDOW)).astype(F32)


def _swa_prompt(qt, k, vt, sinks, out_g, n, seq, tq):
    nt, nblk = seq // tq, tq // WINDOW
    mstd, mfirst = _swa_masks()
    gain = jnp.broadcast_to(out_g.reshape(Q_WIDTH, 1), (Q_WIDTH, WINDOW))
    prev_blk = lambda b, i: jnp.maximum((b * nt + i) * nblk - 1, b * nt * nblk)
    cur_t = lambda r: pl.BlockSpec((r, tq), lambda b, i: (0, b * nt + i))
    return pl.pallas_call(
        functools.partial(_swa_prompt_kernel, nblk=nblk),
        grid=(n, nt),
        in_specs=[pl.BlockSpec(memory_space=pltpu.SMEM), cur_t(Q_WIDTH),
                  pl.BlockSpec((WINDOW, KV_WIDTH), lambda b, i: (prev_blk(b, i), 0)),
                  pl.BlockSpec((tq, KV_WIDTH), lambda b, i: (b * nt + i, 0)),
                  pl.BlockSpec((KV_WIDTH, WINDOW), lambda b, i: (0, prev_blk(b, i))), cur_t(KV_WIDTH),
                  _const_spec(mstd.shape), _const_spec(mfirst.shape), _const_spec(gain.shape)],
        out_specs=cur_t(Q_WIDTH),
        out_shape=jax.ShapeDtypeStruct((Q_WIDTH, n * seq), BF16),
        compiler_params=_params(("arbitrary", "arbitrary")),
        name="swa_prompt",
    )(sinks, qt, k, k, vt, vt, mstd, mfirst, gain)


def _swa_sample_kernel(sink_ref, q_ref, kn_ref, vn_ref, ck_ref, cv_ref, g_ref, o_ref, nk_ref, nv_ref, *, ns, t):
    rows = GQA_GROUP * t
    tok = lax.broadcasted_iota(jnp.int32, (rows, WINDOW), 0) % t
    valid_c = lax.broadcasted_iota(jnp.int32, (rows, WINDOW), 1) >= tok
    tok_n = lax.broadcasted_iota(jnp.int32, (rows, t), 0) % t
    valid_n = lax.broadcasted_iota(jnp.int32, (rows, t), 1) <= tok_n
    gain = g_ref[...]

    sinks = [jnp.concatenate([jnp.full((t, 1), sink_ref[kh * GQA_GROUP + g], F32) for g in range(GQA_GROUP)], axis=0)
             for kh in range(N_KV_HEADS)]
    heads = [(s, kh) for s in range(ns) for kh in range(N_KV_HEADS)]
    hs = lambda kh: slice(kh * HEAD_DIM, (kh + 1) * HEAD_DIM)

    newest = lax.broadcasted_iota(jnp.int32, (KV_WIDTH, WINDOW), 1) >= WINDOW - t
    pad = jnp.zeros((WINDOW - 2 * t, KV_WIDTH), F32)

    def shifted(cache_ref, new_ref, s):
        new_rows = jnp.concatenate([pad, jnp.zeros((t, KV_WIDTH), F32), new_ref[s]], axis=0)
        return jnp.where(newest, new_rows.T, pltpu.roll(cache_ref[s], WINDOW - t, 1))

    for s in range(ns):
        nk_ref[s] = shifted(ck_ref, kn_ref, s)
        nv_ref[s] = shifted(cv_ref, vn_ref, s)

    def scores(s, kh):
        q = q_ref[s].astype(F32)
        q4 = jnp.concatenate([q[:, (kh * GQA_GROUP + g) * HEAD_DIM:(kh * GQA_GROUP + g + 1) * HEAD_DIM]
                              for g in range(GQA_GROUP)], axis=0).astype(BF16)
        sc_c = jnp.where(valid_c, jnp.dot(q4, ck_ref[s, hs(kh), :].astype(BF16), preferred_element_type=F32), NEG_INF)
        sc_n = jnp.where(valid_n, _nt_dot(q4, kn_ref[s, :, hs(kh)].astype(BF16)), NEG_INF)
        return sc_c, sc_n

    def softmax(kh, sc_c, sc_n):
        m = jnp.maximum(jnp.maximum(jnp.max(sc_c, axis=-1, keepdims=True),
                                    jnp.max(sc_n, axis=-1, keepdims=True)), sinks[kh])
        e_c, e_n = jnp.exp(sc_c - m), jnp.exp(sc_n - m)
        inv = 1.0 / (jnp.sum(e_c, axis=-1, keepdims=True) + jnp.sum(e_n, axis=-1, keepdims=True)
                     + jnp.exp(sinks[kh] - m))
        return (e_c * inv).astype(BF16), (e_n * inv).astype(BF16)

    def values(s, kh, p_c, p_n):
        return (_nt_dot(p_c, cv_ref[s, hs(kh), :].astype(BF16))
                + jnp.dot(p_n, vn_ref[s, :, hs(kh)].astype(BF16), preferred_element_type=F32))

    sc = [scores(s, kh) for s, kh in heads]
    pr = [softmax(kh, *x) for (s, kh), x in zip(heads, sc)]
    o4 = [values(s, kh, *x) for (s, kh), x in zip(heads, pr)]
    for s in range(ns):
        o = jnp.concatenate([o4[s * N_KV_HEADS + kh][g * t:(g + 1) * t]
                             for kh in range(N_KV_HEADS) for g in range(GQA_GROUP)], axis=1)
        o_ref[s] = _rms(o, gain).astype(BF16)


def _swa_sample(q3, kn3, vn3, ck, cv, sinks, out_g, sb):
    ns, t, _ = q3.shape
    blk = lambda a: pl.BlockSpec((sb,) + a.shape[1:], lambda i: (i, 0, 0))
    kern = functools.partial(_swa_sample_kernel, ns=sb, t=t)
    return pl.pallas_call(
        kern,
        grid=(ns // sb,),
        in_specs=[pl.BlockSpec(memory_space=pltpu.SMEM), blk(q3), blk(kn3), blk(vn3), blk(ck), blk(cv),
                  pl.BlockSpec((1, Q_WIDTH), lambda i: (0, 0))],
        out_specs=[blk(q3), blk(ck), blk(cv)],
        out_shape=[jax.ShapeDtypeStruct(q3.shape, BF16), jax.ShapeDtypeStruct(ck.shape, F32),
                   jax.ShapeDtypeStruct(cv.shape, F32)],
        compiler_params=_params(("arbitrary",)),
        name="swa_sample",
    )(sinks, q3, kn3, vn3, ck, cv, out_g)


def _gelu_tanh(x):
    return 0.5 * x * (1.0 + jnp.tanh(math.sqrt(2.0 / math.pi) * (x + 0.044715 * (x * x * x))))


def _merge_tile(h, y, at, wglu_ref, bglu, sg, wout_ref, post_g, xa_g, wq_ref, at_transposed=False):
    g = _gelu_tanh(y)
    lin = jnp.dot(g.astype(BF16), wglu_ref[...], preferred_element_type=F32) + bglu
    y_ssm = g * (1.0 / (1.0 + jnp.exp(-lin)))
    ssm_n = _rms(y_ssm, sg).astype(BF16)
    at_dims = (((0,), (0,)), ((), ())) if at_transposed else (((1,), (0,)), ((), ()))
    mixed = (jnp.dot(ssm_n, wout_ref[:SSM_WIDTH, :], preferred_element_type=F32)
             + lax.dot_general(at, wout_ref[SSM_WIDTH:, :], at_dims, preferred_element_type=F32))
    h2 = h + _rms(mixed, post_g)
    qm = jnp.dot(_rms(h2, xa_g).astype(BF16), wq_ref[...], preferred_element_type=F32)
    return h2, (qm * (MEM_HEAD_DIM ** -0.5)).astype(BF16)


def _merge_kernel(h_ref, y_ref, at_ref, wglu_ref, bglu_ref, sg_ref, wout_ref, post_g_ref, xa_g_ref, wq_ref,
                  h2_ref, qm_ref):
    h2_ref[...], qm_ref[...] = _merge_tile(h_ref[...], y_ref[...], at_ref[...], wglu_ref, bglu_ref[...], sg_ref[...],
                                           wout_ref, post_g_ref[...], xa_g_ref[...], wq_ref)


def _merge_specs(wglu, wout, wq):
    return [_const_spec(wglu.shape), _const_spec((1, SSM_WIDTH)), _const_spec((1, SSM_WIDTH)), _const_spec(wout.shape),
            _const_spec((1, D_MODEL)), _const_spec((1, D_MODEL)), _const_spec(wq.shape)]


def _merge(h, y, at, merge_w, tm):
    t = h.shape[0]
    row = lambda w: pl.BlockSpec((tm, w), lambda i: (i, 0))
    return pl.pallas_call(
        _merge_kernel,
        grid=(t // tm,),
        in_specs=[row(D_MODEL), row(SSM_WIDTH), row(Q_WIDTH)] + _merge_specs(merge_w[0], merge_w[3], merge_w[6]),
        out_specs=[row(D_MODEL), row(D_MODEL)],
        out_shape=[jax.ShapeDtypeStruct((t, D_MODEL), F32), jax.ShapeDtypeStruct((t, D_MODEL), BF16)],
        compiler_params=_params(("arbitrary",)),
        name="merge",
    )(h, y, at, *merge_w)


def _mem_attn_sample_kernel(q_ref, k_ref, v_ref, o_ref, *, gb, t):
    rows = t * MEM_HEADS
    halves = MEM_HEAD_DIM // LANES
    kv_rows = N_MEM * halves * MEM_HEADS
    period = halves * MEM_HEADS
    lane = lax.broadcasted_iota(jnp.int32, (rows, kv_rows), 1) % period
    head = lax.broadcasted_iota(jnp.int32, (rows, kv_rows), 0) % MEM_HEADS
    in_half = [lane == head + hf * MEM_HEADS for hf in range(halves)]
    for b in range(gb):
        kb = k_ref[b].astype(BF16)
        vb = v_ref[b].astype(BF16)
        part = _nt_dot(q_ref[b], kb)
        sc = jnp.where(in_half[0], part[:rows], 0.0)
        for hf in range(1, halves):
            sc = sc + pltpu.roll(jnp.where(in_half[hf], part[hf * rows:(hf + 1) * rows], 0.0),
                                 kv_rows - hf * MEM_HEADS, 1)
        sc = jnp.where(in_half[0], sc, -jnp.inf)
        e = jnp.exp(sc - jnp.max(sc, axis=-1, keepdims=True))
        inv = 1.0 / jnp.sum(e, axis=-1, keepdims=True)
        e_all = jnp.concatenate([e] + [pltpu.roll(e, hf * MEM_HEADS, 1) for hf in range(1, halves)], axis=0)
        o = jnp.dot(e_all.astype(BF16), vb, preferred_element_type=F32)
        o_ref[b] = (o * jnp.concatenate([inv] * halves, axis=0)).astype(BF16)


def _mem_attn_sample(qm, cache_k, cache_v, n_s, t_s, gb):
    halves = MEM_HEAD_DIM // LANES
    rows = halves * t_s * MEM_HEADS
    kv_rows = N_MEM * halves * MEM_HEADS

    def stored_rows(c):
        c = c.reshape(n_s, N_MEM, MEM_HEADS, halves, LANES).transpose(0, 1, 3, 2, 4)
        return c.reshape(n_s, kv_rows, LANES)

    q = qm.reshape(n_s, t_s, MEM_HEADS, halves, LANES).transpose(0, 3, 1, 2, 4).reshape(n_s, rows, LANES)
    blk = lambda r: pl.BlockSpec((gb, r, LANES), lambda i: (i, 0, 0))
    o = pl.pallas_call(
        functools.partial(_mem_attn_sample_kernel, gb=gb, t=t_s),
        grid=(n_s // gb,),
        in_specs=[blk(rows), blk(kv_rows), blk(kv_rows)],
        out_specs=blk(rows),
        out_shape=jax.ShapeDtypeStruct((n_s, rows, LANES), BF16),
        compiler_params=_params(("arbitrary",)),
        name="mem_attn_sample",
    )(q, stored_rows(cache_k), stored_rows(cache_v))
    o = o.reshape(n_s, halves, t_s, MEM_HEADS, LANES).transpose(0, 2, 3, 1, 4)
    return o.reshape(n_s * t_s, D_MODEL)


def _mem_heads(q, k, v):
    outs = []
    for hh in range(MEM_HEADS):
        hs = slice(hh * MEM_HEAD_DIM, (hh + 1) * MEM_HEAD_DIM)
        sc = _nt_dot(q[:, hs], k[:, hs])
        e = jnp.exp(sc - jnp.max(sc, axis=-1, keepdims=True))
        inv = 1.0 / jnp.sum(e, axis=-1, keepdims=True)
        outs.append((jnp.dot(e.astype(BF16), v[:, hs], preferred_element_type=F32) * inv).astype(BF16))
    return jnp.concatenate(outs, axis=1)


def _ffn_out_tile(h2, om, wo_ref, xa_post, pre_g, wg_ref, wu_ref, wd_ref, post_g, act_ref):
    c = jnp.dot(om, wo_ref[...], preferred_element_type=F32)
    h3 = h2 + _rms(c, xa_post)
    return _ffn_tile(h3, pre_g, wg_ref, wu_ref, wd_ref, post_g, act_ref)


def _ffn_out_specs(wo, ffn_w):
    return [_const_spec(wo.shape), _const_spec((1, D_MODEL)), _const_spec((1, D_MODEL)),
            *[_const_spec(w.shape) for w in ffn_w], _const_spec((1, D_MODEL))]


def _ffn_out_kernel(h_ref, o_ref, wo_ref, xa_post_ref, pre_g_ref, wg_ref, wu_ref, wd_ref, post_g_ref, out_ref, act_ref):
    out_ref[...] = _ffn_out_tile(h_ref[...], o_ref[...], wo_ref, xa_post_ref[...], pre_g_ref[...], wg_ref, wu_ref,
                                 wd_ref, post_g_ref[...], act_ref)


def _ffn_out(h, o, out_w, tm):
    t = h.shape[0]
    row = pl.BlockSpec((tm, D_MODEL), lambda i: (i, 0))
    return pl.pallas_call(
        _ffn_out_kernel,
        grid=(t // tm,),
        in_specs=[row, row] + _ffn_out_specs(out_w[0], out_w[3:6]),
        out_specs=row,
        out_shape=jax.ShapeDtypeStruct((t, D_MODEL), F32),
        scratch_shapes=[pltpu.VMEM((tm, D_FF), BF16)],
        compiler_params=_params(("arbitrary",)),
        name="ffn_out",
    )(h, o, *out_w)


def _post_kernel(h_ref, y_ref, at_ref, k_ref, v_ref,
                 wglu_ref, bglu_ref, sg_ref, wout_ref, post_g_ref, xa_g_ref, wq_ref,
                 wo_ref, xa_post_ref, pre_g_ref, wg_ref, wu_ref, wd_ref, ffn_post_ref, out_ref, act_ref):
    y = jnp.concatenate([y_ref[b] for b in range(N_LANE_BLOCKS)], axis=1)
    h2, qm = _merge_tile(h_ref[...], y, at_ref[...], wglu_ref, bglu_ref[...], sg_ref[...],
                         wout_ref, post_g_ref[...], xa_g_ref[...], wq_ref, at_transposed=True)
    om = _mem_heads(qm, k_ref[0], v_ref[0])
    out_ref[...] = _ffn_out_tile(h2, om, wo_ref, xa_post_ref[...], pre_g_ref[...], wg_ref, wu_ref, wd_ref,
                                 ffn_post_ref[...], act_ref)


def _post(h, y, at, k3, v3, merge_w, out_w, tm):
    t = h.shape[0]
    tiles_per_batch = t // k3.shape[0] // tm
    row = lambda w: pl.BlockSpec((tm, w), lambda i: (i, 0))
    kv = pl.BlockSpec((1, N_MEM, D_MODEL), lambda i: (i // tiles_per_batch, 0, 0))
    return pl.pallas_call(
        _post_kernel,
        grid=(t // tm,),
        in_specs=([row(D_MODEL), pl.BlockSpec((N_LANE_BLOCKS, tm, LANES), lambda i: (0, i, 0)),
                   pl.BlockSpec((Q_WIDTH, tm), lambda i: (0, i)), kv, kv]
                  + _merge_specs(merge_w[0], merge_w[3], merge_w[6]) + _ffn_out_specs(out_w[0], out_w[3:6])),
        out_specs=row(D_MODEL),
        out_shape=jax.ShapeDtypeStruct((t, D_MODEL), F32),
        scratch_shapes=[pltpu.VMEM((tm, D_FF), BF16)],
        compiler_params=_params(("arbitrary",)),
        name="post",
    )(h, y, at, k3, v3, *merge_w, *out_w)


def _mem_kv_kernel(m_ref, g_ref, wkv_ref, k_ref, v_ref, kb_ref, vb_ref):
    kv = jnp.dot(_rms(m_ref[...], g_ref[...]).astype(BF16), wkv_ref[...], preferred_element_type=F32)
    k_ref[...] = kv[:, :D_MODEL]
    v_ref[...] = kv[:, D_MODEL:]
    kb_ref[...] = kv[:, :D_MODEL].astype(BF16)
    vb_ref[...] = kv[:, D_MODEL:].astype(BF16)


def _mem_kv(mem, g, wkv, tm):
    t = mem.shape[0]
    row = pl.BlockSpec((tm, D_MODEL), lambda i: (i, 0))
    return pl.pallas_call(
        _mem_kv_kernel,
        grid=(t // tm,),
        in_specs=[row, _const_spec((1, D_MODEL)), _const_spec(wkv.shape)],
        out_specs=[row] * 4,
        out_shape=[jax.ShapeDtypeStruct((t, D_MODEL), F32)] * 2 + [jax.ShapeDtypeStruct((t, D_MODEL), BF16)] * 2,
        compiler_params=_params(("arbitrary",)),
        name="mem_kv",
    )(mem, g, wkv)


def _rope_tables(pos):
    half = ROPE_DIM // 2
    inv = ROPE_THETA ** (-jnp.arange(half, dtype=F32) * (2.0 / ROPE_DIM))
    ang = pos.astype(F32)[:, None] * inv[None, :]
    cos, sin = jnp.cos(ang), jnp.sin(ang)
    n = pos.shape[0]
    pad = jnp.zeros((n, HEAD_DIM - ROPE_DIM), F32)
    zero = jnp.zeros((n, half), F32)
    cos_h = jnp.concatenate([cos, cos, pad + 1.0], axis=1)
    lo_h = jnp.concatenate([-sin, zero, pad], axis=1)
    hi_h = jnp.concatenate([zero, sin, pad], axis=1)
    rep = LANES // HEAD_DIM
    return tuple(jnp.tile(a, (1, rep)) for a in (cos_h, lo_h, hi_h)), (cos.T, sin.T)


def _ffn_weights(w_gate, w_up, w_down):
    return w_gate.astype(BF16), w_up.astype(BF16), w_down.astype(BF16)


def _lane_block_states(st, n):
    st = st.reshape(N_LANE_BLOCKS, n, 2, GROUPS_PER_LANE_BLOCK, SSM_STATE).transpose(2, 1, 0, 3, 4)
    st = st.reshape(2, n, N_SSM_GROUPS, SSM_STATE)
    return st[0], st[1]


def kernel(x_prompt, x_sample, state_ssm_re, state_ssm_im, cache_swa_k, cache_swa_v, cache_mem_k, cache_mem_v, mem_prompt, ffn1_pre_g, ffn1_w_gate, ffn1_w_up, ffn1_w_down, ffn1_post_g, mix_pre_g, w_in, ssm_a_re, ssm_a_im, ssm_log_step, ssm_b_re, ssm_b_im, ssm_c_re, ssm_c_im, ssm_d, ssm_w_glu, ssm_b_glu, attn_sinks, ssm_out_g, attn_out_g, w_out, mix_post_g, mem_norm_g, w_mem_q, w_mem_k, w_mem_v, w_mem_o, xa_pre_g, xa_post_g, ffn2_pre_g, ffn2_w_gate, ffn2_w_up, ffn2_w_down, ffn2_post_g):
    n_p, s_p, _ = x_prompt.shape
    n_s, t_s, _ = x_sample.shape
    tm = 512
    row = lambda a: a.reshape(1, -1).astype(F32)

    ffn1_w = _ffn_weights(ffn1_w_gate, ffn1_w_up, ffn1_w_down)
    win = w_in.astype(BF16)
    merge_w = (ssm_w_glu.astype(BF16), row(ssm_b_glu), row(ssm_out_g), w_out.astype(BF16), row(mix_post_g),
               row(xa_pre_g), w_mem_q.astype(BF16))
    out_w = (w_mem_o.astype(BF16), row(xa_post_g), row(ffn2_pre_g),
             *_ffn_weights(ffn2_w_gate, ffn2_w_up, ffn2_w_down), row(ffn2_post_g))
    wkv = jnp.concatenate([w_mem_k, w_mem_v], axis=1).astype(BF16)
    d_row = row(ssm_d)
    ssm_args = (ssm_a_re.astype(F32), ssm_a_im.astype(F32), ssm_log_step.astype(F32), ssm_b_re.astype(F32),
                ssm_b_im.astype(F32), ssm_c_re.astype(F32), ssm_c_im.astype(F32))
    sinks = attn_sinks.astype(F32)

    pm_k, pm_v, pm_kb, pm_vb = _mem_kv(mem_prompt.reshape(n_p * N_MEM, D_MODEL), row(mem_norm_g), wkv, N_MEM)

    def tokenwise_in(x2, pos_tab):
        return _ffn_in(x2, row(ffn1_pre_g), ffn1_w, row(ffn1_post_g), row(mix_pre_g), win, *pos_tab, tm)

    lc_p = 2 * t_s
    ssm_m, ssm_w, ssm_v, lam_p, lam_s = _ssm_tables(*ssm_args, lc_p)
    o1 = SSM_WIDTH + Q_WIDTH
    wuk = jnp.concatenate([win[:, :SSM_WIDTH], win[:, o1:o1 + KV_WIDTH]], axis=1)
    wqv_t = jnp.concatenate([win[:, SSM_WIDTH:o1], win[:, o1 + KV_WIDTH:]], axis=1).T
    tab_p, tab_p_t = _rope_tables(jnp.arange(s_p, dtype=jnp.int32))
    h1, u, k, qt, vt = _ffn_in_t(x_prompt.reshape(n_p * s_p, D_MODEL), row(ffn1_pre_g), ffn1_w, row(ffn1_post_g),
                                 row(mix_pre_g), wuk, wqv_t, *tab_p, *tab_p_t, tm)
    y4, st_p = _ssm_prompt(u.reshape(N_LANE_BLOCKS, n_p, s_p, LANES), ssm_m, ssm_w, ssm_v, lam_p,
                           d_row.reshape(N_LANE_BLOCKS, 1, LANES), lc_p, 1024, 2)
    at = _swa_prompt(qt, k, vt, sinks, attn_out_g.astype(F32), n_p, s_p, 512)
    y_prompt = _post(h1, y4.reshape(N_LANE_BLOCKS, n_p * s_p, LANES), at, pm_kb.reshape(n_p, N_MEM, D_MODEL),
                     pm_vb.reshape(n_p, N_MEM, D_MODEL), merge_w, out_w, tm).reshape(n_p, s_p, D_MODEL)
    p_sre, p_sim = _lane_block_states(st_p, n_p)
    p_wk = k.reshape(n_p, s_p, N_KV_HEADS, HEAD_DIM)[:, -WINDOW:]
    p_wv = vt.reshape(N_KV_HEADS, HEAD_DIM, n_p, s_p)[..., -WINDOW:].transpose(2, 3, 0, 1)

    pos_s = jnp.tile(PAST_LEN + jnp.arange(t_s, dtype=jnp.int32), n_s)
    h1s, us, qs, ks, vs = tokenwise_in(x_sample.reshape(n_s * t_s, D_MODEL), _rope_tables(pos_s)[0])
    ys, s_sre, s_sim = _ssm_sample(us, state_ssm_re.reshape(n_s, -1).astype(F32), state_ssm_im.reshape(n_s, -1).astype(F32),
                                   ssm_m, ssm_w, ssm_v, lam_s, d_row, t_s)
    win_len = cache_swa_k.shape[1]
    ats, s_wk, s_wv = _swa_sample(qs.reshape(n_s, t_s, Q_WIDTH), ks.reshape(n_s, t_s, KV_WIDTH), vs.reshape(n_s, t_s, KV_WIDTH),
                                  cache_swa_k.transpose(0, 2, 3, 1).reshape(n_s, KV_WIDTH, win_len),
                                  cache_swa_v.transpose(0, 2, 3, 1).reshape(n_s, KV_WIDTH, win_len),
                                  sinks, row(attn_out_g), 16)
    s_wk = s_wk.reshape(n_s, N_KV_HEADS, HEAD_DIM, win_len).transpose(0, 3, 1, 2)
    s_wv = s_wv.reshape(n_s, N_KV_HEADS, HEAD_DIM, win_len).transpose(0, 3, 1, 2)
    h2s, qms = _merge(h1s, ys, ats.reshape(n_s * t_s, Q_WIDTH), merge_w, tm)
    oms = _mem_attn_sample(qms, cache_mem_k, cache_mem_v, n_s, t_s, 4)
    y_sample = _ffn_out(h2s, oms, out_w, tm).reshape(n_s, t_s, D_MODEL)

    return (y_prompt, y_sample, p_sre, p_sim, p_wk, p_wv,
            pm_k.reshape(n_p, N_MEM, MEM_HEADS, MEM_HEAD_DIM), pm_v.reshape(n_p, N_MEM, MEM_HEADS, MEM_HEAD_DIM),
            s_sre.reshape(n_s, N_SSM_GROUPS, SSM_STATE), s_sim.reshape(n_s, N_SSM_GROUPS, SSM_STATE),
            s_wk, s_wv)
```

```python
import functools
import math

import jax
import jax.numpy as jnp
from jax import lax
from jax.experimental import pallas as pl
from jax.experimental.pallas import tpu as pltpu

F32 = jnp.float32
BF16 = jnp.bfloat16

D_MODEL = 1024
PAST_LEN = 16384
SSM_WIDTH = 512
SSM_GROUP = 16
N_SSM_GROUPS = 32
SSM_STATE = 64
HEAD_DIM = 64
N_HEADS = 8
N_KV_HEADS = 2
GQA_GROUP = 4
Q_WIDTH = 512
KV_WIDTH = 128
WINDOW = 128
ROPE_THETA = 500000.0
ROPE_DIM = 16
N_MEM = 256
MEM_HEADS = 4
MEM_HEAD_DIM = 256
D_FF = 2816
RMS_EPS = 1e-6
IN_WIDTH = SSM_WIDTH + Q_WIDTH + 2 * KV_WIDTH
NEG_INF = -1e30

LANES = 128
FF_CHUNK = 256
N_FF_CHUNKS = D_FF // FF_CHUNK
GROUPS_PER_LANE_BLOCK = LANES // SSM_GROUP
N_LANE_BLOCKS = SSM_WIDTH // LANES
STATE_LANES = GROUPS_PER_LANE_BLOCK * SSM_STATE
VMEM_LIMIT = 56 * 1024 * 1024


def _rms(x, g):
    return x * lax.rsqrt(jnp.mean(x * x, axis=-1, keepdims=True) + RMS_EPS) * g


def _const_spec(shape):
    nd = len(shape)
    return pl.BlockSpec(shape, lambda *_: (0,) * nd, pipeline_mode=pl.Buffered(1))


def _params(sem):
    return pltpu.CompilerParams(dimension_semantics=sem, vmem_limit_bytes=VMEM_LIMIT)


def _ffn_tile(x, pre_g, wg_ref, wu_ref, wd_ref, post_g, act_ref, after_chunk=None):
    xn = _rms(x, pre_g).astype(BF16)
    for c in range(N_FF_CHUNKS):
        if after_chunk is not None and c == after_chunk[0] + 1:
            after_chunk[1]()
        cols = slice(c * FF_CHUNK, (c + 1) * FF_CHUNK)
        gate = jnp.dot(xn, wg_ref[:, cols], preferred_element_type=F32)
        up = jnp.dot(xn, wu_ref[:, cols], preferred_element_type=F32)
        act = gate * (1.0 / (1.0 + jnp.exp(-gate))) * up
        act_ref[:, cols] = act.astype(BF16)
    down = jnp.dot(act_ref[...], wd_ref[...], preferred_element_type=F32)
    return x + 0.5 * _rms(down, post_g)


def _rope(x, cos, sin_lo, sin_hi):
    w = x.shape[1]
    half = ROPE_DIM // 2
    return (x * cos + pltpu.roll(x, w - half, 1) * sin_lo + pltpu.roll(x, half, 1) * sin_hi)


def _ffn_in_kernel(x_ref, pre_g_ref, wg_ref, wu_ref, wd_ref, post_g_ref, mix_g_ref, win_ref,
                   cos_ref, slo_ref, shi_ref,
                   h_ref, u_ref, q_ref, k_ref, v_ref, act_ref):
    h = _ffn_tile(x_ref[...], pre_g_ref[...], wg_ref, wu_ref, wd_ref, post_g_ref[...], act_ref)
    h_ref[...] = h
    z = jnp.dot(_rms(h, mix_g_ref[...]).astype(BF16), win_ref[...], preferred_element_type=F32)
    u_ref[...] = z[:, :SSM_WIDTH]
    o1 = SSM_WIDTH + Q_WIDTH
    cos, slo, shi = cos_ref[...], slo_ref[...], shi_ref[...]
    rep = Q_WIDTH // LANES
    q = _rope(z[:, SSM_WIDTH:o1], jnp.tile(cos, (1, rep)), jnp.tile(slo, (1, rep)), jnp.tile(shi, (1, rep)))
    q_ref[...] = (q * (HEAD_DIM ** -0.5)).astype(BF16)
    k_ref[...] = _rope(z[:, o1:o1 + KV_WIDTH], cos, slo, shi)
    v_ref[...] = z[:, o1 + KV_WIDTH:]


def _ffn_in_t_kernel(x_ref, pre_g_ref, wg_ref, wu_ref, wd_ref, post_g_ref, mix_g_ref, wuk_ref, wqv_t_ref,
                     cos_ref, slo_ref, shi_ref, cos_t_ref, sin_t_ref,
                     h_ref, u_ref, k_ref, qt_ref, vt_ref, act_ref):
    h = _ffn_tile(x_ref[...], pre_g_ref[...], wg_ref, wu_ref, wd_ref, post_g_ref[...], act_ref)
    h_ref[...] = h
    hn = _rms(h, mix_g_ref[...]).astype(BF16)
    z = jnp.dot(hn, wuk_ref[...], preferred_element_type=F32)
    for b in range(N_LANE_BLOCKS):
        u_ref[b] = z[:, b * LANES:(b + 1) * LANES]
    k_ref[...] = _rope(z[:, SSM_WIDTH:], cos_ref[...], slo_ref[...], shi_ref[...])
    zt = _nt_dot(wqv_t_ref[...], hn)
    vt_ref[...] = zt[Q_WIDTH:]
    tm = zt.shape[1]
    half = ROPE_DIM // 2
    q3 = zt[:Q_WIDTH].reshape(N_HEADS, HEAD_DIM, tm)
    x1, x2 = q3[:, :half], q3[:, half:ROPE_DIM]
    cos, sin = cos_t_ref[...][None], sin_t_ref[...][None]
    q3 = jnp.concatenate([x1 * cos - x2 * sin, x2 * cos + x1 * sin, q3[:, ROPE_DIM:]], axis=1)
    qt_ref[...] = (q3.reshape(Q_WIDTH, tm) * (HEAD_DIM ** -0.5)).astype(BF16)


def _ffn_in_t(x, pre_g, ffn_w, post_g, mix_g, wuk, wqv_t, cos, slo, shi, cos_t, sin_t, tm):
    t = x.shape[0]
    n_pos_tiles = cos.shape[0] // tm
    half = ROPE_DIM // 2
    row = lambda w: pl.BlockSpec((tm, w), lambda i: (i, 0))
    col = lambda r: pl.BlockSpec((r, tm), lambda i: (0, i))
    tab = pl.BlockSpec((tm, LANES), lambda i: (i % n_pos_tiles, 0))
    tab_t = pl.BlockSpec((half, tm), lambda i: (0, i % n_pos_tiles))
    return pl.pallas_call(
        _ffn_in_t_kernel,
        grid=(t // tm,),
        in_specs=[row(D_MODEL), _const_spec((1, D_MODEL)), *[_const_spec(w.shape) for w in ffn_w],
                  _const_spec((1, D_MODEL)), _const_spec((1, D_MODEL)), _const_spec(wuk.shape),
                  _const_spec(wqv_t.shape), tab, tab, tab, tab_t, tab_t],
        out_specs=[row(D_MODEL), pl.BlockSpec((N_LANE_BLOCKS, tm, LANES), lambda i: (0, i, 0)), row(KV_WIDTH),
                   col(Q_WIDTH), col(KV_WIDTH)],
        out_shape=[jax.ShapeDtypeStruct((t, D_MODEL), F32), jax.ShapeDtypeStruct((N_LANE_BLOCKS, t, LANES), F32),
                   jax.ShapeDtypeStruct((t, KV_WIDTH), F32), jax.ShapeDtypeStruct((Q_WIDTH, t), BF16),
                   jax.ShapeDtypeStruct((KV_WIDTH, t), F32)],
        scratch_shapes=[pltpu.VMEM((tm, D_FF), BF16)],
        compiler_params=_params(("arbitrary",)),
        name="ffn_in_t",
    )(x, pre_g, *ffn_w, post_g, mix_g, wuk, wqv_t, cos, slo, shi, cos_t, sin_t)


def _ffn_in(x, pre_g, ffn_w, post_g, mix_g, win, cos, slo, shi, tm):
    t = x.shape[0]
    n_pos_tiles = cos.shape[0] // tm
    row = lambda w: pl.BlockSpec((tm, w), lambda i: (i, 0))
    tab = pl.BlockSpec((tm, LANES), lambda i: (i % n_pos_tiles, 0))
    return pl.pallas_call(
        _ffn_in_kernel,
        grid=(t // tm,),
        in_specs=[row(D_MODEL), _const_spec((1, D_MODEL)), *[_const_spec(w.shape) for w in ffn_w],
                  _const_spec((1, D_MODEL)), _const_spec((1, D_MODEL)), _const_spec(win.shape),
                  tab, tab, tab],
        out_specs=[row(D_MODEL), row(SSM_WIDTH), row(Q_WIDTH), row(KV_WIDTH), row(KV_WIDTH)],
        out_shape=[jax.ShapeDtypeStruct((t, D_MODEL), F32), jax.ShapeDtypeStruct((t, SSM_WIDTH), F32),
                   jax.ShapeDtypeStruct((t, Q_WIDTH), BF16), jax.ShapeDtypeStruct((t, KV_WIDTH), F32),
                   jax.ShapeDtypeStruct((t, KV_WIDTH), F32)],
        scratch_shapes=[pltpu.VMEM((tm, D_FF), BF16)],
        compiler_params=_params(("arbitrary",)),
        name="ffn_in",
    )(x, pre_g, *ffn_w, post_g, mix_g, win, cos, slo, shi)


def _complex_step(s_re, s_im, l_re, l_im, x_re, x_im):
    return l_re * s_re - l_im * s_im + x_re, l_re * s_im + l_im * s_re + x_im


def _ssm_prompt_kernel(u_ref, m_ref, w_ref, v_ref, lam_ref, d_ref, y_ref, st_ref,
                       x_scr, ss_scr, s_scr, *, lc, nb, ncl, nlb):
    tt = pl.program_id(1)
    nrow = nlb * nb
    nq = STATE_LANES // LANES

    @pl.when(tt == 0)
    def _():
        s_scr[...] = jnp.zeros_like(s_scr)

    def piece(b, n, j):
        return u_ref[b, n, pl.ds(j, ncl, stride=lc), :]

    a = []
    for b in range(nlb):
        a.append(jnp.concatenate(
            [jnp.concatenate([piece(b, n, j) for j in range(lc)], axis=1) for n in range(nb)], axis=0).astype(BF16))
        x = jnp.dot(a[b], w_ref[b], preferred_element_type=F32)
        for qq in range(2 * nq):
            for n in range(nb):
                x_scr[qq, pl.ds(b * nb + n, ncl, stride=nrow), :] = x[n * ncl:(n + 1) * ncl, qq * LANES:(qq + 1) * LANES]
    lam_rows = lambda qq: jnp.concatenate(
        [jnp.broadcast_to(lam_ref[b, :, qq * LANES:(qq + 1) * LANES], (nb, LANES)) for b in range(nlb)], axis=0)
    l_re = [lam_rows(qq) for qq in range(nq)]
    l_im = [lam_rows(nq + qq) for qq in range(nq)]

    s = [s_scr[qq] for qq in range(2 * nq)]
    for c in range(ncl):
        rows = slice(c * nrow, (c + 1) * nrow)
        for qq in range(nq):
            ss_scr[qq, rows, :] = s[qq]
            ss_scr[nq + qq, rows, :] = s[nq + qq]
            s[qq], s[nq + qq] = _complex_step(s[qq], s[nq + qq], l_re[qq], l_im[qq],
                                              x_scr[qq, rows, :], x_scr[nq + qq, rows, :])
    for qq in range(2 * nq):
        s_scr[qq] = s[qq]
    for b in range(nlb):
        st_ref[b] = jnp.concatenate([s[qq][b * nb:(b + 1) * nb] for qq in range(2 * nq)], axis=1)
        s_start = jnp.concatenate(
            [jnp.concatenate([ss_scr[qq, pl.ds(b * nb + n, ncl, stride=nrow), :] for n in range(nb)], axis=0)
             for qq in range(2 * nq)], axis=1).astype(BF16)
        y = (jnp.dot(a[b], m_ref[b], preferred_element_type=F32)
             + jnp.dot(s_start, v_ref[b], preferred_element_type=F32))
        d = d_ref[b]
        for n in range(nb):
            for j in range(lc):
                y_ref[b, n, pl.ds(j, ncl, stride=lc), :] = (
                    y[n * ncl:(n + 1) * ncl, j * LANES:(j + 1) * LANES] + d * piece(b, n, j))


def _ssm_prompt(u4, m, w, v, lam, d, lc, tl, nlb):
    _, nb, seq, _ = u4.shape
    ncl = tl // lc
    nslab = 2 * STATE_LANES // LANES
    kern = functools.partial(_ssm_prompt_kernel, lc=lc, nb=nb, ncl=ncl, nlb=nlb)
    wspec = lambda a: pl.BlockSpec((nlb,) + a.shape[1:], lambda p, t: (p, 0, 0), pipeline_mode=pl.Buffered(1))
    io = pl.BlockSpec((nlb, nb, tl, LANES), lambda p, t: (p, 0, t, 0))
    return pl.pallas_call(
        kern,
        grid=(N_LANE_BLOCKS // nlb, seq // tl),
        in_specs=[io, wspec(m), wspec(w), wspec(v), wspec(lam), wspec(d)],
        out_specs=[io, pl.BlockSpec((nlb, nb, 2 * STATE_LANES), lambda p, t: (p, 0, 0))],
        out_shape=[jax.ShapeDtypeStruct(u4.shape, F32),
                   jax.ShapeDtypeStruct((N_LANE_BLOCKS, nb, 2 * STATE_LANES), F32)],
        scratch_shapes=[pltpu.VMEM((nslab, nlb * nb * ncl, LANES), F32),
                        pltpu.VMEM((nslab, nlb * nb * ncl, LANES), F32),
                        pltpu.VMEM((nslab, nlb * nb, LANES), F32)],
        compiler_params=_params(("arbitrary", "arbitrary")),
        name="ssm_prompt",
    )(u4, m, w, v, lam, d)


def _ssm_sample_kernel(u_ref, sre_ref, sim_ref, m_ref, w_ref, v_ref, lam_ref, d_ref,
                       y_ref, ore_ref, oim_ref, *, lc, ns):
    def piece(j):
        return u_ref[pl.ds(j, ns, stride=lc), :]

    a = jnp.concatenate([piece(j) for j in range(lc)], axis=1).astype(BF16)
    s_re, s_im = sre_ref[...], sim_ref[...]
    x = jnp.dot(a, w_ref[0], preferred_element_type=F32)
    e_re, e_im = _complex_step(s_re, s_im, lam_ref[0, :, :STATE_LANES], lam_ref[0, :, STATE_LANES:],
                               x[:, :STATE_LANES], x[:, STATE_LANES:])
    ore_ref[...] = e_re
    oim_ref[...] = e_im
    s0 = jnp.concatenate([s_re, s_im], axis=1).astype(BF16)
    y = (jnp.dot(a, m_ref[0], preferred_element_type=F32) + jnp.dot(s0, v_ref[0], preferred_element_type=F32))
    d = d_ref[...]
    for j in range(lc):
        y_ref[pl.ds(j, ns, stride=lc), :] = y[:, j * LANES:(j + 1) * LANES] + d * piece(j)


def _ssm_sample(u, s_re, s_im, m, w, v, lam, d, lc):
    t = u.shape[0]
    ns = t // lc
    kern = functools.partial(_ssm_sample_kernel, lc=lc, ns=ns)
    assert m.shape[1] == 2 * lc * LANES
    n = lc * LANES
    col = lambda rows, width: pl.BlockSpec((rows, width), lambda b: (0, b))
    return pl.pallas_call(
        kern,
        grid=(N_LANE_BLOCKS,),
        in_specs=[col(t, LANES), col(ns, STATE_LANES), col(ns, STATE_LANES),
                  pl.BlockSpec((1, n, n), lambda b: (b, 0, 0)),
                  pl.BlockSpec((1, n, 2 * STATE_LANES), lambda b: (b, 1, 0)),
                  pl.BlockSpec((1, 2 * STATE_LANES, n), lambda b: (b, 0, 0)),
                  pl.BlockSpec((1, 1, 2 * STATE_LANES), lambda b: (b, 0, 0)), col(1, LANES)],
        out_specs=[col(t, LANES), col(ns, STATE_LANES), col(ns, STATE_LANES)],
        out_shape=[jax.ShapeDtypeStruct(u.shape, F32), jax.ShapeDtypeStruct(s_re.shape, F32),
                   jax.ShapeDtypeStruct(s_im.shape, F32)],
        compiler_params=_params(("arbitrary",)),
        name="ssm_sample",
    )(u, s_re, s_im, m, w, v, lam, d)


def _ssm_discretise(a_re, a_im, log_step):
    dt = jnp.exp(log_step)
    mag = jnp.exp(a_re * dt)
    l_re, l_im = mag * jnp.cos(a_im * dt), mag * jnp.sin(a_im * dt)
    den = a_re * a_re + a_im * a_im
    n_re, n_im = l_re - 1.0, l_im
    return l_re, l_im, (n_re * a_re + n_im * a_im) / den, (n_im * a_re - n_re * a_im) / den


def _complex_powers(l_re, l_im, n):
    p_re, p_im = [jnp.ones_like(l_re)], [jnp.zeros_like(l_re)]
    for _ in range(n):
        p_re, p_im = p_re + [p_re[-1] * l_re - p_im[-1] * l_im], p_im + [p_re[-1] * l_im + p_im[-1] * l_re]
    return p_re, p_im


def _split_bf16(x):
    hi = x.astype(BF16)
    return hi, (x - hi.astype(F32)).astype(BF16)


def _dot_split(a, b):
    dot = lambda x, y: jnp.dot(x, y, preferred_element_type=F32)
    return dot(a[0], b[0]) + (dot(a[0], b[1]) + dot(a[1], b[0]))


def _ssm_tables_kernel(ac_re_ref, ac_im_ref, lsc_ref, ar_re_ref, ar_im_ref, lsr_ref, b_re_ref, b_im_ref,
                       c_re_ref, c_im_ref, m_ref, w_ref, v_ref, lam_ref, lam_half_ref, *, lc):
    l_re, l_im, cf_re, cf_im = _ssm_discretise(ac_re_ref[0], ac_im_ref[0], lsc_ref[0])
    b_re, b_im, c_re, c_im = b_re_ref[0], b_im_ref[0], c_re_ref[0], c_im_ref[0]
    c_re_parts, c_im_parts = _split_bf16(c_re), _split_bf16(c_im)
    bb_re = cf_re * b_re - cf_im * b_im
    bb_im = cf_re * b_im + cf_im * b_re
    p_re, p_im = _complex_powers(l_re, l_im, lc)
    lag = []
    for k in range(lc):
        et_re = (p_re[k] * bb_re - p_im[k] * bb_im).T
        et_im = (p_re[k] * bb_im + p_im[k] * bb_re).T
        j = lc - 1 - k
        w_ref[0, j * LANES:(j + 1) * LANES, :STATE_LANES] = et_re.astype(BF16)
        w_ref[0, j * LANES:(j + 1) * LANES, STATE_LANES:] = et_im.astype(BF16)
        lag.append((_dot_split(_split_bf16(et_re), c_re_parts) - _dot_split(_split_bf16(et_im), c_im_parts)).astype(BF16))
        v_ref[0, :STATE_LANES, k * LANES:(k + 1) * LANES] = (p_re[k + 1] * c_re - p_im[k + 1] * c_im).astype(BF16)
        v_ref[0, STATE_LANES:, k * LANES:(k + 1) * LANES] = (-(p_im[k + 1] * c_re + p_re[k + 1] * c_im)).astype(BF16)
    zero = jnp.zeros((LANES, LANES), BF16)
    for j in range(lc):
        for jj in range(lc):
            m_ref[0, j * LANES:(j + 1) * LANES, jj * LANES:(jj + 1) * LANES] = lag[jj - j] if jj >= j else zero
    r_re, r_im, _, _ = _ssm_discretise(ar_re_ref[0], ar_im_ref[0], lsr_ref[0])
    q_re, q_im = _complex_powers(r_re, r_im, lc)
    lam_ref[0] = jnp.concatenate([q_re[lc], q_im[lc]], axis=1)
    lam_half_ref[0] = jnp.concatenate([q_re[lc // 2], q_im[lc // 2]], axis=1)


def _ssm_tables(a_re, a_im, log_step, b_re, b_im, c_re, c_im, lc):
    g, p, h = b_re.shape
    nbk, r = N_LANE_BLOCKS, GROUPS_PER_LANE_BLOCK
    ls = jnp.broadcast_to(log_step[:, None], (g, p))
    cols = [jnp.broadcast_to(x.reshape(nbk, STATE_LANES, 1), (nbk, STATE_LANES, LANES)) for x in (a_re, a_im, ls)]
    rows = [x.reshape(nbk, 1, STATE_LANES) for x in (a_re, a_im, ls)]
    eye = jnp.eye(r, dtype=F32)[None, :, None, :, None]

    def block_diag(x):
        return (x[:, :, :, None, :] * eye).reshape(nbk, STATE_LANES, LANES)

    mats = [block_diag(b_re.reshape(nbk, r, p, h)), block_diag(b_im.reshape(nbk, r, p, h)),
            block_diag(c_re.reshape(nbk, r, h, p).transpose(0, 1, 3, 2)),
            block_diag(c_im.reshape(nbk, r, h, p).transpose(0, 1, 3, 2))]
    spec = lambda shape: pl.BlockSpec((1,) + shape, lambda b: (b, 0, 0))
    n = lc * LANES
    return pl.pallas_call(
        functools.partial(_ssm_tables_kernel, lc=lc),
        grid=(nbk,),
        in_specs=[spec((STATE_LANES, LANES))] * 3 + [spec((1, STATE_LANES))] * 3 + [spec((STATE_LANES, LANES))] * 4,
        out_specs=[spec((n, n)), spec((n, 2 * STATE_LANES)), spec((2 * STATE_LANES, n)),
                   spec((1, 2 * STATE_LANES)), spec((1, 2 * STATE_LANES))],
        out_shape=[jax.ShapeDtypeStruct((nbk, n, n), BF16), jax.ShapeDtypeStruct((nbk, n, 2 * STATE_LANES), BF16),
                   jax.ShapeDtypeStruct((nbk, 2 * STATE_LANES, n), BF16),
                   jax.ShapeDtypeStruct((nbk, 1, 2 * STATE_LANES), F32),
                   jax.ShapeDtypeStruct((nbk, 1, 2 * STATE_LANES), F32)],
        compiler_params=_params(("arbitrary",)),
        name="ssm_tables",
    )(*cols, *rows, *mats)


def _nt_dot(a, b):
    return lax.dot_general(a, b, (((1,), (1,)), ((), ())), preferred_element_type=F32)


def _swa_prompt_tile(sink_ref, qt_ref, kcat, vcat_t, mstd_ref, mfirst_ref, g_ref, ot_ref, first_tile, nblk):
    keys = lambda j: slice(j * WINDOW, (j + 2) * WINDOW)
    zeros = jnp.zeros((HEAD_DIM, WINDOW), BF16)
    sinks = [jnp.concatenate([jnp.full((1, WINDOW), sink_ref[kh * GQA_GROUP + g], F32) for g in range(GQA_GROUP)],
                             axis=1) for kh in range(N_KV_HEADS)]
    chains = [(j, kh) for j in range(nblk) for kh in range(N_KV_HEADS)]

    def scores(j, kh):
        def rhs(g):
            h = kh * GQA_GROUP + g
            q = qt_ref[h * HEAD_DIM:(h + 1) * HEAD_DIM, j * WINDOW:(j + 1) * WINDOW]
            return jnp.concatenate([q, zeros] if kh == 0 else [zeros, q], axis=0)
        r = jnp.concatenate([rhs(g) for g in range(GQA_GROUP)], axis=1)
        vmask = mstd_ref[...]
        if j == 0:
            vmask = jnp.where(first_tile, mfirst_ref[...], vmask)
        valid = jnp.tile(vmask, (1, GQA_GROUP)) > 0.0
        return jnp.where(valid, jnp.dot(kcat[keys(j)], r, preferred_element_type=F32), NEG_INF)

    def softmax(kh, sc):
        m = jnp.maximum(jnp.max(sc, axis=0, keepdims=True), sinks[kh])
        e = jnp.exp(sc - m)
        return e.astype(BF16), 1.0 / (jnp.sum(e, axis=0, keepdims=True) + jnp.exp(sinks[kh] - m))

    sc = [scores(*c) for c in chains]
    pr = [softmax(kh, x) for (j, kh), x in zip(chains, sc)]

    def finish():
        out = [jnp.dot(vcat_t[kh * HEAD_DIM:(kh + 1) * HEAD_DIM, keys(j)], e, preferred_element_type=F32) * inv
               for (j, kh), (e, inv) in zip(chains, pr)]
        gain = g_ref[...]
        for j in range(nblk):
            o = jnp.concatenate([out[j * N_KV_HEADS + kh][:, g * WINDOW:(g + 1) * WINDOW]
                                 for kh in range(N_KV_HEADS) for g in range(GQA_GROUP)], axis=0)
            scale = lax.rsqrt(jnp.mean(o * o, axis=0, keepdims=True) + RMS_EPS)
            ot_ref[:, j * WINDOW:(j + 1) * WINDOW] = (o * scale * gain).astype(BF16)

    return finish


def _swa_masks():
    kj = jnp.arange(2 * WINDOW)[:, None]
    diff = jnp.arange(WINDOW)[None, :] + WINDOW - kj
    std = (diff >= 0) & (diff <= WINDOW)
    return std.astype(F32), (std & (kj >= WINDOW)).astype(F32)


def _ffn_in_swa_kernel(sink_ref, x_ref, pre_g_ref, wg_ref, wu_ref, wd_ref, post_g_ref, mix_g_ref, wuk_ref, wqv_t_ref,
                       cos_ref, slo_ref, shi_ref, cos_t_ref, sin_t_ref, mstd_ref, mfirst_ref, swa_g_ref,
                       h_ref, u_ref, k_ref, vt_ref, ot_ref, act_ref, q_scr, kcat_scr, vcat_scr, *, nt, nblk):
    i = pl.program_id(0)
    tm = x_ref.shape[0]

    @pl.when(i == 0)
    def _():
        q_scr[...] = jnp.zeros_like(q_scr)
        kcat_scr[...] = jnp.zeros_like(kcat_scr)
        vcat_scr[...] = jnp.zeros_like(vcat_scr)

    first_tile = lax.rem(i + (nt - 1), nt) == 0
    swa_finish = _swa_prompt_tile(sink_ref, q_scr, kcat_scr[...], vcat_scr[...], mstd_ref, mfirst_ref, swa_g_ref,
                                  ot_ref, first_tile, nblk)

    h = _ffn_tile(x_ref[...], pre_g_ref[...], wg_ref, wu_ref, wd_ref, post_g_ref[...], act_ref,
                  after_chunk=(N_FF_CHUNKS - 3, swa_finish))
    h_ref[...] = h
    hn = _rms(h, mix_g_ref[...]).astype(BF16)
    z = jnp.dot(hn, wuk_ref[...], preferred_element_type=F32)
    for b in range(N_LANE_BLOCKS):
        u_ref[b] = z[:, b * LANES:(b + 1) * LANES]
    k = _rope(z[:, SSM_WIDTH:], cos_ref[...], slo_ref[...], shi_ref[...])
    k_ref[...] = k
    zt = _nt_dot(wqv_t_ref[...], hn)
    vt = zt[Q_WIDTH:]
    vt_ref[...] = vt
    half = ROPE_DIM // 2
    q3 = zt[:Q_WIDTH].reshape(N_HEADS, HEAD_DIM, tm)
    x1, x2 = q3[:, :half], q3[:, half:ROPE_DIM]
    cos, sin = cos_t_ref[...][None], sin_t_ref[...][None]
    q3 = jnp.concatenate([x1 * cos - x2 * sin, x2 * cos + x1 * sin, q3[:, ROPE_DIM:]], axis=1)
    kcat_scr[:WINDOW, :] = kcat_scr[tm:, :]
    kcat_scr[WINDOW:, :] = k.astype(BF16)
    vcat_scr[:, :WINDOW] = vcat_scr[:, tm:]
    vcat_scr[:, WINDOW:] = vt.astype(BF16)
    q_scr[...] = (q3.reshape(Q_WIDTH, tm) * (HEAD_DIM ** -0.5)).astype(BF16)


def _ffn_in_swa(x, pre_g, ffn_w, post_g, mix_g, wuk, wqv_t, cos, slo, shi, cos_t, sin_t, sinks, swa_g, seq, tm):
    t = x.shape[0]
    n_tiles, nt, nblk = t // tm, seq // tm, tm // WINDOW
    half = ROPE_DIM // 2
    mstd, mfirst = _swa_masks()
    gain = jnp.broadcast_to(swa_g.reshape(Q_WIDTH, 1), (Q_WIDTH, WINDOW))
    cur = lambda i: jnp.minimum(i, n_tiles - 1)
    row = lambda w: pl.BlockSpec((tm, w), lambda i: (cur(i), 0))
    tab = pl.BlockSpec((tm, LANES), lambda i: (cur(i) % nt, 0))
    tab_t = pl.BlockSpec((half, tm), lambda i: (0, cur(i) % nt))
    return pl.pallas_call(
        functools.partial(_ffn_in_swa_kernel, nt=nt, nblk=nblk),
        grid=(n_tiles + 1,),
        in_specs=[pl.BlockSpec(memory_space=pltpu.SMEM), row(D_MODEL), _const_spec((1, D_MODEL)),
                  *[_const_spec(w.shape) for w in ffn_w], _const_spec((1, D_MODEL)), _const_spec((1, D_MODEL)),
                  _const_spec(wuk.shape), _const_spec(wqv_t.shape), tab, tab, tab, tab_t, tab_t,
                  _const_spec(mstd.shape), _const_spec(mfirst.shape), _const_spec(gain.shape)],
        out_specs=[row(D_MODEL), pl.BlockSpec((N_LANE_BLOCKS, tm, LANES), lambda i: (0, cur(i), 0)), row(KV_WIDTH),
                   pl.BlockSpec((KV_WIDTH, tm), lambda i: (0, cur(i))),
                   pl.BlockSpec((Q_WIDTH, tm), lambda i: (0, jnp.maximum(i - 1, 0)))],
        out_shape=[jax.ShapeDtypeStruct((t, D_MODEL), F32), jax.ShapeDtypeStruct((N_LANE_BLOCKS, t, LANES), F32),
                   jax.ShapeDtypeStruct((t, KV_WIDTH), F32), jax.ShapeDtypeStruct((KV_WIDTH, t), F32),
                   jax.ShapeDtypeStruct((Q_WIDTH, t), BF16)],
        scratch_shapes=[pltpu.VMEM((tm, D_FF), BF16), pltpu.VMEM((Q_WIDTH, tm), BF16),
                        pltpu.VMEM((WINDOW + tm, KV_WIDTH), BF16), pltpu.VMEM((KV_WIDTH, WINDOW + tm), BF16)],
        compiler_params=_params(("arbitrary",)),
        name="ffn_in_swa",
    )(sinks, x, pre_g, *ffn_w, post_g, mix_g, wuk, wqv_t, cos, slo, shi, cos_t, sin_t, mstd, mfirst, gain)


def _swa_sample_kernel(sink_ref, q_ref, kn_ref, vn_ref, ck_ref, cv_ref, g_ref, o_ref, nk_ref, nv_ref, *, ns, t):
    rows = GQA_GROUP * t
    tok = lax.broadcasted_iota(jnp.int32, (rows, WINDOW), 0) % t
    valid_c = lax.broadcasted_iota(jnp.int32, (rows, WINDOW), 1) >= tok
    tok_n = lax.broadcasted_iota(jnp.int32, (rows, t), 0) % t
    valid_n = lax.broadcasted_iota(jnp.int32, (rows, t), 1) <= tok_n
    gain = g_ref[...]

    sinks = [jnp.concatenate([jnp.full((t, 1), sink_ref[kh * GQA_GROUP + g], F32) for g in range(GQA_GROUP)], axis=0)
             for kh in range(N_KV_HEADS)]
    heads = [(s, kh) for s in range(ns) for kh in range(N_KV_HEADS)]
    hs = lambda kh: slice(kh * HEAD_DIM, (kh + 1) * HEAD_DIM)

    newest = lax.broadcasted_iota(jnp.int32, (KV_WIDTH, WINDOW), 1) >= WINDOW - t
    pad = jnp.zeros((WINDOW - 2 * t, KV_WIDTH), F32)

    def shifted(cache_ref, new_ref, s):
        new_rows = jnp.concatenate([pad, jnp.zeros((t, KV_WIDTH), F32), new_ref[s]], axis=0)
        return jnp.where(newest, new_rows.T, pltpu.roll(cache_ref[s], WINDOW - t, 1))

    for s in range(ns):
        nk_ref[s] = shifted(ck_ref, kn_ref, s)
        nv_ref[s] = shifted(cv_ref, vn_ref, s)

    def scores(s, kh):
        q = q_ref[s].astype(F32)
        q4 = jnp.concatenate([q[:, (kh * GQA_GROUP + g) * HEAD_DIM:(kh * GQA_GROUP + g + 1) * HEAD_DIM]
                              for g in range(GQA_GROUP)], axis=0).astype(BF16)
        sc_c = jnp.where(valid_c, jnp.dot(q4, ck_ref[s, hs(kh), :].astype(BF16), preferred_element_type=F32), NEG_INF)
        sc_n = jnp.where(valid_n, _nt_dot(q4, kn_ref[s, :, hs(kh)].astype(BF16)), NEG_INF)
        return sc_c, sc_n

    def softmax(kh, sc_c, sc_n):
        m = jnp.maximum(jnp.maximum(jnp.max(sc_c, axis=-1, keepdims=True),
                                    jnp.max(sc_n, axis=-1, keepdims=True)), sinks[kh])
        e_c, e_n = jnp.exp(sc_c - m), jnp.exp(sc_n - m)
        inv = 1.0 / (jnp.sum(e_c, axis=-1, keepdims=True) + jnp.sum(e_n, axis=-1, keepdims=True)
                     + jnp.exp(sinks[kh] - m))
        return (e_c * inv).astype(BF16), (e_n * inv).astype(BF16)

    def values(s, kh, p_c, p_n):
        return (_nt_dot(p_c, cv_ref[s, hs(kh), :].astype(BF16))
                + jnp.dot(p_n, vn_ref[s, :, hs(kh)].astype(BF16), preferred_element_type=F32))

    sc = [scores(s, kh) for s, kh in heads]
    pr = [softmax(kh, *x) for (s, kh), x in zip(heads, sc)]
    o4 = [values(s, kh, *x) for (s, kh), x in zip(heads, pr)]
    for s in range(ns):
        o = jnp.concatenate([o4[s * N_KV_HEADS + kh][g * t:(g + 1) * t]
                             for kh in range(N_KV_HEADS) for g in range(GQA_GROUP)], axis=1)
        o_ref[s] = _rms(o, gain).astype(BF16)


def _swa_sample(q3, kn3, vn3, ck, cv, sinks, out_g, sb):
    ns, t, _ = q3.shape
    blk = lambda a: pl.BlockSpec((sb,) + a.shape[1:], lambda i: (i, 0, 0))
    kern = functools.partial(_swa_sample_kernel, ns=sb, t=t)
    return pl.pallas_call(
        kern,
        grid=(ns // sb,),
        in_specs=[pl.BlockSpec(memory_space=pltpu.SMEM), blk(q3), blk(kn3), blk(vn3), blk(ck), blk(cv),
                  pl.BlockSpec((1, Q_WIDTH), lambda i: (0, 0))],
        out_specs=[blk(q3), blk(ck), blk(cv)],
        out_shape=[jax.ShapeDtypeStruct(q3.shape, BF16), jax.ShapeDtypeStruct(ck.shape, F32),
                   jax.ShapeDtypeStruct(cv.shape, F32)],
        compiler_params=_params(("arbitrary",)),
        name="swa_sample",
    )(sinks, q3, kn3, vn3, ck, cv, out_g)


def _gelu_tanh(x):
    return 0.5 * x * (1.0 + jnp.tanh(math.sqrt(2.0 / math.pi) * (x + 0.044715 * (x * x * x))))


def _merge_tile(h, y, at, wglu_ref, bglu, sg, wout_ref, post_g, xa_g, wq_ref, at_transposed=False):
    g = _gelu_tanh(y)
    lin = jnp.dot(g.astype(BF16), wglu_ref[...], preferred_element_type=F32) + bglu
    y_ssm = g * (1.0 / (1.0 + jnp.exp(-lin)))
    ssm_n = _rms(y_ssm, sg).astype(BF16)
    at_dims = (((0,), (0,)), ((), ())) if at_transposed else (((1,), (0,)), ((), ()))
    mixed = (jnp.dot(ssm_n, wout_ref[:SSM_WIDTH, :], preferred_element_type=F32)
             + lax.dot_general(at, wout_ref[SSM_WIDTH:, :], at_dims, preferred_element_type=F32))
    h2 = h + _rms(mixed, post_g)
    qm = jnp.dot(_rms(h2, xa_g).astype(BF16), wq_ref[...], preferred_element_type=F32)
    return h2, (qm * (MEM_HEAD_DIM ** -0.5)).astype(BF16)


def _merge_kernel(h_ref, y_ref, at_ref, wglu_ref, bglu_ref, sg_ref, wout_ref, post_g_ref, xa_g_ref, wq_ref,
                  h2_ref, qm_ref):
    h2_ref[...], qm_ref[...] = _merge_tile(h_ref[...], y_ref[...], at_ref[...], wglu_ref, bglu_ref[...], sg_ref[...],
                                           wout_ref, post_g_ref[...], xa_g_ref[...], wq_ref)


def _merge_specs(wglu, wout, wq):
    return [_const_spec(wglu.shape), _const_spec((1, SSM_WIDTH)), _const_spec((1, SSM_WIDTH)), _const_spec(wout.shape),
            _const_spec((1, D_MODEL)), _const_spec((1, D_MODEL)), _const_spec(wq.shape)]


def _merge(h, y, at, merge_w, tm):
    t = h.shape[0]
    row = lambda w: pl.BlockSpec((tm, w), lambda i: (i, 0))
    return pl.pallas_call(
        _merge_kernel,
        grid=(t // tm,),
        in_specs=[row(D_MODEL), row(SSM_WIDTH), row(Q_WIDTH)] + _merge_specs(merge_w[0], merge_w[3], merge_w[6]),
        out_specs=[row(D_MODEL), row(D_MODEL)],
        out_shape=[jax.ShapeDtypeStruct((t, D_MODEL), F32), jax.ShapeDtypeStruct((t, D_MODEL), BF16)],
        compiler_params=_params(("arbitrary",)),
        name="merge",
    )(h, y, at, *merge_w)


def _mem_attn_sample_kernel(q_ref, k_ref, v_ref, o_ref, *, gb, t):
    rows = t * MEM_HEADS
    halves = MEM_HEAD_DIM // LANES
    kv_rows = N_MEM * halves * MEM_HEADS
    period = halves * MEM_HEADS
    lane = lax.broadcasted_iota(jnp.int32, (rows, kv_rows), 1) % period
    head = lax.broadcasted_iota(jnp.int32, (rows, kv_rows), 0) % MEM_HEADS
    in_half = [lane == head + hf * MEM_HEADS for hf in range(halves)]
    for b in range(gb):
        kb = k_ref[b].astype(BF16)
        vb = v_ref[b].astype(BF16)
        part = _nt_dot(q_ref[b], kb)
        sc = jnp.where(in_half[0], part[:rows], 0.0)
        for hf in range(1, halves):
            sc = sc + pltpu.roll(jnp.where(in_half[hf], part[hf * rows:(hf + 1) * rows], 0.0),
                                 kv_rows - hf * MEM_HEADS, 1)
        sc = jnp.where(in_half[0], sc, -jnp.inf)
        e = jnp.exp(sc - jnp.max(sc, axis=-1, keepdims=True))
        inv = 1.0 / jnp.sum(e, axis=-1, keepdims=True)
        e_all = jnp.concatenate([e] + [pltpu.roll(e, hf * MEM_HEADS, 1) for hf in range(1, halves)], axis=0)
        o = jnp.dot(e_all.astype(BF16), vb, preferred_element_type=F32)
        o_ref[b] = (o * jnp.concatenate([inv] * halves, axis=0)).astype(BF16)


def _mem_attn_sample(qm, cache_k, cache_v, n_s, t_s, gb):
    halves = MEM_HEAD_DIM // LANES
    rows = halves * t_s * MEM_HEADS
    kv_rows = N_MEM * halves * MEM_HEADS

    def stored_rows(c):
        c = c.reshape(n_s, N_MEM, MEM_HEADS, halves, LANES).transpose(0, 1, 3, 2, 4)
        return c.reshape(n_s, kv_rows, LANES)

    q = qm.reshape(n_s, t_s, MEM_HEADS, halves, LANES).transpose(0, 3, 1, 2, 4).reshape(n_s, rows, LANES)
    blk = lambda r: pl.BlockSpec((gb, r, LANES), lambda i: (i, 0, 0))
    o = pl.pallas_call(
        functools.partial(_mem_attn_sample_kernel, gb=gb, t=t_s),
        grid=(n_s // gb,),
        in_specs=[blk(rows), blk(kv_rows), blk(kv_rows)],
        out_specs=blk(rows),
        out_shape=jax.ShapeDtypeStruct((n_s, rows, LANES), BF16),
        compiler_params=_params(("arbitrary",)),
        name="mem_attn_sample",
    )(q, stored_rows(cache_k), stored_rows(cache_v))
    o = o.reshape(n_s, halves, t_s, MEM_HEADS, LANES).transpose(0, 2, 3, 1, 4)
    return o.reshape(n_s * t_s, D_MODEL)


def _mem_heads(q, k, v):
    outs = []
    for hh in range(MEM_HEADS):
        hs = slice(hh * MEM_HEAD_DIM, (hh + 1) * MEM_HEAD_DIM)
        sc = _nt_dot(q[:, hs], k[:, hs])
        e = jnp.exp(sc - jnp.max(sc, axis=-1, keepdims=True))
        inv = 1.0 / jnp.sum(e, axis=-1, keepdims=True)
        outs.append((jnp.dot(e.astype(BF16), v[:, hs], preferred_element_type=F32) * inv).astype(BF16))
    return jnp.concatenate(outs, axis=1)


def _ffn_out_tile(h2, om, wo_ref, xa_post, pre_g, wg_ref, wu_ref, wd_ref, post_g, act_ref):
    c = jnp.dot(om, wo_ref[...], preferred_element_type=F32)
    h3 = h2 + _rms(c, xa_post)
    return _ffn_tile(h3, pre_g, wg_ref, wu_ref, wd_ref, post_g, act_ref)


def _ffn_out_specs(wo, ffn_w):
    return [_const_spec(wo.shape), _const_spec((1, D_MODEL)), _const_spec((1, D_MODEL)),
            *[_const_spec(w.shape) for w in ffn_w], _const_spec((1, D_MODEL))]


def _ffn_out_kernel(h_ref, o_ref, wo_ref, xa_post_ref, pre_g_ref, wg_ref, wu_ref, wd_ref, post_g_ref, out_ref, act_ref):
    out_ref[...] = _ffn_out_tile(h_ref[...], o_ref[...], wo_ref, xa_post_ref[...], pre_g_ref[...], wg_ref, wu_ref,
                                 wd_ref, post_g_ref[...], act_ref)


def _ffn_out(h, o, out_w, tm):
    t = h.shape[0]
    row = pl.BlockSpec((tm, D_MODEL), lambda i: (i, 0))
    return pl.pallas_call(
        _ffn_out_kernel,
        grid=(t // tm,),
        in_specs=[row, row] + _ffn_out_specs(out_w[0], out_w[3:6]),
        out_specs=row,
        out_shape=jax.ShapeDtypeStruct((t, D_MODEL), F32),
        scratch_shapes=[pltpu.VMEM((tm, D_FF), BF16)],
        compiler_params=_params(("arbitrary",)),
        name="ffn_out",
    )(h, o, *out_w)


def _post_kernel(h_ref, y_ref, at_ref, k_ref, v_ref,
                 wglu_ref, bglu_ref, sg_ref, wout_ref, post_g_ref, xa_g_ref, wq_ref,
                 wo_ref, xa_post_ref, pre_g_ref, wg_ref, wu_ref, wd_ref, ffn_post_ref, out_ref, act_ref):
    y = jnp.concatenate([y_ref[b] for b in range(N_LANE_BLOCKS)], axis=1)
    h2, qm = _merge_tile(h_ref[...], y, at_ref[...], wglu_ref, bglu_ref[...], sg_ref[...],
                         wout_ref, post_g_ref[...], xa_g_ref[...], wq_ref, at_transposed=True)
    om = _mem_heads(qm, k_ref[0], v_ref[0])
    out_ref[...] = _ffn_out_tile(h2, om, wo_ref, xa_post_ref[...], pre_g_ref[...], wg_ref, wu_ref, wd_ref,
                                 ffn_post_ref[...], act_ref)


def _post(h, y, at, k3, v3, merge_w, out_w, tm):
    t = h.shape[0]
    tiles_per_batch = t // k3.shape[0] // tm
    row = lambda w: pl.BlockSpec((tm, w), lambda i: (i, 0))
    kv = pl.BlockSpec((1, N_MEM, D_MODEL), lambda i: (i // tiles_per_batch, 0, 0))
    return pl.pallas_call(
        _post_kernel,
        grid=(t // tm,),
        in_specs=([row(D_MODEL), pl.BlockSpec((N_LANE_BLOCKS, tm, LANES), lambda i: (0, i, 0)),
                   pl.BlockSpec((Q_WIDTH, tm), lambda i: (0, i)), kv, kv]
                  + _merge_specs(merge_w[0], merge_w[3], merge_w[6]) + _ffn_out_specs(out_w[0], out_w[3:6])),
        out_specs=row(D_MODEL),
        out_shape=jax.ShapeDtypeStruct((t, D_MODEL), F32),
        scratch_shapes=[pltpu.VMEM((tm, D_FF), BF16)],
        compiler_params=_params(("arbitrary",)),
        name="post",
    )(h, y, at, k3, v3, *merge_w, *out_w)


def _mem_kv_kernel(m_ref, g_ref, wkv_ref, k_ref, v_ref, kb_ref, vb_ref):
    kv = jnp.dot(_rms(m_ref[...], g_ref[...]).astype(BF16), wkv_ref[...], preferred_element_type=F32)
    k_ref[...] = kv[:, :D_MODEL]
    v_ref[...] = kv[:, D_MODEL:]
    kb_ref[...] = kv[:, :D_MODEL].astype(BF16)
    vb_ref[...] = kv[:, D_MODEL:].astype(BF16)


def _mem_kv(mem, g, wkv, tm):
    t = mem.shape[0]
    row = pl.BlockSpec((tm, D_MODEL), lambda i: (i, 0))
    return pl.pallas_call(
        _mem_kv_kernel,
        grid=(t // tm,),
        in_specs=[row, _const_spec((1, D_MODEL)), _const_spec(wkv.shape)],
        out_specs=[row] * 4,
        out_shape=[jax.ShapeDtypeStruct((t, D_MODEL), F32)] * 2 + [jax.ShapeDtypeStruct((t, D_MODEL), BF16)] * 2,
        compiler_params=_params(("arbitrary",)),
        name="mem_kv",
    )(mem, g, wkv)


def _rope_tables(pos):
    half = ROPE_DIM // 2
    inv = ROPE_THETA ** (-jnp.arange(half, dtype=F32) * (2.0 / ROPE_DIM))
    ang = pos.astype(F32)[:, None] * inv[None, :]
    cos, sin = jnp.cos(ang), jnp.sin(ang)
    n = pos.shape[0]
    pad = jnp.zeros((n, HEAD_DIM - ROPE_DIM), F32)
    zero = jnp.zeros((n, half), F32)
    cos_h = jnp.concatenate([cos, cos, pad + 1.0], axis=1)
    lo_h = jnp.concatenate([-sin, zero, pad], axis=1)
    hi_h = jnp.concatenate([zero, sin, pad], axis=1)
    rep = LANES // HEAD_DIM
    return tuple(jnp.tile(a, (1, rep)) for a in (cos_h, lo_h, hi_h)), (cos.T, sin.T)


def _ffn_weights(w_gate, w_up, w_down):
    return w_gate.astype(BF16), w_up.astype(BF16), w_down.astype(BF16)


def _lane_block_states(st, n):
    st = st.reshape(N_LANE_BLOCKS, n, 2, GROUPS_PER_LANE_BLOCK, SSM_STATE).transpose(2, 1, 0, 3, 4)
    st = st.reshape(2, n, N_SSM_GROUPS, SSM_STATE)
    return st[0], st[1]


def kernel(x_prompt, x_sample, state_ssm_re, state_ssm_im, cache_swa_k, cache_swa_v, cache_mem_k, cache_mem_v, mem_prompt, ffn1_pre_g, ffn1_w_gate, ffn1_w_up, ffn1_w_down, ffn1_post_g, mix_pre_g, w_in, ssm_a_re, ssm_a_im, ssm_log_step, ssm_b_re, ssm_b_im, ssm_c_re, ssm_c_im, ssm_d, ssm_w_glu, ssm_b_glu, attn_sinks, ssm_out_g, attn_out_g, w_out, mix_post_g, mem_norm_g, w_mem_q, w_mem_k, w_mem_v, w_mem_o, xa_pre_g, xa_post_g, ffn2_pre_g, ffn2_w_gate, ffn2_w_up, ffn2_w_down, ffn2_post_g):
    n_p, s_p, _ = x_prompt.shape
    n_s, t_s, _ = x_sample.shape
    tm = 512
    row = lambda a: a.reshape(1, -1).astype(F32)

    ffn1_w = _ffn_weights(ffn1_w_gate, ffn1_w_up, ffn1_w_down)
    win = w_in.astype(BF16)
    merge_w = (ssm_w_glu.astype(BF16), row(ssm_b_glu), row(ssm_out_g), w_out.astype(BF16), row(mix_post_g),
               row(xa_pre_g), w_mem_q.astype(BF16))
    out_w = (w_mem_o.astype(BF16), row(xa_post_g), row(ffn2_pre_g),
             *_ffn_weights(ffn2_w_gate, ffn2_w_up, ffn2_w_down), row(ffn2_post_g))
    wkv = jnp.concatenate([w_mem_k, w_mem_v], axis=1).astype(BF16)
    d_row = row(ssm_d)
    ssm_args = (ssm_a_re.astype(F32), ssm_a_im.astype(F32), ssm_log_step.astype(F32), ssm_b_re.astype(F32),
                ssm_b_im.astype(F32), ssm_c_re.astype(F32), ssm_c_im.astype(F32))
    sinks = attn_sinks.astype(F32)

    pm_k, pm_v, pm_kb, pm_vb = _mem_kv(mem_prompt.reshape(n_p * N_MEM, D_MODEL), row(mem_norm_g), wkv, N_MEM)

    def tokenwise_in(x2, pos_tab):
        return _ffn_in(x2, row(ffn1_pre_g), ffn1_w, row(ffn1_post_g), row(mix_pre_g), win, *pos_tab, tm)

    lc_p = 2 * t_s
    ssm_m, ssm_w, ssm_v, lam_p, lam_s = _ssm_tables(*ssm_args, lc_p)
    o1 = SSM_WIDTH + Q_WIDTH
    wuk = jnp.concatenate([win[:, :SSM_WIDTH], win[:, o1:o1 + KV_WIDTH]], axis=1)
    wqv_t = jnp.concatenate([win[:, SSM_WIDTH:o1], win[:, o1 + KV_WIDTH:]], axis=1).T
    tab_p, tab_p_t = _rope_tables(jnp.arange(s_p, dtype=jnp.int32))
    h1, u, k, vt, at = _ffn_in_swa(x_prompt.reshape(n_p * s_p, D_MODEL), row(ffn1_pre_g), ffn1_w, row(ffn1_post_g),
                                   row(mix_pre_g), wuk, wqv_t, *tab_p, *tab_p_t, sinks, attn_out_g.astype(F32),
                                   s_p, tm)
    y4, st_p = _ssm_prompt(u.reshape(N_LANE_BLOCKS, n_p, s_p, LANES), ssm_m, ssm_w, ssm_v, lam_p,
                           d_row.reshape(N_LANE_BLOCKS, 1, LANES), lc_p, 1024, 2)
    y_prompt = _post(h1, y4.reshape(N_LANE_BLOCKS, n_p * s_p, LANES), at, pm_kb.reshape(n_p, N_MEM, D_MODEL),
                     pm_vb.reshape(n_p, N_MEM, D_MODEL), merge_w, out_w, tm).reshape(n_p, s_p, D_MODEL)
    p_sre, p_sim = _lane_block_states(st_p, n_p)
    p_wk = k.reshape(n_p, s_p, KV_WIDTH)[:, -WINDOW:].reshape(n_p, WINDOW, N_KV_HEADS, HEAD_DIM)
    p_wv = vt.reshape(KV_WIDTH, n_p, s_p)[:, :, -WINDOW:].reshape(N_KV_HEADS, HEAD_DIM, n_p, WINDOW).transpose(2, 3, 0, 1)

    pos_s = jnp.tile(PAST_LEN + jnp.arange(t_s, dtype=jnp.int32), n_s)
    h1s, us, qs, ks, vs = tokenwise_in(x_sample.reshape(n_s * t_s, D_MODEL), _rope_tables(pos_s)[0])
    ys, s_sre, s_sim = _ssm_sample(us, state_ssm_re.reshape(n_s, -1).astype(F32), state_ssm_im.reshape(n_s, -1).astype(F32),
                                   ssm_m, ssm_w, ssm_v, lam_s, d_row, t_s)
    win_len = cache_swa_k.shape[1]
    ats, s_wk, s_wv = _swa_sample(qs.reshape(n_s, t_s, Q_WIDTH), ks.reshape(n_s, t_s, KV_WIDTH), vs.reshape(n_s, t_s, KV_WIDTH),
                                  cache_swa_k.transpose(0, 2, 3, 1).reshape(n_s, KV_WIDTH, win_len),
                                  cache_swa_v.transpose(0, 2, 3, 1).reshape(n_s, KV_WIDTH, win_len),
                                  sinks, row(attn_out_g), 16)
    s_wk = s_wk.reshape(n_s, N_KV_HEADS, HEAD_DIM, win_len).transpose(0, 3, 1, 2)
    s_wv = s_wv.reshape(n_s, N_KV_HEADS, HEAD_DIM, win_len).transpose(0, 3, 1, 2)
    h2s, qms = _merge(h1s, ys, ats.reshape(n_s * t_s, Q_WIDTH), merge_w, tm)
    oms = _mem_attn_sample(qms, cache_mem_k, cache_mem_v, n_s, t_s, 4)
    y_sample = _ffn_out(h2s, oms, out_w, tm).reshape(n_s, t_s, D_MODEL)

    return (y_prompt, y_sample, p_sre, p_sim, p_wk, p_wv,
            pm_k.reshape(n_p, N_MEM, MEM_HEADS, MEM_HEAD_DIM), pm_v.reshape(n_p, N_MEM, MEM_HEADS, MEM_HEAD_DIM),
            s_sre.reshape(n_s, N_SSM_GROUPS, SSM_STATE), s_sim.reshape(n_s, N_SSM_GROUPS, SSM_STATE),
            s_wk, s_wv)
```

```python
import functools
import math

import jax
import jax.numpy as jnp
from jax import lax
from jax.experimental import pallas as pl
from jax.experimental.pallas import tpu as pltpu

F32 = jnp.float32
BF16 = jnp.bfloat16

D_MODEL = 1024
PAST_LEN = 16384
SSM_WIDTH = 512
SSM_GROUP = 16
N_SSM_GROUPS = 32
SSM_STATE = 64
HEAD_DIM = 64
N_HEADS = 8
N_KV_HEADS = 2
GQA_GROUP = 4
Q_WIDTH = 512
KV_WIDTH = 128
WINDOW = 128
ROPE_THETA = 500000.0
ROPE_DIM = 16
N_MEM = 256
MEM_HEADS = 4
MEM_HEAD_DIM = 256
D_FF = 2816
RMS_EPS = 1e-6
IN_WIDTH = SSM_WIDTH + Q_WIDTH + 2 * KV_WIDTH
NEG_INF = -1e30

LANES = 128
FF_CHUNK = 256
ROW_BLOCKS = 2
N_FF_CHUNKS = D_FF // FF_CHUNK
GROUPS_PER_LANE_BLOCK = LANES // SSM_GROUP
N_LANE_BLOCKS = SSM_WIDTH // LANES
STATE_LANES = GROUPS_PER_LANE_BLOCK * SSM_STATE
VMEM_LIMIT = 56 * 1024 * 1024


def _rms(x, g):
    return x * lax.rsqrt(jnp.mean(x * x, axis=-1, keepdims=True) + RMS_EPS) * g


def _const_spec(shape):
    nd = len(shape)
    return pl.BlockSpec(shape, lambda *_: (0,) * nd, pipeline_mode=pl.Buffered(1))


def _params(sem):
    return pltpu.CompilerParams(dimension_semantics=sem, vmem_limit_bytes=VMEM_LIMIT)


def _run(gen):
    try:
        while True:
            next(gen)
    except StopIteration as done:
        return done.value


def _interleave(gens):
    live = list(gens)
    while live:
        for g in list(live):
            try:
                next(g)
            except StopIteration:
                live.remove(g)


def _ffn_stages(x, pre_g, wg_ref, wu_ref, wd_ref, post_g, act_ref, rows=slice(None), after_chunk=None):
    xn = _rms(x, pre_g).astype(BF16)
    yield
    for c in range(N_FF_CHUNKS):
        if after_chunk is not None and c == after_chunk[0] + 1:
            after_chunk[1]()
        cols = slice(c * FF_CHUNK, (c + 1) * FF_CHUNK)
        gate = jnp.dot(xn, wg_ref[:, cols], preferred_element_type=F32)
        up = jnp.dot(xn, wu_ref[:, cols], preferred_element_type=F32)
        act = gate * (1.0 / (1.0 + jnp.exp(-gate))) * up
        act_ref[rows, cols] = act.astype(BF16)
        yield
    down = jnp.dot(act_ref[rows, :], wd_ref[...], preferred_element_type=F32)
    yield
    return x + 0.5 * _rms(down, post_g)


def _ffn_tile(*args, **kwargs):
    return _run(_ffn_stages(*args, **kwargs))


def _rope(x, cos, sin_lo, sin_hi):
    w = x.shape[1]
    half = ROPE_DIM // 2
    return (x * cos + pltpu.roll(x, w - half, 1) * sin_lo + pltpu.roll(x, half, 1) * sin_hi)


def _ffn_in_kernel(x_ref, pre_g_ref, wg_ref, wu_ref, wd_ref, post_g_ref, mix_g_ref, win_ref,
                   cos_ref, slo_ref, shi_ref,
                   h_ref, u_ref, q_ref, k_ref, v_ref, act_ref):
    h = _ffn_tile(x_ref[...], pre_g_ref[...], wg_ref, wu_ref, wd_ref, post_g_ref[...], act_ref)
    h_ref[...] = h
    z = jnp.dot(_rms(h, mix_g_ref[...]).astype(BF16), win_ref[...], preferred_element_type=F32)
    u_ref[...] = z[:, :SSM_WIDTH]
    o1 = SSM_WIDTH + Q_WIDTH
    cos, slo, shi = cos_ref[...], slo_ref[...], shi_ref[...]
    rep = Q_WIDTH // LANES
    q = _rope(z[:, SSM_WIDTH:o1], jnp.tile(cos, (1, rep)), jnp.tile(slo, (1, rep)), jnp.tile(shi, (1, rep)))
    q_ref[...] = (q * (HEAD_DIM ** -0.5)).astype(BF16)
    k_ref[...] = _rope(z[:, o1:o1 + KV_WIDTH], cos, slo, shi)
    v_ref[...] = z[:, o1 + KV_WIDTH:]


def _ffn_in_t_kernel(x_ref, pre_g_ref, wg_ref, wu_ref, wd_ref, post_g_ref, mix_g_ref, wuk_ref, wqv_t_ref,
                     cos_ref, slo_ref, shi_ref, cos_t_ref, sin_t_ref,
                     h_ref, u_ref, k_ref, qt_ref, vt_ref, act_ref):
    h = _ffn_tile(x_ref[...], pre_g_ref[...], wg_ref, wu_ref, wd_ref, post_g_ref[...], act_ref)
    h_ref[...] = h
    hn = _rms(h, mix_g_ref[...]).astype(BF16)
    z = jnp.dot(hn, wuk_ref[...], preferred_element_type=F32)
    for b in range(N_LANE_BLOCKS):
        u_ref[b] = z[:, b * LANES:(b + 1) * LANES]
    k_ref[...] = _rope(z[:, SSM_WIDTH:], cos_ref[...], slo_ref[...], shi_ref[...])
    zt = _nt_dot(wqv_t_ref[...], hn)
    vt_ref[...] = zt[Q_WIDTH:]
    tm = zt.shape[1]
    half = ROPE_DIM // 2
    q3 = zt[:Q_WIDTH].reshape(N_HEADS, HEAD_DIM, tm)
    x1, x2 = q3[:, :half], q3[:, half:ROPE_DIM]
    cos, sin = cos_t_ref[...][None], sin_t_ref[...][None]
    q3 = jnp.concatenate([x1 * cos - x2 * sin, x2 * cos + x1 * sin, q3[:, ROPE_DIM:]], axis=1)
    qt_ref[...] = (q3.reshape(Q_WIDTH, tm) * (HEAD_DIM ** -0.5)).astype(BF16)


def _ffn_in_t(x, pre_g, ffn_w, post_g, mix_g, wuk, wqv_t, cos, slo, shi, cos_t, sin_t, tm):
    t = x.shape[0]
    n_pos_tiles = cos.shape[0] // tm
    half = ROPE_DIM // 2
    row = lambda w: pl.BlockSpec((tm, w), lambda i: (i, 0))
    col = lambda r: pl.BlockSpec((r, tm), lambda i: (0, i))
    tab = pl.BlockSpec((tm, LANES), lambda i: (i % n_pos_tiles, 0))
    tab_t = pl.BlockSpec((half, tm), lambda i: (0, i % n_pos_tiles))
    return pl.pallas_call(
        _ffn_in_t_kernel,
        grid=(t // tm,),
        in_specs=[row(D_MODEL), _const_spec((1, D_MODEL)), *[_const_spec(w.shape) for w in ffn_w],
                  _const_spec((1, D_MODEL)), _const_spec((1, D_MODEL)), _const_spec(wuk.shape),
                  _const_spec(wqv_t.shape), tab, tab, tab, tab_t, tab_t],
        out_specs=[row(D_MODEL), pl.BlockSpec((N_LANE_BLOCKS, tm, LANES), lambda i: (0, i, 0)), row(KV_WIDTH),
                   col(Q_WIDTH), col(KV_WIDTH)],
        out_shape=[jax.ShapeDtypeStruct((t, D_MODEL), F32), jax.ShapeDtypeStruct((N_LANE_BLOCKS, t, LANES), F32),
                   jax.ShapeDtypeStruct((t, KV_WIDTH), F32), jax.ShapeDtypeStruct((Q_WIDTH, t), BF16),
                   jax.ShapeDtypeStruct((KV_WIDTH, t), F32)],
        scratch_shapes=[pltpu.VMEM((tm, D_FF), BF16)],
        compiler_params=_params(("arbitrary",)),
        name="ffn_in_t",
    )(x, pre_g, *ffn_w, post_g, mix_g, wuk, wqv_t, cos, slo, shi, cos_t, sin_t)


def _ffn_in(x, pre_g, ffn_w, post_g, mix_g, win, cos, slo, shi, tm):
    t = x.shape[0]
    n_pos_tiles = cos.shape[0] // tm
    row = lambda w: pl.BlockSpec((tm, w), lambda i: (i, 0))
    tab = pl.BlockSpec((tm, LANES), lambda i: (i % n_pos_tiles, 0))
    return pl.pallas_call(
        _ffn_in_kernel,
        grid=(t // tm,),
        in_specs=[row(D_MODEL), _const_spec((1, D_MODEL)), *[_const_spec(w.shape) for w in ffn_w],
                  _const_spec((1, D_MODEL)), _const_spec((1, D_MODEL)), _const_spec(win.shape),
                  tab, tab, tab],
        out_specs=[row(D_MODEL), row(SSM_WIDTH), row(Q_WIDTH), row(KV_WIDTH), row(KV_WIDTH)],
        out_shape=[jax.ShapeDtypeStruct((t, D_MODEL), F32), jax.ShapeDtypeStruct((t, SSM_WIDTH), F32),
                   jax.ShapeDtypeStruct((t, Q_WIDTH), BF16), jax.ShapeDtypeStruct((t, KV_WIDTH), F32),
                   jax.ShapeDtypeStruct((t, KV_WIDTH), F32)],
        scratch_shapes=[pltpu.VMEM((tm, D_FF), BF16)],
        compiler_params=_params(("arbitrary",)),
        name="ffn_in",
    )(x, pre_g, *ffn_w, post_g, mix_g, win, cos, slo, shi)


def _complex_step(s_re, s_im, l_re, l_im, x_re, x_im):
    return l_re * s_re - l_im * s_im + x_re, l_re * s_im + l_im * s_re + x_im


def _ssm_prompt_kernel(u_ref, m_ref, w_ref, v_ref, lam_ref, d_ref, y_ref, st_ref,
                       x_scr, ss_scr, s_scr, *, lc, nb, ncl, nlb):
    tt = pl.program_id(1)
    nrow = nlb * nb
    nq = STATE_LANES // LANES

    @pl.when(tt == 0)
    def _():
        s_scr[...] = jnp.zeros_like(s_scr)

    def piece(b, n, j):
        return u_ref[b, n, pl.ds(j, ncl, stride=lc), :]

    a = []
    for b in range(nlb):
        a.append(jnp.concatenate(
            [jnp.concatenate([piece(b, n, j) for j in range(lc)], axis=1) for n in range(nb)], axis=0).astype(BF16))
        x = jnp.dot(a[b], w_ref[b], preferred_element_type=F32)
        for qq in range(2 * nq):
            for n in range(nb):
                x_scr[qq, pl.ds(b * nb + n, ncl, stride=nrow), :] = x[n * ncl:(n + 1) * ncl, qq * LANES:(qq + 1) * LANES]
    lam_rows = lambda qq: jnp.concatenate(
        [jnp.broadcast_to(lam_ref[b, :, qq * LANES:(qq + 1) * LANES], (nb, LANES)) for b in range(nlb)], axis=0)
    l_re = [lam_rows(qq) for qq in range(nq)]
    l_im = [lam_rows(nq + qq) for qq in range(nq)]

    s = [s_scr[qq] for qq in range(2 * nq)]
    for c in range(ncl):
        rows = slice(c * nrow, (c + 1) * nrow)
        for qq in range(nq):
            ss_scr[qq, rows, :] = s[qq]
            ss_scr[nq + qq, rows, :] = s[nq + qq]
            s[qq], s[nq + qq] = _complex_step(s[qq], s[nq + qq], l_re[qq], l_im[qq],
                                              x_scr[qq, rows, :], x_scr[nq + qq, rows, :])
    for qq in range(2 * nq):
        s_scr[qq] = s[qq]
    for b in range(nlb):
        st_ref[b] = jnp.concatenate([s[qq][b * nb:(b + 1) * nb] for qq in range(2 * nq)], axis=1)
        s_start = jnp.concatenate(
            [jnp.concatenate([ss_scr[qq, pl.ds(b * nb + n, ncl, stride=nrow), :] for n in range(nb)], axis=0)
             for qq in range(2 * nq)], axis=1).astype(BF16)
        y = (jnp.dot(a[b], m_ref[b], preferred_element_type=F32)
             + jnp.dot(s_start, v_ref[b], preferred_element_type=F32))
        d = d_ref[b]
        for n in range(nb):
            for j in range(lc):
                y_ref[b, n, pl.ds(j, ncl, stride=lc), :] = (
                    y[n * ncl:(n + 1) * ncl, j * LANES:(j + 1) * LANES] + d * piece(b, n, j))


def _ssm_prompt(u4, m, w, v, lam, d, lc, tl, nlb):
    _, nb, seq, _ = u4.shape
    ncl = tl // lc
    nslab = 2 * STATE_LANES // LANES
    kern = functools.partial(_ssm_prompt_kernel, lc=lc, nb=nb, ncl=ncl, nlb=nlb)
    wspec = lambda a: pl.BlockSpec((nlb,) + a.shape[1:], lambda p, t: (p, 0, 0), pipeline_mode=pl.Buffered(1))
    io = pl.BlockSpec((nlb, nb, tl, LANES), lambda p, t: (p, 0, t, 0))
    return pl.pallas_call(
        kern,
        grid=(N_LANE_BLOCKS // nlb, seq // tl),
        in_specs=[io, wspec(m), wspec(w), wspec(v), wspec(lam), wspec(d)],
        out_specs=[io, pl.BlockSpec((nlb, nb, 2 * STATE_LANES), lambda p, t: (p, 0, 0))],
        out_shape=[jax.ShapeDtypeStruct(u4.shape, F32),
                   jax.ShapeDtypeStruct((N_LANE_BLOCKS, nb, 2 * STATE_LANES), F32)],
        scratch_shapes=[pltpu.VMEM((nslab, nlb * nb * ncl, LANES), F32),
                        pltpu.VMEM((nslab, nlb * nb * ncl, LANES), F32),
                        pltpu.VMEM((nslab, nlb * nb, LANES), F32)],
        compiler_params=_params(("arbitrary", "arbitrary")),
        name="ssm_prompt",
    )(u4, m, w, v, lam, d)


def _ssm_sample_kernel(u_ref, sre_ref, sim_ref, m_ref, w_ref, v_ref, lam_ref, d_ref,
                       y_ref, ore_ref, oim_ref, *, lc, ns):
    def piece(j):
        return u_ref[pl.ds(j, ns, stride=lc), :]

    a = jnp.concatenate([piece(j) for j in range(lc)], axis=1).astype(BF16)
    s_re, s_im = sre_ref[...], sim_ref[...]
    x = jnp.dot(a, w_ref[0], preferred_element_type=F32)
    e_re, e_im = _complex_step(s_re, s_im, lam_ref[0, :, :STATE_LANES], lam_ref[0, :, STATE_LANES:],
                               x[:, :STATE_LANES], x[:, STATE_LANES:])
    ore_ref[...] = e_re
    oim_ref[...] = e_im
    s0 = jnp.concatenate([s_re, s_im], axis=1).astype(BF16)
    y = (jnp.dot(a, m_ref[0], preferred_element_type=F32) + jnp.dot(s0, v_ref[0], preferred_element_type=F32))
    d = d_ref[...]
    for j in range(lc):
        y_ref[pl.ds(j, ns, stride=lc), :] = y[:, j * LANES:(j + 1) * LANES] + d * piece(j)


def _ssm_sample(u, s_re, s_im, m, w, v, lam, d, lc):
    t = u.shape[0]
    ns = t // lc
    kern = functools.partial(_ssm_sample_kernel, lc=lc, ns=ns)
    assert m.shape[1] == 2 * lc * LANES
    n = lc * LANES
    col = lambda rows, width: pl.BlockSpec((rows, width), lambda b: (0, b))
    return pl.pallas_call(
        kern,
        grid=(N_LANE_BLOCKS,),
        in_specs=[col(t, LANES), col(ns, STATE_LANES), col(ns, STATE_LANES),
                  pl.BlockSpec((1, n, n), lambda b: (b, 0, 0)),
                  pl.BlockSpec((1, n, 2 * STATE_LANES), lambda b: (b, 1, 0)),
                  pl.BlockSpec((1, 2 * STATE_LANES, n), lambda b: (b, 0, 0)),
                  pl.BlockSpec((1, 1, 2 * STATE_LANES), lambda b: (b, 0, 0)), col(1, LANES)],
        out_specs=[col(t, LANES), col(ns, STATE_LANES), col(ns, STATE_LANES)],
        out_shape=[jax.ShapeDtypeStruct(u.shape, F32), jax.ShapeDtypeStruct(s_re.shape, F32),
                   jax.ShapeDtypeStruct(s_im.shape, F32)],
        compiler_params=_params(("arbitrary",)),
        name="ssm_sample",
    )(u, s_re, s_im, m, w, v, lam, d)


def _ssm_discretise(a_re, a_im, log_step):
    dt = jnp.exp(log_step)
    mag = jnp.exp(a_re * dt)
    l_re, l_im = mag * jnp.cos(a_im * dt), mag * jnp.sin(a_im * dt)
    den = a_re * a_re + a_im * a_im
    n_re, n_im = l_re - 1.0, l_im
    return l_re, l_im, (n_re * a_re + n_im * a_im) / den, (n_im * a_re - n_re * a_im) / den


def _complex_powers(l_re, l_im, n):
    p_re, p_im = [jnp.ones_like(l_re)], [jnp.zeros_like(l_re)]
    for _ in range(n):
        p_re, p_im = p_re + [p_re[-1] * l_re - p_im[-1] * l_im], p_im + [p_re[-1] * l_im + p_im[-1] * l_re]
    return p_re, p_im


def _split_bf16(x):
    hi = x.astype(BF16)
    return hi, (x - hi.astype(F32)).astype(BF16)


def _dot_split(a, b):
    dot = lambda x, y: jnp.dot(x, y, preferred_element_type=F32)
    return dot(a[0], b[0]) + (dot(a[0], b[1]) + dot(a[1], b[0]))


def _ssm_tables_kernel(ac_re_ref, ac_im_ref, lsc_ref, ar_re_ref, ar_im_ref, lsr_ref, b_re_ref, b_im_ref,
                       c_re_ref, c_im_ref, m_ref, w_ref, v_ref, lam_ref, lam_half_ref, *, lc):
    l_re, l_im, cf_re, cf_im = _ssm_discretise(ac_re_ref[0], ac_im_ref[0], lsc_ref[0])
    b_re, b_im, c_re, c_im = b_re_ref[0], b_im_ref[0], c_re_ref[0], c_im_ref[0]
    c_re_parts, c_im_parts = _split_bf16(c_re), _split_bf16(c_im)
    bb_re = cf_re * b_re - cf_im * b_im
    bb_im = cf_re * b_im + cf_im * b_re
    p_re, p_im = _complex_powers(l_re, l_im, lc)
    lag = []
    for k in range(lc):
        et_re = (p_re[k] * bb_re - p_im[k] * bb_im).T
        et_im = (p_re[k] * bb_im + p_im[k] * bb_re).T
        j = lc - 1 - k
        w_ref[0, j * LANES:(j + 1) * LANES, :STATE_LANES] = et_re.astype(BF16)
        w_ref[0, j * LANES:(j + 1) * LANES, STATE_LANES:] = et_im.astype(BF16)
        lag.append((_dot_split(_split_bf16(et_re), c_re_parts) - _dot_split(_split_bf16(et_im), c_im_parts)).astype(BF16))
        v_ref[0, :STATE_LANES, k * LANES:(k + 1) * LANES] = (p_re[k + 1] * c_re - p_im[k + 1] * c_im).astype(BF16)
        v_ref[0, STATE_LANES:, k * LANES:(k + 1) * LANES] = (-(p_im[k + 1] * c_re + p_re[k + 1] * c_im)).astype(BF16)
    zero = jnp.zeros((LANES, LANES), BF16)
    for j in range(lc):
        for jj in range(lc):
            m_ref[0, j * LANES:(j + 1) * LANES, jj * LANES:(jj + 1) * LANES] = lag[jj - j] if jj >= j else zero
    r_re, r_im, _, _ = _ssm_discretise(ar_re_ref[0], ar_im_ref[0], lsr_ref[0])
    q_re, q_im = _complex_powers(r_re, r_im, lc)
    lam_ref[0] = jnp.concatenate([q_re[lc], q_im[lc]], axis=1)
    lam_half_ref[0] = jnp.concatenate([q_re[lc // 2], q_im[lc // 2]], axis=1)


def _ssm_tables(a_re, a_im, log_step, b_re, b_im, c_re, c_im, lc):
    g, p, h = b_re.shape
    nbk, r = N_LANE_BLOCKS, GROUPS_PER_LANE_BLOCK
    ls = jnp.broadcast_to(log_step[:, None], (g, p))
    cols = [jnp.broadcast_to(x.reshape(nbk, STATE_LANES, 1), (nbk, STATE_LANES, LANES)) for x in (a_re, a_im, ls)]
    rows = [x.reshape(nbk, 1, STATE_LANES) for x in (a_re, a_im, ls)]
    eye = jnp.eye(r, dtype=F32)[None, :, None, :, None]

    def block_diag(x):
        return (x[:, :, :, None, :] * eye).reshape(nbk, STATE_LANES, LANES)

    mats = [block_diag(b_re.reshape(nbk, r, p, h)), block_diag(b_im.reshape(nbk, r, p, h)),
            block_diag(c_re.reshape(nbk, r, h, p).transpose(0, 1, 3, 2)),
            block_diag(c_im.reshape(nbk, r, h, p).transpose(0, 1, 3, 2))]
    spec = lambda shape: pl.BlockSpec((1,) + shape, lambda b: (b, 0, 0))
    n = lc * LANES
    return pl.pallas_call(
        functools.partial(_ssm_tables_kernel, lc=lc),
        grid=(nbk,),
        in_specs=[spec((STATE_LANES, LANES))] * 3 + [spec((1, STATE_LANES))] * 3 + [spec((STATE_LANES, LANES))] * 4,
        out_specs=[spec((n, n)), spec((n, 2 * STATE_LANES)), spec((2 * STATE_LANES, n)),
                   spec((1, 2 * STATE_LANES)), spec((1, 2 * STATE_LANES))],
        out_shape=[jax.ShapeDtypeStruct((nbk, n, n), BF16), jax.ShapeDtypeStruct((nbk, n, 2 * STATE_LANES), BF16),
                   jax.ShapeDtypeStruct((nbk, 2 * STATE_LANES, n), BF16),
                   jax.ShapeDtypeStruct((nbk, 1, 2 * STATE_LANES), F32),
                   jax.ShapeDtypeStruct((nbk, 1, 2 * STATE_LANES), F32)],
        compiler_params=_params(("arbitrary",)),
        name="ssm_tables",
    )(*cols, *rows, *mats)


def _nt_dot(a, b):
    return lax.dot_general(a, b, (((1,), (1,)), ((), ())), preferred_element_type=F32)


def _swa_prompt_tile(sink_ref, qt_ref, kcat, vcat_t, mstd_ref, mfirst_ref, g_ref, ot_ref, first_tile, nblk):
    keys = lambda j: slice(j * WINDOW, (j + 2) * WINDOW)
    zeros = jnp.zeros((HEAD_DIM, WINDOW), BF16)
    sinks = [jnp.concatenate([jnp.full((1, WINDOW), sink_ref[kh * GQA_GROUP + g], F32) for g in range(GQA_GROUP)],
                             axis=1) for kh in range(N_KV_HEADS)]
    chains = [(j, kh) for j in range(nblk) for kh in range(N_KV_HEADS)]

    def scores(j, kh):
        def rhs(g):
            h = kh * GQA_GROUP + g
            q = qt_ref[h * HEAD_DIM:(h + 1) * HEAD_DIM, j * WINDOW:(j + 1) * WINDOW]
            return jnp.concatenate([q, zeros] if kh == 0 else [zeros, q], axis=0)
        r = jnp.concatenate([rhs(g) for g in range(GQA_GROUP)], axis=1)
        vmask = mstd_ref[...]
        if j == 0:
            vmask = jnp.where(first_tile, mfirst_ref[...], vmask)
        valid = jnp.tile(vmask, (1, GQA_GROUP)) > 0.0
        return jnp.where(valid, jnp.dot(kcat[keys(j)], r, preferred_element_type=F32), NEG_INF)

    def softmax(kh, sc):
        m = jnp.maximum(jnp.max(sc, axis=0, keepdims=True), sinks[kh])
        e = jnp.exp(sc - m)
        return e.astype(BF16), 1.0 / (jnp.sum(e, axis=0, keepdims=True) + jnp.exp(sinks[kh] - m))

    sc = [scores(*c) for c in chains]
    pr = [softmax(kh, x) for (j, kh), x in zip(chains, sc)]

    def finish():
        out = [jnp.dot(vcat_t[kh * HEAD_DIM:(kh + 1) * HEAD_DIM, keys(j)], e, preferred_element_type=F32) * inv
               for (j, kh), (e, inv) in zip(chains, pr)]
        gain = g_ref[...]
        for j in range(nblk):
            o = jnp.concatenate([out[j * N_KV_HEADS + kh][:, g * WINDOW:(g + 1) * WINDOW]
                                 for kh in range(N_KV_HEADS) for g in range(GQA_GROUP)], axis=0)
            scale = lax.rsqrt(jnp.mean(o * o, axis=0, keepdims=True) + RMS_EPS)
            ot_ref[:, j * WINDOW:(j + 1) * WINDOW] = (o * scale * gain).astype(BF16)

    return finish


def _swa_masks():
    kj = jnp.arange(2 * WINDOW)[:, None]
    diff = jnp.arange(WINDOW)[None, :] + WINDOW - kj
    std = (diff >= 0) & (diff <= WINDOW)
    return std.astype(F32), (std & (kj >= WINDOW)).astype(F32)


def _ffn_in_swa_kernel(sink_ref, x_ref, pre_g_ref, wg_ref, wu_ref, wd_ref, post_g_ref, mix_g_ref, wuk_ref, wqv_t_ref,
                       cos_ref, slo_ref, shi_ref, cos_t_ref, sin_t_ref, mstd_ref, mfirst_ref, swa_g_ref,
                       h_ref, u_ref, k_ref, vt_ref, ot_ref, act_ref, q_scr, kcat_scr, vcat_scr, *, nt, nblk):
    i = pl.program_id(0)
    tm = x_ref.shape[0]

    @pl.when(i == 0)
    def _():
        q_scr[...] = jnp.zeros_like(q_scr)
        kcat_scr[...] = jnp.zeros_like(kcat_scr)
        vcat_scr[...] = jnp.zeros_like(vcat_scr)

    first_tile = lax.rem(i + (nt - 1), nt) == 0
    swa_finish = _swa_prompt_tile(sink_ref, q_scr, kcat_scr[...], vcat_scr[...], mstd_ref, mfirst_ref, swa_g_ref,
                                  ot_ref, first_tile, nblk)

    kcat_scr[:WINDOW, :] = kcat_scr[tm:, :]
    vcat_scr[:, :WINDOW] = vcat_scr[:, tm:]
    half = ROPE_DIM // 2

    def row_block(r, swa_hook):
        n = r.stop - r.start
        h = yield from _ffn_stages(x_ref[r, :], pre_g_ref[...], wg_ref, wu_ref, wd_ref, post_g_ref[...], act_ref, r,
                                   after_chunk=swa_hook)
        h_ref[r, :] = h
        hn = _rms(h, mix_g_ref[...]).astype(BF16)
        z = jnp.dot(hn, wuk_ref[...], preferred_element_type=F32)
        yield
        for b in range(N_LANE_BLOCKS):
            u_ref[b, r, :] = z[:, b * LANES:(b + 1) * LANES]
        k = _rope(z[:, SSM_WIDTH:], cos_ref[r, :], slo_ref[r, :], shi_ref[r, :])
        k_ref[r, :] = k
        zt = _nt_dot(wqv_t_ref[...], hn)
        yield
        vt = zt[Q_WIDTH:]
        vt_ref[:, r] = vt
        q3 = zt[:Q_WIDTH].reshape(N_HEADS, HEAD_DIM, n)
        x1, x2 = q3[:, :half], q3[:, half:ROPE_DIM]
        cos, sin = cos_t_ref[:, r][None], sin_t_ref[:, r][None]
        q3 = jnp.concatenate([x1 * cos - x2 * sin, x2 * cos + x1 * sin, q3[:, ROPE_DIM:]], axis=1)
        shifted = slice(WINDOW + r.start, WINDOW + r.stop)
        kcat_scr[shifted, :] = k.astype(BF16)
        vcat_scr[:, shifted] = vt.astype(BF16)
        q_scr[:, r] = (q3.reshape(Q_WIDTH, n) * (HEAD_DIM ** -0.5)).astype(BF16)

    blocks = [slice(b * tm // ROW_BLOCKS, (b + 1) * tm // ROW_BLOCKS) for b in range(ROW_BLOCKS)]
    _interleave([row_block(r, (N_FF_CHUNKS - 3, swa_finish) if r.start == 0 else None) for r in blocks])


def _ffn_in_swa(x, pre_g, ffn_w, post_g, mix_g, wuk, wqv_t, cos, slo, shi, cos_t, sin_t, sinks, swa_g, seq, tm):
    t = x.shape[0]
    n_tiles, nt, nblk = t // tm, seq // tm, tm // WINDOW
    half = ROPE_DIM // 2
    mstd, mfirst = _swa_masks()
    gain = jnp.broadcast_to(swa_g.reshape(Q_WIDTH, 1), (Q_WIDTH, WINDOW))
    cur = lambda i: jnp.minimum(i, n_tiles - 1)
    row = lambda w: pl.BlockSpec((tm, w), lambda i: (cur(i), 0))
    tab = pl.BlockSpec((tm, LANES), lambda i: (cur(i) % nt, 0))
    tab_t = pl.BlockSpec((half, tm), lambda i: (0, cur(i) % nt))
    return pl.pallas_call(
        functools.partial(_ffn_in_swa_kernel, nt=nt, nblk=nblk),
        grid=(n_tiles + 1,),
        in_specs=[pl.BlockSpec(memory_space=pltpu.SMEM), row(D_MODEL), _const_spec((1, D_MODEL)),
                  *[_const_spec(w.shape) for w in ffn_w], _const_spec((1, D_MODEL)), _const_spec((1, D_MODEL)),
                  _const_spec(wuk.shape), _const_spec(wqv_t.shape), tab, tab, tab, tab_t, tab_t,
                  _const_spec(mstd.shape), _const_spec(mfirst.shape), _const_spec(gain.shape)],
        out_specs=[row(D_MODEL), pl.BlockSpec((N_LANE_BLOCKS, tm, LANES), lambda i: (0, cur(i), 0)), row(KV_WIDTH),
                   pl.BlockSpec((KV_WIDTH, tm), lambda i: (0, cur(i))),
                   pl.BlockSpec((Q_WIDTH, tm), lambda i: (0, jnp.maximum(i - 1, 0)))],
        out_shape=[jax.ShapeDtypeStruct((t, D_MODEL), F32), jax.ShapeDtypeStruct((N_LANE_BLOCKS, t, LANES), F32),
                   jax.ShapeDtypeStruct((t, KV_WIDTH), F32), jax.ShapeDtypeStruct((KV_WIDTH, t), F32),
                   jax.ShapeDtypeStruct((Q_WIDTH, t), BF16)],
        scratch_shapes=[pltpu.VMEM((tm, D_FF), BF16), pltpu.VMEM((Q_WIDTH, tm), BF16),
                        pltpu.VMEM((WINDOW + tm, KV_WIDTH), BF16), pltpu.VMEM((KV_WIDTH, WINDOW + tm), BF16)],
        compiler_params=_params(("arbitrary",)),
        name="ffn_in_swa",
    )(sinks, x, pre_g, *ffn_w, post_g, mix_g, wuk, wqv_t, cos, slo, shi, cos_t, sin_t, mstd, mfirst, gain)


def _swa_sample_kernel(sink_ref, q_ref, kn_ref, vn_ref, ck_ref, cv_ref, g_ref, o_ref, nk_ref, nv_ref, *, ns, t):
    rows = GQA_GROUP * t
    tok = lax.broadcasted_iota(jnp.int32, (rows, WINDOW), 0) % t
    valid_c = lax.broadcasted_iota(jnp.int32, (rows, WINDOW), 1) >= tok
    tok_n = lax.broadcasted_iota(jnp.int32, (rows, t), 0) % t
    valid_n = lax.broadcasted_iota(jnp.int32, (rows, t), 1) <= tok_n
    gain = g_ref[...]

    sinks = [jnp.concatenate([jnp.full((t, 1), sink_ref[kh * GQA_GROUP + g], F32) for g in range(GQA_GROUP)], axis=0)
             for kh in range(N_KV_HEADS)]
    heads = [(s, kh) for s in range(ns) for kh in range(N_KV_HEADS)]
    hs = lambda kh: slice(kh * HEAD_DIM, (kh + 1) * HEAD_DIM)

    newest = lax.broadcasted_iota(jnp.int32, (KV_WIDTH, WINDOW), 1) >= WINDOW - t
    pad = jnp.zeros((WINDOW - 2 * t, KV_WIDTH), F32)

    def shifted(cache_ref, new_ref, s):
        new_rows = jnp.concatenate([pad, jnp.zeros((t, KV_WIDTH), F32), new_ref[s]], axis=0)
        return jnp.where(newest, new_rows.T, pltpu.roll(cache_ref[s], WINDOW - t, 1))

    for s in range(ns):
        nk_ref[s] = shifted(ck_ref, kn_ref, s)
        nv_ref[s] = shifted(cv_ref, vn_ref, s)

    def scores(s, kh):
        q = q_ref[s].astype(F32)
        q4 = jnp.concatenate([q[:, (kh * GQA_GROUP + g) * HEAD_DIM:(kh * GQA_GROUP + g + 1) * HEAD_DIM]
                              for g in range(GQA_GROUP)], axis=0).astype(BF16)
        sc_c = jnp.where(valid_c, jnp.dot(q4, ck_ref[s, hs(kh), :].astype(BF16), preferred_element_type=F32), NEG_INF)
        sc_n = jnp.where(valid_n, _nt_dot(q4, kn_ref[s, :, hs(kh)].astype(BF16)), NEG_INF)
        return sc_c, sc_n

    def softmax(kh, sc_c, sc_n):
        m = jnp.maximum(jnp.maximum(jnp.max(sc_c, axis=-1, keepdims=True),
                                    jnp.max(sc_n, axis=-1, keepdims=True)), sinks[kh])
        e_c, e_n = jnp.exp(sc_c - m), jnp.exp(sc_n - m)
        inv = 1.0 / (jnp.sum(e_c, axis=-1, keepdims=True) + jnp.sum(e_n, axis=-1, keepdims=True)
                     + jnp.exp(sinks[kh] - m))
        return (e_c * inv).astype(BF16), (e_n * inv).astype(BF16)

    def values(s, kh, p_c, p_n):
        return (_nt_dot(p_c, cv_ref[s, hs(kh), :].astype(BF16))
                + jnp.dot(p_n, vn_ref[s, :, hs(kh)].astype(BF16), preferred_element_type=F32))

    sc = [scores(s, kh) for s, kh in heads]
    pr = [softmax(kh, *x) for (s, kh), x in zip(heads, sc)]
    o4 = [values(s, kh, *x) for (s, kh), x in zip(heads, pr)]
    for s in range(ns):
        o = jnp.concatenate([o4[s * N_KV_HEADS + kh][g * t:(g + 1) * t]
                             for kh in range(N_KV_HEADS) for g in range(GQA_GROUP)], axis=1)
        o_ref[s] = _rms(o, gain).astype(BF16)


def _swa_sample(q3, kn3, vn3, ck, cv, sinks, out_g, sb):
    ns, t, _ = q3.shape
    blk = lambda a: pl.BlockSpec((sb,) + a.shape[1:], lambda i: (i, 0, 0))
    kern = functools.partial(_swa_sample_kernel, ns=sb, t=t)
    return pl.pallas_call(
        kern,
        grid=(ns // sb,),
        in_specs=[pl.BlockSpec(memory_space=pltpu.SMEM), blk(q3), blk(kn3), blk(vn3), blk(ck), blk(cv),
                  pl.BlockSpec((1, Q_WIDTH), lambda i: (0, 0))],
        out_specs=[blk(q3), blk(ck), blk(cv)],
        out_shape=[jax.ShapeDtypeStruct(q3.shape, BF16), jax.ShapeDtypeStruct(ck.shape, F32),
                   jax.ShapeDtypeStruct(cv.shape, F32)],
        compiler_params=_params(("arbitrary",)),
        name="swa_sample",
    )(sinks, q3, kn3, vn3, ck, cv, out_g)


def _gelu_tanh(x):
    return 0.5 * x * (1.0 + jnp.tanh(math.sqrt(2.0 / math.pi) * (x + 0.044715 * (x * x * x))))


def _merge_stages(h, y, at, wglu_ref, bglu, sg, wout_ref, post_g, xa_g, wq_ref, at_transposed=False):
    g = _gelu_tanh(y)
    lin = jnp.dot(g.astype(BF16), wglu_ref[...], preferred_element_type=F32) + bglu
    yield
    y_ssm = g * (1.0 / (1.0 + jnp.exp(-lin)))
    ssm_n = _rms(y_ssm, sg).astype(BF16)
    at_dims = (((0,), (0,)), ((), ())) if at_transposed else (((1,), (0,)), ((), ()))
    mixed = (jnp.dot(ssm_n, wout_ref[:SSM_WIDTH, :], preferred_element_type=F32)
             + lax.dot_general(at, wout_ref[SSM_WIDTH:, :], at_dims, preferred_element_type=F32))
    yield
    h2 = h + _rms(mixed, post_g)
    qm = jnp.dot(_rms(h2, xa_g).astype(BF16), wq_ref[...], preferred_element_type=F32)
    yield
    return h2, (qm * (MEM_HEAD_DIM ** -0.5)).astype(BF16)


def _merge_tile(*args, **kwargs):
    return _run(_merge_stages(*args, **kwargs))


def _merge_kernel(h_ref, y_ref, at_ref, wglu_ref, bglu_ref, sg_ref, wout_ref, post_g_ref, xa_g_ref, wq_ref,
                  h2_ref, qm_ref):
    h2_ref[...], qm_ref[...] = _merge_tile(h_ref[...], y_ref[...], at_ref[...], wglu_ref, bglu_ref[...], sg_ref[...],
                                           wout_ref, post_g_ref[...], xa_g_ref[...], wq_ref)


def _merge_specs(wglu, wout, wq):
    return [_const_spec(wglu.shape), _const_spec((1, SSM_WIDTH)), _const_spec((1, SSM_WIDTH)), _const_spec(wout.shape),
            _const_spec((1, D_MODEL)), _const_spec((1, D_MODEL)), _const_spec(wq.shape)]


def _merge(h, y, at, merge_w, tm):
    t = h.shape[0]
    row = lambda w: pl.BlockSpec((tm, w), lambda i: (i, 0))
    return pl.pallas_call(
        _merge_kernel,
        grid=(t // tm,),
        in_specs=[row(D_MODEL), row(SSM_WIDTH), row(Q_WIDTH)] + _merge_specs(merge_w[0], merge_w[3], merge_w[6]),
        out_specs=[row(D_MODEL), row(D_MODEL)],
        out_shape=[jax.ShapeDtypeStruct((t, D_MODEL), F32), jax.ShapeDtypeStruct((t, D_MODEL), BF16)],
        compiler_params=_params(("arbitrary",)),
        name="merge",
    )(h, y, at, *merge_w)


def _mem_attn_sample_kernel(q_ref, k_ref, v_ref, o_ref, *, gb, t):
    rows = t * MEM_HEADS
    halves = MEM_HEAD_DIM // LANES
    kv_rows = N_MEM * halves * MEM_HEADS
    period = halves * MEM_HEADS
    lane = lax.broadcasted_iota(jnp.int32, (rows, kv_rows), 1) % period
    head = lax.broadcasted_iota(jnp.int32, (rows, kv_rows), 0) % MEM_HEADS
    in_half = [lane == head + hf * MEM_HEADS for hf in range(halves)]
    for b in range(gb):
        kb = k_ref[b].astype(BF16)
        vb = v_ref[b].astype(BF16)
        part = _nt_dot(q_ref[b], kb)
        sc = jnp.where(in_half[0], part[:rows], 0.0)
        for hf in range(1, halves):
            sc = sc + pltpu.roll(jnp.where(in_half[hf], part[hf * rows:(hf + 1) * rows], 0.0),
                                 kv_rows - hf * MEM_HEADS, 1)
        sc = jnp.where(in_half[0], sc, -jnp.inf)
        e = jnp.exp(sc - jnp.max(sc, axis=-1, keepdims=True))
        inv = 1.0 / jnp.sum(e, axis=-1, keepdims=True)
        e_all = jnp.concatenate([e] + [pltpu.roll(e, hf * MEM_HEADS, 1) for hf in range(1, halves)], axis=0)
        o = jnp.dot(e_all.astype(BF16), vb, preferred_element_type=F32)
        o_ref[b] = (o * jnp.concatenate([inv] * halves, axis=0)).astype(BF16)


def _mem_attn_sample(qm, cache_k, cache_v, n_s, t_s, gb):
    halves = MEM_HEAD_DIM // LANES
    rows = halves * t_s * MEM_HEADS
    kv_rows = N_MEM * halves * MEM_HEADS

    def stored_rows(c):
        c = c.reshape(n_s, N_MEM, MEM_HEADS, halves, LANES).transpose(0, 1, 3, 2, 4)
        return c.reshape(n_s, kv_rows, LANES)

    q = qm.reshape(n_s, t_s, MEM_HEADS, halves, LANES).transpose(0, 3, 1, 2, 4).reshape(n_s, rows, LANES)
    blk = lambda r: pl.BlockSpec((gb, r, LANES), lambda i: (i, 0, 0))
    o = pl.pallas_call(
        functools.partial(_mem_attn_sample_kernel, gb=gb, t=t_s),
        grid=(n_s // gb,),
        in_specs=[blk(rows), blk(kv_rows), blk(kv_rows)],
        out_specs=blk(rows),
        out_shape=jax.ShapeDtypeStruct((n_s, rows, LANES), BF16),
        compiler_params=_params(("arbitrary",)),
        name="mem_attn_sample",
    )(q, stored_rows(cache_k), stored_rows(cache_v))
    o = o.reshape(n_s, halves, t_s, MEM_HEADS, LANES).transpose(0, 2, 3, 1, 4)
    return o.reshape(n_s * t_s, D_MODEL)


def _mem_heads_stages(q, k_ref, v_ref):
    outs = []
    for hh in range(MEM_HEADS):
        hs = slice(hh * MEM_HEAD_DIM, (hh + 1) * MEM_HEAD_DIM)
        sc = _nt_dot(q[:, hs], k_ref[0, :, hs])
        yield
        e = jnp.exp(sc - jnp.max(sc, axis=-1, keepdims=True))
        inv = 1.0 / jnp.sum(e, axis=-1, keepdims=True)
        outs.append((jnp.dot(e.astype(BF16), v_ref[0, :, hs], preferred_element_type=F32) * inv).astype(BF16))
        yield
    return jnp.concatenate(outs, axis=1)


def _ffn_out_stages(h2, om, wo_ref, xa_post, pre_g, wg_ref, wu_ref, wd_ref, post_g, act_ref, rows=slice(None)):
    c = jnp.dot(om, wo_ref[...], preferred_element_type=F32)
    yield
    h3 = h2 + _rms(c, xa_post)
    return (yield from _ffn_stages(h3, pre_g, wg_ref, wu_ref, wd_ref, post_g, act_ref, rows))


def _ffn_out_tile(*args, **kwargs):
    return _run(_ffn_out_stages(*args, **kwargs))


def _ffn_out_specs(wo, ffn_w):
    return [_const_spec(wo.shape), _const_spec((1, D_MODEL)), _const_spec((1, D_MODEL)),
            *[_const_spec(w.shape) for w in ffn_w], _const_spec((1, D_MODEL))]


def _ffn_out_kernel(h_ref, o_ref, wo_ref, xa_post_ref, pre_g_ref, wg_ref, wu_ref, wd_ref, post_g_ref, out_ref, act_ref):
    out_ref[...] = _ffn_out_tile(h_ref[...], o_ref[...], wo_ref, xa_post_ref[...], pre_g_ref[...], wg_ref, wu_ref,
                                 wd_ref, post_g_ref[...], act_ref)


def _ffn_out(h, o, out_w, tm):
    t = h.shape[0]
    row = pl.BlockSpec((tm, D_MODEL), lambda i: (i, 0))
    return pl.pallas_call(
        _ffn_out_kernel,
        grid=(t // tm,),
        in_specs=[row, row] + _ffn_out_specs(out_w[0], out_w[3:6]),
        out_specs=row,
        out_shape=jax.ShapeDtypeStruct((t, D_MODEL), F32),
        scratch_shapes=[pltpu.VMEM((tm, D_FF), BF16)],
        compiler_params=_params(("arbitrary",)),
        name="ffn_out",
    )(h, o, *out_w)


def _post_kernel(h_ref, y_ref, at_ref, k_ref, v_ref,
                 wglu_ref, bglu_ref, sg_ref, wout_ref, post_g_ref, xa_g_ref, wq_ref,
                 wo_ref, xa_post_ref, pre_g_ref, wg_ref, wu_ref, wd_ref, ffn_post_ref, out_ref, act_ref):
    def row_block(r):
        y = jnp.concatenate([y_ref[b, r, :] for b in range(N_LANE_BLOCKS)], axis=1)
        h2, qm = yield from _merge_stages(h_ref[r, :], y, at_ref[:, r], wglu_ref, bglu_ref[...], sg_ref[...],
                                          wout_ref, post_g_ref[...], xa_g_ref[...], wq_ref, at_transposed=True)
        om = yield from _mem_heads_stages(qm, k_ref, v_ref)
        out_ref[r, :] = yield from _ffn_out_stages(h2, om, wo_ref, xa_post_ref[...], pre_g_ref[...], wg_ref, wu_ref,
                                                   wd_ref, ffn_post_ref[...], act_ref, r)

    tm = h_ref.shape[0]
    _interleave([row_block(slice(i * tm // ROW_BLOCKS, (i + 1) * tm // ROW_BLOCKS)) for i in range(ROW_BLOCKS)])


def _post(h, y, at, k3, v3, merge_w, out_w, tm):
    t = h.shape[0]
    tiles_per_batch = t // k3.shape[0] // tm
    row = lambda w: pl.BlockSpec((tm, w), lambda i: (i, 0))
    kv = pl.BlockSpec((1, N_MEM, D_MODEL), lambda i: (i // tiles_per_batch, 0, 0))
    return pl.pallas_call(
        _post_kernel,
        grid=(t // tm,),
        in_specs=([row(D_MODEL), pl.BlockSpec((N_LANE_BLOCKS, tm, LANES), lambda i: (0, i, 0)),
                   pl.BlockSpec((Q_WIDTH, tm), lambda i: (0, i)), kv, kv]
                  + _merge_specs(merge_w[0], merge_w[3], merge_w[6]) + _ffn_out_specs(out_w[0], out_w[3:6])),
        out_specs=row(D_MODEL),
        out_shape=jax.ShapeDtypeStruct((t, D_MODEL), F32),
        scratch_shapes=[pltpu.VMEM((tm, D_FF), BF16)],
        compiler_params=_params(("arbitrary",)),
        name="post",
    )(h, y, at, k3, v3, *merge_w, *out_w)


def _mem_kv_kernel(m_ref, g_ref, wkv_ref, k_ref, v_ref, kb_ref, vb_ref):
    kv = jnp.dot(_rms(m_ref[...], g_ref[...]).astype(BF16), wkv_ref[...], preferred_element_type=F32)
    k_ref[...] = kv[:, :D_MODEL]
    v_ref[...] = kv[:, D_MODEL:]
    kb_ref[...] = kv[:, :D_MODEL].astype(BF16)
    vb_ref[...] = kv[:, D_MODEL:].astype(BF16)


def _mem_kv(mem, g, wkv, tm):
    t = mem.shape[0]
    row = pl.BlockSpec((tm, D_MODEL), lambda i: (i, 0))
    return pl.pallas_call(
        _mem_kv_kernel,
        grid=(t // tm,),
        in_specs=[row, _const_spec((1, D_MODEL)), _const_spec(wkv.shape)],
        out_specs=[row] * 4,
        out_shape=[jax.ShapeDtypeStruct((t, D_MODEL), F32)] * 2 + [jax.ShapeDtypeStruct((t, D_MODEL), BF16)] * 2,
        compiler_params=_params(("arbitrary",)),
        name="mem_kv",
    )(mem, g, wkv)


def _rope_tables(pos):
    half = ROPE_DIM // 2
    inv = ROPE_THETA ** (-jnp.arange(half, dtype=F32) * (2.0 / ROPE_DIM))
    ang = pos.astype(F32)[:, None] * inv[None, :]
    cos, sin = jnp.cos(ang), jnp.sin(ang)
    n = pos.shape[0]
    pad = jnp.zeros((n, HEAD_DIM - ROPE_DIM), F32)
    zero = jnp.zeros((n, half), F32)
    cos_h = jnp.concatenate([cos, cos, pad + 1.0], axis=1)
    lo_h = jnp.concatenate([-sin, zero, pad], axis=1)
    hi_h = jnp.concatenate([zero, sin, pad], axis=1)
    rep = LANES // HEAD_DIM
    return tuple(jnp.tile(a, (1, rep)) for a in (cos_h, lo_h, hi_h)), (cos.T, sin.T)


def _ffn_weights(w_gate, w_up, w_down):
    return w_gate.astype(BF16), w_up.astype(BF16), w_down.astype(BF16)


def _lane_block_states(st, n):
    st = st.reshape(N_LANE_BLOCKS, n, 2, GROUPS_PER_LANE_BLOCK, SSM_STATE).transpose(2, 1, 0, 3, 4)
    st = st.reshape(2, n, N_SSM_GROUPS, SSM_STATE)
    return st[0], st[1]


def kernel(x_prompt, x_sample, state_ssm_re, state_ssm_im, cache_swa_k, cache_swa_v, cache_mem_k, cache_mem_v, mem_prompt, ffn1_pre_g, ffn1_w_gate, ffn1_w_up, ffn1_w_down, ffn1_post_g, mix_pre_g, w_in, ssm_a_re, ssm_a_im, ssm_log_step, ssm_b_re, ssm_b_im, ssm_c_re, ssm_c_im, ssm_d, ssm_w_glu, ssm_b_glu, attn_sinks, ssm_out_g, attn_out_g, w_out, mix_post_g, mem_norm_g, w_mem_q, w_mem_k, w_mem_v, w_mem_o, xa_pre_g, xa_post_g, ffn2_pre_g, ffn2_w_gate, ffn2_w_up, ffn2_w_down, ffn2_post_g):
    n_p, s_p, _ = x_prompt.shape
    n_s, t_s, _ = x_sample.shape
    tm = 512
    row = lambda a: a.reshape(1, -1).astype(F32)

    ffn1_w = _ffn_weights(ffn1_w_gate, ffn1_w_up, ffn1_w_down)
    win = w_in.astype(BF16)
    merge_w = (ssm_w_glu.astype(BF16), row(ssm_b_glu), row(ssm_out_g), w_out.astype(BF16), row(mix_post_g),
               row(xa_pre_g), w_mem_q.astype(BF16))
    out_w = (w_mem_o.astype(BF16), row(xa_post_g), row(ffn2_pre_g),
             *_ffn_weights(ffn2_w_gate, ffn2_w_up, ffn2_w_down), row(ffn2_post_g))
    wkv = jnp.concatenate([w_mem_k, w_mem_v], axis=1).astype(BF16)
    d_row = row(ssm_d)
    ssm_args = (ssm_a_re.astype(F32), ssm_a_im.astype(F32), ssm_log_step.astype(F32), ssm_b_re.astype(F32),
                ssm_b_im.astype(F32), ssm_c_re.astype(F32), ssm_c_im.astype(F32))
    sinks = attn_sinks.astype(F32)

    pm_k, pm_v, pm_kb, pm_vb = _mem_kv(mem_prompt.reshape(n_p * N_MEM, D_MODEL), row(mem_norm_g), wkv, N_MEM)

    def tokenwise_in(x2, pos_tab):
        return _ffn_in(x2, row(ffn1_pre_g), ffn1_w, row(ffn1_post_g), row(mix_pre_g), win, *pos_tab, tm)

    lc_p = 2 * t_s
    ssm_m, ssm_w, ssm_v, lam_p, lam_s = _ssm_tables(*ssm_args, lc_p)
    o1 = SSM_WIDTH + Q_WIDTH
    wuk = jnp.concatenate([win[:, :SSM_WIDTH], win[:, o1:o1 + KV_WIDTH]], axis=1)
    wqv_t = jnp.concatenate([win[:, SSM_WIDTH:o1], win[:, o1 + KV_WIDTH:]], axis=1).T
    tab_p, tab_p_t = _rope_tables(jnp.arange(s_p, dtype=jnp.int32))
    h1, u, k, vt, at = _ffn_in_swa(x_prompt.reshape(n_p * s_p, D_MODEL), row(ffn1_pre_g), ffn1_w, row(ffn1_post_g),
                                   row(mix_pre_g), wuk, wqv_t, *tab_p, *tab_p_t, sinks, attn_out_g.astype(F32),
                                   s_p, tm)
    y4, st_p = _ssm_prompt(u.reshape(N_LANE_BLOCKS, n_p, s_p, LANES), ssm_m, ssm_w, ssm_v, lam_p,
                           d_row.reshape(N_LANE_BLOCKS, 1, LANES), lc_p, 1024, 2)
    y_prompt = _post(h1, y4.reshape(N_LANE_BLOCKS, n_p * s_p, LANES), at, pm_kb.reshape(n_p, N_MEM, D_MODEL),
                     pm_vb.reshape(n_p, N_MEM, D_MODEL), merge_w, out_w, tm).reshape(n_p, s_p, D_MODEL)
    p_sre, p_sim = _lane_block_states(st_p, n_p)
    p_wk = k.reshape(n_p, s_p, KV_WIDTH)[:, -WINDOW:].reshape(n_p, WINDOW, N_KV_HEADS, HEAD_DIM)
    p_wv = jnp.stack([vt[:, (b + 1) * s_p - WINDOW:(b + 1) * s_p] for b in range(n_p)])
    p_wv = p_wv.reshape(n_p, N_KV_HEADS, HEAD_DIM, WINDOW).transpose(0, 3, 1, 2)

    pos_s = jnp.tile(PAST_LEN + jnp.arange(t_s, dtype=jnp.int32), n_s)
    h1s, us, qs, ks, vs = tokenwise_in(x_sample.reshape(n_s * t_s, D_MODEL), _rope_tables(pos_s)[0])
    ys, s_sre, s_sim = _ssm_sample(us, state_ssm_re.reshape(n_s, -1).astype(F32), state_ssm_im.reshape(n_s, -1).astype(F32),
                                   ssm_m, ssm_w, ssm_v, lam_s, d_row, t_s)
    win_len = cache_swa_k.shape[1]
    ats, s_wk, s_wv = _swa_sample(qs.reshape(n_s, t_s, Q_WIDTH), ks.reshape(n_s, t_s, KV_WIDTH), vs.reshape(n_s, t_s, KV_WIDTH),
                                  cache_swa_k.transpose(0, 2, 3, 1).reshape(n_s, KV_WIDTH, win_len),
                                  cache_swa_v.transpose(0, 2, 3, 1).reshape(n_s, KV_WIDTH, win_len),
                                  sinks, row(attn_out_g), 16)
    s_wk = s_wk.reshape(n_s, N_KV_HEADS, HEAD_DIM, win_len).transpose(0, 3, 1, 2)
    s_wv = s_wv.reshape(n_s, N_KV_HEADS, HEAD_DIM, win_len).transpose(0, 3, 1, 2)
    h2s, qms = _merge(h1s, ys, ats.reshape(n_s * t_s, Q_WIDTH), merge_w, tm)
    oms = _mem_attn_sample(qms, cache_mem_k, cache_mem_v, n_s, t_s, 4)
    y_sample = _ffn_out(h2s, oms, out_w, tm).reshape(n_s, t_s, D_MODEL)

    return (y_prompt, y_sample, p_sre, p_sim, p_wk, p_wv,
            pm_k.reshape(n_p, N_MEM, MEM_HEADS, MEM_HEAD_DIM), pm_v.reshape(n_p, N_MEM, MEM_HEADS, MEM_HEAD_DIM),
            s_sre.reshape(n_s, N_SSM_GROUPS, SSM_STATE), s_sim.reshape(n_s, N_SSM_GROUPS, SSM_STATE),
            s_wk, s_wv)
```

```python
import functools
import math

import jax
import jax.numpy as jnp
from jax import lax
from jax.experimental import pallas as pl
from jax.experimental.pallas import tpu as pltpu

F32 = jnp.float32
BF16 = jnp.bfloat16

D_MODEL = 1024
PAST_LEN = 16384
SSM_WIDTH = 512
SSM_GROUP = 16
N_SSM_GROUPS = 32
SSM_STATE = 64
HEAD_DIM = 64
N_HEADS = 8
N_KV_HEADS = 2
GQA_GROUP = 4
Q_WIDTH = 512
KV_WIDTH = 128
WINDOW = 128
ROPE_THETA = 500000.0
ROPE_DIM = 16
N_MEM = 256
MEM_HEADS = 4
MEM_HEAD_DIM = 256
D_FF = 2816
RMS_EPS = 1e-6
IN_WIDTH = SSM_WIDTH + Q_WIDTH + 2 * KV_WIDTH
NEG_INF = -1e30

LANES = 128
FF_CHUNK = 256
ROW_BLOCKS = 2
N_FF_CHUNKS = D_FF // FF_CHUNK
GROUPS_PER_LANE_BLOCK = LANES // SSM_GROUP
N_LANE_BLOCKS = SSM_WIDTH // LANES
STATE_LANES = GROUPS_PER_LANE_BLOCK * SSM_STATE
VMEM_LIMIT = 56 * 1024 * 1024


def _rms(x, g):
    return x * lax.rsqrt(jnp.mean(x * x, axis=-1, keepdims=True) + RMS_EPS) * g


def _const_spec(shape):
    nd = len(shape)
    return pl.BlockSpec(shape, lambda *_: (0,) * nd, pipeline_mode=pl.Buffered(1))


def _params(sem):
    return pltpu.CompilerParams(dimension_semantics=sem, vmem_limit_bytes=VMEM_LIMIT)


def _run(gen):
    try:
        while True:
            next(gen)
    except StopIteration as done:
        return done.value


def _interleave(gens):
    live = list(gens)
    while live:
        for g in list(live):
            try:
                next(g)
            except StopIteration:
                live.remove(g)


def _ffn_stages(x, pre_g, wg_ref, wu_ref, wd_ref, post_g, act_ref, rows=slice(None), after_chunk=None):
    xn = _rms(x, pre_g).astype(BF16)
    yield
    for c in range(N_FF_CHUNKS):
        if after_chunk is not None and c == after_chunk[0] + 1:
            after_chunk[1]()
        cols = slice(c * FF_CHUNK, (c + 1) * FF_CHUNK)
        gate = jnp.dot(xn, wg_ref[:, cols], preferred_element_type=F32)
        up = jnp.dot(xn, wu_ref[:, cols], preferred_element_type=F32)
        act = gate * (1.0 / (1.0 + jnp.exp(-gate))) * up
        act_ref[rows, cols] = act.astype(BF16)
        yield
    down = jnp.dot(act_ref[rows, :], wd_ref[...], preferred_element_type=F32)
    yield
    return x + 0.5 * _rms(down, post_g)


def _ffn_tile(*args, **kwargs):
    return _run(_ffn_stages(*args, **kwargs))


def _rope(x, cos, sin_lo, sin_hi):
    w = x.shape[1]
    half = ROPE_DIM // 2
    return (x * cos + pltpu.roll(x, w - half, 1) * sin_lo + pltpu.roll(x, half, 1) * sin_hi)


def _ffn_in_kernel(x_ref, pre_g_ref, wg_ref, wu_ref, wd_ref, post_g_ref, mix_g_ref, win_ref,
                   cos_ref, slo_ref, shi_ref,
                   h_ref, u_ref, q_ref, k_ref, v_ref, act_ref):
    h = _ffn_tile(x_ref[...], pre_g_ref[...], wg_ref, wu_ref, wd_ref, post_g_ref[...], act_ref)
    h_ref[...] = h
    z = jnp.dot(_rms(h, mix_g_ref[...]).astype(BF16), win_ref[...], preferred_element_type=F32)
    u_ref[...] = z[:, :SSM_WIDTH]
    o1 = SSM_WIDTH + Q_WIDTH
    cos, slo, shi = cos_ref[...], slo_ref[...], shi_ref[...]
    rep = Q_WIDTH // LANES
    q = _rope(z[:, SSM_WIDTH:o1], jnp.tile(cos, (1, rep)), jnp.tile(slo, (1, rep)), jnp.tile(shi, (1, rep)))
    q_ref[...] = (q * (HEAD_DIM ** -0.5)).astype(BF16)
    k_ref[...] = _rope(z[:, o1:o1 + KV_WIDTH], cos, slo, shi)
    v_ref[...] = z[:, o1 + KV_WIDTH:]


def _ffn_in_t_kernel(x_ref, pre_g_ref, wg_ref, wu_ref, wd_ref, post_g_ref, mix_g_ref, wuk_ref, wqv_t_ref,
                     cos_ref, slo_ref, shi_ref, cos_t_ref, sin_t_ref,
                     h_ref, u_ref, k_ref, qt_ref, vt_ref, act_ref):
    h = _ffn_tile(x_ref[...], pre_g_ref[...], wg_ref, wu_ref, wd_ref, post_g_ref[...], act_ref)
    h_ref[...] = h
    hn = _rms(h, mix_g_ref[...]).astype(BF16)
    z = jnp.dot(hn, wuk_ref[...], preferred_element_type=F32)
    for b in range(N_LANE_BLOCKS):
        u_ref[b] = z[:, b * LANES:(b + 1) * LANES]
    k_ref[...] = _rope(z[:, SSM_WIDTH:], cos_ref[...], slo_ref[...], shi_ref[...])
    zt = _nt_dot(wqv_t_ref[...], hn)
    vt_ref[...] = zt[Q_WIDTH:]
    tm = zt.shape[1]
    half = ROPE_DIM // 2
    q3 = zt[:Q_WIDTH].reshape(N_HEADS, HEAD_DIM, tm)
    x1, x2 = q3[:, :half], q3[:, half:ROPE_DIM]
    cos, sin = cos_t_ref[...][None], sin_t_ref[...][None]
    q3 = jnp.concatenate([x1 * cos - x2 * sin, x2 * cos + x1 * sin, q3[:, ROPE_DIM:]], axis=1)
    qt_ref[...] = (q3.reshape(Q_WIDTH, tm) * (HEAD_DIM ** -0.5)).astype(BF16)


def _ffn_in_t(x, pre_g, ffn_w, post_g, mix_g, wuk, wqv_t, cos, slo, shi, cos_t, sin_t, tm):
    t = x.shape[0]
    n_pos_tiles = cos.shape[0] // tm
    half = ROPE_DIM // 2
    row = lambda w: pl.BlockSpec((tm, w), lambda i: (i, 0))
    col = lambda r: pl.BlockSpec((r, tm), lambda i: (0, i))
    tab = pl.BlockSpec((tm, LANES), lambda i: (i % n_pos_tiles, 0))
    tab_t = pl.BlockSpec((half, tm), lambda i: (0, i % n_pos_tiles))
    return pl.pallas_call(
        _ffn_in_t_kernel,
        grid=(t // tm,),
        in_specs=[row(D_MODEL), _const_spec((1, D_MODEL)), *[_const_spec(w.shape) for w in ffn_w],
                  _const_spec((1, D_MODEL)), _const_spec((1, D_MODEL)), _const_spec(wuk.shape),
                  _const_spec(wqv_t.shape), tab, tab, tab, tab_t, tab_t],
        out_specs=[row(D_MODEL), pl.BlockSpec((N_LANE_BLOCKS, tm, LANES), lambda i: (0, i, 0)), row(KV_WIDTH),
                   col(Q_WIDTH), col(KV_WIDTH)],
        out_shape=[jax.ShapeDtypeStruct((t, D_MODEL), F32), jax.ShapeDtypeStruct((N_LANE_BLOCKS, t, LANES), F32),
                   jax.ShapeDtypeStruct((t, KV_WIDTH), F32), jax.ShapeDtypeStruct((Q_WIDTH, t), BF16),
                   jax.ShapeDtypeStruct((KV_WIDTH, t), F32)],
        scratch_shapes=[pltpu.VMEM((tm, D_FF), BF16)],
        compiler_params=_params(("arbitrary",)),
        name="ffn_in_t",
    )(x, pre_g, *ffn_w, post_g, mix_g, wuk, wqv_t, cos, slo, shi, cos_t, sin_t)


def _ffn_in(x, pre_g, ffn_w, post_g, mix_g, win, cos, slo, shi, tm):
    t = x.shape[0]
    n_pos_tiles = cos.shape[0] // tm
    row = lambda w: pl.BlockSpec((tm, w), lambda i: (i, 0))
    tab = pl.BlockSpec((tm, LANES), lambda i: (i % n_pos_tiles, 0))
    return pl.pallas_call(
        _ffn_in_kernel,
        grid=(t // tm,),
        in_specs=[row(D_MODEL), _const_spec((1, D_MODEL)), *[_const_spec(w.shape) for w in ffn_w],
                  _const_spec((1, D_MODEL)), _const_spec((1, D_MODEL)), _const_spec(win.shape),
                  tab, tab, tab],
        out_specs=[row(D_MODEL), row(SSM_WIDTH), row(Q_WIDTH), row(KV_WIDTH), row(KV_WIDTH)],
        out_shape=[jax.ShapeDtypeStruct((t, D_MODEL), F32), jax.ShapeDtypeStruct((t, SSM_WIDTH), F32),
                   jax.ShapeDtypeStruct((t, Q_WIDTH), BF16), jax.ShapeDtypeStruct((t, KV_WIDTH), F32),
                   jax.ShapeDtypeStruct((t, KV_WIDTH), F32)],
        scratch_shapes=[pltpu.VMEM((tm, D_FF), BF16)],
        compiler_params=_params(("arbitrary",)),
        name="ffn_in",
    )(x, pre_g, *ffn_w, post_g, mix_g, win, cos, slo, shi)


def _complex_step(s_re, s_im, l_re, l_im, x_re, x_im):
    return l_re * s_re - l_im * s_im + x_re, l_re * s_im + l_im * s_re + x_im


def _ssm_prompt_kernel(u_ref, m_ref, w_ref, v_ref, lam_ref, d_ref, y_ref, st_ref,
                       x_scr, ss_scr, s_scr, *, lc, nb, ncl, nlb):
    tt = pl.program_id(1)
    nrow = nlb * nb
    nq = STATE_LANES // LANES

    @pl.when(tt == 0)
    def _():
        s_scr[...] = jnp.zeros_like(s_scr)

    def piece(b, n, j):
        return u_ref[b, n, pl.ds(j, ncl, stride=lc), :]

    a = []
    for b in range(nlb):
        a.append(jnp.concatenate(
            [jnp.concatenate([piece(b, n, j) for j in range(lc)], axis=1) for n in range(nb)], axis=0).astype(BF16))
        x = jnp.dot(a[b], w_ref[b], preferred_element_type=F32)
        for qq in range(2 * nq):
            for n in range(nb):
                x_scr[qq, pl.ds(b * nb + n, ncl, stride=nrow), :] = x[n * ncl:(n + 1) * ncl, qq * LANES:(qq + 1) * LANES]
    mt = 2 * LANES
    y_intra = [jnp.concatenate(
        [jnp.dot(a[b][:, :(jt + 1) * mt], m_ref[b, :(jt + 1) * mt, jt * mt:(jt + 1) * mt], preferred_element_type=F32)
         for jt in range(lc * LANES // mt)], axis=1) for b in range(nlb)]
    lam_rows = lambda qq: jnp.concatenate(
        [jnp.broadcast_to(lam_ref[b, :, qq * LANES:(qq + 1) * LANES], (nb, LANES)) for b in range(nlb)], axis=0)
    l_re = [lam_rows(qq) for qq in range(nq)]
    l_im = [lam_rows(nq + qq) for qq in range(nq)]

    s = [s_scr[qq] for qq in range(2 * nq)]
    for c in range(ncl):
        rows = slice(c * nrow, (c + 1) * nrow)
        for qq in range(nq):
            ss_scr[qq, rows, :] = s[qq]
            ss_scr[nq + qq, rows, :] = s[nq + qq]
            s[qq], s[nq + qq] = _complex_step(s[qq], s[nq + qq], l_re[qq], l_im[qq],
                                              x_scr[qq, rows, :], x_scr[nq + qq, rows, :])
    for qq in range(2 * nq):
        s_scr[qq] = s[qq]
    for b in range(nlb):
        st_ref[b] = jnp.concatenate([s[qq][b * nb:(b + 1) * nb] for qq in range(2 * nq)], axis=1)
        s_start = jnp.concatenate(
            [jnp.concatenate([ss_scr[qq, pl.ds(b * nb + n, ncl, stride=nrow), :] for n in range(nb)], axis=0)
             for qq in range(2 * nq)], axis=1).astype(BF16)
        y = y_intra[b] + jnp.dot(s_start, v_ref[b], preferred_element_type=F32)
        d = d_ref[b]
        for n in range(nb):
            for j in range(lc):
                y_ref[b, n, pl.ds(j, ncl, stride=lc), :] = (
                    y[n * ncl:(n + 1) * ncl, j * LANES:(j + 1) * LANES] + d * piece(b, n, j))


def _ssm_prompt(u4, m, w, v, lam, d, lc, tl, nlb):
    _, nb, seq, _ = u4.shape
    ncl = tl // lc
    nslab = 2 * STATE_LANES // LANES
    kern = functools.partial(_ssm_prompt_kernel, lc=lc, nb=nb, ncl=ncl, nlb=nlb)
    wspec = lambda a: pl.BlockSpec((nlb,) + a.shape[1:], lambda p, t: (p, 0, 0), pipeline_mode=pl.Buffered(1))
    io = pl.BlockSpec((nlb, nb, tl, LANES), lambda p, t: (p, 0, t, 0))
    return pl.pallas_call(
        kern,
        grid=(N_LANE_BLOCKS // nlb, seq // tl),
        in_specs=[io, wspec(m), wspec(w), wspec(v), wspec(lam), wspec(d)],
        out_specs=[io, pl.BlockSpec((nlb, nb, 2 * STATE_LANES), lambda p, t: (p, 0, 0))],
        out_shape=[jax.ShapeDtypeStruct(u4.shape, F32),
                   jax.ShapeDtypeStruct((N_LANE_BLOCKS, nb, 2 * STATE_LANES), F32)],
        scratch_shapes=[pltpu.VMEM((nslab, nlb * nb * ncl, LANES), F32),
                        pltpu.VMEM((nslab, nlb * nb * ncl, LANES), F32),
                        pltpu.VMEM((nslab, nlb * nb, LANES), F32)],
        compiler_params=_params(("arbitrary", "arbitrary")),
        name="ssm_prompt",
    )(u4, m, w, v, lam, d)


def _ssm_sample_kernel(u_ref, sre_ref, sim_ref, m_ref, w_ref, v_ref, lam_ref, d_ref,
                       y_ref, ore_ref, oim_ref, *, lc, ns):
    def piece(j):
        return u_ref[pl.ds(j, ns, stride=lc), :]

    a = jnp.concatenate([piece(j) for j in range(lc)], axis=1).astype(BF16)
    s_re, s_im = sre_ref[...], sim_ref[...]
    x = jnp.dot(a, w_ref[0], preferred_element_type=F32)
    e_re, e_im = _complex_step(s_re, s_im, lam_ref[0, :, :STATE_LANES], lam_ref[0, :, STATE_LANES:],
                               x[:, :STATE_LANES], x[:, STATE_LANES:])
    ore_ref[...] = e_re
    oim_ref[...] = e_im
    s0 = jnp.concatenate([s_re, s_im], axis=1).astype(BF16)
    y = (jnp.dot(a, m_ref[0], preferred_element_type=F32) + jnp.dot(s0, v_ref[0], preferred_element_type=F32))
    d = d_ref[...]
    for j in range(lc):
        y_ref[pl.ds(j, ns, stride=lc), :] = y[:, j * LANES:(j + 1) * LANES] + d * piece(j)


def _ssm_sample(u, s_re, s_im, m, w, v, lam, d, lc):
    t = u.shape[0]
    ns = t // lc
    kern = functools.partial(_ssm_sample_kernel, lc=lc, ns=ns)
    assert m.shape[1] == 2 * lc * LANES
    n = lc * LANES
    col = lambda rows, width: pl.BlockSpec((rows, width), lambda b: (0, b))
    return pl.pallas_call(
        kern,
        grid=(N_LANE_BLOCKS,),
        in_specs=[col(t, LANES), col(ns, STATE_LANES), col(ns, STATE_LANES),
                  pl.BlockSpec((1, n, n), lambda b: (b, 0, 0)),
                  pl.BlockSpec((1, n, 2 * STATE_LANES), lambda b: (b, 1, 0)),
                  pl.BlockSpec((1, 2 * STATE_LANES, n), lambda b: (b, 0, 0)),
                  pl.BlockSpec((1, 1, 2 * STATE_LANES), lambda b: (b, 0, 0)), col(1, LANES)],
        out_specs=[col(t, LANES), col(ns, STATE_LANES), col(ns, STATE_LANES)],
        out_shape=[jax.ShapeDtypeStruct(u.shape, F32), jax.ShapeDtypeStruct(s_re.shape, F32),
                   jax.ShapeDtypeStruct(s_im.shape, F32)],
        compiler_params=_params(("arbitrary",)),
        name="ssm_sample",
    )(u, s_re, s_im, m, w, v, lam, d)


def _ssm_discretise(a_re, a_im, log_step):
    dt = jnp.exp(log_step)
    mag = jnp.exp(a_re * dt)
    l_re, l_im = mag * jnp.cos(a_im * dt), mag * jnp.sin(a_im * dt)
    den = a_re * a_re + a_im * a_im
    n_re, n_im = l_re - 1.0, l_im
    return l_re, l_im, (n_re * a_re + n_im * a_im) / den, (n_im * a_re - n_re * a_im) / den


def _complex_powers(l_re, l_im, n):
    p_re, p_im = [jnp.ones_like(l_re)], [jnp.zeros_like(l_re)]
    for _ in range(n):
        p_re, p_im = p_re + [p_re[-1] * l_re - p_im[-1] * l_im], p_im + [p_re[-1] * l_im + p_im[-1] * l_re]
    return p_re, p_im


def _split_bf16(x):
    hi = x.astype(BF16)
    return hi, (x - hi.astype(F32)).astype(BF16)


def _dot_split(a, b):
    dot = lambda x, y: jnp.dot(x, y, preferred_element_type=F32)
    return dot(a[0], b[0]) + (dot(a[0], b[1]) + dot(a[1], b[0]))


def _ssm_tables_kernel(ac_re_ref, ac_im_ref, lsc_ref, ar_re_ref, ar_im_ref, lsr_ref, b_re_ref, b_im_ref,
                       c_re_ref, c_im_ref, m_ref, w_ref, v_ref, lam_ref, lam_half_ref, *, lc):
    l_re, l_im, cf_re, cf_im = _ssm_discretise(ac_re_ref[0], ac_im_ref[0], lsc_ref[0])
    b_re, b_im, c_re, c_im = b_re_ref[0], b_im_ref[0], c_re_ref[0], c_im_ref[0]
    c_re_parts, c_im_parts = _split_bf16(c_re), _split_bf16(c_im)
    bb_re = cf_re * b_re - cf_im * b_im
    bb_im = cf_re * b_im + cf_im * b_re
    p_re, p_im = _complex_powers(l_re, l_im, lc)
    lag = []
    for k in range(lc):
        et_re = (p_re[k] * bb_re - p_im[k] * bb_im).T
        et_im = (p_re[k] * bb_im + p_im[k] * bb_re).T
        j = lc - 1 - k
        w_ref[0, j * LANES:(j + 1) * LANES, :STATE_LANES] = et_re.astype(BF16)
        w_ref[0, j * LANES:(j + 1) * LANES, STATE_LANES:] = et_im.astype(BF16)
        lag.append((_dot_split(_split_bf16(et_re), c_re_parts) - _dot_split(_split_bf16(et_im), c_im_parts)).astype(BF16))
        v_ref[0, :STATE_LANES, k * LANES:(k + 1) * LANES] = (p_re[k + 1] * c_re - p_im[k + 1] * c_im).astype(BF16)
        v_ref[0, STATE_LANES:, k * LANES:(k + 1) * LANES] = (-(p_im[k + 1] * c_re + p_re[k + 1] * c_im)).astype(BF16)
    zero = jnp.zeros((LANES, LANES), BF16)
    for j in range(lc):
        for jj in range(lc):
            m_ref[0, j * LANES:(j + 1) * LANES, jj * LANES:(jj + 1) * LANES] = lag[jj - j] if jj >= j else zero
    r_re, r_im, _, _ = _ssm_discretise(ar_re_ref[0], ar_im_ref[0], lsr_ref[0])
    q_re, q_im = _complex_powers(r_re, r_im, lc)
    lam_ref[0] = jnp.concatenate([q_re[lc], q_im[lc]], axis=1)
    lam_half_ref[0] = jnp.concatenate([q_re[lc // 2], q_im[lc // 2]], axis=1)


def _ssm_tables(a_re, a_im, log_step, b_re, b_im, c_re, c_im, lc):
    g, p, h = b_re.shape
    nbk, r = N_LANE_BLOCKS, GROUPS_PER_LANE_BLOCK
    ls = jnp.broadcast_to(log_step[:, None], (g, p))
    cols = [jnp.broadcast_to(x.reshape(nbk, STATE_LANES, 1), (nbk, STATE_LANES, LANES)) for x in (a_re, a_im, ls)]
    rows = [x.reshape(nbk, 1, STATE_LANES) for x in (a_re, a_im, ls)]
    eye = jnp.eye(r, dtype=F32)[None, :, None, :, None]

    def block_diag(x):
        return (x[:, :, :, None, :] * eye).reshape(nbk, STATE_LANES, LANES)

    mats = [block_diag(b_re.reshape(nbk, r, p, h)), block_diag(b_im.reshape(nbk, r, p, h)),
            block_diag(c_re.reshape(nbk, r, h, p).transpose(0, 1, 3, 2)),
            block_diag(c_im.reshape(nbk, r, h, p).transpose(0, 1, 3, 2))]
    spec = lambda shape: pl.BlockSpec((1,) + shape, lambda b: (b, 0, 0))
    n = lc * LANES
    return pl.pallas_call(
        functools.partial(_ssm_tables_kernel, lc=lc),
        grid=(nbk,),
        in_specs=[spec((STATE_LANES, LANES))] * 3 + [spec((1, STATE_LANES))] * 3 + [spec((STATE_LANES, LANES))] * 4,
        out_specs=[spec((n, n)), spec((n, 2 * STATE_LANES)), spec((2 * STATE_LANES, n)),
                   spec((1, 2 * STATE_LANES)), spec((1, 2 * STATE_LANES))],
        out_shape=[jax.ShapeDtypeStruct((nbk, n, n), BF16), jax.ShapeDtypeStruct((nbk, n, 2 * STATE_LANES), BF16),
                   jax.ShapeDtypeStruct((nbk, 2 * STATE_LANES, n), BF16),
                   jax.ShapeDtypeStruct((nbk, 1, 2 * STATE_LANES), F32),
                   jax.ShapeDtypeStruct((nbk, 1, 2 * STATE_LANES), F32)],
        compiler_params=_params(("arbitrary",)),
        name="ssm_tables",
    )(*cols, *rows, *mats)


def _nt_dot(a, b):
    return lax.dot_general(a, b, (((1,), (1,)), ((), ())), preferred_element_type=F32)


def _swa_prompt_tile(sink_ref, qt_ref, kcat, vcat_t, mstd_ref, mfirst_ref, g_ref, ot_ref, first_tile, nblk):
    keys = lambda j: slice(j * WINDOW, (j + 2) * WINDOW)
    zeros = jnp.zeros((HEAD_DIM, WINDOW), BF16)
    sinks = [jnp.concatenate([jnp.full((1, WINDOW), sink_ref[kh * GQA_GROUP + g], F32) for g in range(GQA_GROUP)],
                             axis=1) for kh in range(N_KV_HEADS)]
    chains = [(j, kh) for j in range(nblk) for kh in range(N_KV_HEADS)]

    def scores(j, kh):
        def rhs(g):
            h = kh * GQA_GROUP + g
            q = qt_ref[h * HEAD_DIM:(h + 1) * HEAD_DIM, j * WINDOW:(j + 1) * WINDOW]
            return jnp.concatenate([q, zeros] if kh == 0 else [zeros, q], axis=0)
        r = jnp.concatenate([rhs(g) for g in range(GQA_GROUP)], axis=1)
        vmask = mstd_ref[...]
        if j == 0:
            vmask = jnp.where(first_tile, mfirst_ref[...], vmask)
        valid = jnp.tile(vmask, (1, GQA_GROUP)) > 0.0
        return jnp.where(valid, jnp.dot(kcat[keys(j)], r, preferred_element_type=F32), NEG_INF)

    def softmax(kh, sc):
        m = jnp.maximum(jnp.max(sc, axis=0, keepdims=True), sinks[kh])
        e = jnp.exp(sc - m)
        return e.astype(BF16), 1.0 / (jnp.sum(e, axis=0, keepdims=True) + jnp.exp(sinks[kh] - m))

    sc = [scores(*c) for c in chains]
    pr = [softmax(kh, x) for (j, kh), x in zip(chains, sc)]

    def finish():
        out = [jnp.dot(vcat_t[kh * HEAD_DIM:(kh + 1) * HEAD_DIM, keys(j)], e, preferred_element_type=F32) * inv
               for (j, kh), (e, inv) in zip(chains, pr)]
        gain = g_ref[...]
        for j in range(nblk):
            o = jnp.concatenate([out[j * N_KV_HEADS + kh][:, g * WINDOW:(g + 1) * WINDOW]
                                 for kh in range(N_KV_HEADS) for g in range(GQA_GROUP)], axis=0)
            scale = lax.rsqrt(jnp.mean(o * o, axis=0, keepdims=True) + RMS_EPS)
            ot_ref[:, j * WINDOW:(j + 1) * WINDOW] = (o * scale * gain).astype(BF16)

    return finish


def _swa_masks():
    kj = jnp.arange(2 * WINDOW)[:, None]
    diff = jnp.arange(WINDOW)[None, :] + WINDOW - kj
    std = (diff >= 0) & (diff <= WINDOW)
    return std.astype(F32), (std & (kj >= WINDOW)).astype(F32)


def _ffn_in_swa_kernel(sink_ref, x_ref, pre_g_ref, wg_ref, wu_ref, wd_ref, post_g_ref, mix_g_ref, wuk_ref, wqv_t_ref,
                       cos_ref, slo_ref, shi_ref, cos_t_ref, sin_t_ref, mstd_ref, mfirst_ref, swa_g_ref,
                       h_ref, u_ref, k_ref, vt_ref, ot_ref, act_ref, q_scr, kcat_scr, vcat_scr, *, nt, nblk):
    i = pl.program_id(0)
    tm = x_ref.shape[0]

    @pl.when(i == 0)
    def _():
        q_scr[...] = jnp.zeros_like(q_scr)
        kcat_scr[...] = jnp.zeros_like(kcat_scr)
        vcat_scr[...] = jnp.zeros_like(vcat_scr)

    first_tile = lax.rem(i + (nt - 1), nt) == 0
    swa_finish = _swa_prompt_tile(sink_ref, q_scr, kcat_scr[...], vcat_scr[...], mstd_ref, mfirst_ref, swa_g_ref,
                                  ot_ref, first_tile, nblk)

    kcat_scr[:WINDOW, :] = kcat_scr[tm:, :]
    vcat_scr[:, :WINDOW] = vcat_scr[:, tm:]
    half = ROPE_DIM // 2

    def row_block(r, swa_hook):
        n = r.stop - r.start
        h = yield from _ffn_stages(x_ref[r, :], pre_g_ref[...], wg_ref, wu_ref, wd_ref, post_g_ref[...], act_ref, r,
                                   after_chunk=swa_hook)
        h_ref[r, :] = h
        hn = _rms(h, mix_g_ref[...]).astype(BF16)
        z = jnp.dot(hn, wuk_ref[...], preferred_element_type=F32)
        yield
        for b in range(N_LANE_BLOCKS):
            u_ref[b, r, :] = z[:, b * LANES:(b + 1) * LANES]
        k = _rope(z[:, SSM_WIDTH:], cos_ref[r, :], slo_ref[r, :], shi_ref[r, :])
        k_ref[r, :] = k
        zt = _nt_dot(wqv_t_ref[...], hn)
        yield
        vt = zt[Q_WIDTH:]
        vt_ref[:, r] = vt
        q3 = zt[:Q_WIDTH].reshape(N_HEADS, HEAD_DIM, n)
        x1, x2 = q3[:, :half], q3[:, half:ROPE_DIM]
        cos, sin = cos_t_ref[:, r][None], sin_t_ref[:, r][None]
        q3 = jnp.concatenate([x1 * cos - x2 * sin, x2 * cos + x1 * sin, q3[:, ROPE_DIM:]], axis=1)
        shifted = slice(WINDOW + r.start, WINDOW + r.stop)
        kcat_scr[shifted, :] = k.astype(BF16)
        vcat_scr[:, shifted] = vt.astype(BF16)
        q_scr[:, r] = (q3.reshape(Q_WIDTH, n) * (HEAD_DIM ** -0.5)).astype(BF16)

    blocks = [slice(b * tm // ROW_BLOCKS, (b + 1) * tm // ROW_BLOCKS) for b in range(ROW_BLOCKS)]
    _interleave([row_block(r, (N_FF_CHUNKS - 3, swa_finish) if r.start == 0 else None) for r in blocks])


def _ffn_in_swa(x, pre_g, ffn_w, post_g, mix_g, wuk, wqv_t, cos, slo, shi, cos_t, sin_t, sinks, swa_g, seq, tm):
    t = x.shape[0]
    n_tiles, nt, nblk = t // tm, seq // tm, tm // WINDOW
    half = ROPE_DIM // 2
    mstd, mfirst = _swa_masks()
    gain = jnp.broadcast_to(swa_g.reshape(Q_WIDTH, 1), (Q_WIDTH, WINDOW))
    cur = lambda i: jnp.minimum(i, n_tiles - 1)
    row = lambda w: pl.BlockSpec((tm, w), lambda i: (cur(i), 0))
    tab = pl.BlockSpec((tm, LANES), lambda i: (cur(i) % nt, 0))
    tab_t = pl.BlockSpec((half, tm), lambda i: (0, cur(i) % nt))
    return pl.pallas_call(
        functools.partial(_ffn_in_swa_kernel, nt=nt, nblk=nblk),
        grid=(n_tiles + 1,),
        in_specs=[pl.BlockSpec(memory_space=pltpu.SMEM), row(D_MODEL), _const_spec((1, D_MODEL)),
                  *[_const_spec(w.shape) for w in ffn_w], _const_spec((1, D_MODEL)), _const_spec((1, D_MODEL)),
                  _const_spec(wuk.shape), _const_spec(wqv_t.shape), tab, tab, tab, tab_t, tab_t,
                  _const_spec(mstd.shape), _const_spec(mfirst.shape), _const_spec(gain.shape)],
        out_specs=[row(D_MODEL), pl.BlockSpec((N_LANE_BLOCKS, tm, LANES), lambda i: (0, cur(i), 0)), row(KV_WIDTH),
                   pl.BlockSpec((KV_WIDTH, tm), lambda i: (0, cur(i))),
                   pl.BlockSpec((Q_WIDTH, tm), lambda i: (0, jnp.maximum(i - 1, 0)))],
        out_shape=[jax.ShapeDtypeStruct((t, D_MODEL), F32), jax.ShapeDtypeStruct((N_LANE_BLOCKS, t, LANES), F32),
                   jax.ShapeDtypeStruct((t, KV_WIDTH), F32), jax.ShapeDtypeStruct((KV_WIDTH, t), F32),
                   jax.ShapeDtypeStruct((Q_WIDTH, t), BF16)],
        scratch_shapes=[pltpu.VMEM((tm, D_FF), BF16), pltpu.VMEM((Q_WIDTH, tm), BF16),
                        pltpu.VMEM((WINDOW + tm, KV_WIDTH), BF16), pltpu.VMEM((KV_WIDTH, WINDOW + tm), BF16)],
        compiler_params=_params(("arbitrary",)),
        name="ffn_in_swa",
    )(sinks, x, pre_g, *ffn_w, post_g, mix_g, wuk, wqv_t, cos, slo, shi, cos_t, sin_t, mstd, mfirst, gain)


def _swa_sample_kernel(sink_ref, q_ref, kn_ref, vn_ref, ck_ref, cv_ref, g_ref, o_ref, nk_ref, nv_ref, *, ns, t):
    rows = GQA_GROUP * t
    tok = lax.broadcasted_iota(jnp.int32, (rows, WINDOW), 0) % t
    valid_c = lax.broadcasted_iota(jnp.int32, (rows, WINDOW), 1) >= tok
    tok_n = lax.broadcasted_iota(jnp.int32, (rows, t), 0) % t
    valid_n = lax.broadcasted_iota(jnp.int32, (rows, t), 1) <= tok_n
    gain = g_ref[...]

    sinks = [jnp.concatenate([jnp.full((t, 1), sink_ref[kh * GQA_GROUP + g], F32) for g in range(GQA_GROUP)], axis=0)
             for kh in range(N_KV_HEADS)]
    heads = [(s, kh) for s in range(ns) for kh in range(N_KV_HEADS)]
    hs = lambda kh: slice(kh * HEAD_DIM, (kh + 1) * HEAD_DIM)

    newest = lax.broadcasted_iota(jnp.int32, (KV_WIDTH, WINDOW), 1) >= WINDOW - t
    pad = jnp.zeros((WINDOW - 2 * t, KV_WIDTH), F32)

    def shifted(cache_ref, new_ref, s):
        new_rows = jnp.concatenate([pad, jnp.zeros((t, KV_WIDTH), F32), new_ref[s]], axis=0)
        return jnp.where(newest, new_rows.T, pltpu.roll(cache_ref[s], WINDOW - t, 1))

    for s in range(ns):
        nk_ref[s] = shifted(ck_ref, kn_ref, s)
        nv_ref[s] = shifted(cv_ref, vn_ref, s)

    def scores(s, kh):
        q = q_ref[s].astype(F32)
        q4 = jnp.concatenate([q[:, (kh * GQA_GROUP + g) * HEAD_DIM:(kh * GQA_GROUP + g + 1) * HEAD_DIM]
                              for g in range(GQA_GROUP)], axis=0).astype(BF16)
        sc_c = jnp.where(valid_c, jnp.dot(q4, ck_ref[s, hs(kh), :].astype(BF16), preferred_element_type=F32), NEG_INF)
        sc_n = jnp.where(valid_n, _nt_dot(q4, kn_ref[s, :, hs(kh)].astype(BF16)), NEG_INF)
        return sc_c, sc_n

    def softmax(kh, sc_c, sc_n):
        m = jnp.maximum(jnp.maximum(jnp.max(sc_c, axis=-1, keepdims=True),
                                    jnp.max(sc_n, axis=-1, keepdims=True)), sinks[kh])
        e_c, e_n = jnp.exp(sc_c - m), jnp.exp(sc_n - m)
        inv = 1.0 / (jnp.sum(e_c, axis=-1, keepdims=True) + jnp.sum(e_n, axis=-1, keepdims=True)
                     + jnp.exp(sinks[kh] - m))
        return (e_c * inv).astype(BF16), (e_n * inv).astype(BF16)

    def values(s, kh, p_c, p_n):
        return (_nt_dot(p_c, cv_ref[s, hs(kh), :].astype(BF16))
                + jnp.dot(p_n, vn_ref[s, :, hs(kh)].astype(BF16), preferred_element_type=F32))

    sc = [scores(s, kh) for s, kh in heads]
    pr = [softmax(kh, *x) for (s, kh), x in zip(heads, sc)]
    o4 = [values(s, kh, *x) for (s, kh), x in zip(heads, pr)]
    for s in range(ns):
        o = jnp.concatenate([o4[s * N_KV_HEADS + kh][g * t:(g + 1) * t]
                             for kh in range(N_KV_HEADS) for g in range(GQA_GROUP)], axis=1)
        o_ref[s] = _rms(o, gain).astype(BF16)


def _swa_sample(q3, kn3, vn3, ck, cv, sinks, out_g, sb):
    ns, t, _ = q3.shape
    blk = lambda a: pl.BlockSpec((sb,) + a.shape[1:], lambda i: (i, 0, 0))
    kern = functools.partial(_swa_sample_kernel, ns=sb, t=t)
    return pl.pallas_call(
        kern,
        grid=(ns // sb,),
        in_specs=[pl.BlockSpec(memory_space=pltpu.SMEM), blk(q3), blk(kn3), blk(vn3), blk(ck), blk(cv),
                  pl.BlockSpec((1, Q_WIDTH), lambda i: (0, 0))],
        out_specs=[blk(q3), blk(ck), blk(cv)],
        out_shape=[jax.ShapeDtypeStruct(q3.shape, BF16), jax.ShapeDtypeStruct(ck.shape, F32),
                   jax.ShapeDtypeStruct(cv.shape, F32)],
        compiler_params=_params(("arbitrary",)),
        name="swa_sample",
    )(sinks, q3, kn3, vn3, ck, cv, out_g)


def _gelu_tanh(x):
    return 0.5 * x * (1.0 + jnp.tanh(math.sqrt(2.0 / math.pi) * (x + 0.044715 * (x * x * x))))


def _merge_stages(h, y, at, wglu_ref, bglu, sg, wout_ref, post_g, xa_g, wq_ref, at_transposed=False):
    g = _gelu_tanh(y)
    lin = jnp.dot(g.astype(BF16), wglu_ref[...], preferred_element_type=F32) + bglu
    yield
    y_ssm = g * (1.0 / (1.0 + jnp.exp(-lin)))
    ssm_n = _rms(y_ssm, sg).astype(BF16)
    at_dims = (((0,), (0,)), ((), ())) if at_transposed else (((1,), (0,)), ((), ()))
    mixed = (jnp.dot(ssm_n, wout_ref[:SSM_WIDTH, :], preferred_element_type=F32)
             + lax.dot_general(at, wout_ref[SSM_WIDTH:, :], at_dims, preferred_element_type=F32))
    yield
    h2 = h + _rms(mixed, post_g)
    qm = jnp.dot(_rms(h2, xa_g).astype(BF16), wq_ref[...], preferred_element_type=F32)
    yield
    return h2, (qm * (MEM_HEAD_DIM ** -0.5)).astype(BF16)


def _merge_tile(*args, **kwargs):
    return _run(_merge_stages(*args, **kwargs))


def _merge_kernel(h_ref, y_ref, at_ref, wglu_ref, bglu_ref, sg_ref, wout_ref, post_g_ref, xa_g_ref, wq_ref,
                  h2_ref, qm_ref):
    h2_ref[...], qm_ref[...] = _merge_tile(h_ref[...], y_ref[...], at_ref[...], wglu_ref, bglu_ref[...], sg_ref[...],
                                           wout_ref, post_g_ref[...], xa_g_ref[...], wq_ref)


def _merge_specs(wglu, wout, wq):
    return [_const_spec(wglu.shape), _const_spec((1, SSM_WIDTH)), _const_spec((1, SSM_WIDTH)), _const_spec(wout.shape),
            _const_spec((1, D_MODEL)), _const_spec((1, D_MODEL)), _const_spec(wq.shape)]


def _merge(h, y, at, merge_w, tm):
    t = h.shape[0]
    row = lambda w: pl.BlockSpec((tm, w), lambda i: (i, 0))
    return pl.pallas_call(
        _merge_kernel,
        grid=(t // tm,),
        in_specs=[row(D_MODEL), row(SSM_WIDTH), row(Q_WIDTH)] + _merge_specs(merge_w[0], merge_w[3], merge_w[6]),
        out_specs=[row(D_MODEL), row(D_MODEL)],
        out_shape=[jax.ShapeDtypeStruct((t, D_MODEL), F32), jax.ShapeDtypeStruct((t, D_MODEL), BF16)],
        compiler_params=_params(("arbitrary",)),
        name="merge",
    )(h, y, at, *merge_w)


def _mem_attn_sample_kernel(q_ref, k_ref, v_ref, o_ref, *, gb, t):
    rows = t * MEM_HEADS
    halves = MEM_HEAD_DIM // LANES
    kv_rows = N_MEM * halves * MEM_HEADS
    period = halves * MEM_HEADS
    lane = lax.broadcasted_iota(jnp.int32, (rows, kv_rows), 1) % period
    head = lax.broadcasted_iota(jnp.int32, (rows, kv_rows), 0) % MEM_HEADS
    in_half = [lane == head + hf * MEM_HEADS for hf in range(halves)]
    for b in range(gb):
        kb = k_ref[b].astype(BF16)
        vb = v_ref[b].astype(BF16)
        part = _nt_dot(q_ref[b], kb)
        sc = jnp.where(in_half[0], part[:rows], 0.0)
        for hf in range(1, halves):
            sc = sc + pltpu.roll(jnp.where(in_half[hf], part[hf * rows:(hf + 1) * rows], 0.0),
                                 kv_rows - hf * MEM_HEADS, 1)
        sc = jnp.where(in_half[0], sc, -jnp.inf)
        e = jnp.exp(sc - jnp.max(sc, axis=-1, keepdims=True))
        inv = 1.0 / jnp.sum(e, axis=-1, keepdims=True)
        e_all = jnp.concatenate([e] + [pltpu.roll(e, hf * MEM_HEADS, 1) for hf in range(1, halves)], axis=0)
        o = jnp.dot(e_all.astype(BF16), vb, preferred_element_type=F32)
        o_ref[b] = (o * jnp.concatenate([inv] * halves, axis=0)).astype(BF16)


def _mem_attn_sample(qm, cache_k, cache_v, n_s, t_s, gb):
    halves = MEM_HEAD_DIM // LANES
    rows = halves * t_s * MEM_HEADS
    kv_rows = N_MEM * halves * MEM_HEADS

    def stored_rows(c):
        c = c.reshape(n_s, N_MEM, MEM_HEADS, halves, LANES).transpose(0, 1, 3, 2, 4)
        return c.reshape(n_s, kv_rows, LANES)

    q = qm.reshape(n_s, t_s, MEM_HEADS, halves, LANES).transpose(0, 3, 1, 2, 4).reshape(n_s, rows, LANES)
    blk = lambda r: pl.BlockSpec((gb, r, LANES), lambda i: (i, 0, 0))
    o = pl.pallas_call(
        functools.partial(_mem_attn_sample_kernel, gb=gb, t=t_s),
        grid=(n_s // gb,),
        in_specs=[blk(rows), blk(kv_rows), blk(kv_rows)],
        out_specs=blk(rows),
        out_shape=jax.ShapeDtypeStruct((n_s, rows, LANES), BF16),
        compiler_params=_params(("arbitrary",)),
        name="mem_attn_sample",
    )(q, stored_rows(cache_k), stored_rows(cache_v))
    o = o.reshape(n_s, halves, t_s, MEM_HEADS, LANES).transpose(0, 2, 3, 1, 4)
    return o.reshape(n_s * t_s, D_MODEL)


def _mem_heads_stages(q, k_ref, v_ref):
    outs = []
    for hh in range(MEM_HEADS):
        hs = slice(hh * MEM_HEAD_DIM, (hh + 1) * MEM_HEAD_DIM)
        sc = _nt_dot(q[:, hs], k_ref[0, :, hs])
        yield
        e = jnp.exp(sc - jnp.max(sc, axis=-1, keepdims=True))
        inv = 1.0 / jnp.sum(e, axis=-1, keepdims=True)
        outs.append((jnp.dot(e.astype(BF16), v_ref[0, :, hs], preferred_element_type=F32) * inv).astype(BF16))
        yield
    return jnp.concatenate(outs, axis=1)


def _ffn_out_stages(h2, om, wo_ref, xa_post, pre_g, wg_ref, wu_ref, wd_ref, post_g, act_ref, rows=slice(None)):
    c = jnp.dot(om, wo_ref[...], preferred_element_type=F32)
    yield
    h3 = h2 + _rms(c, xa_post)
    return (yield from _ffn_stages(h3, pre_g, wg_ref, wu_ref, wd_ref, post_g, act_ref, rows))


def _ffn_out_tile(*args, **kwargs):
    return _run(_ffn_out_stages(*args, **kwargs))


def _ffn_out_specs(wo, ffn_w):
    return [_const_spec(wo.shape), _const_spec((1, D_MODEL)), _const_spec((1, D_MODEL)),
            *[_const_spec(w.shape) for w in ffn_w], _const_spec((1, D_MODEL))]


def _ffn_out_kernel(h_ref, o_ref, wo_ref, xa_post_ref, pre_g_ref, wg_ref, wu_ref, wd_ref, post_g_ref, out_ref, act_ref):
    out_ref[...] = _ffn_out_tile(h_ref[...], o_ref[...], wo_ref, xa_post_ref[...], pre_g_ref[...], wg_ref, wu_ref,
                                 wd_ref, post_g_ref[...], act_ref)


def _ffn_out(h, o, out_w, tm):
    t = h.shape[0]
    row = pl.BlockSpec((tm, D_MODEL), lambda i: (i, 0))
    return pl.pallas_call(
        _ffn_out_kernel,
        grid=(t // tm,),
        in_specs=[row, row] + _ffn_out_specs(out_w[0], out_w[3:6]),
        out_specs=row,
        out_shape=jax.ShapeDtypeStruct((t, D_MODEL), F32),
        scratch_shapes=[pltpu.VMEM((tm, D_FF), BF16)],
        compiler_params=_params(("arbitrary",)),
        name="ffn_out",
    )(h, o, *out_w)


def _post_kernel(h_ref, y_ref, at_ref, k_ref, v_ref,
                 wglu_ref, bglu_ref, sg_ref, wout_ref, post_g_ref, xa_g_ref, wq_ref,
                 wo_ref, xa_post_ref, pre_g_ref, wg_ref, wu_ref, wd_ref, ffn_post_ref, out_ref, act_ref):
    def row_block(r):
        y = jnp.concatenate([y_ref[b, r, :] for b in range(N_LANE_BLOCKS)], axis=1)
        h2, qm = yield from _merge_stages(h_ref[r, :], y, at_ref[:, r], wglu_ref, bglu_ref[...], sg_ref[...],
                                          wout_ref, post_g_ref[...], xa_g_ref[...], wq_ref, at_transposed=True)
        om = yield from _mem_heads_stages(qm, k_ref, v_ref)
        out_ref[r, :] = yield from _ffn_out_stages(h2, om, wo_ref, xa_post_ref[...], pre_g_ref[...], wg_ref, wu_ref,
                                                   wd_ref, ffn_post_ref[...], act_ref, r)

    tm = h_ref.shape[0]
    _interleave([row_block(slice(i * tm // ROW_BLOCKS, (i + 1) * tm // ROW_BLOCKS)) for i in range(ROW_BLOCKS)])


def _post(h, y, at, k3, v3, merge_w, out_w, tm):
    t = h.shape[0]
    tiles_per_batch = t // k3.shape[0] // tm
    row = lambda w: pl.BlockSpec((tm, w), lambda i: (i, 0))
    kv = pl.BlockSpec((1, N_MEM, D_MODEL), lambda i: (i // tiles_per_batch, 0, 0))
    return pl.pallas_call(
        _post_kernel,
        grid=(t // tm,),
        in_specs=([row(D_MODEL), pl.BlockSpec((N_LANE_BLOCKS, tm, LANES), lambda i: (0, i, 0)),
                   pl.BlockSpec((Q_WIDTH, tm), lambda i: (0, i)), kv, kv]
                  + _merge_specs(merge_w[0], merge_w[3], merge_w[6]) + _ffn_out_specs(out_w[0], out_w[3:6])),
        out_specs=row(D_MODEL),
        out_shape=jax.ShapeDtypeStruct((t, D_MODEL), F32),
        scratch_shapes=[pltpu.VMEM((tm, D_FF), BF16)],
        compiler_params=_params(("arbitrary",)),
        name="post",
    )(h, y, at, k3, v3, *merge_w, *out_w)


def _mem_kv_kernel(m_ref, g_ref, wkv_ref, k_ref, v_ref, kb_ref, vb_ref):
    kv = jnp.dot(_rms(m_ref[...], g_ref[...]).astype(BF16), wkv_ref[...], preferred_element_type=F32)
    k_ref[...] = kv[:, :D_MODEL]
    v_ref[...] = kv[:, D_MODEL:]
    kb_ref[...] = kv[:, :D_MODEL].astype(BF16)
    vb_ref[...] = kv[:, D_MODEL:].astype(BF16)


def _mem_kv(mem, g, wkv, tm):
    t = mem.shape[0]
    row = pl.BlockSpec((tm, D_MODEL), lambda i: (i, 0))
    return pl.pallas_call(
        _mem_kv_kernel,
        grid=(t // tm,),
        in_specs=[row, _const_spec((1, D_MODEL)), _const_spec(wkv.shape)],
        out_specs=[row] * 4,
        out_shape=[jax.ShapeDtypeStruct((t, D_MODEL), F32)] * 2 + [jax.ShapeDtypeStruct((t, D_MODEL), BF16)] * 2,
        compiler_params=_params(("arbitrary",)),
        name="mem_kv",
    )(mem, g, wkv)


def _rope_tables(pos):
    half = ROPE_DIM // 2
    inv = ROPE_THETA ** (-jnp.arange(half, dtype=F32) * (2.0 / ROPE_DIM))
    ang = pos.astype(F32)[:, None] * inv[None, :]
    cos, sin = jnp.cos(ang), jnp.sin(ang)
    n = pos.shape[0]
    pad = jnp.zeros((n, HEAD_DIM - ROPE_DIM), F32)
    zero = jnp.zeros((n, half), F32)
    cos_h = jnp.concatenate([cos, cos, pad + 1.0], axis=1)
    lo_h = jnp.concatenate([-sin, zero, pad], axis=1)
    hi_h = jnp.concatenate([zero, sin, pad], axis=1)
    rep = LANES // HEAD_DIM
    return tuple(jnp.tile(a, (1, rep)) for a in (cos_h, lo_h, hi_h)), (cos.T, sin.T)


def _ffn_weights(w_gate, w_up, w_down):
    return w_gate.astype(BF16), w_up.astype(BF16), w_down.astype(BF16)


def _lane_block_states(st, n):
    st = st.reshape(N_LANE_BLOCKS, n, 2, GROUPS_PER_LANE_BLOCK, SSM_STATE).transpose(2, 1, 0, 3, 4)
    st = st.reshape(2, n, N_SSM_GROUPS, SSM_STATE)
    return st[0], st[1]


def kernel(x_prompt, x_sample, state_ssm_re, state_ssm_im, cache_swa_k, cache_swa_v, cache_mem_k, cache_mem_v, mem_prompt, ffn1_pre_g, ffn1_w_gate, ffn1_w_up, ffn1_w_down, ffn1_post_g, mix_pre_g, w_in, ssm_a_re, ssm_a_im, ssm_log_step, ssm_b_re, ssm_b_im, ssm_c_re, ssm_c_im, ssm_d, ssm_w_glu, ssm_b_glu, attn_sinks, ssm_out_g, attn_out_g, w_out, mix_post_g, mem_norm_g, w_mem_q, w_mem_k, w_mem_v, w_mem_o, xa_pre_g, xa_post_g, ffn2_pre_g, ffn2_w_gate, ffn2_w_up, ffn2_w_down, ffn2_post_g):
    n_p, s_p, _ = x_prompt.shape
    n_s, t_s, _ = x_sample.shape
    tm = 512
    row = lambda a: a.reshape(1, -1).astype(F32)

    ffn1_w = _ffn_weights(ffn1_w_gate, ffn1_w_up, ffn1_w_down)
    win = w_in.astype(BF16)
    merge_w = (ssm_w_glu.astype(BF16), row(ssm_b_glu), row(ssm_out_g), w_out.astype(BF16), row(mix_post_g),
               row(xa_pre_g), w_mem_q.astype(BF16))
    out_w = (w_mem_o.astype(BF16), row(xa_post_g), row(ffn2_pre_g),
             *_ffn_weights(ffn2_w_gate, ffn2_w_up, ffn2_w_down), row(ffn2_post_g))
    wkv = jnp.concatenate([w_mem_k, w_mem_v], axis=1).astype(BF16)
    d_row = row(ssm_d)
    ssm_args = (ssm_a_re.astype(F32), ssm_a_im.astype(F32), ssm_log_step.astype(F32), ssm_b_re.astype(F32),
                ssm_b_im.astype(F32), ssm_c_re.astype(F32), ssm_c_im.astype(F32))
    sinks = attn_sinks.astype(F32)

    pm_k, pm_v, pm_kb, pm_vb = _mem_kv(mem_prompt.reshape(n_p * N_MEM, D_MODEL), row(mem_norm_g), wkv, N_MEM)

    def tokenwise_in(x2, pos_tab):
        return _ffn_in(x2, row(ffn1_pre_g), ffn1_w, row(ffn1_post_g), row(mix_pre_g), win, *pos_tab, tm)

    lc_p = 2 * t_s
    ssm_m, ssm_w, ssm_v, lam_p, lam_s = _ssm_tables(*ssm_args, lc_p)
    o1 = SSM_WIDTH + Q_WIDTH
    wuk = jnp.concatenate([win[:, :SSM_WIDTH], win[:, o1:o1 + KV_WIDTH]], axis=1)
    wqv_t = jnp.concatenate([win[:, SSM_WIDTH:o1], win[:, o1 + KV_WIDTH:]], axis=1).T
    tab_p, tab_p_t = _rope_tables(jnp.arange(s_p, dtype=jnp.int32))
    h1, u, k, vt, at = _ffn_in_swa(x_prompt.reshape(n_p * s_p, D_MODEL), row(ffn1_pre_g), ffn1_w, row(ffn1_post_g),
                                   row(mix_pre_g), wuk, wqv_t, *tab_p, *tab_p_t, sinks, attn_out_g.astype(F32),
                                   s_p, tm)
    y4, st_p = _ssm_prompt(u.reshape(N_LANE_BLOCKS, n_p, s_p, LANES), ssm_m, ssm_w, ssm_v, lam_p,
                           d_row.reshape(N_LANE_BLOCKS, 1, LANES), lc_p, 1024, 2)
    y_prompt = _post(h1, y4.reshape(N_LANE_BLOCKS, n_p * s_p, LANES), at, pm_kb.reshape(n_p, N_MEM, D_MODEL),
                     pm_vb.reshape(n_p, N_MEM, D_MODEL), merge_w, out_w, tm).reshape(n_p, s_p, D_MODEL)
    p_sre, p_sim = _lane_block_states(st_p, n_p)
    p_wk = k.reshape(n_p, s_p, KV_WIDTH)[:, -WINDOW:].reshape(n_p, WINDOW, N_KV_HEADS, HEAD_DIM)
    p_wv = jnp.stack([vt[:, (b + 1) * s_p - WINDOW:(b + 1) * s_p] for b in range(n_p)])
    p_wv = p_wv.reshape(n_p, N_KV_HEADS, HEAD_DIM, WINDOW).transpose(0, 3, 1, 2)

    pos_s = jnp.tile(PAST_LEN + jnp.arange(t_s, dtype=jnp.int32), n_s)
    h1s, us, qs, ks, vs = tokenwise_in(x_sample.reshape(n_s * t_s, D_MODEL), _rope_tables(pos_s)[0])
    ys, s_sre, s_sim = _ssm_sample(us, state_ssm_re.reshape(n_s, -1).astype(F32), state_ssm_im.reshape(n_s, -1).astype(F32),
                                   ssm_m, ssm_w, ssm_v, lam_s, d_row, t_s)
    win_len = cache_swa_k.shape[1]
    ats, s_wk, s_wv = _swa_sample(qs.reshape(n_s, t_s, Q_WIDTH), ks.reshape(n_s, t_s, KV_WIDTH), vs.reshape(n_s, t_s, KV_WIDTH),
                                  cache_swa_k.transpose(0, 2, 3, 1).reshape(n_s, KV_WIDTH, win_len),
                                  cache_swa_v.transpose(0, 2, 3, 1).reshape(n_s, KV_WIDTH, win_len),
                                  sinks, row(attn_out_g), 16)
    s_wk = s_wk.reshape(n_s, N_KV_HEADS, HEAD_DIM, win_len).transpose(0, 3, 1, 2)
    s_wv = s_wv.reshape(n_s, N_KV_HEADS, HEAD_DIM, win_len).transpose(0, 3, 1, 2)
    h2s, qms = _merge(h1s, ys, ats.reshape(n_s * t_s, Q_WIDTH), merge_w, tm)
    oms = _mem_attn_sample(qms, cache_mem_k, cache_mem_v, n_s, t_s, 4)
    y_sample = _ffn_out(h2s, oms, out_w, tm).reshape(n_s, t_s, D_MODEL)

    return (y_prompt, y_sample, p_sre, p_sim, p_wk, p_wv,
            pm_k.reshape(n_p, N_MEM, MEM_HEADS, MEM_HEAD_DIM), pm_v.reshape(n_p, N_MEM, MEM_HEADS, MEM_HEAD_DIM),
            s_sre.reshape(n_s, N_SSM_GROUPS, SSM_STATE), s_sim.reshape(n_s, N_SSM_GROUPS, SSM_STATE),
            s_wk, s_wv)
```

```python
import functools
import math

import jax
import jax.numpy as jnp
from jax import lax
from jax.experimental import pallas as pl
from jax.experimental.pallas import tpu as pltpu

F32 = jnp.float32
BF16 = jnp.bfloat16

D_MODEL = 1024
PAST_LEN = 16384
SSM_WIDTH = 512
SSM_GROUP = 16
N_SSM_GROUPS = 32
SSM_STATE = 64
HEAD_DIM = 64
N_HEADS = 8
N_KV_HEADS = 2
GQA_GROUP = 4
Q_WIDTH = 512
KV_WIDTH = 128
WINDOW = 128
ROPE_THETA = 500000.0
ROPE_DIM = 16
N_MEM = 256
MEM_HEADS = 4
MEM_HEAD_DIM = 256
D_FF = 2816
RMS_EPS = 1e-6
IN_WIDTH = SSM_WIDTH + Q_WIDTH + 2 * KV_WIDTH
NEG_INF = -1e30

LANES = 128
FF_CHUNK = 256
ROW_BLOCKS = 2
N_FF_CHUNKS = D_FF // FF_CHUNK
GROUPS_PER_LANE_BLOCK = LANES // SSM_GROUP
N_LANE_BLOCKS = SSM_WIDTH // LANES
STATE_LANES = GROUPS_PER_LANE_BLOCK * SSM_STATE
VMEM_LIMIT = 56 * 1024 * 1024


def _rms(x, g):
    return x * lax.rsqrt(jnp.mean(x * x, axis=-1, keepdims=True) + RMS_EPS) * g


def _const_spec(shape):
    nd = len(shape)
    return pl.BlockSpec(shape, lambda *_: (0,) * nd, pipeline_mode=pl.Buffered(1))


def _params(sem):
    return pltpu.CompilerParams(dimension_semantics=sem, vmem_limit_bytes=VMEM_LIMIT)


def _run(gen):
    try:
        while True:
            next(gen)
    except StopIteration as done:
        return done.value


def _interleave(gens):
    live = list(gens)
    while live:
        for g in list(live):
            try:
                next(g)
            except StopIteration:
                live.remove(g)


def _ffn_stages(x, pre_g, wg_ref, wu_ref, wd_ref, post_g, act_ref, rows=slice(None), after_chunk=None):
    xn = _rms(x, pre_g).astype(BF16)
    yield
    for c in range(N_FF_CHUNKS):
        if after_chunk is not None and c == after_chunk[0] + 1:
            after_chunk[1]()
        cols = slice(c * FF_CHUNK, (c + 1) * FF_CHUNK)
        gate = jnp.dot(xn, wg_ref[:, cols], preferred_element_type=F32)
        up = jnp.dot(xn, wu_ref[:, cols], preferred_element_type=F32)
        act = gate * (1.0 / (1.0 + jnp.exp(-gate))) * up
        act_ref[rows, cols] = act.astype(BF16)
        yield
    down = jnp.dot(act_ref[rows, :], wd_ref[...], preferred_element_type=F32)
    yield
    return x + 0.5 * _rms(down, post_g)


def _ffn_tile(*args, **kwargs):
    return _run(_ffn_stages(*args, **kwargs))


def _rope(x, cos, sin_lo, sin_hi):
    w = x.shape[1]
    half = ROPE_DIM // 2
    return (x * cos + pltpu.roll(x, w - half, 1) * sin_lo + pltpu.roll(x, half, 1) * sin_hi)


def _ffn_in_kernel(x_ref, pre_g_ref, wg_ref, wu_ref, wd_ref, post_g_ref, mix_g_ref, win_ref,
                   cos_ref, slo_ref, shi_ref,
                   h_ref, u_ref, q_ref, k_ref, v_ref, act_ref):
    h = _ffn_tile(x_ref[...], pre_g_ref[...], wg_ref, wu_ref, wd_ref, post_g_ref[...], act_ref)
    h_ref[...] = h
    z = jnp.dot(_rms(h, mix_g_ref[...]).astype(BF16), win_ref[...], preferred_element_type=F32)
    u_ref[...] = z[:, :SSM_WIDTH]
    o1 = SSM_WIDTH + Q_WIDTH
    cos, slo, shi = cos_ref[...], slo_ref[...], shi_ref[...]
    rep = Q_WIDTH // LANES
    q = _rope(z[:, SSM_WIDTH:o1], jnp.tile(cos, (1, rep)), jnp.tile(slo, (1, rep)), jnp.tile(shi, (1, rep)))
    q_ref[...] = (q * (HEAD_DIM ** -0.5)).astype(BF16)
    k_ref[...] = _rope(z[:, o1:o1 + KV_WIDTH], cos, slo, shi)
    v_ref[...] = z[:, o1 + KV_WIDTH:]


def _ffn_in(x, pre_g, ffn_w, post_g, mix_g, win, cos, slo, shi, tm):
    t = x.shape[0]
    n_pos_tiles = cos.shape[0] // tm
    row = lambda w: pl.BlockSpec((tm, w), lambda i: (i, 0))
    tab = pl.BlockSpec((tm, LANES), lambda i: (i % n_pos_tiles, 0))
    return pl.pallas_call(
        _ffn_in_kernel,
        grid=(t // tm,),
        in_specs=[row(D_MODEL), _const_spec((1, D_MODEL)), *[_const_spec(w.shape) for w in ffn_w],
                  _const_spec((1, D_MODEL)), _const_spec((1, D_MODEL)), _const_spec(win.shape),
                  tab, tab, tab],
        out_specs=[row(D_MODEL), row(SSM_WIDTH), row(Q_WIDTH), row(KV_WIDTH), row(KV_WIDTH)],
        out_shape=[jax.ShapeDtypeStruct((t, D_MODEL), F32), jax.ShapeDtypeStruct((t, SSM_WIDTH), F32),
                   jax.ShapeDtypeStruct((t, Q_WIDTH), BF16), jax.ShapeDtypeStruct((t, KV_WIDTH), F32),
                   jax.ShapeDtypeStruct((t, KV_WIDTH), F32)],
        scratch_shapes=[pltpu.VMEM((tm, D_FF), BF16)],
        compiler_params=_params(("arbitrary",)),
        name="ffn_in",
    )(x, pre_g, *ffn_w, post_g, mix_g, win, cos, slo, shi)


def _complex_step(s_re, s_im, l_re, l_im, x_re, x_im):
    return l_re * s_re - l_im * s_im + x_re, l_re * s_im + l_im * s_re + x_im


def _ssm_prompt_kernel(u_ref, m_ref, w_ref, v_ref, lam_ref, d_ref, y_ref, st_ref,
                       x_scr, ss_scr, s_scr, *, lc, nb, ncl, nlb):
    tt = pl.program_id(1)
    nrow = nlb * nb
    nq = STATE_LANES // LANES

    @pl.when(tt == 0)
    def _():
        s_scr[...] = jnp.zeros_like(s_scr)

    def piece(b, n, j):
        return u_ref[b, n, pl.ds(j, ncl, stride=lc), :]

    a = []
    for b in range(nlb):
        a.append(jnp.concatenate(
            [jnp.concatenate([piece(b, n, j) for j in range(lc)], axis=1) for n in range(nb)], axis=0).astype(BF16))
        x = jnp.dot(a[b], w_ref[b], preferred_element_type=F32)
        for qq in range(2 * nq):
            for n in range(nb):
                x_scr[qq, pl.ds(b * nb + n, ncl, stride=nrow), :] = x[n * ncl:(n + 1) * ncl, qq * LANES:(qq + 1) * LANES]
    mt = 2 * LANES
    y_intra = [jnp.concatenate(
        [jnp.dot(a[b][:, :(jt + 1) * mt], m_ref[b, :(jt + 1) * mt, jt * mt:(jt + 1) * mt], preferred_element_type=F32)
         for jt in range(lc * LANES // mt)], axis=1) for b in range(nlb)]
    lam_rows = lambda qq: jnp.concatenate(
        [jnp.broadcast_to(lam_ref[b, :, qq * LANES:(qq + 1) * LANES], (nb, LANES)) for b in range(nlb)], axis=0)
    l_re = [lam_rows(qq) for qq in range(nq)]
    l_im = [lam_rows(nq + qq) for qq in range(nq)]

    s = [s_scr[qq] for qq in range(2 * nq)]
    for c in range(ncl):
        rows = slice(c * nrow, (c + 1) * nrow)
        for qq in range(nq):
            ss_scr[qq, rows, :] = s[qq]
            ss_scr[nq + qq, rows, :] = s[nq + qq]
            s[qq], s[nq + qq] = _complex_step(s[qq], s[nq + qq], l_re[qq], l_im[qq],
                                              x_scr[qq, rows, :], x_scr[nq + qq, rows, :])
    for qq in range(2 * nq):
        s_scr[qq] = s[qq]
    for b in range(nlb):
        st_ref[b] = jnp.concatenate([s[qq][b * nb:(b + 1) * nb] for qq in range(2 * nq)], axis=1)
        s_start = jnp.concatenate(
            [jnp.concatenate([ss_scr[qq, pl.ds(b * nb + n, ncl, stride=nrow), :] for n in range(nb)], axis=0)
             for qq in range(2 * nq)], axis=1).astype(BF16)
        y = y_intra[b] + jnp.dot(s_start, v_ref[b], preferred_element_type=F32)
        d = d_ref[b]
        for n in range(nb):
            for j in range(lc):
                y_ref[b, n, pl.ds(j, ncl, stride=lc), :] = (
                    y[n * ncl:(n + 1) * ncl, j * LANES:(j + 1) * LANES] + d * piece(b, n, j))


def _ssm_prompt(u4, m, w, v, lam, d, lc, tl, nlb):
    _, nb, seq, _ = u4.shape
    ncl = tl // lc
    nslab = 2 * STATE_LANES // LANES
    kern = functools.partial(_ssm_prompt_kernel, lc=lc, nb=nb, ncl=ncl, nlb=nlb)
    wspec = lambda a: pl.BlockSpec((nlb,) + a.shape[1:], lambda p, t: (p, 0, 0), pipeline_mode=pl.Buffered(1))
    io = pl.BlockSpec((nlb, nb, tl, LANES), lambda p, t: (p, 0, t, 0))
    return pl.pallas_call(
        kern,
        grid=(N_LANE_BLOCKS // nlb, seq // tl),
        in_specs=[io, wspec(m), wspec(w), wspec(v), wspec(lam), wspec(d)],
        out_specs=[io, pl.BlockSpec((nlb, nb, 2 * STATE_LANES), lambda p, t: (p, 0, 0))],
        out_shape=[jax.ShapeDtypeStruct(u4.shape, F32),
                   jax.ShapeDtypeStruct((N_LANE_BLOCKS, nb, 2 * STATE_LANES), F32)],
        scratch_shapes=[pltpu.VMEM((nslab, nlb * nb * ncl, LANES), F32),
                        pltpu.VMEM((nslab, nlb * nb * ncl, LANES), F32),
                        pltpu.VMEM((nslab, nlb * nb, LANES), F32)],
        compiler_params=_params(("arbitrary", "arbitrary")),
        name="ssm_prompt",
    )(u4, m, w, v, lam, d)


def _ssm_sample_kernel(u_ref, sre_ref, sim_ref, m_ref, w_ref, v_ref, lam_ref, d_ref,
                       y_ref, ore_ref, oim_ref, *, lc, ns):
    def piece(j):
        return u_ref[pl.ds(j, ns, stride=lc), :]

    a = jnp.concatenate([piece(j) for j in range(lc)], axis=1).astype(BF16)
    s_re, s_im = sre_ref[...], sim_ref[...]
    x = jnp.dot(a, w_ref[0], preferred_element_type=F32)
    e_re, e_im = _complex_step(s_re, s_im, lam_ref[0, :, :STATE_LANES], lam_ref[0, :, STATE_LANES:],
                               x[:, :STATE_LANES], x[:, STATE_LANES:])
    ore_ref[...] = e_re
    oim_ref[...] = e_im
    s0 = jnp.concatenate([s_re, s_im], axis=1).astype(BF16)
    y = (jnp.dot(a, m_ref[0], preferred_element_type=F32) + jnp.dot(s0, v_ref[0], preferred_element_type=F32))
    d = d_ref[...]
    for j in range(lc):
        y_ref[pl.ds(j, ns, stride=lc), :] = y[:, j * LANES:(j + 1) * LANES] + d * piece(j)


def _ssm_sample(u, s_re, s_im, m, w, v, lam, d, lc):
    t = u.shape[0]
    ns = t // lc
    kern = functools.partial(_ssm_sample_kernel, lc=lc, ns=ns)
    assert m.shape[1] == 2 * lc * LANES
    n = lc * LANES
    col = lambda rows, width: pl.BlockSpec((rows, width), lambda b: (0, b))
    return pl.pallas_call(
        kern,
        grid=(N_LANE_BLOCKS,),
        in_specs=[col(t, LANES), col(ns, STATE_LANES), col(ns, STATE_LANES),
                  pl.BlockSpec((1, n, n), lambda b: (b, 0, 0)),
                  pl.BlockSpec((1, n, 2 * STATE_LANES), lambda b: (b, 1, 0)),
                  pl.BlockSpec((1, 2 * STATE_LANES, n), lambda b: (b, 0, 0)),
                  pl.BlockSpec((1, 1, 2 * STATE_LANES), lambda b: (b, 0, 0)), col(1, LANES)],
        out_specs=[col(t, LANES), col(ns, STATE_LANES), col(ns, STATE_LANES)],
        out_shape=[jax.ShapeDtypeStruct(u.shape, F32), jax.ShapeDtypeStruct(s_re.shape, F32),
                   jax.ShapeDtypeStruct(s_im.shape, F32)],
        compiler_params=_params(("arbitrary",)),
        name="ssm_sample",
    )(u, s_re, s_im, m, w, v, lam, d)


def _ssm_discretise(a_re, a_im, log_step):
    dt = jnp.exp(log_step)
    mag = jnp.exp(a_re * dt)
    l_re, l_im = mag * jnp.cos(a_im * dt), mag * jnp.sin(a_im * dt)
    den = a_re * a_re + a_im * a_im
    n_re, n_im = l_re - 1.0, l_im
    return l_re, l_im, (n_re * a_re + n_im * a_im) / den, (n_im * a_re - n_re * a_im) / den


def _complex_powers(l_re, l_im, n):
    p_re, p_im = [jnp.ones_like(l_re)], [jnp.zeros_like(l_re)]
    for _ in range(n):
        p_re, p_im = p_re + [p_re[-1] * l_re - p_im[-1] * l_im], p_im + [p_re[-1] * l_im + p_im[-1] * l_re]
    return p_re, p_im


def _split_bf16(x):
    hi = x.astype(BF16)
    return hi, (x - hi.astype(F32)).astype(BF16)


def _dot_split(a, b):
    dot = lambda x, y: jnp.dot(x, y, preferred_element_type=F32)
    return dot(a[0], b[0]) + (dot(a[0], b[1]) + dot(a[1], b[0]))


def _ssm_tables_kernel(ac_re_ref, ac_im_ref, lsc_ref, ar_re_ref, ar_im_ref, lsr_ref, b_re_ref, b_im_ref,
                       c_re_ref, c_im_ref, m_ref, w_ref, v_ref, lam_ref, lam_half_ref, *, lc):
    l_re, l_im, cf_re, cf_im = _ssm_discretise(ac_re_ref[0], ac_im_ref[0], lsc_ref[0])
    b_re, b_im, c_re, c_im = b_re_ref[0], b_im_ref[0], c_re_ref[0], c_im_ref[0]
    c_re_parts, c_im_parts = _split_bf16(c_re), _split_bf16(c_im)
    bb_re = cf_re * b_re - cf_im * b_im
    bb_im = cf_re * b_im + cf_im * b_re
    p_re, p_im = _complex_powers(l_re, l_im, lc)
    lag = []
    for k in range(lc):
        et_re = (p_re[k] * bb_re - p_im[k] * bb_im).T
        et_im = (p_re[k] * bb_im + p_im[k] * bb_re).T
        j = lc - 1 - k
        w_ref[0, j * LANES:(j + 1) * LANES, :STATE_LANES] = et_re.astype(BF16)
        w_ref[0, j * LANES:(j + 1) * LANES, STATE_LANES:] = et_im.astype(BF16)
        lag.append((_dot_split(_split_bf16(et_re), c_re_parts) - _dot_split(_split_bf16(et_im), c_im_parts)).astype(BF16))
        v_ref[0, :STATE_LANES, k * LANES:(k + 1) * LANES] = (p_re[k + 1] * c_re - p_im[k + 1] * c_im).astype(BF16)
        v_ref[0, STATE_LANES:, k * LANES:(k + 1) * LANES] = (-(p_im[k + 1] * c_re + p_re[k + 1] * c_im)).astype(BF16)
    zero = jnp.zeros((LANES, LANES), BF16)
    for j in range(lc):
        for jj in range(lc):
            m_ref[0, j * LANES:(j + 1) * LANES, jj * LANES:(jj + 1) * LANES] = lag[jj - j] if jj >= j else zero
    r_re, r_im, _, _ = _ssm_discretise(ar_re_ref[0], ar_im_ref[0], lsr_ref[0])
    q_re, q_im = _complex_powers(r_re, r_im, lc)
    lam_ref[0] = jnp.concatenate([q_re[lc], q_im[lc]], axis=1)
    lam_half_ref[0] = jnp.concatenate([q_re[lc // 2], q_im[lc // 2]], axis=1)


def _ssm_tables(a_re, a_im, log_step, b_re, b_im, c_re, c_im, lc):
    g, p, h = b_re.shape
    nbk, r = N_LANE_BLOCKS, GROUPS_PER_LANE_BLOCK
    ls = jnp.broadcast_to(log_step[:, None], (g, p))
    cols = [jnp.broadcast_to(x.reshape(nbk, STATE_LANES, 1), (nbk, STATE_LANES, LANES)) for x in (a_re, a_im, ls)]
    rows = [x.reshape(nbk, 1, STATE_LANES) for x in (a_re, a_im, ls)]
    eye = jnp.eye(r, dtype=F32)[None, :, None, :, None]

    def block_diag(x):
        return (x[:, :, :, None, :] * eye).reshape(nbk, STATE_LANES, LANES)

    mats = [block_diag(b_re.reshape(nbk, r, p, h)), block_diag(b_im.reshape(nbk, r, p, h)),
            block_diag(c_re.reshape(nbk, r, h, p).transpose(0, 1, 3, 2)),
            block_diag(c_im.reshape(nbk, r, h, p).transpose(0, 1, 3, 2))]
    spec = lambda shape: pl.BlockSpec((1,) + shape, lambda b: (b, 0, 0))
    n = lc * LANES
    return pl.pallas_call(
        functools.partial(_ssm_tables_kernel, lc=lc),
        grid=(nbk,),
        in_specs=[spec((STATE_LANES, LANES))] * 3 + [spec((1, STATE_LANES))] * 3 + [spec((STATE_LANES, LANES))] * 4,
        out_specs=[spec((n, n)), spec((n, 2 * STATE_LANES)), spec((2 * STATE_LANES, n)),
                   spec((1, 2 * STATE_LANES)), spec((1, 2 * STATE_LANES))],
        out_shape=[jax.ShapeDtypeStruct((nbk, n, n), BF16), jax.ShapeDtypeStruct((nbk, n, 2 * STATE_LANES), BF16),
                   jax.ShapeDtypeStruct((nbk, 2 * STATE_LANES, n), BF16),
                   jax.ShapeDtypeStruct((nbk, 1, 2 * STATE_LANES), F32),
                   jax.ShapeDtypeStruct((nbk, 1, 2 * STATE_LANES), F32)],
        compiler_params=_params(("arbitrary",)),
        name="ssm_tables",
    )(*cols, *rows, *mats)


def _nt_dot(a, b):
    return lax.dot_general(a, b, (((1,), (1,)), ((), ())), preferred_element_type=F32)


def _swa_prompt_tile(sink_ref, qt_ref, kcat, vcat_t, mstd_ref, mfirst_ref, g_ref, ot_ref, first_tile, nblk):
    keys = lambda j: slice(j * WINDOW, (j + 2) * WINDOW)
    zeros = jnp.zeros((HEAD_DIM, WINDOW), BF16)
    sinks = [jnp.concatenate([jnp.full((1, WINDOW), sink_ref[kh * GQA_GROUP + g], F32) for g in range(GQA_GROUP)],
                             axis=1) for kh in range(N_KV_HEADS)]
    chains = [(j, kh) for j in range(nblk) for kh in range(N_KV_HEADS)]

    def scores(j, kh):
        def rhs(g):
            h = kh * GQA_GROUP + g
            q = qt_ref[h * HEAD_DIM:(h + 1) * HEAD_DIM, j * WINDOW:(j + 1) * WINDOW]
            return jnp.concatenate([q, zeros] if kh == 0 else [zeros, q], axis=0)
        r = jnp.concatenate([rhs(g) for g in range(GQA_GROUP)], axis=1)
        vmask = mstd_ref[...]
        if j == 0:
            vmask = jnp.where(first_tile, mfirst_ref[...], vmask)
        valid = jnp.tile(vmask, (1, GQA_GROUP)) > 0.0
        return jnp.where(valid, jnp.dot(kcat[keys(j)], r, preferred_element_type=F32), NEG_INF)

    def softmax(kh, sc):
        m = jnp.maximum(jnp.max(sc, axis=0, keepdims=True), sinks[kh])
        e = jnp.exp(sc - m)
        return e.astype(BF16), 1.0 / (jnp.sum(e, axis=0, keepdims=True) + jnp.exp(sinks[kh] - m))

    sc = [scores(*c) for c in chains]
    pr = [softmax(kh, x) for (j, kh), x in zip(chains, sc)]

    def finish():
        out = [jnp.dot(vcat_t[kh * HEAD_DIM:(kh + 1) * HEAD_DIM, keys(j)], e, preferred_element_type=F32) * inv
               for (j, kh), (e, inv) in zip(chains, pr)]
        gain = g_ref[...]
        for j in range(nblk):
            o = jnp.concatenate([out[j * N_KV_HEADS + kh][:, g * WINDOW:(g + 1) * WINDOW]
                                 for kh in range(N_KV_HEADS) for g in range(GQA_GROUP)], axis=0)
            scale = lax.rsqrt(jnp.mean(o * o, axis=0, keepdims=True) + RMS_EPS)
            ot_ref[:, j * WINDOW:(j + 1) * WINDOW] = (o * scale * gain).astype(BF16)

    return finish


def _swa_masks():
    kj = jnp.arange(2 * WINDOW)[:, None]
    diff = jnp.arange(WINDOW)[None, :] + WINDOW - kj
    std = (diff >= 0) & (diff <= WINDOW)
    return std.astype(F32), (std & (kj >= WINDOW)).astype(F32)


def _ffn_in_swa_kernel(sink_ref, x_ref, pre_g_ref, wg_ref, wu_ref, wd_ref, post_g_ref, mix_g_ref, win_u_ref, wqkv_t_ref,
                       cos_t_ref, sin_t_ref, mstd_ref, mfirst_ref, swa_g_ref,
                       h_ref, u_ref, k_ref, vt_ref, ot_ref, act_ref, q_scr, kcat_scr, vcat_scr, *, nt, nblk):
    i = pl.program_id(0)
    tm = x_ref.shape[0]

    @pl.when(i == 0)
    def _():
        q_scr[...] = jnp.zeros_like(q_scr)
        kcat_scr[...] = jnp.zeros_like(kcat_scr)
        vcat_scr[...] = jnp.zeros_like(vcat_scr)

    first_tile = lax.rem(i + (nt - 1), nt) == 0
    swa_finish = _swa_prompt_tile(sink_ref, q_scr, kcat_scr[...], vcat_scr[...], mstd_ref, mfirst_ref, swa_g_ref,
                                  ot_ref, first_tile, nblk)

    kcat_scr[:WINDOW, :] = kcat_scr[tm:, :]
    vcat_scr[:, :WINDOW] = vcat_scr[:, tm:]
    half = ROPE_DIM // 2

    def row_block(r, swa_hook):
        n = r.stop - r.start
        h = yield from _ffn_stages(x_ref[r, :], pre_g_ref[...], wg_ref, wu_ref, wd_ref, post_g_ref[...], act_ref, r,
                                   after_chunk=swa_hook)
        h_ref[r, :] = h
        hn = _rms(h, mix_g_ref[...]).astype(BF16)
        z = jnp.dot(hn, win_u_ref[...], preferred_element_type=F32)
        yield
        for b in range(N_LANE_BLOCKS):
            u_ref[b, r, :] = z[:, b * LANES:(b + 1) * LANES]
        zt = _nt_dot(wqkv_t_ref[...], hn)
        yield
        vt = zt[Q_WIDTH + KV_WIDTH:]
        vt_ref[:, r] = vt
        qk = zt[:Q_WIDTH + KV_WIDTH].reshape(N_HEADS + N_KV_HEADS, HEAD_DIM, n)
        x1, x2 = qk[:, :half], qk[:, half:ROPE_DIM]
        cos, sin = cos_t_ref[:, r][None], sin_t_ref[:, r][None]
        qk = jnp.concatenate([x1 * cos - x2 * sin, x2 * cos + x1 * sin, qk[:, ROPE_DIM:]], axis=1)
        k = qk[N_HEADS:].reshape(KV_WIDTH, n).T
        k_ref[r, :] = k
        shifted = slice(WINDOW + r.start, WINDOW + r.stop)
        kcat_scr[shifted, :] = k.astype(BF16)
        vcat_scr[:, shifted] = vt.astype(BF16)
        q_scr[:, r] = (qk[:N_HEADS].reshape(Q_WIDTH, n) * (HEAD_DIM ** -0.5)).astype(BF16)

    blocks = [slice(b * tm // ROW_BLOCKS, (b + 1) * tm // ROW_BLOCKS) for b in range(ROW_BLOCKS)]
    _interleave([row_block(r, (N_FF_CHUNKS - 3, swa_finish) if r.start == 0 else None) for r in blocks])


def _ffn_in_swa(x, pre_g, ffn_w, post_g, mix_g, win_u, wqkv_t, cos_t, sin_t, sinks, swa_g, seq, tm):
    t = x.shape[0]
    n_tiles, nt, nblk = t // tm, seq // tm, tm // WINDOW
    half = ROPE_DIM // 2
    mstd, mfirst = _swa_masks()
    gain = jnp.broadcast_to(swa_g.reshape(Q_WIDTH, 1), (Q_WIDTH, WINDOW))
    cur = lambda i: jnp.minimum(i, n_tiles - 1)
    row = lambda w: pl.BlockSpec((tm, w), lambda i: (cur(i), 0))
    tab_t = pl.BlockSpec((half, tm), lambda i: (0, cur(i) % nt))
    return pl.pallas_call(
        functools.partial(_ffn_in_swa_kernel, nt=nt, nblk=nblk),
        grid=(n_tiles + 1,),
        in_specs=[pl.BlockSpec(memory_space=pltpu.SMEM), row(D_MODEL), _const_spec((1, D_MODEL)),
                  *[_const_spec(w.shape) for w in ffn_w], _const_spec((1, D_MODEL)), _const_spec((1, D_MODEL)),
                  _const_spec(win_u.shape), _const_spec(wqkv_t.shape), tab_t, tab_t,
                  _const_spec(mstd.shape), _const_spec(mfirst.shape), _const_spec(gain.shape)],
        out_specs=[row(D_MODEL), pl.BlockSpec((N_LANE_BLOCKS, tm, LANES), lambda i: (0, cur(i), 0)), row(KV_WIDTH),
                   pl.BlockSpec((KV_WIDTH, tm), lambda i: (0, cur(i))),
                   pl.BlockSpec((Q_WIDTH, tm), lambda i: (0, jnp.maximum(i - 1, 0)))],
        out_shape=[jax.ShapeDtypeStruct((t, D_MODEL), F32), jax.ShapeDtypeStruct((N_LANE_BLOCKS, t, LANES), F32),
                   jax.ShapeDtypeStruct((t, KV_WIDTH), F32), jax.ShapeDtypeStruct((KV_WIDTH, t), F32),
                   jax.ShapeDtypeStruct((Q_WIDTH, t), BF16)],
        scratch_shapes=[pltpu.VMEM((tm, D_FF), BF16), pltpu.VMEM((Q_WIDTH, tm), BF16),
                        pltpu.VMEM((WINDOW + tm, KV_WIDTH), BF16), pltpu.VMEM((KV_WIDTH, WINDOW + tm), BF16)],
        compiler_params=_params(("arbitrary",)),
        name="ffn_in_swa",
    )(sinks, x, pre_g, *ffn_w, post_g, mix_g, win_u, wqkv_t, cos_t, sin_t, mstd, mfirst, gain)


def _swa_sample_kernel(sink_ref, q_ref, kn_ref, vn_ref, ck_ref, cv_ref, g_ref, o_ref, nk_ref, nv_ref, *, ns, t):
    rows = GQA_GROUP * t
    tok = lax.broadcasted_iota(jnp.int32, (rows, WINDOW), 0) % t
    valid_c = lax.broadcasted_iota(jnp.int32, (rows, WINDOW), 1) >= tok
    tok_n = lax.broadcasted_iota(jnp.int32, (rows, t), 0) % t
    valid_n = lax.broadcasted_iota(jnp.int32, (rows, t), 1) <= tok_n
    gain = g_ref[...]

    sinks = [jnp.concatenate([jnp.full((t, 1), sink_ref[kh * GQA_GROUP + g], F32) for g in range(GQA_GROUP)], axis=0)
             for kh in range(N_KV_HEADS)]
    heads = [(s, kh) for s in range(ns) for kh in range(N_KV_HEADS)]
    hs = lambda kh: slice(kh * HEAD_DIM, (kh + 1) * HEAD_DIM)

    newest = lax.broadcasted_iota(jnp.int32, (KV_WIDTH, WINDOW), 1) >= WINDOW - t
    pad = jnp.zeros((WINDOW - 2 * t, KV_WIDTH), F32)

    def shifted(cache_ref, new_ref, s):
        new_rows = jnp.concatenate([pad, jnp.zeros((t, KV_WIDTH), F32), new_ref[s]], axis=0)
        return jnp.where(newest, new_rows.T, pltpu.roll(cache_ref[s], WINDOW - t, 1))

    for s in range(ns):
        nk_ref[s] = shifted(ck_ref, kn_ref, s)
        nv_ref[s] = shifted(cv_ref, vn_ref, s)

    def scores(s, kh):
        q = q_ref[s].astype(F32)
        q4 = jnp.concatenate([q[:, (kh * GQA_GROUP + g) * HEAD_DIM:(kh * GQA_GROUP + g + 1) * HEAD_DIM]
                              for g in range(GQA_GROUP)], axis=0).astype(BF16)
        sc_c = jnp.where(valid_c, jnp.dot(q4, ck_ref[s, hs(kh), :].astype(BF16), preferred_element_type=F32), NEG_INF)
        sc_n = jnp.where(valid_n, _nt_dot(q4, kn_ref[s, :, hs(kh)].astype(BF16)), NEG_INF)
        return sc_c, sc_n

    def softmax(kh, sc_c, sc_n):
        m = jnp.maximum(jnp.maximum(jnp.max(sc_c, axis=-1, keepdims=True),
                                    jnp.max(sc_n, axis=-1, keepdims=True)), sinks[kh])
        e_c, e_n = jnp.exp(sc_c - m), jnp.exp(sc_n - m)
        inv = 1.0 / (jnp.sum(e_c, axis=-1, keepdims=True) + jnp.sum(e_n, axis=-1, keepdims=True)
                     + jnp.exp(sinks[kh] - m))
        return (e_c * inv).astype(BF16), (e_n * inv).astype(BF16)

    def values(s, kh, p_c, p_n):
        return (_nt_dot(p_c, cv_ref[s, hs(kh), :].astype(BF16))
                + jnp.dot(p_n, vn_ref[s, :, hs(kh)].astype(BF16), preferred_element_type=F32))

    sc = [scores(s, kh) for s, kh in heads]
    pr = [softmax(kh, *x) for (s, kh), x in zip(heads, sc)]
    o4 = [values(s, kh, *x) for (s, kh), x in zip(heads, pr)]
    for s in range(ns):
        o = jnp.concatenate([o4[s * N_KV_HEADS + kh][g * t:(g + 1) * t]
                             for kh in range(N_KV_HEADS) for g in range(GQA_GROUP)], axis=1)
        o_ref[s] = _rms(o, gain).astype(BF16)


def _swa_sample(q3, kn3, vn3, ck, cv, sinks, out_g, sb):
    ns, t, _ = q3.shape
    blk = lambda a: pl.BlockSpec((sb,) + a.shape[1:], lambda i: (i, 0, 0))
    kern = functools.partial(_swa_sample_kernel, ns=sb, t=t)
    return pl.pallas_call(
        kern,
        grid=(ns // sb,),
        in_specs=[pl.BlockSpec(memory_space=pltpu.SMEM), blk(q3), blk(kn3), blk(vn3), blk(ck), blk(cv),
                  pl.BlockSpec((1, Q_WIDTH), lambda i: (0, 0))],
        out_specs=[blk(q3), blk(ck), blk(cv)],
        out_shape=[jax.ShapeDtypeStruct(q3.shape, BF16), jax.ShapeDtypeStruct(ck.shape, F32),
                   jax.ShapeDtypeStruct(cv.shape, F32)],
        compiler_params=_params(("arbitrary",)),
        name="swa_sample",
    )(sinks, q3, kn3, vn3, ck, cv, out_g)


def _gelu_tanh(x):
    return 0.5 * x * (1.0 + jnp.tanh(math.sqrt(2.0 / math.pi) * (x + 0.044715 * (x * x * x))))


def _merge_stages(h, y, at, wglu_ref, bglu, sg, wout_ref, post_g, xa_g, wq_ref, at_transposed=False):
    g = _gelu_tanh(y)
    lin = jnp.dot(g.astype(BF16), wglu_ref[...], preferred_element_type=F32) + bglu
    yield
    y_ssm = g * (1.0 / (1.0 + jnp.exp(-lin)))
    ssm_n = _rms(y_ssm, sg).astype(BF16)
    at_dims = (((0,), (0,)), ((), ())) if at_transposed else (((1,), (0,)), ((), ()))
    mixed = (jnp.dot(ssm_n, wout_ref[:SSM_WIDTH, :], preferred_element_type=F32)
             + lax.dot_general(at, wout_ref[SSM_WIDTH:, :], at_dims, preferred_element_type=F32))
    yield
    h2 = h + _rms(mixed, post_g)
    qm = jnp.dot(_rms(h2, xa_g).astype(BF16), wq_ref[...], preferred_element_type=F32)
    yield
    return h2, (qm * (MEM_HEAD_DIM ** -0.5)).astype(BF16)


def _merge_tile(*args, **kwargs):
    return _run(_merge_stages(*args, **kwargs))


def _merge_kernel(h_ref, y_ref, at_ref, wglu_ref, bglu_ref, sg_ref, wout_ref, post_g_ref, xa_g_ref, wq_ref,
                  h2_ref, qm_ref):
    h2_ref[...], qm_ref[...] = _merge_tile(h_ref[...], y_ref[...], at_ref[...], wglu_ref, bglu_ref[...], sg_ref[...],
                                           wout_ref, post_g_ref[...], xa_g_ref[...], wq_ref)


def _merge_specs(wglu, wout, wq):
    return [_const_spec(wglu.shape), _const_spec((1, SSM_WIDTH)), _const_spec((1, SSM_WIDTH)), _const_spec(wout.shape),
            _const_spec((1, D_MODEL)), _const_spec((1, D_MODEL)), _const_spec(wq.shape)]


def _merge(h, y, at, merge_w, tm):
    t = h.shape[0]
    row = lambda w: pl.BlockSpec((tm, w), lambda i: (i, 0))
    return pl.pallas_call(
        _merge_kernel,
        grid=(t // tm,),
        in_specs=[row(D_MODEL), row(SSM_WIDTH), row(Q_WIDTH)] + _merge_specs(merge_w[0], merge_w[3], merge_w[6]),
        out_specs=[row(D_MODEL), row(D_MODEL)],
        out_shape=[jax.ShapeDtypeStruct((t, D_MODEL), F32), jax.ShapeDtypeStruct((t, D_MODEL), BF16)],
        compiler_params=_params(("arbitrary",)),
        name="merge",
    )(h, y, at, *merge_w)


def _mem_attn_sample_kernel(q_ref, k_ref, v_ref, o_ref, *, gb, t):
    rows = t * MEM_HEADS
    halves = MEM_HEAD_DIM // LANES
    kv_rows = N_MEM * halves * MEM_HEADS
    period = halves * MEM_HEADS
    lane = lax.broadcasted_iota(jnp.int32, (rows, kv_rows), 1) % period
    head = lax.broadcasted_iota(jnp.int32, (rows, kv_rows), 0) % MEM_HEADS
    in_half = [lane == head + hf * MEM_HEADS for hf in range(halves)]
    for b in range(gb):
        kb = k_ref[b].astype(BF16)
        vb = v_ref[b].astype(BF16)
        part = _nt_dot(q_ref[b], kb)
        sc = jnp.where(in_half[0], part[:rows], 0.0)
        for hf in range(1, halves):
            sc = sc + pltpu.roll(jnp.where(in_half[hf], part[hf * rows:(hf + 1) * rows], 0.0),
                                 kv_rows - hf * MEM_HEADS, 1)
        sc = jnp.where(in_half[0], sc, -jnp.inf)
        e = jnp.exp(sc - jnp.max(sc, axis=-1, keepdims=True))
        inv = 1.0 / jnp.sum(e, axis=-1, keepdims=True)
        e_all = jnp.concatenate([e] + [pltpu.roll(e, hf * MEM_HEADS, 1) for hf in range(1, halves)], axis=0)
        o = jnp.dot(e_all.astype(BF16), vb, preferred_element_type=F32)
        o_ref[b] = (o * jnp.concatenate([inv] * halves, axis=0)).astype(BF16)


def _mem_attn_sample(qm, cache_k, cache_v, n_s, t_s, gb):
    halves = MEM_HEAD_DIM // LANES
    rows = halves * t_s * MEM_HEADS
    kv_rows = N_MEM * halves * MEM_HEADS

    def stored_rows(c):
        c = c.reshape(n_s, N_MEM, MEM_HEADS, halves, LANES).transpose(0, 1, 3, 2, 4)
        return c.reshape(n_s, kv_rows, LANES)

    q = qm.reshape(n_s, t_s, MEM_HEADS, halves, LANES).transpose(0, 3, 1, 2, 4).reshape(n_s, rows, LANES)
    blk = lambda r: pl.BlockSpec((gb, r, LANES), lambda i: (i, 0, 0))
    o = pl.pallas_call(
        functools.partial(_mem_attn_sample_kernel, gb=gb, t=t_s),
        grid=(n_s // gb,),
        in_specs=[blk(rows), blk(kv_rows), blk(kv_rows)],
        out_specs=blk(rows),
        out_shape=jax.ShapeDtypeStruct((n_s, rows, LANES), BF16),
        compiler_params=_params(("arbitrary",)),
        name="mem_attn_sample",
    )(q, stored_rows(cache_k), stored_rows(cache_v))
    o = o.reshape(n_s, halves, t_s, MEM_HEADS, LANES).transpose(0, 2, 3, 1, 4)
    return o.reshape(n_s * t_s, D_MODEL)


def _mem_heads_stages(q, k_ref, v_ref):
    outs = []
    for hh in range(MEM_HEADS):
        hs = slice(hh * MEM_HEAD_DIM, (hh + 1) * MEM_HEAD_DIM)
        sc = _nt_dot(q[:, hs], k_ref[0, :, hs])
        yield
        e = jnp.exp(sc - jnp.max(sc, axis=-1, keepdims=True))
        inv = 1.0 / jnp.sum(e, axis=-1, keepdims=True)
        outs.append((jnp.dot(e.astype(BF16), v_ref[0, :, hs], preferred_element_type=F32) * inv).astype(BF16))
        yield
    return jnp.concatenate(outs, axis=1)


def _ffn_out_stages(h2, om, wo_ref, xa_post, pre_g, wg_ref, wu_ref, wd_ref, post_g, act_ref, rows=slice(None)):
    c = jnp.dot(om, wo_ref[...], preferred_element_type=F32)
    yield
    h3 = h2 + _rms(c, xa_post)
    return (yield from _ffn_stages(h3, pre_g, wg_ref, wu_ref, wd_ref, post_g, act_ref, rows))


def _ffn_out_tile(*args, **kwargs):
    return _run(_ffn_out_stages(*args, **kwargs))


def _ffn_out_specs(wo, ffn_w):
    return [_const_spec(wo.shape), _const_spec((1, D_MODEL)), _const_spec((1, D_MODEL)),
            *[_const_spec(w.shape) for w in ffn_w], _const_spec((1, D_MODEL))]


def _ffn_out_kernel(h_ref, o_ref, wo_ref, xa_post_ref, pre_g_ref, wg_ref, wu_ref, wd_ref, post_g_ref, out_ref, act_ref):
    out_ref[...] = _ffn_out_tile(h_ref[...], o_ref[...], wo_ref, xa_post_ref[...], pre_g_ref[...], wg_ref, wu_ref,
                                 wd_ref, post_g_ref[...], act_ref)


def _ffn_out(h, o, out_w, tm):
    t = h.shape[0]
    row = pl.BlockSpec((tm, D_MODEL), lambda i: (i, 0))
    return pl.pallas_call(
        _ffn_out_kernel,
        grid=(t // tm,),
        in_specs=[row, row] + _ffn_out_specs(out_w[0], out_w[3:6]),
        out_specs=row,
        out_shape=jax.ShapeDtypeStruct((t, D_MODEL), F32),
        scratch_shapes=[pltpu.VMEM((tm, D_FF), BF16)],
        compiler_params=_params(("arbitrary",)),
        name="ffn_out",
    )(h, o, *out_w)


def _post_kernel(h_ref, y_ref, at_ref, k_ref, v_ref,
                 wglu_ref, bglu_ref, sg_ref, wout_ref, post_g_ref, xa_g_ref, wq_ref,
                 wo_ref, xa_post_ref, pre_g_ref, wg_ref, wu_ref, wd_ref, ffn_post_ref, out_ref, act_ref):
    def row_block(r):
        y = jnp.concatenate([y_ref[b, r, :] for b in range(N_LANE_BLOCKS)], axis=1)
        h2, qm = yield from _merge_stages(h_ref[r, :], y, at_ref[:, r], wglu_ref, bglu_ref[...], sg_ref[...],
                                          wout_ref, post_g_ref[...], xa_g_ref[...], wq_ref, at_transposed=True)
        om = yield from _mem_heads_stages(qm, k_ref, v_ref)
        out_ref[r, :] = yield from _ffn_out_stages(h2, om, wo_ref, xa_post_ref[...], pre_g_ref[...], wg_ref, wu_ref,
                                                   wd_ref, ffn_post_ref[...], act_ref, r)

    tm = h_ref.shape[0]
    _interleave([row_block(slice(i * tm // ROW_BLOCKS, (i + 1) * tm // ROW_BLOCKS)) for i in range(ROW_BLOCKS)])


def _post(h, y, at, k3, v3, merge_w, out_w, tm):
    t = h.shape[0]
    tiles_per_batch = t // k3.shape[0] // tm
    row = lambda w: pl.BlockSpec((tm, w), lambda i: (i, 0))
    kv = pl.BlockSpec((1, N_MEM, D_MODEL), lambda i: (i // tiles_per_batch, 0, 0))
    return pl.pallas_call(
        _post_kernel,
        grid=(t // tm,),
        in_specs=([row(D_MODEL), pl.BlockSpec((N_LANE_BLOCKS, tm, LANES), lambda i: (0, i, 0)),
                   pl.BlockSpec((Q_WIDTH, tm), lambda i: (0, i)), kv, kv]
                  + _merge_specs(merge_w[0], merge_w[3], merge_w[6]) + _ffn_out_specs(out_w[0], out_w[3:6])),
        out_specs=row(D_MODEL),
        out_shape=jax.ShapeDtypeStruct((t, D_MODEL), F32),
        scratch_shapes=[pltpu.VMEM((tm, D_FF), BF16)],
        compiler_params=_params(("arbitrary",)),
        name="post",
    )(h, y, at, k3, v3, *merge_w, *out_w)


def _mem_kv_kernel(m_ref, g_ref, wkv_ref, k_ref, v_ref, kb_ref, vb_ref):
    kv = jnp.dot(_rms(m_ref[...], g_ref[...]).astype(BF16), wkv_ref[...], preferred_element_type=F32)
    kb_ref[...] = kv[:, :D_MODEL].astype(BF16)
    vb_ref[...] = kv[:, D_MODEL:].astype(BF16)
    halves = MEM_HEAD_DIM // LANES
    n = m_ref.shape[0]
    for out_ref, base in ((k_ref, 0), (v_ref, D_MODEL)):
        for hh in range(MEM_HEADS):
            for hf in range(halves):
                col = base + hh * MEM_HEAD_DIM + hf * LANES
                out_ref[0, pl.ds(hf * MEM_HEADS + hh, n, stride=halves * MEM_HEADS), :] = kv[:, col:col + LANES]


def _mem_kv(mem, g, wkv, tm):
    t = mem.shape[0]
    row = pl.BlockSpec((tm, D_MODEL), lambda i: (i, 0))
    per_slot = (MEM_HEAD_DIM // LANES) * MEM_HEADS
    stored = pl.BlockSpec((1, tm * per_slot, LANES), lambda i: (i, 0, 0))
    return pl.pallas_call(
        _mem_kv_kernel,
        grid=(t // tm,),
        in_specs=[row, _const_spec((1, D_MODEL)), _const_spec(wkv.shape)],
        out_specs=[stored, stored, row, row],
        out_shape=[jax.ShapeDtypeStruct((t // tm, tm * per_slot, LANES), F32)] * 2
        + [jax.ShapeDtypeStruct((t, D_MODEL), BF16)] * 2,
        compiler_params=_params(("arbitrary",)),
        name="mem_kv",
    )(mem, g, wkv)


def _rope_tables(pos):
    half = ROPE_DIM // 2
    inv = ROPE_THETA ** (-jnp.arange(half, dtype=F32) * (2.0 / ROPE_DIM))
    ang = pos.astype(F32)[:, None] * inv[None, :]
    cos, sin = jnp.cos(ang), jnp.sin(ang)
    n = pos.shape[0]
    pad = jnp.zeros((n, HEAD_DIM - ROPE_DIM), F32)
    zero = jnp.zeros((n, half), F32)
    cos_h = jnp.concatenate([cos, cos, pad + 1.0], axis=1)
    lo_h = jnp.concatenate([-sin, zero, pad], axis=1)
    hi_h = jnp.concatenate([zero, sin, pad], axis=1)
    rep = LANES // HEAD_DIM
    return tuple(jnp.tile(a, (1, rep)) for a in (cos_h, lo_h, hi_h)), (cos.T, sin.T)


def _ffn_weights(w_gate, w_up, w_down):
    return w_gate.astype(BF16), w_up.astype(BF16), w_down.astype(BF16)


def _lane_block_states(st, n):
    st = st.reshape(N_LANE_BLOCKS, n, 2, GROUPS_PER_LANE_BLOCK, SSM_STATE).transpose(2, 1, 0, 3, 4)
    st = st.reshape(2, n, N_SSM_GROUPS, SSM_STATE)
    return st[0], st[1]


def kernel(x_prompt, x_sample, state_ssm_re, state_ssm_im, cache_swa_k, cache_swa_v, cache_mem_k, cache_mem_v, mem_prompt, ffn1_pre_g, ffn1_w_gate, ffn1_w_up, ffn1_w_down, ffn1_post_g, mix_pre_g, w_in, ssm_a_re, ssm_a_im, ssm_log_step, ssm_b_re, ssm_b_im, ssm_c_re, ssm_c_im, ssm_d, ssm_w_glu, ssm_b_glu, attn_sinks, ssm_out_g, attn_out_g, w_out, mix_post_g, mem_norm_g, w_mem_q, w_mem_k, w_mem_v, w_mem_o, xa_pre_g, xa_post_g, ffn2_pre_g, ffn2_w_gate, ffn2_w_up, ffn2_w_down, ffn2_post_g):
    n_p, s_p, _ = x_prompt.shape
    n_s, t_s, _ = x_sample.shape
    tm = 512
    row = lambda a: a.reshape(1, -1).astype(F32)

    ffn1_w = _ffn_weights(ffn1_w_gate, ffn1_w_up, ffn1_w_down)
    win = w_in.astype(BF16)
    merge_w = (ssm_w_glu.astype(BF16), row(ssm_b_glu), row(ssm_out_g), w_out.astype(BF16), row(mix_post_g),
               row(xa_pre_g), w_mem_q.astype(BF16))
    out_w = (w_mem_o.astype(BF16), row(xa_post_g), row(ffn2_pre_g),
             *_ffn_weights(ffn2_w_gate, ffn2_w_up, ffn2_w_down), row(ffn2_post_g))
    wkv = jnp.concatenate([w_mem_k, w_mem_v], axis=1).astype(BF16)
    d_row = row(ssm_d)
    ssm_args = (ssm_a_re.astype(F32), ssm_a_im.astype(F32), ssm_log_step.astype(F32), ssm_b_re.astype(F32),
                ssm_b_im.astype(F32), ssm_c_re.astype(F32), ssm_c_im.astype(F32))
    sinks = attn_sinks.astype(F32)

    pm_k, pm_v, pm_kb, pm_vb = _mem_kv(mem_prompt.reshape(n_p * N_MEM, D_MODEL), row(mem_norm_g), wkv, N_MEM)

    def from_stored(c):
        halves = MEM_HEAD_DIM // LANES
        return c.reshape(n_p, N_MEM, halves, MEM_HEADS, LANES).transpose(0, 1, 3, 2, 4).reshape(
            n_p, N_MEM, MEM_HEADS, MEM_HEAD_DIM)

    def tokenwise_in(x2, pos_tab):
        return _ffn_in(x2, row(ffn1_pre_g), ffn1_w, row(ffn1_post_g), row(mix_pre_g), win, *pos_tab, tm)

    lc_p = 2 * t_s
    ssm_m, ssm_w, ssm_v, lam_p, lam_s = _ssm_tables(*ssm_args, lc_p)
    _, tab_p_t = _rope_tables(jnp.arange(s_p, dtype=jnp.int32))
    h1, u, k, vt, at = _ffn_in_swa(x_prompt.reshape(n_p * s_p, D_MODEL), row(ffn1_pre_g), ffn1_w, row(ffn1_post_g),
                                   row(mix_pre_g), win[:, :SSM_WIDTH], win[:, SSM_WIDTH:].T, *tab_p_t, sinks,
                                   attn_out_g.astype(F32), s_p, tm)
    y4, st_p = _ssm_prompt(u.reshape(N_LANE_BLOCKS, n_p, s_p, LANES), ssm_m, ssm_w, ssm_v, lam_p,
                           d_row.reshape(N_LANE_BLOCKS, 1, LANES), lc_p, 1024, 2)
    y_prompt = _post(h1, y4.reshape(N_LANE_BLOCKS, n_p * s_p, LANES), at, pm_kb.reshape(n_p, N_MEM, D_MODEL),
                     pm_vb.reshape(n_p, N_MEM, D_MODEL), merge_w, out_w, tm).reshape(n_p, s_p, D_MODEL)
    p_sre, p_sim = _lane_block_states(st_p, n_p)
    p_wk = k.reshape(n_p, s_p, KV_WIDTH)[:, -WINDOW:].reshape(n_p, WINDOW, N_KV_HEADS, HEAD_DIM)
    p_wv = jnp.stack([vt[:, (b + 1) * s_p - WINDOW:(b + 1) * s_p] for b in range(n_p)])
    p_wv = p_wv.reshape(n_p, N_KV_HEADS, HEAD_DIM, WINDOW).transpose(0, 3, 1, 2)

    pos_s = jnp.tile(PAST_LEN + jnp.arange(t_s, dtype=jnp.int32), n_s)
    h1s, us, qs, ks, vs = tokenwise_in(x_sample.reshape(n_s * t_s, D_MODEL), _rope_tables(pos_s)[0])
    ys, s_sre, s_sim = _ssm_sample(us, state_ssm_re.reshape(n_s, -1).astype(F32), state_ssm_im.reshape(n_s, -1).astype(F32),
                                   ssm_m, ssm_w, ssm_v, lam_s, d_row, t_s)
    win_len = cache_swa_k.shape[1]
    ats, s_wk, s_wv = _swa_sample(qs.reshape(n_s, t_s, Q_WIDTH), ks.reshape(n_s, t_s, KV_WIDTH), vs.reshape(n_s, t_s, KV_WIDTH),
                                  cache_swa_k.transpose(0, 2, 3, 1).reshape(n_s, KV_WIDTH, win_len),
                                  cache_swa_v.transpose(0, 2, 3, 1).reshape(n_s, KV_WIDTH, win_len),
                                  sinks, row(attn_out_g), 16)
    s_wk = s_wk.reshape(n_s, N_KV_HEADS, HEAD_DIM, win_len).transpose(0, 3, 1, 2)
    s_wv = s_wv.reshape(n_s, N_KV_HEADS, HEAD_DIM, win_len).transpose(0, 3, 1, 2)
    h2s, qms = _merge(h1s, ys, ats.reshape(n_s * t_s, Q_WIDTH), merge_w, tm)
    oms = _mem_attn_sample(qms, cache_mem_k, cache_mem_v, n_s, t_s, 8)
    y_sample = _ffn_out(h2s, oms, out_w, tm).reshape(n_s, t_s, D_MODEL)

    return (y_prompt, y_sample, p_sre, p_sim, p_wk, p_wv,
            from_stored(pm_k), from_stored(pm_v),
            s_sre.reshape(n_s, N_SSM_GROUPS, SSM_STATE), s_sim.reshape(n_s, N_SSM_GROUPS, SSM_STATE),
            s_wk, s_wv)
```

```python
import functools
import math

import jax
import jax.numpy as jnp
from jax import lax
from jax.experimental import pallas as pl
from jax.experimental.pallas import tpu as pltpu

F32 = jnp.float32
BF16 = jnp.bfloat16

D_MODEL = 1024
PAST_LEN = 16384
SSM_WIDTH = 512
SSM_GROUP = 16
N_SSM_GROUPS = 32
SSM_STATE = 64
HEAD_DIM = 64
N_HEADS = 8
N_KV_HEADS = 2
GQA_GROUP = 4
Q_WIDTH = 512
KV_WIDTH = 128
WINDOW = 128
ROPE_THETA = 500000.0
ROPE_DIM = 16
N_MEM = 256
MEM_HEADS = 4
MEM_HEAD_DIM = 256
D_FF = 2816
RMS_EPS = 1e-6
IN_WIDTH = SSM_WIDTH + Q_WIDTH + 2 * KV_WIDTH
NEG_INF = -1e30

LANES = 128
FF_CHUNK = 256
ROW_BLOCKS = 2
N_FF_CHUNKS = D_FF // FF_CHUNK
GROUPS_PER_LANE_BLOCK = LANES // SSM_GROUP
N_LANE_BLOCKS = SSM_WIDTH // LANES
STATE_LANES = GROUPS_PER_LANE_BLOCK * SSM_STATE
VMEM_LIMIT = 56 * 1024 * 1024


def _rms(x, g):
    return x * lax.rsqrt(jnp.mean(x * x, axis=-1, keepdims=True) + RMS_EPS) * g


def _const_spec(shape):
    nd = len(shape)
    return pl.BlockSpec(shape, lambda *_: (0,) * nd, pipeline_mode=pl.Buffered(1))


def _params(sem):
    return pltpu.CompilerParams(dimension_semantics=sem, vmem_limit_bytes=VMEM_LIMIT)


def _run(gen):
    try:
        while True:
            next(gen)
    except StopIteration as done:
        return done.value


def _interleave(gens):
    live = list(gens)
    while live:
        for g in list(live):
            try:
                next(g)
            except StopIteration:
                live.remove(g)


def _ffn_stages(x, pre_g, wg_ref, wu_ref, wd_ref, post_g, act_ref, rows=slice(None), after_chunk=None):
    xn = _rms(x, pre_g).astype(BF16)
    yield
    for c in range(N_FF_CHUNKS):
        if after_chunk is not None and c == after_chunk[0] + 1:
            after_chunk[1]()
        cols = slice(c * FF_CHUNK, (c + 1) * FF_CHUNK)
        gate = jnp.dot(xn, wg_ref[:, cols], preferred_element_type=F32)
        up = jnp.dot(xn, wu_ref[:, cols], preferred_element_type=F32)
        act = gate * (1.0 / (1.0 + jnp.exp(-gate))) * up
        act_ref[rows, cols] = act.astype(BF16)
        yield
    down = jnp.dot(act_ref[rows, :], wd_ref[...], preferred_element_type=F32)
    yield
    return x + 0.5 * _rms(down, post_g)


def _ffn_tile(*args, **kwargs):
    return _run(_ffn_stages(*args, **kwargs))


def _rope(x, cos, sin_lo, sin_hi):
    w = x.shape[1]
    half = ROPE_DIM // 2
    return (x * cos + pltpu.roll(x, w - half, 1) * sin_lo + pltpu.roll(x, half, 1) * sin_hi)


def _ffn_in_kernel(x_ref, pre_g_ref, wg_ref, wu_ref, wd_ref, post_g_ref, mix_g_ref, win_ref,
                   cos_ref, slo_ref, shi_ref,
                   h_ref, u_ref, q_ref, k_ref, v_ref, act_ref):
    h = _ffn_tile(x_ref[...], pre_g_ref[...], wg_ref, wu_ref, wd_ref, post_g_ref[...], act_ref)
    h_ref[...] = h
    z = jnp.dot(_rms(h, mix_g_ref[...]).astype(BF16), win_ref[...], preferred_element_type=F32)
    u_ref[...] = z[:, :SSM_WIDTH]
    o1 = SSM_WIDTH + Q_WIDTH
    cos, slo, shi = cos_ref[...], slo_ref[...], shi_ref[...]
    rep = Q_WIDTH // LANES
    q = _rope(z[:, SSM_WIDTH:o1], jnp.tile(cos, (1, rep)), jnp.tile(slo, (1, rep)), jnp.tile(shi, (1, rep)))
    q_ref[...] = (q * (HEAD_DIM ** -0.5)).astype(BF16)
    k_ref[...] = _rope(z[:, o1:o1 + KV_WIDTH], cos, slo, shi)
    v_ref[...] = z[:, o1 + KV_WIDTH:]


def _ffn_in(x, pre_g, ffn_w, post_g, mix_g, win, cos, slo, shi, tm):
    t = x.shape[0]
    n_pos_tiles = cos.shape[0] // tm
    row = lambda w: pl.BlockSpec((tm, w), lambda i: (i, 0))
    tab = pl.BlockSpec((tm, LANES), lambda i: (i % n_pos_tiles, 0))
    return pl.pallas_call(
        _ffn_in_kernel,
        grid=(t // tm,),
        in_specs=[row(D_MODEL), _const_spec((1, D_MODEL)), *[_const_spec(w.shape) for w in ffn_w],
                  _const_spec((1, D_MODEL)), _const_spec((1, D_MODEL)), _const_spec(win.shape),
                  tab, tab, tab],
        out_specs=[row(D_MODEL), row(SSM_WIDTH), row(Q_WIDTH), row(KV_WIDTH), row(KV_WIDTH)],
        out_shape=[jax.ShapeDtypeStruct((t, D_MODEL), F32), jax.ShapeDtypeStruct((t, SSM_WIDTH), F32),
                   jax.ShapeDtypeStruct((t, Q_WIDTH), BF16), jax.ShapeDtypeStruct((t, KV_WIDTH), F32),
                   jax.ShapeDtypeStruct((t, KV_WIDTH), F32)],
        scratch_shapes=[pltpu.VMEM((tm, D_FF), BF16)],
        compiler_params=_params(("arbitrary",)),
        name="ffn_in",
    )(x, pre_g, *ffn_w, post_g, mix_g, win, cos, slo, shi)


def _complex_step(s_re, s_im, l_re, l_im, x_re, x_im):
    return l_re * s_re - l_im * s_im + x_re, l_re * s_im + l_im * s_re + x_im


def _ssm_prompt_kernel(u_ref, m_ref, w_ref, v_ref, lam_ref, d_ref, y_ref, st_ref,
                       x_scr, ss_scr, s_scr, *, lc, nb, ncl, nlb):
    tt = pl.program_id(1)
    nrow = nlb * nb
    nq = STATE_LANES // LANES

    @pl.when(tt == 0)
    def _():
        s_scr[...] = jnp.zeros_like(s_scr)

    def piece(b, n, j):
        return u_ref[b, n, pl.ds(j, ncl, stride=lc), :]

    a = []
    for b in range(nlb):
        a.append(jnp.concatenate(
            [jnp.concatenate([piece(b, n, j) for j in range(lc)], axis=1) for n in range(nb)], axis=0).astype(BF16))
        x = jnp.dot(a[b], w_ref[b], preferred_element_type=F32)
        for qq in range(2 * nq):
            for n in range(nb):
                x_scr[qq, pl.ds(b * nb + n, ncl, stride=nrow), :] = x[n * ncl:(n + 1) * ncl, qq * LANES:(qq + 1) * LANES]
    mt = 2 * LANES
    y_intra = [jnp.concatenate(
        [jnp.dot(a[b][:, :(jt + 1) * mt], m_ref[b, :(jt + 1) * mt, jt * mt:(jt + 1) * mt], preferred_element_type=F32)
         for jt in range(lc * LANES // mt)], axis=1) for b in range(nlb)]
    lam_rows = lambda qq: jnp.concatenate(
        [jnp.broadcast_to(lam_ref[b, :, qq * LANES:(qq + 1) * LANES], (nb, LANES)) for b in range(nlb)], axis=0)
    l_re = [lam_rows(qq) for qq in range(nq)]
    l_im = [lam_rows(nq + qq) for qq in range(nq)]

    s = [s_scr[qq] for qq in range(2 * nq)]
    for c in range(ncl):
        rows = slice(c * nrow, (c + 1) * nrow)
        for qq in range(nq):
            ss_scr[qq, rows, :] = s[qq]
            ss_scr[nq + qq, rows, :] = s[nq + qq]
            s[qq], s[nq + qq] = _complex_step(s[qq], s[nq + qq], l_re[qq], l_im[qq],
                                              x_scr[qq, rows, :], x_scr[nq + qq, rows, :])
    for qq in range(2 * nq):
        s_scr[qq] = s[qq]
    for b in range(nlb):
        st_ref[b] = jnp.concatenate([s[qq][b * nb:(b + 1) * nb] for qq in range(2 * nq)], axis=1)
        s_start = jnp.concatenate(
            [jnp.concatenate([ss_scr[qq, pl.ds(b * nb + n, ncl, stride=nrow), :] for n in range(nb)], axis=0)
             for qq in range(2 * nq)], axis=1).astype(BF16)
        y = y_intra[b] + jnp.dot(s_start, v_ref[b], preferred_element_type=F32)
        d = d_ref[b]
        for n in range(nb):
            for j in range(lc):
                y_ref[b, n, pl.ds(j, ncl, stride=lc), :] = (
                    y[n * ncl:(n + 1) * ncl, j * LANES:(j + 1) * LANES] + d * piece(b, n, j))


def _ssm_prompt(u4, m, w, v, lam, d, lc, tl, nlb):
    _, nb, seq, _ = u4.shape
    ncl = tl // lc
    nslab = 2 * STATE_LANES // LANES
    kern = functools.partial(_ssm_prompt_kernel, lc=lc, nb=nb, ncl=ncl, nlb=nlb)
    wspec = lambda a: pl.BlockSpec((nlb,) + a.shape[1:], lambda p, t: (p, 0, 0), pipeline_mode=pl.Buffered(1))
    io = pl.BlockSpec((nlb, nb, tl, LANES), lambda p, t: (p, 0, t, 0))
    return pl.pallas_call(
        kern,
        grid=(N_LANE_BLOCKS // nlb, seq // tl),
        in_specs=[io, wspec(m), wspec(w), wspec(v), wspec(lam), wspec(d)],
        out_specs=[io, pl.BlockSpec((nlb, nb, 2 * STATE_LANES), lambda p, t: (p, 0, 0))],
        out_shape=[jax.ShapeDtypeStruct(u4.shape, F32),
                   jax.ShapeDtypeStruct((N_LANE_BLOCKS, nb, 2 * STATE_LANES), F32)],
        scratch_shapes=[pltpu.VMEM((nslab, nlb * nb * ncl, LANES), F32),
                        pltpu.VMEM((nslab, nlb * nb * ncl, LANES), F32),
                        pltpu.VMEM((nslab, nlb * nb, LANES), F32)],
        compiler_params=_params(("arbitrary", "arbitrary")),
        name="ssm_prompt",
    )(u4, m, w, v, lam, d)


def _ssm_sample_kernel(u_ref, sre_ref, sim_ref, m_ref, w_ref, v_ref, lam_ref, d_ref,
                       y_ref, ore_ref, oim_ref, *, lc, ns):
    def piece(j):
        return u_ref[pl.ds(j, ns, stride=lc), :]

    a = jnp.concatenate([piece(j) for j in range(lc)], axis=1).astype(BF16)
    s_re, s_im = sre_ref[...], sim_ref[...]
    x = jnp.dot(a, w_ref[0], preferred_element_type=F32)
    e_re, e_im = _complex_step(s_re, s_im, lam_ref[0, :, :STATE_LANES], lam_ref[0, :, STATE_LANES:],
                               x[:, :STATE_LANES], x[:, STATE_LANES:])
    ore_ref[...] = e_re
    oim_ref[...] = e_im
    s0 = jnp.concatenate([s_re, s_im], axis=1).astype(BF16)
    y = (jnp.dot(a, m_ref[0], preferred_element_type=F32) + jnp.dot(s0, v_ref[0], preferred_element_type=F32))
    d = d_ref[...]
    for j in range(lc):
        y_ref[pl.ds(j, ns, stride=lc), :] = y[:, j * LANES:(j + 1) * LANES] + d * piece(j)


def _ssm_sample(u, s_re, s_im, m, w, v, lam, d, lc):
    t = u.shape[0]
    ns = t // lc
    kern = functools.partial(_ssm_sample_kernel, lc=lc, ns=ns)
    assert m.shape[1] == 2 * lc * LANES
    n = lc * LANES
    col = lambda rows, width: pl.BlockSpec((rows, width), lambda b: (0, b))
    return pl.pallas_call(
        kern,
        grid=(N_LANE_BLOCKS,),
        in_specs=[col(t, LANES), col(ns, STATE_LANES), col(ns, STATE_LANES),
                  pl.BlockSpec((1, n, n), lambda b: (b, 0, 0)),
                  pl.BlockSpec((1, n, 2 * STATE_LANES), lambda b: (b, 1, 0)),
                  pl.BlockSpec((1, 2 * STATE_LANES, n), lambda b: (b, 0, 0)),
                  pl.BlockSpec((1, 1, 2 * STATE_LANES), lambda b: (b, 0, 0)), col(1, LANES)],
        out_specs=[col(t, LANES), col(ns, STATE_LANES), col(ns, STATE_LANES)],
        out_shape=[jax.ShapeDtypeStruct(u.shape, F32), jax.ShapeDtypeStruct(s_re.shape, F32),
                   jax.ShapeDtypeStruct(s_im.shape, F32)],
        compiler_params=_params(("arbitrary",)),
        name="ssm_sample",
    )(u, s_re, s_im, m, w, v, lam, d)


def _ssm_discretise(a_re, a_im, log_step):
    dt = jnp.exp(log_step)
    mag = jnp.exp(a_re * dt)
    l_re, l_im = mag * jnp.cos(a_im * dt), mag * jnp.sin(a_im * dt)
    den = a_re * a_re + a_im * a_im
    n_re, n_im = l_re - 1.0, l_im
    return l_re, l_im, (n_re * a_re + n_im * a_im) / den, (n_im * a_re - n_re * a_im) / den


def _complex_powers(l_re, l_im, n):
    p_re, p_im = [jnp.ones_like(l_re)], [jnp.zeros_like(l_re)]
    for _ in range(n):
        p_re, p_im = p_re + [p_re[-1] * l_re - p_im[-1] * l_im], p_im + [p_re[-1] * l_im + p_im[-1] * l_re]
    return p_re, p_im


def _split_bf16(x):
    hi = x.astype(BF16)
    return hi, (x - hi.astype(F32)).astype(BF16)


def _dot_split(a, b):
    dot = lambda x, y: jnp.dot(x, y, preferred_element_type=F32)
    return dot(a[0], b[0]) + (dot(a[0], b[1]) + dot(a[1], b[0]))


def _ssm_tables_kernel(ac_re_ref, ac_im_ref, lsc_ref, ar_re_ref, ar_im_ref, lsr_ref, b_re_ref, b_im_ref,
                       c_re_ref, c_im_ref, m_ref, w_ref, v_ref, lam_ref, lam_half_ref, *, lc):
    l_re, l_im, cf_re, cf_im = _ssm_discretise(ac_re_ref[0], ac_im_ref[0], lsc_ref[0])
    b_re, b_im, c_re, c_im = b_re_ref[0], b_im_ref[0], c_re_ref[0], c_im_ref[0]
    c_re_parts, c_im_parts = _split_bf16(c_re), _split_bf16(c_im)
    bb_re = cf_re * b_re - cf_im * b_im
    bb_im = cf_re * b_im + cf_im * b_re
    p_re, p_im = _complex_powers(l_re, l_im, lc)
    lag = []
    for k in range(lc):
        et_re = (p_re[k] * bb_re - p_im[k] * bb_im).T
        et_im = (p_re[k] * bb_im + p_im[k] * bb_re).T
        j = lc - 1 - k
        w_ref[0, j * LANES:(j + 1) * LANES, :STATE_LANES] = et_re.astype(BF16)
        w_ref[0, j * LANES:(j + 1) * LANES, STATE_LANES:] = et_im.astype(BF16)
        lag.append((_dot_split(_split_bf16(et_re), c_re_parts) - _dot_split(_split_bf16(et_im), c_im_parts)).astype(BF16))
        v_ref[0, :STATE_LANES, k * LANES:(k + 1) * LANES] = (p_re[k + 1] * c_re - p_im[k + 1] * c_im).astype(BF16)
        v_ref[0, STATE_LANES:, k * LANES:(k + 1) * LANES] = (-(p_im[k + 1] * c_re + p_re[k + 1] * c_im)).astype(BF16)
    zero = jnp.zeros((LANES, LANES), BF16)
    for j in range(lc):
        for jj in range(lc):
            m_ref[0, j * LANES:(j + 1) * LANES, jj * LANES:(jj + 1) * LANES] = lag[jj - j] if jj >= j else zero
    r_re, r_im, _, _ = _ssm_discretise(ar_re_ref[0], ar_im_ref[0], lsr_ref[0])
    q_re, q_im = _complex_powers(r_re, r_im, lc)
    lam_ref[0] = jnp.concatenate([q_re[lc], q_im[lc]], axis=1)
    lam_half_ref[0] = jnp.concatenate([q_re[lc // 2], q_im[lc // 2]], axis=1)


def _ssm_tables(a_re, a_im, log_step, b_re, b_im, c_re, c_im, lc):
    g, p, h = b_re.shape
    nbk, r = N_LANE_BLOCKS, GROUPS_PER_LANE_BLOCK
    ls = jnp.broadcast_to(log_step[:, None], (g, p))
    cols = [jnp.broadcast_to(x.reshape(nbk, STATE_LANES, 1), (nbk, STATE_LANES, LANES)) for x in (a_re, a_im, ls)]
    rows = [x.reshape(nbk, 1, STATE_LANES) for x in (a_re, a_im, ls)]
    eye = jnp.eye(r, dtype=F32)[None, :, None, :, None]

    def block_diag(x):
        return (x[:, :, :, None, :] * eye).reshape(nbk, STATE_LANES, LANES)

    mats = [block_diag(b_re.reshape(nbk, r, p, h)), block_diag(b_im.reshape(nbk, r, p, h)),
            block_diag(c_re.reshape(nbk, r, h, p).transpose(0, 1, 3, 2)),
            block_diag(c_im.reshape(nbk, r, h, p).transpose(0, 1, 3, 2))]
    spec = lambda shape: pl.BlockSpec((1,) + shape, lambda b: (b, 0, 0))
    n = lc * LANES
    return pl.pallas_call(
        functools.partial(_ssm_tables_kernel, lc=lc),
        grid=(nbk,),
        in_specs=[spec((STATE_LANES, LANES))] * 3 + [spec((1, STATE_LANES))] * 3 + [spec((STATE_LANES, LANES))] * 4,
        out_specs=[spec((n, n)), spec((n, 2 * STATE_LANES)), spec((2 * STATE_LANES, n)),
                   spec((1, 2 * STATE_LANES)), spec((1, 2 * STATE_LANES))],
        out_shape=[jax.ShapeDtypeStruct((nbk, n, n), BF16), jax.ShapeDtypeStruct((nbk, n, 2 * STATE_LANES), BF16),
                   jax.ShapeDtypeStruct((nbk, 2 * STATE_LANES, n), BF16),
                   jax.ShapeDtypeStruct((nbk, 1, 2 * STATE_LANES), F32),
                   jax.ShapeDtypeStruct((nbk, 1, 2 * STATE_LANES), F32)],
        compiler_params=_params(("arbitrary",)),
        name="ssm_tables",
    )(*cols, *rows, *mats)


def _nt_dot(a, b):
    return lax.dot_general(a, b, (((1,), (1,)), ((), ())), preferred_element_type=F32)


def _swa_prompt_tile(sink_ref, qt_ref, kcat, vcat_t, mstd_ref, mfirst_ref, g_ref, ot_ref, first_tile, nblk):
    keys = lambda j: slice(j * WINDOW, (j + 2) * WINDOW)
    zeros = jnp.zeros((HEAD_DIM, WINDOW), BF16)
    sinks = [jnp.concatenate([jnp.full((1, WINDOW), sink_ref[kh * GQA_GROUP + g], F32) for g in range(GQA_GROUP)],
                             axis=1) for kh in range(N_KV_HEADS)]
    chains = [(j, kh) for j in range(nblk) for kh in range(N_KV_HEADS)]

    def scores(j, kh):
        def rhs(g):
            h = kh * GQA_GROUP + g
            q = qt_ref[h * HEAD_DIM:(h + 1) * HEAD_DIM, j * WINDOW:(j + 1) * WINDOW]
            return jnp.concatenate([q, zeros] if kh == 0 else [zeros, q], axis=0)
        r = jnp.concatenate([rhs(g) for g in range(GQA_GROUP)], axis=1)
        vmask = mstd_ref[...]
        if j == 0:
            vmask = jnp.where(first_tile, mfirst_ref[...], vmask)
        valid = jnp.tile(vmask, (1, GQA_GROUP)) > 0.0
        return jnp.where(valid, jnp.dot(kcat[keys(j)], r, preferred_element_type=F32), NEG_INF)

    def softmax(kh, sc):
        m = jnp.maximum(jnp.max(sc, axis=0, keepdims=True), sinks[kh])
        e = jnp.exp(sc - m)
        return e.astype(BF16), 1.0 / (jnp.sum(e, axis=0, keepdims=True) + jnp.exp(sinks[kh] - m))

    sc = [scores(*c) for c in chains]
    pr = [softmax(kh, x) for (j, kh), x in zip(chains, sc)]

    def finish():
        out = [jnp.dot(vcat_t[kh * HEAD_DIM:(kh + 1) * HEAD_DIM, keys(j)], e, preferred_element_type=F32) * inv
               for (j, kh), (e, inv) in zip(chains, pr)]
        gain = g_ref[...]
        for j in range(nblk):
            o = jnp.concatenate([out[j * N_KV_HEADS + kh][:, g * WINDOW:(g + 1) * WINDOW]
                                 for kh in range(N_KV_HEADS) for g in range(GQA_GROUP)], axis=0)
            scale = lax.rsqrt(jnp.mean(o * o, axis=0, keepdims=True) + RMS_EPS)
            ot_ref[:, j * WINDOW:(j + 1) * WINDOW] = (o * scale * gain).astype(BF16)

    return finish


def _swa_masks():
    kj = jnp.arange(2 * WINDOW)[:, None]
    diff = jnp.arange(WINDOW)[None, :] + WINDOW - kj
    std = (diff >= 0) & (diff <= WINDOW)
    return std.astype(F32), (std & (kj >= WINDOW)).astype(F32)


def _ffn_in_swa_kernel(sink_ref, x_ref, pre_g_ref, wg_ref, wu_ref, wd_ref, post_g_ref, mix_g_ref, win_u_ref, wqkv_t_ref,
                       cos_t_ref, sin_t_ref, mstd_ref, mfirst_ref, swa_g_ref,
                       h_ref, u_ref, k_win_ref, vt_win_ref, ot_ref, act_ref, q_scr, kcat_scr, vcat_scr, *, nt, nblk):
    i = pl.program_id(0)
    tm = x_ref.shape[0]

    @pl.when(i == 0)
    def _():
        q_scr[...] = jnp.zeros_like(q_scr)
        kcat_scr[...] = jnp.zeros_like(kcat_scr)
        vcat_scr[...] = jnp.zeros_like(vcat_scr)

    first_tile = lax.rem(i + (nt - 1), nt) == 0
    swa_finish = _swa_prompt_tile(sink_ref, q_scr, kcat_scr[...], vcat_scr[...], mstd_ref, mfirst_ref, swa_g_ref,
                                  ot_ref, first_tile, nblk)

    kcat_scr[:WINDOW, :] = kcat_scr[tm:, :]
    vcat_scr[:, :WINDOW] = vcat_scr[:, tm:]
    half = ROPE_DIM // 2

    def row_block(r, swa_hook):
        n = r.stop - r.start
        h = yield from _ffn_stages(x_ref[r, :], pre_g_ref[...], wg_ref, wu_ref, wd_ref, post_g_ref[...], act_ref, r,
                                   after_chunk=swa_hook)
        h_ref[r, :] = h
        hn = _rms(h, mix_g_ref[...]).astype(BF16)
        z = jnp.dot(hn, win_u_ref[...], preferred_element_type=F32)
        yield
        for b in range(N_LANE_BLOCKS):
            u_ref[b, r, :] = z[:, b * LANES:(b + 1) * LANES]
        zt = _nt_dot(wqkv_t_ref[...], hn)
        yield
        vt = zt[Q_WIDTH + KV_WIDTH:]
        if r.stop == tm:
            vt_win_ref[...] = vt[:, -WINDOW:]
        qk = zt[:Q_WIDTH + KV_WIDTH].reshape(N_HEADS + N_KV_HEADS, HEAD_DIM, n)
        x1, x2 = qk[:, :half], qk[:, half:ROPE_DIM]
        cos, sin = cos_t_ref[:, r][None], sin_t_ref[:, r][None]
        qk = jnp.concatenate([x1 * cos - x2 * sin, x2 * cos + x1 * sin, qk[:, ROPE_DIM:]], axis=1)
        k = qk[N_HEADS:].reshape(KV_WIDTH, n).T
        if r.stop == tm:
            k_win_ref[...] = k[-WINDOW:]
        shifted = slice(WINDOW + r.start, WINDOW + r.stop)
        kcat_scr[shifted, :] = k.astype(BF16)
        vcat_scr[:, shifted] = vt.astype(BF16)
        q_scr[:, r] = (qk[:N_HEADS].reshape(Q_WIDTH, n) * (HEAD_DIM ** -0.5)).astype(BF16)

    blocks = [slice(b * tm // ROW_BLOCKS, (b + 1) * tm // ROW_BLOCKS) for b in range(ROW_BLOCKS)]
    _interleave([row_block(r, (N_FF_CHUNKS - 3, swa_finish) if r.start == 0 else None) for r in blocks])


def _ffn_in_swa(x, pre_g, ffn_w, post_g, mix_g, win_u, wqkv_t, cos_t, sin_t, sinks, swa_g, seq, tm):
    t = x.shape[0]
    n_tiles, nt, nblk, n_seq = t // tm, seq // tm, tm // WINDOW, t // seq
    half = ROPE_DIM // 2
    mstd, mfirst = _swa_masks()
    gain = jnp.broadcast_to(swa_g.reshape(Q_WIDTH, 1), (Q_WIDTH, WINDOW))
    cur = lambda i: jnp.minimum(i, n_tiles - 1)
    row = lambda w: pl.BlockSpec((tm, w), lambda i: (cur(i), 0))
    tab_t = pl.BlockSpec((half, tm), lambda i: (0, cur(i) % nt))
    return pl.pallas_call(
        functools.partial(_ffn_in_swa_kernel, nt=nt, nblk=nblk),
        grid=(n_tiles + 1,),
        in_specs=[pl.BlockSpec(memory_space=pltpu.SMEM), row(D_MODEL), _const_spec((1, D_MODEL)),
                  *[_const_spec(w.shape) for w in ffn_w], _const_spec((1, D_MODEL)), _const_spec((1, D_MODEL)),
                  _const_spec(win_u.shape), _const_spec(wqkv_t.shape), tab_t, tab_t,
                  _const_spec(mstd.shape), _const_spec(mfirst.shape), _const_spec(gain.shape)],
        out_specs=[row(D_MODEL), pl.BlockSpec((N_LANE_BLOCKS, tm, LANES), lambda i: (0, cur(i), 0)),
                   pl.BlockSpec((WINDOW, KV_WIDTH), lambda i: (cur(i) // nt, 0)),
                   pl.BlockSpec((KV_WIDTH, WINDOW), lambda i: (0, cur(i) // nt)),
                   pl.BlockSpec((Q_WIDTH, tm), lambda i: (0, jnp.maximum(i - 1, 0)))],
        out_shape=[jax.ShapeDtypeStruct((t, D_MODEL), F32), jax.ShapeDtypeStruct((N_LANE_BLOCKS, t, LANES), F32),
                   jax.ShapeDtypeStruct((n_seq * WINDOW, KV_WIDTH), F32),
                   jax.ShapeDtypeStruct((KV_WIDTH, n_seq * WINDOW), F32), jax.ShapeDtypeStruct((Q_WIDTH, t), BF16)],
        scratch_shapes=[pltpu.VMEM((tm, D_FF), BF16), pltpu.VMEM((Q_WIDTH, tm), BF16),
                        pltpu.VMEM((WINDOW + tm, KV_WIDTH), BF16), pltpu.VMEM((KV_WIDTH, WINDOW + tm), BF16)],
        compiler_params=_params(("arbitrary",)),
        name="ffn_in_swa",
    )(sinks, x, pre_g, *ffn_w, post_g, mix_g, win_u, wqkv_t, cos_t, sin_t, mstd, mfirst, gain)


def _swa_sample_kernel(sink_ref, q_ref, kn_ref, vn_ref, ck_ref, cv_ref, g_ref, o_ref, nk_ref, nv_ref, *, ns, t):
    rows = GQA_GROUP * t
    tok = lax.broadcasted_iota(jnp.int32, (rows, WINDOW), 0) % t
    valid_c = lax.broadcasted_iota(jnp.int32, (rows, WINDOW), 1) >= tok
    tok_n = lax.broadcasted_iota(jnp.int32, (rows, t), 0) % t
    valid_n = lax.broadcasted_iota(jnp.int32, (rows, t), 1) <= tok_n
    gain = g_ref[...]

    sinks = [jnp.concatenate([jnp.full((t, 1), sink_ref[kh * GQA_GROUP + g], F32) for g in range(GQA_GROUP)], axis=0)
             for kh in range(N_KV_HEADS)]
    heads = [(s, kh) for s in range(ns) for kh in range(N_KV_HEADS)]
    hs = lambda kh: slice(kh * HEAD_DIM, (kh + 1) * HEAD_DIM)

    newest = lax.broadcasted_iota(jnp.int32, (KV_WIDTH, WINDOW), 1) >= WINDOW - t
    pad = jnp.zeros((WINDOW - 2 * t, KV_WIDTH), F32)

    def shifted(cache_ref, new_ref, s):
        new_rows = jnp.concatenate([pad, jnp.zeros((t, KV_WIDTH), F32), new_ref[s]], axis=0)
        return jnp.where(newest, new_rows.T, pltpu.roll(cache_ref[s], WINDOW - t, 1))

    for s in range(ns):
        nk_ref[s] = shifted(ck_ref, kn_ref, s)
        nv_ref[s] = shifted(cv_ref, vn_ref, s)

    def scores(s, kh):
        q = q_ref[s].astype(F32)
        q4 = jnp.concatenate([q[:, (kh * GQA_GROUP + g) * HEAD_DIM:(kh * GQA_GROUP + g + 1) * HEAD_DIM]
                              for g in range(GQA_GROUP)], axis=0).astype(BF16)
        sc_c = jnp.where(valid_c, jnp.dot(q4, ck_ref[s, hs(kh), :].astype(BF16), preferred_element_type=F32), NEG_INF)
        sc_n = jnp.where(valid_n, _nt_dot(q4, kn_ref[s, :, hs(kh)].astype(BF16)), NEG_INF)
        return sc_c, sc_n

    def softmax(kh, sc_c, sc_n):
        m = jnp.maximum(jnp.maximum(jnp.max(sc_c, axis=-1, keepdims=True),
                                    jnp.max(sc_n, axis=-1, keepdims=True)), sinks[kh])
        e_c, e_n = jnp.exp(sc_c - m), jnp.exp(sc_n - m)
        inv = 1.0 / (jnp.sum(e_c, axis=-1, keepdims=True) + jnp.sum(e_n, axis=-1, keepdims=True)
                     + jnp.exp(sinks[kh] - m))
        return (e_c * inv).astype(BF16), (e_n * inv).astype(BF16)

    def values(s, kh, p_c, p_n):
        return (_nt_dot(p_c, cv_ref[s, hs(kh), :].astype(BF16))
                + jnp.dot(p_n, vn_ref[s, :, hs(kh)].astype(BF16), preferred_element_type=F32))

    sc = [scores(s, kh) for s, kh in heads]
    pr = [softmax(kh, *x) for (s, kh), x in zip(heads, sc)]
    o4 = [values(s, kh, *x) for (s, kh), x in zip(heads, pr)]
    for s in range(ns):
        o = jnp.concatenate([o4[s * N_KV_HEADS + kh][g * t:(g + 1) * t]
                             for kh in range(N_KV_HEADS) for g in range(GQA_GROUP)], axis=1)
        o_ref[s] = _rms(o, gain).astype(BF16)


def _swa_sample(q3, kn3, vn3, ck, cv, sinks, out_g, sb):
    ns, t, _ = q3.shape
    blk = lambda a: pl.BlockSpec((sb,) + a.shape[1:], lambda i: (i, 0, 0))
    kern = functools.partial(_swa_sample_kernel, ns=sb, t=t)
    return pl.pallas_call(
        kern,
        grid=(ns // sb,),
        in_specs=[pl.BlockSpec(memory_space=pltpu.SMEM), blk(q3), blk(kn3), blk(vn3), blk(ck), blk(cv),
                  pl.BlockSpec((1, Q_WIDTH), lambda i: (0, 0))],
        out_specs=[blk(q3), blk(ck), blk(cv)],
        out_shape=[jax.ShapeDtypeStruct(q3.shape, BF16), jax.ShapeDtypeStruct(ck.shape, F32),
                   jax.ShapeDtypeStruct(cv.shape, F32)],
        compiler_params=_params(("arbitrary",)),
        name="swa_sample",
    )(sinks, q3, kn3, vn3, ck, cv, out_g)


def _gelu_tanh(x):
    return 0.5 * x * (1.0 + jnp.tanh(math.sqrt(2.0 / math.pi) * (x + 0.044715 * (x * x * x))))


def _merge_stages(h, y, at, wglu_ref, bglu, sg, wout_ref, post_g, xa_g, wq_ref, at_transposed=False):
    g = _gelu_tanh(y)
    lin = jnp.dot(g.astype(BF16), wglu_ref[...], preferred_element_type=F32) + bglu
    yield
    y_ssm = g * (1.0 / (1.0 + jnp.exp(-lin)))
    ssm_n = _rms(y_ssm, sg).astype(BF16)
    at_dims = (((0,), (0,)), ((), ())) if at_transposed else (((1,), (0,)), ((), ()))
    mixed = (jnp.dot(ssm_n, wout_ref[:SSM_WIDTH, :], preferred_element_type=F32)
             + lax.dot_general(at, wout_ref[SSM_WIDTH:, :], at_dims, preferred_element_type=F32))
    yield
    h2 = h + _rms(mixed, post_g)
    qm = jnp.dot(_rms(h2, xa_g).astype(BF16), wq_ref[...], preferred_element_type=F32)
    yield
    return h2, (qm * (MEM_HEAD_DIM ** -0.5)).astype(BF16)


def _merge_tile(*args, **kwargs):
    return _run(_merge_stages(*args, **kwargs))


def _merge_kernel(h_ref, y_ref, at_ref, wglu_ref, bglu_ref, sg_ref, wout_ref, post_g_ref, xa_g_ref, wq_ref,
                  h2_ref, qm_ref):
    h2_ref[...], qm_ref[...] = _merge_tile(h_ref[...], y_ref[...], at_ref[...], wglu_ref, bglu_ref[...], sg_ref[...],
                                           wout_ref, post_g_ref[...], xa_g_ref[...], wq_ref)


def _merge_specs(wglu, wout, wq):
    return [_const_spec(wglu.shape), _const_spec((1, SSM_WIDTH)), _const_spec((1, SSM_WIDTH)), _const_spec(wout.shape),
            _const_spec((1, D_MODEL)), _const_spec((1, D_MODEL)), _const_spec(wq.shape)]


def _merge(h, y, at, merge_w, tm):
    t = h.shape[0]
    row = lambda w: pl.BlockSpec((tm, w), lambda i: (i, 0))
    return pl.pallas_call(
        _merge_kernel,
        grid=(t // tm,),
        in_specs=[row(D_MODEL), row(SSM_WIDTH), row(Q_WIDTH)] + _merge_specs(merge_w[0], merge_w[3], merge_w[6]),
        out_specs=[row(D_MODEL), row(D_MODEL)],
        out_shape=[jax.ShapeDtypeStruct((t, D_MODEL), F32), jax.ShapeDtypeStruct((t, D_MODEL), BF16)],
        compiler_params=_params(("arbitrary",)),
        name="merge",
    )(h, y, at, *merge_w)


def _mem_attn_sample_kernel(q_ref, k_ref, v_ref, o_ref, *, gb, t):
    rows = t * MEM_HEADS
    halves = MEM_HEAD_DIM // LANES
    kv_rows = N_MEM * halves * MEM_HEADS
    period = halves * MEM_HEADS
    lane = lax.broadcasted_iota(jnp.int32, (rows, kv_rows), 1) % period
    head = lax.broadcasted_iota(jnp.int32, (rows, kv_rows), 0) % MEM_HEADS
    in_half = [lane == head + hf * MEM_HEADS for hf in range(halves)]
    for b in range(gb):
        kb = k_ref[b].astype(BF16)
        vb = v_ref[b].astype(BF16)
        part = _nt_dot(q_ref[b], kb)
        sc = jnp.where(in_half[0], part[:rows], 0.0)
        for hf in range(1, halves):
            sc = sc + pltpu.roll(jnp.where(in_half[hf], part[hf * rows:(hf + 1) * rows], 0.0),
                                 kv_rows - hf * MEM_HEADS, 1)
        sc = jnp.where(in_half[0], sc, -jnp.inf)
        e = jnp.exp(sc - jnp.max(sc, axis=-1, keepdims=True))
        inv = 1.0 / jnp.sum(e, axis=-1, keepdims=True)
        e_all = jnp.concatenate([e] + [pltpu.roll(e, hf * MEM_HEADS, 1) for hf in range(1, halves)], axis=0)
        o = jnp.dot(e_all.astype(BF16), vb, preferred_element_type=F32)
        o_ref[b] = (o * jnp.concatenate([inv] * halves, axis=0)).astype(BF16)


def _mem_attn_sample(qm, cache_k, cache_v, n_s, t_s, gb):
    halves = MEM_HEAD_DIM // LANES
    rows = halves * t_s * MEM_HEADS
    kv_rows = N_MEM * halves * MEM_HEADS

    def stored_rows(c):
        c = c.reshape(n_s, N_MEM, MEM_HEADS, halves, LANES).transpose(0, 1, 3, 2, 4)
        return c.reshape(n_s, kv_rows, LANES)

    q = qm.reshape(n_s, t_s, MEM_HEADS, halves, LANES).transpose(0, 3, 1, 2, 4).reshape(n_s, rows, LANES)
    blk = lambda r: pl.BlockSpec((gb, r, LANES), lambda i: (i, 0, 0))
    o = pl.pallas_call(
        functools.partial(_mem_attn_sample_kernel, gb=gb, t=t_s),
        grid=(n_s // gb,),
        in_specs=[blk(rows), blk(kv_rows), blk(kv_rows)],
        out_specs=blk(rows),
        out_shape=jax.ShapeDtypeStruct((n_s, rows, LANES), BF16),
        compiler_params=_params(("arbitrary",)),
        name="mem_attn_sample",
    )(q, stored_rows(cache_k), stored_rows(cache_v))
    o = o.reshape(n_s, halves, t_s, MEM_HEADS, LANES).transpose(0, 2, 3, 1, 4)
    return o.reshape(n_s * t_s, D_MODEL)


def _mem_heads_stages(q, k_ref, v_ref):
    outs = []
    for hh in range(MEM_HEADS):
        hs = slice(hh * MEM_HEAD_DIM, (hh + 1) * MEM_HEAD_DIM)
        sc = _nt_dot(q[:, hs], k_ref[0, :, hs])
        yield
        e = jnp.exp(sc - jnp.max(sc, axis=-1, keepdims=True))
        inv = 1.0 / jnp.sum(e, axis=-1, keepdims=True)
        outs.append((jnp.dot(e.astype(BF16), v_ref[0, :, hs], preferred_element_type=F32) * inv).astype(BF16))
        yield
    return jnp.concatenate(outs, axis=1)


def _ffn_out_stages(h2, om, wo_ref, xa_post, pre_g, wg_ref, wu_ref, wd_ref, post_g, act_ref, rows=slice(None)):
    c = jnp.dot(om, wo_ref[...], preferred_element_type=F32)
    yield
    h3 = h2 + _rms(c, xa_post)
    return (yield from _ffn_stages(h3, pre_g, wg_ref, wu_ref, wd_ref, post_g, act_ref, rows))


def _ffn_out_tile(*args, **kwargs):
    return _run(_ffn_out_stages(*args, **kwargs))


def _ffn_out_specs(wo, ffn_w):
    return [_const_spec(wo.shape), _const_spec((1, D_MODEL)), _const_spec((1, D_MODEL)),
            *[_const_spec(w.shape) for w in ffn_w], _const_spec((1, D_MODEL))]


def _ffn_out_kernel(h_ref, o_ref, wo_ref, xa_post_ref, pre_g_ref, wg_ref, wu_ref, wd_ref, post_g_ref, out_ref, act_ref):
    out_ref[...] = _ffn_out_tile(h_ref[...], o_ref[...], wo_ref, xa_post_ref[...], pre_g_ref[...], wg_ref, wu_ref,
                                 wd_ref, post_g_ref[...], act_ref)


def _ffn_out(h, o, out_w, tm):
    t = h.shape[0]
    row = pl.BlockSpec((tm, D_MODEL), lambda i: (i, 0))
    return pl.pallas_call(
        _ffn_out_kernel,
        grid=(t // tm,),
        in_specs=[row, row] + _ffn_out_specs(out_w[0], out_w[3:6]),
        out_specs=row,
        out_shape=jax.ShapeDtypeStruct((t, D_MODEL), F32),
        scratch_shapes=[pltpu.VMEM((tm, D_FF), BF16)],
        compiler_params=_params(("arbitrary",)),
        name="ffn_out",
    )(h, o, *out_w)


def _post_kernel(h_ref, y_ref, at_ref, k_ref, v_ref,
                 wglu_ref, bglu_ref, sg_ref, wout_ref, post_g_ref, xa_g_ref, wq_ref,
                 wo_ref, xa_post_ref, pre_g_ref, wg_ref, wu_ref, wd_ref, ffn_post_ref, out_ref, act_ref):
    def row_block(r):
        y = jnp.concatenate([y_ref[b, r, :] for b in range(N_LANE_BLOCKS)], axis=1)
        h2, qm = yield from _merge_stages(h_ref[r, :], y, at_ref[:, r], wglu_ref, bglu_ref[...], sg_ref[...],
                                          wout_ref, post_g_ref[...], xa_g_ref[...], wq_ref, at_transposed=True)
        om = yield from _mem_heads_stages(qm, k_ref, v_ref)
        out_ref[r, :] = yield from _ffn_out_stages(h2, om, wo_ref, xa_post_ref[...], pre_g_ref[...], wg_ref, wu_ref,
                                                   wd_ref, ffn_post_ref[...], act_ref, r)

    tm = h_ref.shape[0]
    _interleave([row_block(slice(i * tm // ROW_BLOCKS, (i + 1) * tm // ROW_BLOCKS)) for i in range(ROW_BLOCKS)])


def _post(h, y, at, k3, v3, merge_w, out_w, tm):
    t = h.shape[0]
    tiles_per_batch = t // k3.shape[0] // tm
    row = lambda w: pl.BlockSpec((tm, w), lambda i: (i, 0))
    kv = pl.BlockSpec((1, N_MEM, D_MODEL), lambda i: (i // tiles_per_batch, 0, 0))
    return pl.pallas_call(
        _post_kernel,
        grid=(t // tm,),
        in_specs=([row(D_MODEL), pl.BlockSpec((N_LANE_BLOCKS, tm, LANES), lambda i: (0, i, 0)),
                   pl.BlockSpec((Q_WIDTH, tm), lambda i: (0, i)), kv, kv]
                  + _merge_specs(merge_w[0], merge_w[3], merge_w[6]) + _ffn_out_specs(out_w[0], out_w[3:6])),
        out_specs=row(D_MODEL),
        out_shape=jax.ShapeDtypeStruct((t, D_MODEL), F32),
        scratch_shapes=[pltpu.VMEM((tm, D_FF), BF16)],
        compiler_params=_params(("arbitrary",)),
        name="post",
    )(h, y, at, k3, v3, *merge_w, *out_w)


def _mem_kv_kernel(m_ref, g_ref, wkv_ref, k_ref, v_ref, kb_ref, vb_ref):
    kv = jnp.dot(_rms(m_ref[...], g_ref[...]).astype(BF16), wkv_ref[...], preferred_element_type=F32)
    kb_ref[...] = kv[:, :D_MODEL].astype(BF16)
    vb_ref[...] = kv[:, D_MODEL:].astype(BF16)
    halves = MEM_HEAD_DIM // LANES
    n = m_ref.shape[0]
    for out_ref, base in ((k_ref, 0), (v_ref, D_MODEL)):
        for hh in range(MEM_HEADS):
            for hf in range(halves):
                col = base + hh * MEM_HEAD_DIM + hf * LANES
                out_ref[0, pl.ds(hf * MEM_HEADS + hh, n, stride=halves * MEM_HEADS), :] = kv[:, col:col + LANES]


def _mem_kv(mem, g, wkv, tm):
    t = mem.shape[0]
    row = pl.BlockSpec((tm, D_MODEL), lambda i: (i, 0))
    per_slot = (MEM_HEAD_DIM // LANES) * MEM_HEADS
    stored = pl.BlockSpec((1, tm * per_slot, LANES), lambda i: (i, 0, 0))
    return pl.pallas_call(
        _mem_kv_kernel,
        grid=(t // tm,),
        in_specs=[row, _const_spec((1, D_MODEL)), _const_spec(wkv.shape)],
        out_specs=[stored, stored, row, row],
        out_shape=[jax.ShapeDtypeStruct((t // tm, tm * per_slot, LANES), F32)] * 2
        + [jax.ShapeDtypeStruct((t, D_MODEL), BF16)] * 2,
        compiler_params=_params(("arbitrary",)),
        name="mem_kv",
    )(mem, g, wkv)


def _rope_tables(pos):
    half = ROPE_DIM // 2
    inv = ROPE_THETA ** (-jnp.arange(half, dtype=F32) * (2.0 / ROPE_DIM))
    ang = pos.astype(F32)[:, None] * inv[None, :]
    cos, sin = jnp.cos(ang), jnp.sin(ang)
    n = pos.shape[0]
    pad = jnp.zeros((n, HEAD_DIM - ROPE_DIM), F32)
    zero = jnp.zeros((n, half), F32)
    cos_h = jnp.concatenate([cos, cos, pad + 1.0], axis=1)
    lo_h = jnp.concatenate([-sin, zero, pad], axis=1)
    hi_h = jnp.concatenate([zero, sin, pad], axis=1)
    rep = LANES // HEAD_DIM
    return tuple(jnp.tile(a, (1, rep)) for a in (cos_h, lo_h, hi_h)), (cos.T, sin.T)


def _ffn_weights(w_gate, w_up, w_down):
    return w_gate.astype(BF16), w_up.astype(BF16), w_down.astype(BF16)


def _lane_block_states(st, n):
    st = st.reshape(N_LANE_BLOCKS, n, 2, GROUPS_PER_LANE_BLOCK, SSM_STATE).transpose(2, 1, 0, 3, 4)
    st = st.reshape(2, n, N_SSM_GROUPS, SSM_STATE)
    return st[0], st[1]


def kernel(x_prompt, x_sample, state_ssm_re, state_ssm_im, cache_swa_k, cache_swa_v, cache_mem_k, cache_mem_v, mem_prompt, ffn1_pre_g, ffn1_w_gate, ffn1_w_up, ffn1_w_down, ffn1_post_g, mix_pre_g, w_in, ssm_a_re, ssm_a_im, ssm_log_step, ssm_b_re, ssm_b_im, ssm_c_re, ssm_c_im, ssm_d, ssm_w_glu, ssm_b_glu, attn_sinks, ssm_out_g, attn_out_g, w_out, mix_post_g, mem_norm_g, w_mem_q, w_mem_k, w_mem_v, w_mem_o, xa_pre_g, xa_post_g, ffn2_pre_g, ffn2_w_gate, ffn2_w_up, ffn2_w_down, ffn2_post_g):
    n_p, s_p, _ = x_prompt.shape
    n_s, t_s, _ = x_sample.shape
    tm = 512
    row = lambda a: a.reshape(1, -1).astype(F32)

    ffn1_w = _ffn_weights(ffn1_w_gate, ffn1_w_up, ffn1_w_down)
    win = w_in.astype(BF16)
    merge_w = (ssm_w_glu.astype(BF16), row(ssm_b_glu), row(ssm_out_g), w_out.astype(BF16), row(mix_post_g),
               row(xa_pre_g), w_mem_q.astype(BF16))
    out_w = (w_mem_o.astype(BF16), row(xa_post_g), row(ffn2_pre_g),
             *_ffn_weights(ffn2_w_gate, ffn2_w_up, ffn2_w_down), row(ffn2_post_g))
    wkv = jnp.concatenate([w_mem_k, w_mem_v], axis=1).astype(BF16)
    d_row = row(ssm_d)
    ssm_args = (ssm_a_re.astype(F32), ssm_a_im.astype(F32), ssm_log_step.astype(F32), ssm_b_re.astype(F32),
                ssm_b_im.astype(F32), ssm_c_re.astype(F32), ssm_c_im.astype(F32))
    sinks = attn_sinks.astype(F32)

    pm_k, pm_v, pm_kb, pm_vb = _mem_kv(mem_prompt.reshape(n_p * N_MEM, D_MODEL), row(mem_norm_g), wkv, N_MEM)

    def from_stored(c):
        halves = MEM_HEAD_DIM // LANES
        return c.reshape(n_p, N_MEM, halves, MEM_HEADS, LANES).transpose(0, 1, 3, 2, 4).reshape(
            n_p, N_MEM, MEM_HEADS, MEM_HEAD_DIM)

    def tokenwise_in(x2, pos_tab):
        return _ffn_in(x2, row(ffn1_pre_g), ffn1_w, row(ffn1_post_g), row(mix_pre_g), win, *pos_tab, tm)

    lc_p = 2 * t_s
    ssm_m, ssm_w, ssm_v, lam_p, lam_s = _ssm_tables(*ssm_args, lc_p)
    _, tab_p_t = _rope_tables(jnp.arange(s_p, dtype=jnp.int32))
    h1, u, k_win, vt_win, at = _ffn_in_swa(x_prompt.reshape(n_p * s_p, D_MODEL), row(ffn1_pre_g), ffn1_w, row(ffn1_post_g),
                                   row(mix_pre_g), win[:, :SSM_WIDTH], win[:, SSM_WIDTH:].T, *tab_p_t, sinks,
                                   attn_out_g.astype(F32), s_p, tm)
    y4, st_p = _ssm_prompt(u.reshape(N_LANE_BLOCKS, n_p, s_p, LANES), ssm_m, ssm_w, ssm_v, lam_p,
                           d_row.reshape(N_LANE_BLOCKS, 1, LANES), lc_p, 1024, 2)
    y_prompt = _post(h1, y4.reshape(N_LANE_BLOCKS, n_p * s_p, LANES), at, pm_kb.reshape(n_p, N_MEM, D_MODEL),
                     pm_vb.reshape(n_p, N_MEM, D_MODEL), merge_w, out_w, tm).reshape(n_p, s_p, D_MODEL)
    p_sre, p_sim = _lane_block_states(st_p, n_p)
    p_wk = k_win.reshape(n_p, WINDOW, N_KV_HEADS, HEAD_DIM)
    p_wv = vt_win.reshape(N_KV_HEADS, HEAD_DIM, n_p, WINDOW).transpose(2, 3, 0, 1)

    pos_s = jnp.tile(PAST_LEN + jnp.arange(t_s, dtype=jnp.int32), n_s)
    h1s, us, qs, ks, vs = tokenwise_in(x_sample.reshape(n_s * t_s, D_MODEL), _rope_tables(pos_s)[0])
    ys, s_sre, s_sim = _ssm_sample(us, state_ssm_re.reshape(n_s, -1).astype(F32), state_ssm_im.reshape(n_s, -1).astype(F32),
                                   ssm_m, ssm_w, ssm_v, lam_s, d_row, t_s)
    win_len = cache_swa_k.shape[1]
    ats, s_wk, s_wv = _swa_sample(qs.reshape(n_s, t_s, Q_WIDTH), ks.reshape(n_s, t_s, KV_WIDTH), vs.reshape(n_s, t_s, KV_WIDTH),
                                  cache_swa_k.transpose(0, 2, 3, 1).reshape(n_s, KV_WIDTH, win_len),
                                  cache_swa_v.transpose(0, 2, 3, 1).reshape(n_s, KV_WIDTH, win_len),
                                  sinks, row(attn_out_g), 16)
    s_wk = s_wk.reshape(n_s, N_KV_HEADS, HEAD_DIM, win_len).transpose(0, 3, 1, 2)
    s_wv = s_wv.reshape(n_s, N_KV_HEADS, HEAD_DIM, win_len).transpose(0, 3, 1, 2)
    h2s, qms = _merge(h1s, ys, ats.reshape(n_s * t_s, Q_WIDTH), merge_w, tm)
    oms = _mem_attn_sample(qms, cache_mem_k, cache_mem_v, n_s, t_s, 8)
    y_sample = _ffn_out(h2s, oms, out_w, tm).reshape(n_s, t_s, D_MODEL)

    return (y_prompt, y_sample, p_sre, p_sim, p_wk, p_wv,
            from_stored(pm_k), from_stored(pm_v),
            s_sre.reshape(n_s, N_SSM_GROUPS, SSM_STATE), s_sim.reshape(n_s, N_SSM_GROUPS, SSM_STATE),
            s_wk, s_wv)
```

```python
import functools
import math

import jax
import jax.numpy as jnp
from jax import lax
from jax.experimental import pallas as pl
from jax.experimental.pallas import tpu as pltpu

F32 = jnp.float32
BF16 = jnp.bfloat16

D_MODEL = 1024
PAST_LEN = 16384
SSM_WIDTH = 512
SSM_GROUP = 16
N_SSM_GROUPS = 32
SSM_STATE = 64
HEAD_DIM = 64
N_HEADS = 8
N_KV_HEADS = 2
GQA_GROUP = 4
Q_WIDTH = 512
KV_WIDTH = 128
WINDOW = 128
ROPE_THETA = 500000.0
ROPE_DIM = 16
N_MEM = 256
MEM_HEADS = 4
MEM_HEAD_DIM = 256
D_FF = 2816
RMS_EPS = 1e-6
NEG_INF = -1e30

LANES = 128
FF_CHUNK = 256
ROW_BLOCKS = 2
N_FF_CHUNKS = D_FF // FF_CHUNK
GROUPS_PER_LANE_BLOCK = LANES // SSM_GROUP
N_LANE_BLOCKS = SSM_WIDTH // LANES
STATE_LANES = GROUPS_PER_LANE_BLOCK * SSM_STATE
VMEM_LIMIT = 56 * 1024 * 1024
TOKEN_TILE = 512
SSM_TIME_TILE = 1024
SSM_LANE_BLOCKS_PER_STEP = 2
SWA_SAMPLE_SEQS = 16
MEM_SAMPLE_SEQS = 8


def _rms(x, g):
    return x * lax.rsqrt(jnp.mean(x * x, axis=-1, keepdims=True) + RMS_EPS) * g


def _const_spec(shape):
    nd = len(shape)
    return pl.BlockSpec(shape, lambda *_: (0,) * nd, pipeline_mode=pl.Buffered(1))


def _params(sem):
    return pltpu.CompilerParams(dimension_semantics=sem, vmem_limit_bytes=VMEM_LIMIT)


def _run(gen):
    try:
        while True:
            next(gen)
    except StopIteration as done:
        return done.value


def _interleave(gens):
    live = list(gens)
    while live:
        for g in list(live):
            try:
                next(g)
            except StopIteration:
                live.remove(g)


def _ffn_stages(x, pre_g, wg_ref, wu_ref, wd_ref, post_g, act_ref, rows=slice(None), after_chunk=None):
    xn = _rms(x, pre_g).astype(BF16)
    yield
    for c in range(N_FF_CHUNKS):
        if after_chunk is not None and c == after_chunk[0] + 1:
            after_chunk[1]()
        cols = slice(c * FF_CHUNK, (c + 1) * FF_CHUNK)
        gate = jnp.dot(xn, wg_ref[:, cols], preferred_element_type=F32)
        up = jnp.dot(xn, wu_ref[:, cols], preferred_element_type=F32)
        act = gate * (1.0 / (1.0 + jnp.exp(-gate))) * up
        act_ref[rows, cols] = act.astype(BF16)
        yield
    down = jnp.dot(act_ref[rows, :], wd_ref[...], preferred_element_type=F32)
    yield
    return x + 0.5 * _rms(down, post_g)


def _ffn_tile(*args, **kwargs):
    return _run(_ffn_stages(*args, **kwargs))


def _rope(x, cos, sin_lo, sin_hi):
    w = x.shape[1]
    half = ROPE_DIM // 2
    return (x * cos + pltpu.roll(x, w - half, 1) * sin_lo + pltpu.roll(x, half, 1) * sin_hi)


def _ffn_in_kernel(x_ref, pre_g_ref, wg_ref, wu_ref, wd_ref, post_g_ref, mix_g_ref, win_ref,
                   cos_ref, slo_ref, shi_ref,
                   h_ref, u_ref, q_ref, k_ref, v_ref, act_ref):
    h = _ffn_tile(x_ref[...], pre_g_ref[...], wg_ref, wu_ref, wd_ref, post_g_ref[...], act_ref)
    h_ref[...] = h
    z = jnp.dot(_rms(h, mix_g_ref[...]).astype(BF16), win_ref[...], preferred_element_type=F32)
    u_ref[...] = z[:, :SSM_WIDTH]
    o1 = SSM_WIDTH + Q_WIDTH
    cos, slo, shi = cos_ref[...], slo_ref[...], shi_ref[...]
    rep = Q_WIDTH // LANES
    q = _rope(z[:, SSM_WIDTH:o1], jnp.tile(cos, (1, rep)), jnp.tile(slo, (1, rep)), jnp.tile(shi, (1, rep)))
    q_ref[...] = (q * (HEAD_DIM ** -0.5)).astype(BF16)
    k_ref[...] = _rope(z[:, o1:o1 + KV_WIDTH], cos, slo, shi)
    v_ref[...] = z[:, o1 + KV_WIDTH:]


def _ffn_in(x, pre_g, ffn_w, post_g, mix_g, win, cos, slo, shi, tm):
    t = x.shape[0]
    n_pos_tiles = cos.shape[0] // tm
    row = lambda w: pl.BlockSpec((tm, w), lambda i: (i, 0))
    tab = pl.BlockSpec((tm, LANES), lambda i: (i % n_pos_tiles, 0))
    return pl.pallas_call(
        _ffn_in_kernel,
        grid=(t // tm,),
        in_specs=[row(D_MODEL), _const_spec((1, D_MODEL)), *[_const_spec(w.shape) for w in ffn_w],
                  _const_spec((1, D_MODEL)), _const_spec((1, D_MODEL)), _const_spec(win.shape),
                  tab, tab, tab],
        out_specs=[row(D_MODEL), row(SSM_WIDTH), row(Q_WIDTH), row(KV_WIDTH), row(KV_WIDTH)],
        out_shape=[jax.ShapeDtypeStruct((t, D_MODEL), F32), jax.ShapeDtypeStruct((t, SSM_WIDTH), F32),
                   jax.ShapeDtypeStruct((t, Q_WIDTH), BF16), jax.ShapeDtypeStruct((t, KV_WIDTH), F32),
                   jax.ShapeDtypeStruct((t, KV_WIDTH), F32)],
        scratch_shapes=[pltpu.VMEM((tm, D_FF), BF16)],
        compiler_params=_params(("arbitrary",)),
        name="ffn_in",
    )(x, pre_g, *ffn_w, post_g, mix_g, win, cos, slo, shi)


def _complex_step(s_re, s_im, l_re, l_im, x_re, x_im):
    return l_re * s_re - l_im * s_im + x_re, l_re * s_im + l_im * s_re + x_im


def _ssm_prompt_kernel(u_ref, m_ref, w_ref, v_ref, lam_ref, d_ref, y_ref, st_ref,
                       x_scr, ss_scr, s_scr, *, lc, nb, ncl, nlb):
    tt = pl.program_id(1)
    nrow = nlb * nb
    nq = STATE_LANES // LANES

    @pl.when(tt == 0)
    def _():
        s_scr[...] = jnp.zeros_like(s_scr)

    def piece(b, n, j):
        return u_ref[b, n, pl.ds(j, ncl, stride=lc), :]

    a = []
    for b in range(nlb):
        a.append(jnp.concatenate(
            [jnp.concatenate([piece(b, n, j) for j in range(lc)], axis=1) for n in range(nb)], axis=0).astype(BF16))
        x = jnp.dot(a[b], w_ref[b], preferred_element_type=F32)
        for qq in range(2 * nq):
            for n in range(nb):
                x_scr[qq, pl.ds(b * nb + n, ncl, stride=nrow), :] = x[n * ncl:(n + 1) * ncl, qq * LANES:(qq + 1) * LANES]
    mt = 2 * LANES
    y_intra = [jnp.concatenate(
        [jnp.dot(a[b][:, :(jt + 1) * mt], m_ref[b, :(jt + 1) * mt, jt * mt:(jt + 1) * mt], preferred_element_type=F32)
         for jt in range(lc * LANES // mt)], axis=1) for b in range(nlb)]
    lam_rows = lambda qq: jnp.concatenate(
        [jnp.broadcast_to(lam_ref[b, :, qq * LANES:(qq + 1) * LANES], (nb, LANES)) for b in range(nlb)], axis=0)
    l_re = [lam_rows(qq) for qq in range(nq)]
    l_im = [lam_rows(nq + qq) for qq in range(nq)]

    s = [s_scr[qq] for qq in range(2 * nq)]
    for c in range(ncl):
        rows = slice(c * nrow, (c + 1) * nrow)
        for qq in range(nq):
            ss_scr[qq, rows, :] = s[qq]
            ss_scr[nq + qq, rows, :] = s[nq + qq]
            s[qq], s[nq + qq] = _complex_step(s[qq], s[nq + qq], l_re[qq], l_im[qq],
                                              x_scr[qq, rows, :], x_scr[nq + qq, rows, :])
    for qq in range(2 * nq):
        s_scr[qq] = s[qq]
    for b in range(nlb):
        st_ref[b] = jnp.concatenate([s[qq][b * nb:(b + 1) * nb] for qq in range(2 * nq)], axis=1)
        s_start = jnp.concatenate(
            [jnp.concatenate([ss_scr[qq, pl.ds(b * nb + n, ncl, stride=nrow), :] for n in range(nb)], axis=0)
             for qq in range(2 * nq)], axis=1).astype(BF16)
        y = y_intra[b] + jnp.dot(s_start, v_ref[b], preferred_element_type=F32)
        d = d_ref[b]
        for n in range(nb):
            for j in range(lc):
                y_ref[b, n, pl.ds(j, ncl, stride=lc), :] = (
                    y[n * ncl:(n + 1) * ncl, j * LANES:(j + 1) * LANES] + d * piece(b, n, j))


def _ssm_prompt(u4, m, w, v, lam, d, lc, tl, nlb):
    _, nb, seq, _ = u4.shape
    ncl = tl // lc
    nslab = 2 * STATE_LANES // LANES
    kern = functools.partial(_ssm_prompt_kernel, lc=lc, nb=nb, ncl=ncl, nlb=nlb)
    wspec = lambda a: pl.BlockSpec((nlb,) + a.shape[1:], lambda p, t: (p, 0, 0), pipeline_mode=pl.Buffered(1))
    io = pl.BlockSpec((nlb, nb, tl, LANES), lambda p, t: (p, 0, t, 0))
    return pl.pallas_call(
        kern,
        grid=(N_LANE_BLOCKS // nlb, seq // tl),
        in_specs=[io, wspec(m), wspec(w), wspec(v), wspec(lam), wspec(d)],
        out_specs=[io, pl.BlockSpec((nlb, nb, 2 * STATE_LANES), lambda p, t: (p, 0, 0))],
        out_shape=[jax.ShapeDtypeStruct(u4.shape, F32),
                   jax.ShapeDtypeStruct((N_LANE_BLOCKS, nb, 2 * STATE_LANES), F32)],
        scratch_shapes=[pltpu.VMEM((nslab, nlb * nb * ncl, LANES), F32),
                        pltpu.VMEM((nslab, nlb * nb * ncl, LANES), F32),
                        pltpu.VMEM((nslab, nlb * nb, LANES), F32)],
        compiler_params=_params(("arbitrary", "arbitrary")),
        name="ssm_prompt",
    )(u4, m, w, v, lam, d)


def _ssm_sample_kernel(u_ref, sre_ref, sim_ref, m_ref, w_ref, v_ref, lam_ref, d_ref,
                       y_ref, ore_ref, oim_ref, *, lc, ns):
    def piece(j):
        return u_ref[pl.ds(j, ns, stride=lc), :]

    a = jnp.concatenate([piece(j) for j in range(lc)], axis=1).astype(BF16)
    s_re, s_im = sre_ref[...], sim_ref[...]
    x = jnp.dot(a, w_ref[0], preferred_element_type=F32)
    e_re, e_im = _complex_step(s_re, s_im, lam_ref[0, :, :STATE_LANES], lam_ref[0, :, STATE_LANES:],
                               x[:, :STATE_LANES], x[:, STATE_LANES:])
    ore_ref[...] = e_re
    oim_ref[...] = e_im
    s0 = jnp.concatenate([s_re, s_im], axis=1).astype(BF16)
    y = (jnp.dot(a, m_ref[0], preferred_element_type=F32) + jnp.dot(s0, v_ref[0], preferred_element_type=F32))
    d = d_ref[...]
    for j in range(lc):
        y_ref[pl.ds(j, ns, stride=lc), :] = y[:, j * LANES:(j + 1) * LANES] + d * piece(j)


def _ssm_sample(u, s_re, s_im, m, w, v, lam, d, lc):
    t = u.shape[0]
    ns = t // lc
    kern = functools.partial(_ssm_sample_kernel, lc=lc, ns=ns)
    assert m.shape[1] == 2 * lc * LANES
    n = lc * LANES
    col = lambda rows, width: pl.BlockSpec((rows, width), lambda b: (0, b))
    return pl.pallas_call(
        kern,
        grid=(N_LANE_BLOCKS,),
        in_specs=[col(t, LANES), col(ns, STATE_LANES), col(ns, STATE_LANES),
                  pl.BlockSpec((1, n, n), lambda b: (b, 0, 0)),
                  pl.BlockSpec((1, n, 2 * STATE_LANES), lambda b: (b, 1, 0)),
                  pl.BlockSpec((1, 2 * STATE_LANES, n), lambda b: (b, 0, 0)),
                  pl.BlockSpec((1, 1, 2 * STATE_LANES), lambda b: (b, 0, 0)), col(1, LANES)],
        out_specs=[col(t, LANES), col(ns, STATE_LANES), col(ns, STATE_LANES)],
        out_shape=[jax.ShapeDtypeStruct(u.shape, F32), jax.ShapeDtypeStruct(s_re.shape, F32),
                   jax.ShapeDtypeStruct(s_im.shape, F32)],
        compiler_params=_params(("arbitrary",)),
        name="ssm_sample",
    )(u, s_re, s_im, m, w, v, lam, d)


def _ssm_discretise(a_re, a_im, log_step):
    dt = jnp.exp(log_step)
    mag = jnp.exp(a_re * dt)
    l_re, l_im = mag * jnp.cos(a_im * dt), mag * jnp.sin(a_im * dt)
    den = a_re * a_re + a_im * a_im
    n_re, n_im = l_re - 1.0, l_im
    return l_re, l_im, (n_re * a_re + n_im * a_im) / den, (n_im * a_re - n_re * a_im) / den


def _complex_powers(l_re, l_im, n):
    p_re, p_im = [jnp.ones_like(l_re)], [jnp.zeros_like(l_re)]
    for _ in range(n):
        p_re, p_im = p_re + [p_re[-1] * l_re - p_im[-1] * l_im], p_im + [p_re[-1] * l_im + p_im[-1] * l_re]
    return p_re, p_im


def _split_bf16(x):
    hi = x.astype(BF16)
    return hi, (x - hi.astype(F32)).astype(BF16)


def _dot_split(a, b):
    dot = lambda x, y: jnp.dot(x, y, preferred_element_type=F32)
    return dot(a[0], b[0]) + (dot(a[0], b[1]) + dot(a[1], b[0]))


def _ssm_tables_kernel(ac_re_ref, ac_im_ref, lsc_ref, ar_re_ref, ar_im_ref, lsr_ref, b_re_ref, b_im_ref,
                       c_re_ref, c_im_ref, m_ref, w_ref, v_ref, lam_ref, lam_half_ref, *, lc):
    l_re, l_im, cf_re, cf_im = _ssm_discretise(ac_re_ref[0], ac_im_ref[0], lsc_ref[0])
    b_re, b_im, c_re, c_im = b_re_ref[0], b_im_ref[0], c_re_ref[0], c_im_ref[0]
    c_re_parts, c_im_parts = _split_bf16(c_re), _split_bf16(c_im)
    bb_re = cf_re * b_re - cf_im * b_im
    bb_im = cf_re * b_im + cf_im * b_re
    p_re, p_im = _complex_powers(l_re, l_im, lc)
    lag = []
    for k in range(lc):
        et_re = (p_re[k] * bb_re - p_im[k] * bb_im).T
        et_im = (p_re[k] * bb_im + p_im[k] * bb_re).T
        j = lc - 1 - k
        w_ref[0, j * LANES:(j + 1) * LANES, :STATE_LANES] = et_re.astype(BF16)
        w_ref[0, j * LANES:(j + 1) * LANES, STATE_LANES:] = et_im.astype(BF16)
        lag.append((_dot_split(_split_bf16(et_re), c_re_parts) - _dot_split(_split_bf16(et_im), c_im_parts)).astype(BF16))
        v_ref[0, :STATE_LANES, k * LANES:(k + 1) * LANES] = (p_re[k + 1] * c_re - p_im[k + 1] * c_im).astype(BF16)
        v_ref[0, STATE_LANES:, k * LANES:(k + 1) * LANES] = (-(p_im[k + 1] * c_re + p_re[k + 1] * c_im)).astype(BF16)
    zero = jnp.zeros((LANES, LANES), BF16)
    for j in range(lc):
        for jj in range(lc):
            m_ref[0, j * LANES:(j + 1) * LANES, jj * LANES:(jj + 1) * LANES] = lag[jj - j] if jj >= j else zero
    r_re, r_im, _, _ = _ssm_discretise(ar_re_ref[0], ar_im_ref[0], lsr_ref[0])
    q_re, q_im = _complex_powers(r_re, r_im, lc)
    lam_ref[0] = jnp.concatenate([q_re[lc], q_im[lc]], axis=1)
    lam_half_ref[0] = jnp.concatenate([q_re[lc // 2], q_im[lc // 2]], axis=1)


def _ssm_tables(a_re, a_im, log_step, b_re, b_im, c_re, c_im, lc):
    g, p, h = b_re.shape
    nbk, r = N_LANE_BLOCKS, GROUPS_PER_LANE_BLOCK
    ls = jnp.broadcast_to(log_step[:, None], (g, p))
    cols = [jnp.broadcast_to(x.reshape(nbk, STATE_LANES, 1), (nbk, STATE_LANES, LANES)) for x in (a_re, a_im, ls)]
    rows = [x.reshape(nbk, 1, STATE_LANES) for x in (a_re, a_im, ls)]
    eye = jnp.eye(r, dtype=F32)[None, :, None, :, None]

    def block_diag(x):
        return (x[:, :, :, None, :] * eye).reshape(nbk, STATE_LANES, LANES)

    mats = [block_diag(b_re.reshape(nbk, r, p, h)), block_diag(b_im.reshape(nbk, r, p, h)),
            block_diag(c_re.reshape(nbk, r, h, p).transpose(0, 1, 3, 2)),
            block_diag(c_im.reshape(nbk, r, h, p).transpose(0, 1, 3, 2))]
    spec = lambda shape: pl.BlockSpec((1,) + shape, lambda b: (b, 0, 0))
    n = lc * LANES
    return pl.pallas_call(
        functools.partial(_ssm_tables_kernel, lc=lc),
        grid=(nbk,),
        in_specs=[spec((STATE_LANES, LANES))] * 3 + [spec((1, STATE_LANES))] * 3 + [spec((STATE_LANES, LANES))] * 4,
        out_specs=[spec((n, n)), spec((n, 2 * STATE_LANES)), spec((2 * STATE_LANES, n)),
                   spec((1, 2 * STATE_LANES)), spec((1, 2 * STATE_LANES))],
        out_shape=[jax.ShapeDtypeStruct((nbk, n, n), BF16), jax.ShapeDtypeStruct((nbk, n, 2 * STATE_LANES), BF16),
                   jax.ShapeDtypeStruct((nbk, 2 * STATE_LANES, n), BF16),
                   jax.ShapeDtypeStruct((nbk, 1, 2 * STATE_LANES), F32),
                   jax.ShapeDtypeStruct((nbk, 1, 2 * STATE_LANES), F32)],
        compiler_params=_params(("arbitrary",)),
        name="ssm_tables",
    )(*cols, *rows, *mats)


def _nt_dot(a, b):
    return lax.dot_general(a, b, (((1,), (1,)), ((), ())), preferred_element_type=F32)


def _swa_prompt_tile(sink_ref, qt_ref, kcat, vcat_t, mstd_ref, mfirst_ref, g_ref, ot_ref, first_tile, nblk):
    keys = lambda j: slice(j * WINDOW, (j + 2) * WINDOW)
    zeros = jnp.zeros((HEAD_DIM, WINDOW), BF16)
    sinks = [jnp.concatenate([jnp.full((1, WINDOW), sink_ref[kh * GQA_GROUP + g], F32) for g in range(GQA_GROUP)],
                             axis=1) for kh in range(N_KV_HEADS)]
    chains = [(j, kh) for j in range(nblk) for kh in range(N_KV_HEADS)]

    def scores(j, kh):
        def rhs(g):
            h = kh * GQA_GROUP + g
            q = qt_ref[h * HEAD_DIM:(h + 1) * HEAD_DIM, j * WINDOW:(j + 1) * WINDOW]
            return jnp.concatenate([q, zeros] if kh == 0 else [zeros, q], axis=0)
        r = jnp.concatenate([rhs(g) for g in range(GQA_GROUP)], axis=1)
        vmask = mstd_ref[...]
        if j == 0:
            vmask = jnp.where(first_tile, mfirst_ref[...], vmask)
        valid = jnp.tile(vmask, (1, GQA_GROUP)) > 0.0
        return jnp.where(valid, jnp.dot(kcat[keys(j)], r, preferred_element_type=F32), NEG_INF)

    def softmax(kh, sc):
        m = jnp.maximum(jnp.max(sc, axis=0, keepdims=True), sinks[kh])
        e = jnp.exp(sc - m)
        return e.astype(BF16), 1.0 / (jnp.sum(e, axis=0, keepdims=True) + jnp.exp(sinks[kh] - m))

    sc = [scores(*c) for c in chains]
    pr = [softmax(kh, x) for (j, kh), x in zip(chains, sc)]

    def finish():
        out = [jnp.dot(vcat_t[kh * HEAD_DIM:(kh + 1) * HEAD_DIM, keys(j)], e, preferred_element_type=F32) * inv
               for (j, kh), (e, inv) in zip(chains, pr)]
        gain = g_ref[...]
        for j in range(nblk):
            o = jnp.concatenate([out[j * N_KV_HEADS + kh][:, g * WINDOW:(g + 1) * WINDOW]
                                 for kh in range(N_KV_HEADS) for g in range(GQA_GROUP)], axis=0)
            scale = lax.rsqrt(jnp.mean(o * o, axis=0, keepdims=True) + RMS_EPS)
            ot_ref[:, j * WINDOW:(j + 1) * WINDOW] = (o * scale * gain).astype(BF16)

    return finish


def _swa_masks():
    kj = jnp.arange(2 * WINDOW)[:, None]
    diff = jnp.arange(WINDOW)[None, :] + WINDOW - kj
    std = (diff >= 0) & (diff <= WINDOW)
    return std.astype(F32), (std & (kj >= WINDOW)).astype(F32)


def _ffn_in_swa_kernel(sink_ref, x_ref, pre_g_ref, wg_ref, wu_ref, wd_ref, post_g_ref, mix_g_ref, win_u_ref, wqkv_t_ref,
                       cos_t_ref, sin_t_ref, mstd_ref, mfirst_ref, swa_g_ref,
                       h_ref, u_ref, k_win_ref, vt_win_ref, ot_ref, act_ref, q_scr, kcat_scr, vcat_scr, *, nt, nblk):
    i = pl.program_id(0)
    tm = x_ref.shape[0]

    @pl.when(i == 0)
    def _():
        q_scr[...] = jnp.zeros_like(q_scr)
        kcat_scr[...] = jnp.zeros_like(kcat_scr)
        vcat_scr[...] = jnp.zeros_like(vcat_scr)

    first_tile = lax.rem(i + (nt - 1), nt) == 0
    swa_finish = _swa_prompt_tile(sink_ref, q_scr, kcat_scr[...], vcat_scr[...], mstd_ref, mfirst_ref, swa_g_ref,
                                  ot_ref, first_tile, nblk)

    kcat_scr[:WINDOW, :] = kcat_scr[tm:, :]
    vcat_scr[:, :WINDOW] = vcat_scr[:, tm:]
    half = ROPE_DIM // 2

    def row_block(r, swa_hook):
        n = r.stop - r.start
        h = yield from _ffn_stages(x_ref[r, :], pre_g_ref[...], wg_ref, wu_ref, wd_ref, post_g_ref[...], act_ref, r,
                                   after_chunk=swa_hook)
        h_ref[r, :] = h
        hn = _rms(h, mix_g_ref[...]).astype(BF16)
        z = jnp.dot(hn, win_u_ref[...], preferred_element_type=F32)
        yield
        for b in range(N_LANE_BLOCKS):
            u_ref[b, r, :] = z[:, b * LANES:(b + 1) * LANES]
        zt = _nt_dot(wqkv_t_ref[...], hn)
        yield
        vt = zt[Q_WIDTH + KV_WIDTH:]
        if r.stop == tm:
            vt_win_ref[...] = vt[:, -WINDOW:]
        qk = zt[:Q_WIDTH + KV_WIDTH].reshape(N_HEADS + N_KV_HEADS, HEAD_DIM, n)
        x1, x2 = qk[:, :half], qk[:, half:ROPE_DIM]
        cos, sin = cos_t_ref[:, r][None], sin_t_ref[:, r][None]
        qk = jnp.concatenate([x1 * cos - x2 * sin, x2 * cos + x1 * sin, qk[:, ROPE_DIM:]], axis=1)
        k = qk[N_HEADS:].reshape(KV_WIDTH, n).T
        if r.stop == tm:
            k_win_ref[...] = k[-WINDOW:]
        shifted = slice(WINDOW + r.start, WINDOW + r.stop)
        kcat_scr[shifted, :] = k.astype(BF16)
        vcat_scr[:, shifted] = vt.astype(BF16)
        q_scr[:, r] = (qk[:N_HEADS].reshape(Q_WIDTH, n) * (HEAD_DIM ** -0.5)).astype(BF16)

    blocks = [slice(b * tm // ROW_BLOCKS, (b + 1) * tm // ROW_BLOCKS) for b in range(ROW_BLOCKS)]
    _interleave([row_block(r, (N_FF_CHUNKS - 3, swa_finish) if r.start == 0 else None) for r in blocks])


def _ffn_in_swa(x, pre_g, ffn_w, post_g, mix_g, win_u, wqkv_t, cos_t, sin_t, sinks, swa_g, seq, tm):
    t = x.shape[0]
    n_tiles, nt, nblk, n_seq = t // tm, seq // tm, tm // WINDOW, t // seq
    half = ROPE_DIM // 2
    mstd, mfirst = _swa_masks()
    gain = jnp.broadcast_to(swa_g.reshape(Q_WIDTH, 1), (Q_WIDTH, WINDOW))
    cur = lambda i: jnp.minimum(i, n_tiles - 1)
    row = lambda w: pl.BlockSpec((tm, w), lambda i: (cur(i), 0))
    tab_t = pl.BlockSpec((half, tm), lambda i: (0, cur(i) % nt))
    return pl.pallas_call(
        functools.partial(_ffn_in_swa_kernel, nt=nt, nblk=nblk),
        grid=(n_tiles + 1,),
        in_specs=[pl.BlockSpec(memory_space=pltpu.SMEM), row(D_MODEL), _const_spec((1, D_MODEL)),
                  *[_const_spec(w.shape) for w in ffn_w], _const_spec((1, D_MODEL)), _const_spec((1, D_MODEL)),
                  _const_spec(win_u.shape), _const_spec(wqkv_t.shape), tab_t, tab_t,
                  _const_spec(mstd.shape), _const_spec(mfirst.shape), _const_spec(gain.shape)],
        out_specs=[row(D_MODEL), pl.BlockSpec((N_LANE_BLOCKS, tm, LANES), lambda i: (0, cur(i), 0)),
                   pl.BlockSpec((WINDOW, KV_WIDTH), lambda i: (cur(i) // nt, 0)),
                   pl.BlockSpec((KV_WIDTH, WINDOW), lambda i: (0, cur(i) // nt)),
                   pl.BlockSpec((Q_WIDTH, tm), lambda i: (0, jnp.maximum(i - 1, 0)))],
        out_shape=[jax.ShapeDtypeStruct((t, D_MODEL), F32), jax.ShapeDtypeStruct((N_LANE_BLOCKS, t, LANES), F32),
                   jax.ShapeDtypeStruct((n_seq * WINDOW, KV_WIDTH), F32),
                   jax.ShapeDtypeStruct((KV_WIDTH, n_seq * WINDOW), F32), jax.ShapeDtypeStruct((Q_WIDTH, t), BF16)],
        scratch_shapes=[pltpu.VMEM((tm, D_FF), BF16), pltpu.VMEM((Q_WIDTH, tm), BF16),
                        pltpu.VMEM((WINDOW + tm, KV_WIDTH), BF16), pltpu.VMEM((KV_WIDTH, WINDOW + tm), BF16)],
        compiler_params=_params(("arbitrary",)),
        name="ffn_in_swa",
    )(sinks, x, pre_g, *ffn_w, post_g, mix_g, win_u, wqkv_t, cos_t, sin_t, mstd, mfirst, gain)


def _swa_sample_kernel(sink_ref, q_ref, kn_ref, vn_ref, ck_ref, cv_ref, g_ref, o_ref, nk_ref, nv_ref, *, ns, t):
    rows = GQA_GROUP * t
    tok = lax.broadcasted_iota(jnp.int32, (rows, WINDOW), 0) % t
    valid_c = lax.broadcasted_iota(jnp.int32, (rows, WINDOW), 1) >= tok
    tok_n = lax.broadcasted_iota(jnp.int32, (rows, t), 0) % t
    valid_n = lax.broadcasted_iota(jnp.int32, (rows, t), 1) <= tok_n
    gain = g_ref[...]

    sinks = [jnp.concatenate([jnp.full((t, 1), sink_ref[kh * GQA_GROUP + g], F32) for g in range(GQA_GROUP)], axis=0)
             for kh in range(N_KV_HEADS)]
    heads = [(s, kh) for s in range(ns) for kh in range(N_KV_HEADS)]
    hs = lambda kh: slice(kh * HEAD_DIM, (kh + 1) * HEAD_DIM)

    newest = lax.broadcasted_iota(jnp.int32, (KV_WIDTH, WINDOW), 1) >= WINDOW - t
    pad = jnp.zeros((WINDOW - 2 * t, KV_WIDTH), F32)

    def shifted(cache_ref, new_ref, s):
        new_rows = jnp.concatenate([pad, jnp.zeros((t, KV_WIDTH), F32), new_ref[s]], axis=0)
        return jnp.where(newest, new_rows.T, pltpu.roll(cache_ref[s], WINDOW - t, 1))

    for s in range(ns):
        nk_ref[s] = shifted(ck_ref, kn_ref, s)
        nv_ref[s] = shifted(cv_ref, vn_ref, s)

    def scores(s, kh):
        q = q_ref[s].astype(F32)
        q4 = jnp.concatenate([q[:, (kh * GQA_GROUP + g) * HEAD_DIM:(kh * GQA_GROUP + g + 1) * HEAD_DIM]
                              for g in range(GQA_GROUP)], axis=0).astype(BF16)
        sc_c = jnp.where(valid_c, jnp.dot(q4, ck_ref[s, hs(kh), :].astype(BF16), preferred_element_type=F32), NEG_INF)
        sc_n = jnp.where(valid_n, _nt_dot(q4, kn_ref[s, :, hs(kh)].astype(BF16)), NEG_INF)
        return sc_c, sc_n

    def softmax(kh, sc_c, sc_n):
        m = jnp.maximum(jnp.maximum(jnp.max(sc_c, axis=-1, keepdims=True),
                                    jnp.max(sc_n, axis=-1, keepdims=True)), sinks[kh])
        e_c, e_n = jnp.exp(sc_c - m), jnp.exp(sc_n - m)
        inv = 1.0 / (jnp.sum(e_c, axis=-1, keepdims=True) + jnp.sum(e_n, axis=-1, keepdims=True)
                     + jnp.exp(sinks[kh] - m))
        return (e_c * inv).astype(BF16), (e_n * inv).astype(BF16)

    def values(s, kh, p_c, p_n):
        return (_nt_dot(p_c, cv_ref[s, hs(kh), :].astype(BF16))
                + jnp.dot(p_n, vn_ref[s, :, hs(kh)].astype(BF16), preferred_element_type=F32))

    sc = [scores(s, kh) for s, kh in heads]
    pr = [softmax(kh, *x) for (s, kh), x in zip(heads, sc)]
    o4 = [values(s, kh, *x) for (s, kh), x in zip(heads, pr)]
    for s in range(ns):
        o = jnp.concatenate([o4[s * N_KV_HEADS + kh][g * t:(g + 1) * t]
                             for kh in range(N_KV_HEADS) for g in range(GQA_GROUP)], axis=1)
        o_ref[s] = _rms(o, gain).astype(BF16)


def _swa_sample(q3, kn3, vn3, ck, cv, sinks, out_g, sb):
    ns, t, _ = q3.shape
    blk = lambda a: pl.BlockSpec((sb,) + a.shape[1:], lambda i: (i, 0, 0))
    kern = functools.partial(_swa_sample_kernel, ns=sb, t=t)
    return pl.pallas_call(
        kern,
        grid=(ns // sb,),
        in_specs=[pl.BlockSpec(memory_space=pltpu.SMEM), blk(q3), blk(kn3), blk(vn3), blk(ck), blk(cv),
                  pl.BlockSpec((1, Q_WIDTH), lambda i: (0, 0))],
        out_specs=[blk(q3), blk(ck), blk(cv)],
        out_shape=[jax.ShapeDtypeStruct(q3.shape, BF16), jax.ShapeDtypeStruct(ck.shape, F32),
                   jax.ShapeDtypeStruct(cv.shape, F32)],
        compiler_params=_params(("arbitrary",)),
        name="swa_sample",
    )(sinks, q3, kn3, vn3, ck, cv, out_g)


def _gelu_tanh(x):
    return 0.5 * x * (1.0 + jnp.tanh(math.sqrt(2.0 / math.pi) * (x + 0.044715 * (x * x * x))))


def _merge_stages(h, y, at, wglu_ref, bglu, sg, wout_ref, post_g, xa_g, wq_ref, at_transposed=False):
    g = _gelu_tanh(y)
    lin = jnp.dot(g.astype(BF16), wglu_ref[...], preferred_element_type=F32) + bglu
    yield
    y_ssm = g * (1.0 / (1.0 + jnp.exp(-lin)))
    ssm_n = _rms(y_ssm, sg).astype(BF16)
    at_dims = (((0,), (0,)), ((), ())) if at_transposed else (((1,), (0,)), ((), ()))
    mixed = (jnp.dot(ssm_n, wout_ref[:SSM_WIDTH, :], preferred_element_type=F32)
             + lax.dot_general(at, wout_ref[SSM_WIDTH:, :], at_dims, preferred_element_type=F32))
    yield
    h2 = h + _rms(mixed, post_g)
    qm = jnp.dot(_rms(h2, xa_g).astype(BF16), wq_ref[...], preferred_element_type=F32)
    yield
    return h2, (qm * (MEM_HEAD_DIM ** -0.5)).astype(BF16)


def _merge_tile(*args, **kwargs):
    return _run(_merge_stages(*args, **kwargs))


def _merge_kernel(h_ref, y_ref, at_ref, wglu_ref, bglu_ref, sg_ref, wout_ref, post_g_ref, xa_g_ref, wq_ref,
                  h2_ref, qm_ref):
    h2_ref[...], qm_ref[...] = _merge_tile(h_ref[...], y_ref[...], at_ref[...], wglu_ref, bglu_ref[...], sg_ref[...],
                                           wout_ref, post_g_ref[...], xa_g_ref[...], wq_ref)


def _merge_specs(wglu, wout, wq):
    return [_const_spec(wglu.shape), _const_spec((1, SSM_WIDTH)), _const_spec((1, SSM_WIDTH)), _const_spec(wout.shape),
            _const_spec((1, D_MODEL)), _const_spec((1, D_MODEL)), _const_spec(wq.shape)]


def _merge(h, y, at, merge_w, tm):
    t = h.shape[0]
    row = lambda w: pl.BlockSpec((tm, w), lambda i: (i, 0))
    return pl.pallas_call(
        _merge_kernel,
        grid=(t // tm,),
        in_specs=[row(D_MODEL), row(SSM_WIDTH), row(Q_WIDTH)] + _merge_specs(merge_w[0], merge_w[3], merge_w[6]),
        out_specs=[row(D_MODEL), row(D_MODEL)],
        out_shape=[jax.ShapeDtypeStruct((t, D_MODEL), F32), jax.ShapeDtypeStruct((t, D_MODEL), BF16)],
        compiler_params=_params(("arbitrary",)),
        name="merge",
    )(h, y, at, *merge_w)


def _mem_attn_sample_kernel(q_ref, k_ref, v_ref, o_ref, *, gb, t):
    rows = t * MEM_HEADS
    halves = MEM_HEAD_DIM // LANES
    kv_rows = N_MEM * halves * MEM_HEADS
    period = halves * MEM_HEADS
    lane = lax.broadcasted_iota(jnp.int32, (rows, kv_rows), 1) % period
    head = lax.broadcasted_iota(jnp.int32, (rows, kv_rows), 0) % MEM_HEADS
    in_half = [lane == head + hf * MEM_HEADS for hf in range(halves)]
    for b in range(gb):
        kb = k_ref[b].astype(BF16)
        vb = v_ref[b].astype(BF16)
        part = _nt_dot(q_ref[b], kb)
        sc = jnp.where(in_half[0], part[:rows], 0.0)
        for hf in range(1, halves):
            sc = sc + pltpu.roll(jnp.where(in_half[hf], part[hf * rows:(hf + 1) * rows], 0.0),
                                 kv_rows - hf * MEM_HEADS, 1)
        sc = jnp.where(in_half[0], sc, -jnp.inf)
        e = jnp.exp(sc - jnp.max(sc, axis=-1, keepdims=True))
        inv = 1.0 / jnp.sum(e, axis=-1, keepdims=True)
        e_all = jnp.concatenate([e] + [pltpu.roll(e, hf * MEM_HEADS, 1) for hf in range(1, halves)], axis=0)
        o = jnp.dot(e_all.astype(BF16), vb, preferred_element_type=F32)
        o_ref[b] = (o * jnp.concatenate([inv] * halves, axis=0)).astype(BF16)


def _mem_attn_sample(qm, cache_k, cache_v, n_s, t_s, gb):
    halves = MEM_HEAD_DIM // LANES
    rows = halves * t_s * MEM_HEADS
    kv_rows = N_MEM * halves * MEM_HEADS

    def stored_rows(c):
        c = c.reshape(n_s, N_MEM, MEM_HEADS, halves, LANES).transpose(0, 1, 3, 2, 4)
        return c.reshape(n_s, kv_rows, LANES)

    q = qm.reshape(n_s, t_s, MEM_HEADS, halves, LANES).transpose(0, 3, 1, 2, 4).reshape(n_s, rows, LANES)
    blk = lambda r: pl.BlockSpec((gb, r, LANES), lambda i: (i, 0, 0))
    o = pl.pallas_call(
        functools.partial(_mem_attn_sample_kernel, gb=gb, t=t_s),
        grid=(n_s // gb,),
        in_specs=[blk(rows), blk(kv_rows), blk(kv_rows)],
        out_specs=blk(rows),
        out_shape=jax.ShapeDtypeStruct((n_s, rows, LANES), BF16),
        compiler_params=_params(("arbitrary",)),
        name="mem_attn_sample",
    )(q, stored_rows(cache_k), stored_rows(cache_v))
    o = o.reshape(n_s, halves, t_s, MEM_HEADS, LANES).transpose(0, 2, 3, 1, 4)
    return o.reshape(n_s * t_s, D_MODEL)


def _mem_heads_stages(q, k_ref, v_ref):
    outs = []
    for hh in range(MEM_HEADS):
        hs = slice(hh * MEM_HEAD_DIM, (hh + 1) * MEM_HEAD_DIM)
        sc = _nt_dot(q[:, hs], k_ref[0, :, hs])
        yield
        e = jnp.exp(sc - jnp.max(sc, axis=-1, keepdims=True))
        inv = 1.0 / jnp.sum(e, axis=-1, keepdims=True)
        outs.append((jnp.dot(e.astype(BF16), v_ref[0, :, hs], preferred_element_type=F32) * inv).astype(BF16))
        yield
    return jnp.concatenate(outs, axis=1)


def _ffn_out_stages(h2, om, wo_ref, xa_post, pre_g, wg_ref, wu_ref, wd_ref, post_g, act_ref, rows=slice(None)):
    c = jnp.dot(om, wo_ref[...], preferred_element_type=F32)
    yield
    h3 = h2 + _rms(c, xa_post)
    return (yield from _ffn_stages(h3, pre_g, wg_ref, wu_ref, wd_ref, post_g, act_ref, rows))


def _ffn_out_tile(*args, **kwargs):
    return _run(_ffn_out_stages(*args, **kwargs))


def _ffn_out_specs(wo, ffn_w):
    return [_const_spec(wo.shape), _const_spec((1, D_MODEL)), _const_spec((1, D_MODEL)),
            *[_const_spec(w.shape) for w in ffn_w], _const_spec((1, D_MODEL))]


def _ffn_out_kernel(h_ref, o_ref, wo_ref, xa_post_ref, pre_g_ref, wg_ref, wu_ref, wd_ref, post_g_ref, out_ref, act_ref):
    out_ref[...] = _ffn_out_tile(h_ref[...], o_ref[...], wo_ref, xa_post_ref[...], pre_g_ref[...], wg_ref, wu_ref,
                                 wd_ref, post_g_ref[...], act_ref)


def _ffn_out(h, o, out_w, tm):
    t = h.shape[0]
    row = pl.BlockSpec((tm, D_MODEL), lambda i: (i, 0))
    return pl.pallas_call(
        _ffn_out_kernel,
        grid=(t // tm,),
        in_specs=[row, row] + _ffn_out_specs(out_w[0], out_w[3:6]),
        out_specs=row,
        out_shape=jax.ShapeDtypeStruct((t, D_MODEL), F32),
        scratch_shapes=[pltpu.VMEM((tm, D_FF), BF16)],
        compiler_params=_params(("arbitrary",)),
        name="ffn_out",
    )(h, o, *out_w)


def _post_kernel(h_ref, y_ref, at_ref, k_ref, v_ref,
                 wglu_ref, bglu_ref, sg_ref, wout_ref, post_g_ref, xa_g_ref, wq_ref,
                 wo_ref, xa_post_ref, pre_g_ref, wg_ref, wu_ref, wd_ref, ffn_post_ref, out_ref, act_ref):
    def row_block(r):
        y = jnp.concatenate([y_ref[b, r, :] for b in range(N_LANE_BLOCKS)], axis=1)
        h2, qm = yield from _merge_stages(h_ref[r, :], y, at_ref[:, r], wglu_ref, bglu_ref[...], sg_ref[...],
                                          wout_ref, post_g_ref[...], xa_g_ref[...], wq_ref, at_transposed=True)
        om = yield from _mem_heads_stages(qm, k_ref, v_ref)
        out_ref[r, :] = yield from _ffn_out_stages(h2, om, wo_ref, xa_post_ref[...], pre_g_ref[...], wg_ref, wu_ref,
                                                   wd_ref, ffn_post_ref[...], act_ref, r)

    tm = h_ref.shape[0]
    _interleave([row_block(slice(i * tm // ROW_BLOCKS, (i + 1) * tm // ROW_BLOCKS)) for i in range(ROW_BLOCKS)])


def _post(h, y, at, k3, v3, merge_w, out_w, tm):
    t = h.shape[0]
    tiles_per_batch = t // k3.shape[0] // tm
    row = lambda w: pl.BlockSpec((tm, w), lambda i: (i, 0))
    kv = pl.BlockSpec((1, N_MEM, D_MODEL), lambda i: (i // tiles_per_batch, 0, 0))
    return pl.pallas_call(
        _post_kernel,
        grid=(t // tm,),
        in_specs=([row(D_MODEL), pl.BlockSpec((N_LANE_BLOCKS, tm, LANES), lambda i: (0, i, 0)),
                   pl.BlockSpec((Q_WIDTH, tm), lambda i: (0, i)), kv, kv]
                  + _merge_specs(merge_w[0], merge_w[3], merge_w[6]) + _ffn_out_specs(out_w[0], out_w[3:6])),
        out_specs=row(D_MODEL),
        out_shape=jax.ShapeDtypeStruct((t, D_MODEL), F32),
        scratch_shapes=[pltpu.VMEM((tm, D_FF), BF16)],
        compiler_params=_params(("arbitrary",)),
        name="post",
    )(h, y, at, k3, v3, *merge_w, *out_w)


def _mem_kv_kernel(m_ref, g_ref, wkv_ref, k_ref, v_ref, kb_ref, vb_ref):
    kv = jnp.dot(_rms(m_ref[...], g_ref[...]).astype(BF16), wkv_ref[...], preferred_element_type=F32)
    kb_ref[...] = kv[:, :D_MODEL].astype(BF16)
    vb_ref[...] = kv[:, D_MODEL:].astype(BF16)
    halves = MEM_HEAD_DIM // LANES
    n = m_ref.shape[0]
    for out_ref, base in ((k_ref, 0), (v_ref, D_MODEL)):
        for hh in range(MEM_HEADS):
            for hf in range(halves):
                col = base + hh * MEM_HEAD_DIM + hf * LANES
                out_ref[0, pl.ds(hf * MEM_HEADS + hh, n, stride=halves * MEM_HEADS), :] = kv[:, col:col + LANES]


def _mem_kv(mem, g, wkv, tm):
    t = mem.shape[0]
    row = pl.BlockSpec((tm, D_MODEL), lambda i: (i, 0))
    per_slot = (MEM_HEAD_DIM // LANES) * MEM_HEADS
    stored = pl.BlockSpec((1, tm * per_slot, LANES), lambda i: (i, 0, 0))
    return pl.pallas_call(
        _mem_kv_kernel,
        grid=(t // tm,),
        in_specs=[row, _const_spec((1, D_MODEL)), _const_spec(wkv.shape)],
        out_specs=[stored, stored, row, row],
        out_shape=[jax.ShapeDtypeStruct((t // tm, tm * per_slot, LANES), F32)] * 2
        + [jax.ShapeDtypeStruct((t, D_MODEL), BF16)] * 2,
        compiler_params=_params(("arbitrary",)),
        name="mem_kv",
    )(mem, g, wkv)


def _rope_tables(pos):
    half = ROPE_DIM // 2
    inv = ROPE_THETA ** (-jnp.arange(half, dtype=F32) * (2.0 / ROPE_DIM))
    ang = pos.astype(F32)[:, None] * inv[None, :]
    cos, sin = jnp.cos(ang), jnp.sin(ang)
    n = pos.shape[0]
    pad = jnp.zeros((n, HEAD_DIM - ROPE_DIM), F32)
    zero = jnp.zeros((n, half), F32)
    cos_h = jnp.concatenate([cos, cos, pad + 1.0], axis=1)
    lo_h = jnp.concatenate([-sin, zero, pad], axis=1)
    hi_h = jnp.concatenate([zero, sin, pad], axis=1)
    rep = LANES // HEAD_DIM
    return tuple(jnp.tile(a, (1, rep)) for a in (cos_h, lo_h, hi_h)), (cos.T, sin.T)


def _ffn_weights(w_gate, w_up, w_down):
    return w_gate.astype(BF16), w_up.astype(BF16), w_down.astype(BF16)


def _lane_block_states(st, n):
    st = st.reshape(N_LANE_BLOCKS, n, 2, GROUPS_PER_LANE_BLOCK, SSM_STATE).transpose(2, 1, 0, 3, 4)
    st = st.reshape(2, n, N_SSM_GROUPS, SSM_STATE)
    return st[0], st[1]


def kernel(x_prompt, x_sample, state_ssm_re, state_ssm_im, cache_swa_k, cache_swa_v, cache_mem_k, cache_mem_v, mem_prompt, ffn1_pre_g, ffn1_w_gate, ffn1_w_up, ffn1_w_down, ffn1_post_g, mix_pre_g, w_in, ssm_a_re, ssm_a_im, ssm_log_step, ssm_b_re, ssm_b_im, ssm_c_re, ssm_c_im, ssm_d, ssm_w_glu, ssm_b_glu, attn_sinks, ssm_out_g, attn_out_g, w_out, mix_post_g, mem_norm_g, w_mem_q, w_mem_k, w_mem_v, w_mem_o, xa_pre_g, xa_post_g, ffn2_pre_g, ffn2_w_gate, ffn2_w_up, ffn2_w_down, ffn2_post_g):
    n_p, s_p, _ = x_prompt.shape
    n_s, t_s, _ = x_sample.shape
    tm = TOKEN_TILE
    row = lambda a: a.reshape(1, -1).astype(F32)

    ffn1_w = _ffn_weights(ffn1_w_gate, ffn1_w_up, ffn1_w_down)
    win = w_in.astype(BF16)
    merge_w = (ssm_w_glu.astype(BF16), row(ssm_b_glu), row(ssm_out_g), w_out.astype(BF16), row(mix_post_g),
               row(xa_pre_g), w_mem_q.astype(BF16))
    out_w = (w_mem_o.astype(BF16), row(xa_post_g), row(ffn2_pre_g),
             *_ffn_weights(ffn2_w_gate, ffn2_w_up, ffn2_w_down), row(ffn2_post_g))
    wkv = jnp.concatenate([w_mem_k, w_mem_v], axis=1).astype(BF16)
    d_row = row(ssm_d)
    ssm_args = (ssm_a_re.astype(F32), ssm_a_im.astype(F32), ssm_log_step.astype(F32), ssm_b_re.astype(F32),
                ssm_b_im.astype(F32), ssm_c_re.astype(F32), ssm_c_im.astype(F32))
    sinks = attn_sinks.astype(F32)

    pm_k, pm_v, pm_kb, pm_vb = _mem_kv(mem_prompt.reshape(n_p * N_MEM, D_MODEL), row(mem_norm_g), wkv, N_MEM)

    def from_stored(c):
        halves = MEM_HEAD_DIM // LANES
        return c.reshape(n_p, N_MEM, halves, MEM_HEADS, LANES).transpose(0, 1, 3, 2, 4).reshape(
            n_p, N_MEM, MEM_HEADS, MEM_HEAD_DIM)

    def tokenwise_in(x2, pos_tab):
        return _ffn_in(x2, row(ffn1_pre_g), ffn1_w, row(ffn1_post_g), row(mix_pre_g), win, *pos_tab, tm)

    lc_p = 2 * t_s
    ssm_m, ssm_w, ssm_v, lam_p, lam_s = _ssm_tables(*ssm_args, lc_p)
    _, tab_p_t = _rope_tables(jnp.arange(s_p, dtype=jnp.int32))
    h1, u, k_win, vt_win, at = _ffn_in_swa(x_prompt.reshape(n_p * s_p, D_MODEL), row(ffn1_pre_g), ffn1_w, row(ffn1_post_g),
                                   row(mix_pre_g), win[:, :SSM_WIDTH], win[:, SSM_WIDTH:].T, *tab_p_t, sinks,
                                   attn_out_g.astype(F32), s_p, tm)
    y4, st_p = _ssm_prompt(u.reshape(N_LANE_BLOCKS, n_p, s_p, LANES), ssm_m, ssm_w, ssm_v, lam_p,
                           d_row.reshape(N_LANE_BLOCKS, 1, LANES), lc_p, SSM_TIME_TILE, SSM_LANE_BLOCKS_PER_STEP)
    y_prompt = _post(h1, y4.reshape(N_LANE_BLOCKS, n_p * s_p, LANES), at, pm_kb.reshape(n_p, N_MEM, D_MODEL),
                     pm_vb.reshape(n_p, N_MEM, D_MODEL), merge_w, out_w, tm).reshape(n_p, s_p, D_MODEL)
    p_sre, p_sim = _lane_block_states(st_p, n_p)
    p_wk = k_win.reshape(n_p, WINDOW, N_KV_HEADS, HEAD_DIM)
    p_wv = vt_win.reshape(N_KV_HEADS, HEAD_DIM, n_p, WINDOW).transpose(2, 3, 0, 1)

    pos_s = jnp.tile(PAST_LEN + jnp.arange(t_s, dtype=jnp.int32), n_s)
    h1s, us, qs, ks, vs = tokenwise_in(x_sample.reshape(n_s * t_s, D_MODEL), _rope_tables(pos_s)[0])
    ys, s_sre, s_sim = _ssm_sample(us, state_ssm_re.reshape(n_s, -1).astype(F32), state_ssm_im.reshape(n_s, -1).astype(F32),
                                   ssm_m, ssm_w, ssm_v, lam_s, d_row, t_s)
    win_len = cache_swa_k.shape[1]
    ats, s_wk, s_wv = _swa_sample(qs.reshape(n_s, t_s, Q_WIDTH), ks.reshape(n_s, t_s, KV_WIDTH), vs.reshape(n_s, t_s, KV_WIDTH),
                                  cache_swa_k.transpose(0, 2, 3, 1).reshape(n_s, KV_WIDTH, win_len),
                                  cache_swa_v.transpose(0, 2, 3, 1).reshape(n_s, KV_WIDTH, win_len),
                                  sinks, row(attn_out_g), SWA_SAMPLE_SEQS)
    s_wk = s_wk.reshape(n_s, N_KV_HEADS, HEAD_DIM, win_len).transpose(0, 3, 1, 2)
    s_wv = s_wv.reshape(n_s, N_KV_HEADS, HEAD_DIM, win_len).transpose(0, 3, 1, 2)
    h2s, qms = _merge(h1s, ys, ats.reshape(n_s * t_s, Q_WIDTH), merge_w, tm)
    oms = _mem_attn_sample(qms, cache_mem_k, cache_mem_v, n_s, t_s, MEM_SAMPLE_SEQS)
    y_sample = _ffn_out(h2s, oms, out_w, tm).reshape(n_s, t_s, D_MODEL)

    return (y_prompt, y_sample, p_sre, p_sim, p_wk, p_wv,
            from_stored(pm_k), from_stored(pm_v),
            s_sre.reshape(n_s, N_SSM_GROUPS, SSM_STATE), s_sim.reshape(n_s, N_SSM_GROUPS, SSM_STATE),
            s_wk, s_wv)
```

```python
import functools
import math

import jax
import jax.numpy as jnp
from jax import lax
from jax.experimental import pallas as pl
from jax.experimental.pallas import tpu as pltpu

F32 = jnp.float32
BF16 = jnp.bfloat16

D_MODEL = 1024
PAST_LEN = 16384
SSM_WIDTH = 512
SSM_GROUP = 16
N_SSM_GROUPS = 32
SSM_STATE = 64
HEAD_DIM = 64
N_HEADS = 8
N_KV_HEADS = 2
GQA_GROUP = 4
Q_WIDTH = 512
KV_WIDTH = 128
WINDOW = 128
ROPE_THETA = 500000.0
ROPE_DIM = 16
N_MEM = 256
MEM_HEADS = 4
MEM_HEAD_DIM = 256
D_FF = 2816
RMS_EPS = 1e-6
NEG_INF = -1e30

LANES = 128
FF_CHUNK = 256
ROW_BLOCKS = 2
N_FF_CHUNKS = D_FF // FF_CHUNK
GROUPS_PER_LANE_BLOCK = LANES // SSM_GROUP
N_LANE_BLOCKS = SSM_WIDTH // LANES
STATE_LANES = GROUPS_PER_LANE_BLOCK * SSM_STATE
VMEM_LIMIT = 56 * 1024 * 1024
TOKEN_TILE = 512
SSM_TIME_TILE = 1024
SSM_LANE_BLOCKS_PER_STEP = 2
SWA_SAMPLE_SEQS = 16
MEM_SAMPLE_SEQS = 8


def _rms(x, g):
    return x * lax.rsqrt(jnp.mean(x * x, axis=-1, keepdims=True) + RMS_EPS) * g


def _const_spec(shape):
    nd = len(shape)
    return pl.BlockSpec(shape, lambda *_: (0,) * nd, pipeline_mode=pl.Buffered(1))


def _params(sem):
    return pltpu.CompilerParams(dimension_semantics=sem, vmem_limit_bytes=VMEM_LIMIT)


def _run(gen):
    try:
        while True:
            next(gen)
    except StopIteration as done:
        return done.value


def _interleave(gens):
    live = list(gens)
    while live:
        for g in list(live):
            try:
                next(g)
            except StopIteration:
                live.remove(g)


def _ffn_stages(x, pre_g, wg_ref, wu_ref, wd_ref, post_g, act_ref, rows=slice(None), after_chunk=None):
    xn = _rms(x, pre_g).astype(BF16)
    yield
    for c in range(N_FF_CHUNKS):
        if after_chunk is not None and c == after_chunk[0] + 1:
            after_chunk[1]()
        cols = slice(c * FF_CHUNK, (c + 1) * FF_CHUNK)
        gate = jnp.dot(xn, wg_ref[:, cols], preferred_element_type=F32)
        up = jnp.dot(xn, wu_ref[:, cols], preferred_element_type=F32)
        act = gate * (1.0 / (1.0 + jnp.exp(-gate))) * up
        act_ref[rows, cols] = act.astype(BF16)
        yield
    down = jnp.dot(act_ref[rows, :], wd_ref[...], preferred_element_type=F32)
    yield
    return x + 0.5 * _rms(down, post_g)


def _ffn_tile(*args, **kwargs):
    return _run(_ffn_stages(*args, **kwargs))


def _rope(x, cos, sin_lo, sin_hi):
    w = x.shape[1]
    half = ROPE_DIM // 2
    return (x * cos + pltpu.roll(x, w - half, 1) * sin_lo + pltpu.roll(x, half, 1) * sin_hi)


def _ffn_in_kernel(x_ref, pre_g_ref, wg_ref, wu_ref, wd_ref, post_g_ref, mix_g_ref, win_ref,
                   cos_ref, slo_ref, shi_ref,
                   h_ref, u_ref, q_ref, k_ref, v_ref, act_ref):
    h = _ffn_tile(x_ref[...], pre_g_ref[...], wg_ref, wu_ref, wd_ref, post_g_ref[...], act_ref)
    h_ref[...] = h
    z = jnp.dot(_rms(h, mix_g_ref[...]).astype(BF16), win_ref[...], preferred_element_type=F32)
    u_ref[...] = z[:, :SSM_WIDTH]
    o1 = SSM_WIDTH + Q_WIDTH
    cos, slo, shi = cos_ref[...], slo_ref[...], shi_ref[...]
    rep = Q_WIDTH // LANES
    q = _rope(z[:, SSM_WIDTH:o1], jnp.tile(cos, (1, rep)), jnp.tile(slo, (1, rep)), jnp.tile(shi, (1, rep)))
    q_ref[...] = (q * (HEAD_DIM ** -0.5)).astype(BF16)
    k_ref[...] = _rope(z[:, o1:o1 + KV_WIDTH], cos, slo, shi)
    v_ref[...] = z[:, o1 + KV_WIDTH:]


def _ffn_in(x, pre_g, ffn_w, post_g, mix_g, win, cos, slo, shi, tm):
    t = x.shape[0]
    n_pos_tiles = cos.shape[0] // tm
    row = lambda w: pl.BlockSpec((tm, w), lambda i: (i, 0))
    tab = pl.BlockSpec((tm, LANES), lambda i: (i % n_pos_tiles, 0))
    return pl.pallas_call(
        _ffn_in_kernel,
        grid=(t // tm,),
        in_specs=[row(D_MODEL), _const_spec((1, D_MODEL)), *[_const_spec(w.shape) for w in ffn_w],
                  _const_spec((1, D_MODEL)), _const_spec((1, D_MODEL)), _const_spec(win.shape),
                  tab, tab, tab],
        out_specs=[row(D_MODEL), row(SSM_WIDTH), row(Q_WIDTH), row(KV_WIDTH), row(KV_WIDTH)],
        out_shape=[jax.ShapeDtypeStruct((t, D_MODEL), F32), jax.ShapeDtypeStruct((t, SSM_WIDTH), F32),
                   jax.ShapeDtypeStruct((t, Q_WIDTH), BF16), jax.ShapeDtypeStruct((t, KV_WIDTH), F32),
                   jax.ShapeDtypeStruct((t, KV_WIDTH), F32)],
        scratch_shapes=[pltpu.VMEM((tm, D_FF), BF16)],
        compiler_params=_params(("arbitrary",)),
        name="ffn_in",
    )(x, pre_g, *ffn_w, post_g, mix_g, win, cos, slo, shi)


def _complex_step(s_re, s_im, l_re, l_im, x_re, x_im):
    return l_re * s_re - l_im * s_im + x_re, l_re * s_im + l_im * s_re + x_im


def _ssm_prompt_kernel(u_ref, m_ref, w_ref, v_ref, lam_ref, d_ref, y_ref, st_ref,
                       x_scr, ss_scr, s_scr, *, lc, nb, ncl, nlb):
    tt = pl.program_id(1)
    nrow = nlb * nb
    nq = STATE_LANES // LANES

    @pl.when(tt == 0)
    def _():
        s_scr[...] = jnp.zeros_like(s_scr)

    def piece(b, n, j):
        return u_ref[b, n, pl.ds(j, ncl, stride=lc), :]

    a = []
    for b in range(nlb):
        a.append(jnp.concatenate(
            [jnp.concatenate([piece(b, n, j) for j in range(lc)], axis=1) for n in range(nb)], axis=0).astype(BF16))
        x = jnp.dot(a[b], w_ref[b], preferred_element_type=F32)
        for qq in range(2 * nq):
            for n in range(nb):
                x_scr[qq, pl.ds(b * nb + n, ncl, stride=nrow), :] = x[n * ncl:(n + 1) * ncl, qq * LANES:(qq + 1) * LANES]
    mt = 2 * LANES
    y_intra = [jnp.concatenate(
        [jnp.dot(a[b][:, :(jt + 1) * mt], m_ref[b, :(jt + 1) * mt, jt * mt:(jt + 1) * mt], preferred_element_type=F32)
         for jt in range(lc * LANES // mt)], axis=1) for b in range(nlb)]
    lam_rows = lambda qq: jnp.concatenate(
        [jnp.broadcast_to(lam_ref[b, :, qq * LANES:(qq + 1) * LANES], (nb, LANES)) for b in range(nlb)], axis=0)
    l_re = [lam_rows(qq) for qq in range(nq)]
    l_im = [lam_rows(nq + qq) for qq in range(nq)]

    s = [s_scr[qq] for qq in range(2 * nq)]
    for c in range(ncl):
        rows = slice(c * nrow, (c + 1) * nrow)
        for qq in range(nq):
            ss_scr[qq, rows, :] = s[qq]
            ss_scr[nq + qq, rows, :] = s[nq + qq]
            s[qq], s[nq + qq] = _complex_step(s[qq], s[nq + qq], l_re[qq], l_im[qq],
                                              x_scr[qq, rows, :], x_scr[nq + qq, rows, :])
    for qq in range(2 * nq):
        s_scr[qq] = s[qq]
    for b in range(nlb):
        st_ref[b] = jnp.concatenate([s[qq][b * nb:(b + 1) * nb] for qq in range(2 * nq)], axis=1)
        s_start = jnp.concatenate(
            [jnp.concatenate([ss_scr[qq, pl.ds(b * nb + n, ncl, stride=nrow), :] for n in range(nb)], axis=0)
             for qq in range(2 * nq)], axis=1).astype(BF16)
        y = y_intra[b] + jnp.dot(s_start, v_ref[b], preferred_element_type=F32)
        d = d_ref[b]
        for n in range(nb):
            for j in range(lc):
                y_ref[b, n, pl.ds(j, ncl, stride=lc), :] = (
                    y[n * ncl:(n + 1) * ncl, j * LANES:(j + 1) * LANES] + d * piece(b, n, j))


def _ssm_prompt(u4, m, w, v, lam, d, lc, tl, nlb):
    _, nb, seq, _ = u4.shape
    ncl = tl // lc
    nslab = 2 * STATE_LANES // LANES
    kern = functools.partial(_ssm_prompt_kernel, lc=lc, nb=nb, ncl=ncl, nlb=nlb)
    wspec = lambda a: pl.BlockSpec((nlb,) + a.shape[1:], lambda p, t: (p, 0, 0), pipeline_mode=pl.Buffered(1))
    io = pl.BlockSpec((nlb, nb, tl, LANES), lambda p, t: (p, 0, t, 0))
    return pl.pallas_call(
        kern,
        grid=(N_LANE_BLOCKS // nlb, seq // tl),
        in_specs=[io, wspec(m), wspec(w), wspec(v), wspec(lam), wspec(d)],
        out_specs=[io, pl.BlockSpec((nlb, nb, 2 * STATE_LANES), lambda p, t: (p, 0, 0))],
        out_shape=[jax.ShapeDtypeStruct(u4.shape, F32),
                   jax.ShapeDtypeStruct((N_LANE_BLOCKS, nb, 2 * STATE_LANES), F32)],
        scratch_shapes=[pltpu.VMEM((nslab, nlb * nb * ncl, LANES), F32),
                        pltpu.VMEM((nslab, nlb * nb * ncl, LANES), F32),
                        pltpu.VMEM((nslab, nlb * nb, LANES), F32)],
        compiler_params=_params(("arbitrary", "arbitrary")),
        name="ssm_prompt",
    )(u4, m, w, v, lam, d)


def _ssm_sample_kernel(u_ref, sre_ref, sim_ref, m_ref, w_ref, v_ref, lam_ref, d_ref,
                       y_ref, ore_ref, oim_ref, *, lc, ns):
    def piece(j):
        return u_ref[pl.ds(j, ns, stride=lc), :]

    a = jnp.concatenate([piece(j) for j in range(lc)], axis=1).astype(BF16)
    s_re, s_im = sre_ref[...], sim_ref[...]
    x = jnp.dot(a, w_ref[0], preferred_element_type=F32)
    e_re, e_im = _complex_step(s_re, s_im, lam_ref[0, :, :STATE_LANES], lam_ref[0, :, STATE_LANES:],
                               x[:, :STATE_LANES], x[:, STATE_LANES:])
    ore_ref[...] = e_re
    oim_ref[...] = e_im
    s0 = jnp.concatenate([s_re, s_im], axis=1).astype(BF16)
    y = (jnp.dot(a, m_ref[0], preferred_element_type=F32) + jnp.dot(s0, v_ref[0], preferred_element_type=F32))
    d = d_ref[...]
    for j in range(lc):
        y_ref[pl.ds(j, ns, stride=lc), :] = y[:, j * LANES:(j + 1) * LANES] + d * piece(j)


def _ssm_sample(u, s_re, s_im, m, w, v, lam, d, lc):
    t = u.shape[0]
    ns = t // lc
    kern = functools.partial(_ssm_sample_kernel, lc=lc, ns=ns)
    assert m.shape[1] == 2 * lc * LANES
    n = lc * LANES
    col = lambda rows, width: pl.BlockSpec((rows, width), lambda b: (0, b))
    return pl.pallas_call(
        kern,
        grid=(N_LANE_BLOCKS,),
        in_specs=[col(t, LANES), col(ns, STATE_LANES), col(ns, STATE_LANES),
                  pl.BlockSpec((1, n, n), lambda b: (b, 0, 0)),
                  pl.BlockSpec((1, n, 2 * STATE_LANES), lambda b: (b, 1, 0)),
                  pl.BlockSpec((1, 2 * STATE_LANES, n), lambda b: (b, 0, 0)),
                  pl.BlockSpec((1, 1, 2 * STATE_LANES), lambda b: (b, 0, 0)), col(1, LANES)],
        out_specs=[col(t, LANES), col(ns, STATE_LANES), col(ns, STATE_LANES)],
        out_shape=[jax.ShapeDtypeStruct(u.shape, F32), jax.ShapeDtypeStruct(s_re.shape, F32),
                   jax.ShapeDtypeStruct(s_im.shape, F32)],
        compiler_params=_params(("arbitrary",)),
        name="ssm_sample",
    )(u, s_re, s_im, m, w, v, lam, d)


def _ssm_discretise(a_re, a_im, log_step):
    dt = jnp.exp(log_step)
    mag = jnp.exp(a_re * dt)
    l_re, l_im = mag * jnp.cos(a_im * dt), mag * jnp.sin(a_im * dt)
    den = a_re * a_re + a_im * a_im
    n_re, n_im = l_re - 1.0, l_im
    return l_re, l_im, (n_re * a_re + n_im * a_im) / den, (n_im * a_re - n_re * a_im) / den


def _complex_powers(l_re, l_im, n):
    p_re, p_im = [jnp.ones_like(l_re)], [jnp.zeros_like(l_re)]
    for _ in range(n):
        p_re, p_im = p_re + [p_re[-1] * l_re - p_im[-1] * l_im], p_im + [p_re[-1] * l_im + p_im[-1] * l_re]
    return p_re, p_im


def _split_bf16(x):
    hi = x.astype(BF16)
    return hi, (x - hi.astype(F32)).astype(BF16)


def _dot_split(a, b):
    dot = lambda x, y: jnp.dot(x, y, preferred_element_type=F32)
    return dot(a[0], b[0]) + (dot(a[0], b[1]) + dot(a[1], b[0]))


def _ssm_tables_kernel(ac_re_ref, ac_im_ref, lsc_ref, ar_re_ref, ar_im_ref, lsr_ref, b_re_ref, b_im_ref,
                       c_re_ref, c_im_ref, m_ref, w_ref, v_ref, lam_ref, lam_half_ref, *, lc):
    l_re, l_im, cf_re, cf_im = _ssm_discretise(ac_re_ref[0], ac_im_ref[0], lsc_ref[0])
    b_re, b_im, c_re, c_im = b_re_ref[0], b_im_ref[0], c_re_ref[0], c_im_ref[0]
    c_re_parts, c_im_parts = _split_bf16(c_re), _split_bf16(c_im)
    bb_re = cf_re * b_re - cf_im * b_im
    bb_im = cf_re * b_im + cf_im * b_re
    p_re, p_im = _complex_powers(l_re, l_im, lc)
    lag = []
    for k in range(lc):
        et_re = (p_re[k] * bb_re - p_im[k] * bb_im).T
        et_im = (p_re[k] * bb_im + p_im[k] * bb_re).T
        j = lc - 1 - k
        w_ref[0, j * LANES:(j + 1) * LANES, :STATE_LANES] = et_re.astype(BF16)
        w_ref[0, j * LANES:(j + 1) * LANES, STATE_LANES:] = et_im.astype(BF16)
        lag.append((_dot_split(_split_bf16(et_re), c_re_parts) - _dot_split(_split_bf16(et_im), c_im_parts)).astype(BF16))
        v_ref[0, :STATE_LANES, k * LANES:(k + 1) * LANES] = (p_re[k + 1] * c_re - p_im[k + 1] * c_im).astype(BF16)
        v_ref[0, STATE_LANES:, k * LANES:(k + 1) * LANES] = (-(p_im[k + 1] * c_re + p_re[k + 1] * c_im)).astype(BF16)
    zero = jnp.zeros((LANES, LANES), BF16)
    for j in range(lc):
        for jj in range(lc):
            m_ref[0, j * LANES:(j + 1) * LANES, jj * LANES:(jj + 1) * LANES] = lag[jj - j] if jj >= j else zero
    r_re, r_im, _, _ = _ssm_discretise(ar_re_ref[0], ar_im_ref[0], lsr_ref[0])
    q_re, q_im = _complex_powers(r_re, r_im, lc)
    lam_ref[0] = jnp.concatenate([q_re[lc], q_im[lc]], axis=1)
    lam_half_ref[0] = jnp.concatenate([q_re[lc // 2], q_im[lc // 2]], axis=1)


def _ssm_tables(a_re, a_im, log_step, b_re, b_im, c_re, c_im, lc):
    g, p, h = b_re.shape
    nbk, r = N_LANE_BLOCKS, GROUPS_PER_LANE_BLOCK
    ls = jnp.broadcast_to(log_step[:, None], (g, p))
    cols = [jnp.broadcast_to(x.reshape(nbk, STATE_LANES, 1), (nbk, STATE_LANES, LANES)) for x in (a_re, a_im, ls)]
    rows = [x.reshape(nbk, 1, STATE_LANES) for x in (a_re, a_im, ls)]
    eye = jnp.eye(r, dtype=F32)[None, :, None, :, None]

    def block_diag(x):
        return (x[:, :, :, None, :] * eye).reshape(nbk, STATE_LANES, LANES)

    mats = [block_diag(b_re.reshape(nbk, r, p, h)), block_diag(b_im.reshape(nbk, r, p, h)),
            block_diag(c_re.reshape(nbk, r, h, p).transpose(0, 1, 3, 2)),
            block_diag(c_im.reshape(nbk, r, h, p).transpose(0, 1, 3, 2))]
    spec = lambda shape: pl.BlockSpec((1,) + shape, lambda b: (b, 0, 0))
    n = lc * LANES
    return pl.pallas_call(
        functools.partial(_ssm_tables_kernel, lc=lc),
        grid=(nbk,),
        in_specs=[spec((STATE_LANES, LANES))] * 3 + [spec((1, STATE_LANES))] * 3 + [spec((STATE_LANES, LANES))] * 4,
        out_specs=[spec((n, n)), spec((n, 2 * STATE_LANES)), spec((2 * STATE_LANES, n)),
                   spec((1, 2 * STATE_LANES)), spec((1, 2 * STATE_LANES))],
        out_shape=[jax.ShapeDtypeStruct((nbk, n, n), BF16), jax.ShapeDtypeStruct((nbk, n, 2 * STATE_LANES), BF16),
                   jax.ShapeDtypeStruct((nbk, 2 * STATE_LANES, n), BF16),
                   jax.ShapeDtypeStruct((nbk, 1, 2 * STATE_LANES), F32),
                   jax.ShapeDtypeStruct((nbk, 1, 2 * STATE_LANES), F32)],
        compiler_params=_params(("arbitrary",)),
        name="ssm_tables",
    )(*cols, *rows, *mats)


def _nt_dot(a, b):
    return lax.dot_general(a, b, (((1,), (1,)), ((), ())), preferred_element_type=F32)


def _swa_prompt_tile(sink_ref, qt_ref, kcat, vcat_t, mstd_ref, mfirst_ref, g_ref, ot_ref, first_tile, nblk):
    keys = lambda j: slice(j * WINDOW, (j + 2) * WINDOW)
    zeros = jnp.zeros((HEAD_DIM, WINDOW), BF16)
    sinks = [jnp.concatenate([jnp.full((1, WINDOW), sink_ref[kh * GQA_GROUP + g], F32) for g in range(GQA_GROUP)],
                             axis=1) for kh in range(N_KV_HEADS)]
    chains = [(j, kh) for j in range(nblk) for kh in range(N_KV_HEADS)]

    def scores(j, kh):
        def rhs(g):
            h = kh * GQA_GROUP + g
            q = qt_ref[h * HEAD_DIM:(h + 1) * HEAD_DIM, j * WINDOW:(j + 1) * WINDOW]
            return jnp.concatenate([q, zeros] if kh == 0 else [zeros, q], axis=0)
        r = jnp.concatenate([rhs(g) for g in range(GQA_GROUP)], axis=1)
        vmask = mstd_ref[...]
        if j == 0:
            vmask = jnp.where(first_tile, mfirst_ref[...], vmask)
        valid = jnp.tile(vmask, (1, GQA_GROUP)) > 0.0
        return jnp.where(valid, jnp.dot(kcat[keys(j)], r, preferred_element_type=F32), NEG_INF)

    def softmax(kh, sc):
        m = jnp.maximum(jnp.max(sc, axis=0, keepdims=True), sinks[kh])
        e = jnp.exp(sc - m)
        return e.astype(BF16), 1.0 / (jnp.sum(e, axis=0, keepdims=True) + jnp.exp(sinks[kh] - m))

    sc = [scores(*c) for c in chains]
    pr = [softmax(kh, x) for (j, kh), x in zip(chains, sc)]

    def finish():
        out = [jnp.dot(vcat_t[kh * HEAD_DIM:(kh + 1) * HEAD_DIM, keys(j)], e, preferred_element_type=F32) * inv
               for (j, kh), (e, inv) in zip(chains, pr)]
        gain = g_ref[...]
        for j in range(nblk):
            o = jnp.concatenate([out[j * N_KV_HEADS + kh][:, g * WINDOW:(g + 1) * WINDOW]
                                 for kh in range(N_KV_HEADS) for g in range(GQA_GROUP)], axis=0)
            scale = lax.rsqrt(jnp.mean(o * o, axis=0, keepdims=True) + RMS_EPS)
            ot_ref[:, j * WINDOW:(j + 1) * WINDOW] = (o * scale * gain).astype(BF16)

    return finish


def _swa_masks():
    kj = jnp.arange(2 * WINDOW)[:, None]
    diff = jnp.arange(WINDOW)[None, :] + WINDOW - kj
    std = (diff >= 0) & (diff <= WINDOW)
    return std.astype(F32), (std & (kj >= WINDOW)).astype(F32)


def _ffn_in_swa_kernel(sink_ref, x_ref, pre_g_ref, wg_ref, wu_ref, wd_ref, post_g_ref, mix_g_ref, win_u_ref, wqkv_t_ref,
                       cos_t_ref, sin_t_ref, mstd_ref, mfirst_ref, swa_g_ref,
                       h_ref, u_ref, k_win_ref, vt_win_ref, ot_ref, act_ref, q_scr, kcat_scr, vcat_scr, *, nt, nblk):
    i = pl.program_id(0)
    tm = x_ref.shape[0]

    @pl.when(i == 0)
    def _():
        q_scr[...] = jnp.zeros_like(q_scr)
        kcat_scr[...] = jnp.zeros_like(kcat_scr)
        vcat_scr[...] = jnp.zeros_like(vcat_scr)

    first_tile = lax.rem(i + (nt - 1), nt) == 0
    swa_finish = _swa_prompt_tile(sink_ref, q_scr, kcat_scr[...], vcat_scr[...], mstd_ref, mfirst_ref, swa_g_ref,
                                  ot_ref, first_tile, nblk)

    kcat_scr[:WINDOW, :] = kcat_scr[tm:, :]
    vcat_scr[:, :WINDOW] = vcat_scr[:, tm:]
    half = ROPE_DIM // 2

    def row_block(r, swa_hook):
        n = r.stop - r.start
        h = yield from _ffn_stages(x_ref[r, :], pre_g_ref[...], wg_ref, wu_ref, wd_ref, post_g_ref[...], act_ref, r,
                                   after_chunk=swa_hook)
        h_ref[r, :] = h
        hn = _rms(h, mix_g_ref[...]).astype(BF16)
        z = jnp.dot(hn, win_u_ref[...], preferred_element_type=F32)
        yield
        for b in range(N_LANE_BLOCKS):
            u_ref[b, r, :] = z[:, b * LANES:(b + 1) * LANES]
        zt = _nt_dot(wqkv_t_ref[...], hn)
        yield
        vt = zt[Q_WIDTH + KV_WIDTH:]
        if r.stop == tm:
            vt_win_ref[...] = vt[:, -WINDOW:]
        qk = zt[:Q_WIDTH + KV_WIDTH].reshape(N_HEADS + N_KV_HEADS, HEAD_DIM, n)
        x1, x2 = qk[:, :half], qk[:, half:ROPE_DIM]
        cos, sin = cos_t_ref[:, r][None], sin_t_ref[:, r][None]
        qk = jnp.concatenate([x1 * cos - x2 * sin, x2 * cos + x1 * sin, qk[:, ROPE_DIM:]], axis=1)
        k = qk[N_HEADS:].reshape(KV_WIDTH, n).T
        if r.stop == tm:
            k_win_ref[...] = k[-WINDOW:]
        shifted = slice(WINDOW + r.start, WINDOW + r.stop)
        kcat_scr[shifted, :] = k.astype(BF16)
        vcat_scr[:, shifted] = vt.astype(BF16)
        q_scr[:, r] = (qk[:N_HEADS].reshape(Q_WIDTH, n) * (HEAD_DIM ** -0.5)).astype(BF16)

    blocks = [slice(b * tm // ROW_BLOCKS, (b + 1) * tm // ROW_BLOCKS) for b in range(ROW_BLOCKS)]
    _interleave([row_block(r, (N_FF_CHUNKS - 3, swa_finish) if r.start == 0 else None) for r in blocks])


def _ffn_in_swa(x, pre_g, ffn_w, post_g, mix_g, win_u, wqkv_t, cos_t, sin_t, sinks, swa_g, seq, tm):
    t = x.shape[0]
    n_tiles, nt, nblk, n_seq = t // tm, seq // tm, tm // WINDOW, t // seq
    half = ROPE_DIM // 2
    mstd, mfirst = _swa_masks()
    gain = jnp.broadcast_to(swa_g.reshape(Q_WIDTH, 1), (Q_WIDTH, WINDOW))
    cur = lambda i: jnp.minimum(i, n_tiles - 1)
    row = lambda w: pl.BlockSpec((tm, w), lambda i: (cur(i), 0))
    tab_t = pl.BlockSpec((half, tm), lambda i: (0, cur(i) % nt))
    return pl.pallas_call(
        functools.partial(_ffn_in_swa_kernel, nt=nt, nblk=nblk),
        grid=(n_tiles + 1,),
        in_specs=[pl.BlockSpec(memory_space=pltpu.SMEM), row(D_MODEL), _const_spec((1, D_MODEL)),
                  *[_const_spec(w.shape) for w in ffn_w], _const_spec((1, D_MODEL)), _const_spec((1, D_MODEL)),
                  _const_spec(win_u.shape), _const_spec(wqkv_t.shape), tab_t, tab_t,
                  _const_spec(mstd.shape), _const_spec(mfirst.shape), _const_spec(gain.shape)],
        out_specs=[row(D_MODEL), pl.BlockSpec((N_LANE_BLOCKS, tm, LANES), lambda i: (0, cur(i), 0)),
                   pl.BlockSpec((WINDOW, KV_WIDTH), lambda i: (cur(i) // nt, 0)),
                   pl.BlockSpec((KV_WIDTH, WINDOW), lambda i: (0, cur(i) // nt)),
                   pl.BlockSpec((Q_WIDTH, tm), lambda i: (0, jnp.maximum(i - 1, 0)))],
        out_shape=[jax.ShapeDtypeStruct((t, D_MODEL), F32), jax.ShapeDtypeStruct((N_LANE_BLOCKS, t, LANES), F32),
                   jax.ShapeDtypeStruct((n_seq * WINDOW, KV_WIDTH), F32),
                   jax.ShapeDtypeStruct((KV_WIDTH, n_seq * WINDOW), F32), jax.ShapeDtypeStruct((Q_WIDTH, t), BF16)],
        scratch_shapes=[pltpu.VMEM((tm, D_FF), BF16), pltpu.VMEM((Q_WIDTH, tm), BF16),
                        pltpu.VMEM((WINDOW + tm, KV_WIDTH), BF16), pltpu.VMEM((KV_WIDTH, WINDOW + tm), BF16)],
        compiler_params=_params(("arbitrary",)),
        name="ffn_in_swa",
    )(sinks, x, pre_g, *ffn_w, post_g, mix_g, win_u, wqkv_t, cos_t, sin_t, mstd, mfirst, gain)


def _swa_sample_kernel(sink_ref, q_ref, kn_ref, vn_ref, ck_ref, cv_ref, g_ref, o_ref, nk_ref, nv_ref, *, ns, t):
    rows = GQA_GROUP * t
    tok = lax.broadcasted_iota(jnp.int32, (rows, WINDOW), 0) % t
    valid_c = lax.broadcasted_iota(jnp.int32, (rows, WINDOW), 1) >= tok
    tok_n = lax.broadcasted_iota(jnp.int32, (rows, t), 0) % t
    valid_n = lax.broadcasted_iota(jnp.int32, (rows, t), 1) <= tok_n
    gain = g_ref[...]

    sinks = [jnp.concatenate([jnp.full((t, 1), sink_ref[kh * GQA_GROUP + g], F32) for g in range(GQA_GROUP)], axis=0)
             for kh in range(N_KV_HEADS)]
    heads = [(s, kh) for s in range(ns) for kh in range(N_KV_HEADS)]
    hs = lambda kh: slice(kh * HEAD_DIM, (kh + 1) * HEAD_DIM)

    newest = lax.broadcasted_iota(jnp.int32, (KV_WIDTH, WINDOW), 1) >= WINDOW - t
    pad = jnp.zeros((WINDOW - 2 * t, KV_WIDTH), F32)

    def shifted(cache_ref, new_ref, s):
        new_rows = jnp.concatenate([pad, jnp.zeros((t, KV_WIDTH), F32), new_ref[s]], axis=0)
        return jnp.where(newest, new_rows.T, pltpu.roll(cache_ref[s], WINDOW - t, 1))

    for s in range(ns):
        nk_ref[s] = shifted(ck_ref, kn_ref, s)
        nv_ref[s] = shifted(cv_ref, vn_ref, s)

    def scores(s, kh):
        q = q_ref[s].astype(F32)
        q4 = jnp.concatenate([q[:, (kh * GQA_GROUP + g) * HEAD_DIM:(kh * GQA_GROUP + g + 1) * HEAD_DIM]
                              for g in range(GQA_GROUP)], axis=0).astype(BF16)
        sc_c = jnp.where(valid_c, jnp.dot(q4, ck_ref[s, hs(kh), :].astype(BF16), preferred_element_type=F32), NEG_INF)
        sc_n = jnp.where(valid_n, _nt_dot(q4, kn_ref[s, :, hs(kh)].astype(BF16)), NEG_INF)
        return sc_c, sc_n

    def softmax(kh, sc_c, sc_n):
        m = jnp.maximum(jnp.maximum(jnp.max(sc_c, axis=-1, keepdims=True),
                                    jnp.max(sc_n, axis=-1, keepdims=True)), sinks[kh])
        e_c, e_n = jnp.exp(sc_c - m), jnp.exp(sc_n - m)
        inv = 1.0 / (jnp.sum(e_c, axis=-1, keepdims=True) + jnp.sum(e_n, axis=-1, keepdims=True)
                     + jnp.exp(sinks[kh] - m))
        return (e_c * inv).astype(BF16), (e_n * inv).astype(BF16)

    def values(s, kh, p_c, p_n):
        return (_nt_dot(p_c, cv_ref[s, hs(kh), :].astype(BF16))
                + jnp.dot(p_n, vn_ref[s, :, hs(kh)].astype(BF16), preferred_element_type=F32))

    sc = [scores(s, kh) for s, kh in heads]
    pr = [softmax(kh, *x) for (s, kh), x in zip(heads, sc)]
    o4 = [values(s, kh, *x) for (s, kh), x in zip(heads, pr)]
    for s in range(ns):
        o = jnp.concatenate([o4[s * N_KV_HEADS + kh][g * t:(g + 1) * t]
                             for kh in range(N_KV_HEADS) for g in range(GQA_GROUP)], axis=1)
        o_ref[s] = _rms(o, gain).astype(BF16)


def _swa_sample(q3, kn3, vn3, ck, cv, sinks, out_g, sb):
    ns, t, _ = q3.shape
    blk = lambda a: pl.BlockSpec((sb,) + a.shape[1:], lambda i: (i, 0, 0))
    kern = functools.partial(_swa_sample_kernel, ns=sb, t=t)
    return pl.pallas_call(
        kern,
        grid=(ns // sb,),
        in_specs=[pl.BlockSpec(memory_space=pltpu.SMEM), blk(q3), blk(kn3), blk(vn3), blk(ck), blk(cv),
                  pl.BlockSpec((1, Q_WIDTH), lambda i: (0, 0))],
        out_specs=[blk(q3), blk(ck), blk(cv)],
        out_shape=[jax.ShapeDtypeStruct(q3.shape, BF16), jax.ShapeDtypeStruct(ck.shape, F32),
                   jax.ShapeDtypeStruct(cv.shape, F32)],
        compiler_params=_params(("arbitrary",)),
        name="swa_sample",
    )(sinks, q3, kn3, vn3, ck, cv, out_g)


def _gelu_tanh(x):
    return 0.5 * x * (1.0 + jnp.tanh(math.sqrt(2.0 / math.pi) * (x + 0.044715 * (x * x * x))))


def _merge_stages(h, y, at, wglu_ref, bglu, sg, wout_ref, post_g, xa_g, wq_ref, at_transposed=False):
    g = _gelu_tanh(y)
    lin = jnp.dot(g.astype(BF16), wglu_ref[...], preferred_element_type=F32) + bglu
    yield
    y_ssm = g * (1.0 / (1.0 + jnp.exp(-lin)))
    ssm_n = _rms(y_ssm, sg).astype(BF16)
    at_dims = (((0,), (0,)), ((), ())) if at_transposed else (((1,), (0,)), ((), ()))
    mixed = (jnp.dot(ssm_n, wout_ref[:SSM_WIDTH, :], preferred_element_type=F32)
             + lax.dot_general(at, wout_ref[SSM_WIDTH:, :], at_dims, preferred_element_type=F32))
    yield
    h2 = h + _rms(mixed, post_g)
    qm = jnp.dot(_rms(h2, xa_g).astype(BF16), wq_ref[...], preferred_element_type=F32)
    yield
    return h2, (qm * (MEM_HEAD_DIM ** -0.5)).astype(BF16)


def _merge_tile(*args, **kwargs):
    return _run(_merge_stages(*args, **kwargs))


def _merge_kernel(h_ref, y_ref, at_ref, wglu_ref, bglu_ref, sg_ref, wout_ref, post_g_ref, xa_g_ref, wq_ref,
                  h2_ref, qm_ref):
    h2_ref[...], qm_ref[...] = _merge_tile(h_ref[...], y_ref[...], at_ref[...], wglu_ref, bglu_ref[...], sg_ref[...],
                                           wout_ref, post_g_ref[...], xa_g_ref[...], wq_ref)


def _merge_specs(wglu, wout, wq):
    return [_const_spec(wglu.shape), _const_spec((1, SSM_WIDTH)), _const_spec((1, SSM_WIDTH)), _const_spec(wout.shape),
            _const_spec((1, D_MODEL)), _const_spec((1, D_MODEL)), _const_spec(wq.shape)]


def _merge(h, y, at, merge_w, tm):
    t = h.shape[0]
    row = lambda w: pl.BlockSpec((tm, w), lambda i: (i, 0))
    return pl.pallas_call(
        _merge_kernel,
        grid=(t // tm,),
        in_specs=[row(D_MODEL), row(SSM_WIDTH), row(Q_WIDTH)] + _merge_specs(merge_w[0], merge_w[3], merge_w[6]),
        out_specs=[row(D_MODEL), row(D_MODEL)],
        out_shape=[jax.ShapeDtypeStruct((t, D_MODEL), F32), jax.ShapeDtypeStruct((t, D_MODEL), BF16)],
        compiler_params=_params(("arbitrary",)),
        name="merge",
    )(h, y, at, *merge_w)


def _mem_attn_sample_kernel(q_ref, k_ref, v_ref, o_ref, *, gb, t):
    rows = t * MEM_HEADS
    halves = MEM_HEAD_DIM // LANES
    kv_rows = N_MEM * halves * MEM_HEADS
    period = halves * MEM_HEADS
    lane = lax.broadcasted_iota(jnp.int32, (rows, kv_rows), 1) % period
    head = lax.broadcasted_iota(jnp.int32, (rows, kv_rows), 0) % MEM_HEADS
    in_half = [lane == head + hf * MEM_HEADS for hf in range(halves)]
    for b in range(gb):
        kb = k_ref[b].astype(BF16)
        vb = v_ref[b].astype(BF16)
        part = _nt_dot(q_ref[b], kb)
        sc = jnp.where(in_half[0], part[:rows], 0.0)
        for hf in range(1, halves):
            sc = sc + pltpu.roll(jnp.where(in_half[hf], part[hf * rows:(hf + 1) * rows], 0.0),
                                 kv_rows - hf * MEM_HEADS, 1)
        sc = jnp.where(in_half[0], sc, -jnp.inf)
        e = jnp.exp(sc - jnp.max(sc, axis=-1, keepdims=True))
        inv = 1.0 / jnp.sum(e, axis=-1, keepdims=True)
        e_all = jnp.concatenate([e] + [pltpu.roll(e, hf * MEM_HEADS, 1) for hf in range(1, halves)], axis=0)
        o = jnp.dot(e_all.astype(BF16), vb, preferred_element_type=F32)
        o_ref[b] = (o * jnp.concatenate([inv] * halves, axis=0)).astype(BF16)


def _mem_attn_sample(qm, cache_k, cache_v, n_s, t_s, gb):
    halves = MEM_HEAD_DIM // LANES
    rows = halves * t_s * MEM_HEADS
    kv_rows = N_MEM * halves * MEM_HEADS

    def stored_rows(c):
        c = c.reshape(n_s, N_MEM, MEM_HEADS, halves, LANES).transpose(0, 1, 3, 2, 4)
        return c.reshape(n_s, kv_rows, LANES)

    q = qm.reshape(n_s, t_s, MEM_HEADS, halves, LANES).transpose(0, 3, 1, 2, 4).reshape(n_s, rows, LANES)
    blk = lambda r: pl.BlockSpec((gb, r, LANES), lambda i: (i, 0, 0))
    o = pl.pallas_call(
        functools.partial(_mem_attn_sample_kernel, gb=gb, t=t_s),
        grid=(n_s // gb,),
        in_specs=[blk(rows), blk(kv_rows), blk(kv_rows)],
        out_specs=blk(rows),
        out_shape=jax.ShapeDtypeStruct((n_s, rows, LANES), BF16),
        compiler_params=_params(("arbitrary",)),
        name="mem_attn_sample",
    )(q, stored_rows(cache_k), stored_rows(cache_v))
    o = o.reshape(n_s, halves, t_s, MEM_HEADS, LANES).transpose(0, 2, 3, 1, 4)
    return o.reshape(n_s * t_s, D_MODEL)


def _mem_heads_stages(q, k_ref, v_ref):
    heads = [slice(hh * MEM_HEAD_DIM, (hh + 1) * MEM_HEAD_DIM) for hh in range(MEM_HEADS)]
    scs = [_nt_dot(q[:, hs], k_ref[0, :, hs]) for hs in heads]
    yield
    outs = []
    for hs, sc in zip(heads, scs):
        e = jnp.exp(sc - jnp.max(sc, axis=-1, keepdims=True))
        inv = 1.0 / jnp.sum(e, axis=-1, keepdims=True)
        outs.append((jnp.dot(e.astype(BF16), v_ref[0, :, hs], preferred_element_type=F32) * inv).astype(BF16))
        yield
    return jnp.concatenate(outs, axis=1)


def _ffn_out_stages(h2, om, wo_ref, xa_post, pre_g, wg_ref, wu_ref, wd_ref, post_g, act_ref, rows=slice(None)):
    c = jnp.dot(om, wo_ref[...], preferred_element_type=F32)
    yield
    h3 = h2 + _rms(c, xa_post)
    return (yield from _ffn_stages(h3, pre_g, wg_ref, wu_ref, wd_ref, post_g, act_ref, rows))


def _ffn_out_tile(*args, **kwargs):
    return _run(_ffn_out_stages(*args, **kwargs))


def _ffn_out_specs(wo, ffn_w):
    return [_const_spec(wo.shape), _const_spec((1, D_MODEL)), _const_spec((1, D_MODEL)),
            *[_const_spec(w.shape) for w in ffn_w], _const_spec((1, D_MODEL))]


def _ffn_out_kernel(h_ref, o_ref, wo_ref, xa_post_ref, pre_g_ref, wg_ref, wu_ref, wd_ref, post_g_ref, out_ref, act_ref):
    out_ref[...] = _ffn_out_tile(h_ref[...], o_ref[...], wo_ref, xa_post_ref[...], pre_g_ref[...], wg_ref, wu_ref,
                                 wd_ref, post_g_ref[...], act_ref)


def _ffn_out(h, o, out_w, tm):
    t = h.shape[0]
    row = pl.BlockSpec((tm, D_MODEL), lambda i: (i, 0))
    return pl.pallas_call(
        _ffn_out_kernel,
        grid=(t // tm,),
        in_specs=[row, row] + _ffn_out_specs(out_w[0], out_w[3:6]),
        out_specs=row,
        out_shape=jax.ShapeDtypeStruct((t, D_MODEL), F32),
        scratch_shapes=[pltpu.VMEM((tm, D_FF), BF16)],
        compiler_params=_params(("arbitrary",)),
        name="ffn_out",
    )(h, o, *out_w)


def _post_kernel(h_ref, y_ref, at_ref, k_ref, v_ref,
                 wglu_ref, bglu_ref, sg_ref, wout_ref, post_g_ref, xa_g_ref, wq_ref,
                 wo_ref, xa_post_ref, pre_g_ref, wg_ref, wu_ref, wd_ref, ffn_post_ref, out_ref, act_ref):
    def row_block(r):
        y = jnp.concatenate([y_ref[b, r, :] for b in range(N_LANE_BLOCKS)], axis=1)
        h2, qm = yield from _merge_stages(h_ref[r, :], y, at_ref[:, r], wglu_ref, bglu_ref[...], sg_ref[...],
                                          wout_ref, post_g_ref[...], xa_g_ref[...], wq_ref, at_transposed=True)
        om = yield from _mem_heads_stages(qm, k_ref, v_ref)
        out_ref[r, :] = yield from _ffn_out_stages(h2, om, wo_ref, xa_post_ref[...], pre_g_ref[...], wg_ref, wu_ref,
                                                   wd_ref, ffn_post_ref[...], act_ref, r)

    tm = h_ref.shape[0]
    _interleave([row_block(slice(i * tm // ROW_BLOCKS, (i + 1) * tm // ROW_BLOCKS)) for i in range(ROW_BLOCKS)])


def _post(h, y, at, k3, v3, merge_w, out_w, tm):
    t = h.shape[0]
    tiles_per_batch = t // k3.shape[0] // tm
    row = lambda w: pl.BlockSpec((tm, w), lambda i: (i, 0))
    kv = pl.BlockSpec((1, N_MEM, D_MODEL), lambda i: (i // tiles_per_batch, 0, 0))
    return pl.pallas_call(
        _post_kernel,
        grid=(t // tm,),
        in_specs=([row(D_MODEL), pl.BlockSpec((N_LANE_BLOCKS, tm, LANES), lambda i: (0, i, 0)),
                   pl.BlockSpec((Q_WIDTH, tm), lambda i: (0, i)), kv, kv]
                  + _merge_specs(merge_w[0], merge_w[3], merge_w[6]) + _ffn_out_specs(out_w[0], out_w[3:6])),
        out_specs=row(D_MODEL),
        out_shape=jax.ShapeDtypeStruct((t, D_MODEL), F32),
        scratch_shapes=[pltpu.VMEM((tm, D_FF), BF16)],
        compiler_params=_params(("arbitrary",)),
        name="post",
    )(h, y, at, k3, v3, *merge_w, *out_w)


def _mem_kv_kernel(m_ref, g_ref, wkv_ref, k_ref, v_ref, kb_ref, vb_ref):
    kv = jnp.dot(_rms(m_ref[...], g_ref[...]).astype(BF16), wkv_ref[...], preferred_element_type=F32)
    kb_ref[...] = kv[:, :D_MODEL].astype(BF16)
    vb_ref[...] = kv[:, D_MODEL:].astype(BF16)
    halves = MEM_HEAD_DIM // LANES
    n = m_ref.shape[0]
    for out_ref, base in ((k_ref, 0), (v_ref, D_MODEL)):
        for hh in range(MEM_HEADS):
            for hf in range(halves):
                col = base + hh * MEM_HEAD_DIM + hf * LANES
                out_ref[0, pl.ds(hf * MEM_HEADS + hh, n, stride=halves * MEM_HEADS), :] = kv[:, col:col + LANES]


def _mem_kv(mem, g, wkv, tm):
    t = mem.shape[0]
    row = pl.BlockSpec((tm, D_MODEL), lambda i: (i, 0))
    per_slot = (MEM_HEAD_DIM // LANES) * MEM_HEADS
    stored = pl.BlockSpec((1, tm * per_slot, LANES), lambda i: (i, 0, 0))
    return pl.pallas_call(
        _mem_kv_kernel,
        grid=(t // tm,),
        in_specs=[row, _const_spec((1, D_MODEL)), _const_spec(wkv.shape)],
        out_specs=[stored, stored, row, row],
        out_shape=[jax.ShapeDtypeStruct((t // tm, tm * per_slot, LANES), F32)] * 2
        + [jax.ShapeDtypeStruct((t, D_MODEL), BF16)] * 2,
        compiler_params=_params(("arbitrary",)),
        name="mem_kv",
    )(mem, g, wkv)


def _rope_tables(pos):
    half = ROPE_DIM // 2
    inv = ROPE_THETA ** (-jnp.arange(half, dtype=F32) * (2.0 / ROPE_DIM))
    ang = pos.astype(F32)[:, None] * inv[None, :]
    cos, sin = jnp.cos(ang), jnp.sin(ang)
    n = pos.shape[0]
    pad = jnp.zeros((n, HEAD_DIM - ROPE_DIM), F32)
    zero = jnp.zeros((n, half), F32)
    cos_h = jnp.concatenate([cos, cos, pad + 1.0], axis=1)
    lo_h = jnp.concatenate([-sin, zero, pad], axis=1)
    hi_h = jnp.concatenate([zero, sin, pad], axis=1)
    rep = LANES // HEAD_DIM
    return tuple(jnp.tile(a, (1, rep)) for a in (cos_h, lo_h, hi_h)), (cos.T, sin.T)


def _ffn_weights(w_gate, w_up, w_down):
    return w_gate.astype(BF16), w_up.astype(BF16), w_down.astype(BF16)


def _lane_block_states(st, n):
    st = st.reshape(N_LANE_BLOCKS, n, 2, GROUPS_PER_LANE_BLOCK, SSM_STATE).transpose(2, 1, 0, 3, 4)
    st = st.reshape(2, n, N_SSM_GROUPS, SSM_STATE)
    return st[0], st[1]


def kernel(x_prompt, x_sample, state_ssm_re, state_ssm_im, cache_swa_k, cache_swa_v, cache_mem_k, cache_mem_v, mem_prompt, ffn1_pre_g, ffn1_w_gate, ffn1_w_up, ffn1_w_down, ffn1_post_g, mix_pre_g, w_in, ssm_a_re, ssm_a_im, ssm_log_step, ssm_b_re, ssm_b_im, ssm_c_re, ssm_c_im, ssm_d, ssm_w_glu, ssm_b_glu, attn_sinks, ssm_out_g, attn_out_g, w_out, mix_post_g, mem_norm_g, w_mem_q, w_mem_k, w_mem_v, w_mem_o, xa_pre_g, xa_post_g, ffn2_pre_g, ffn2_w_gate, ffn2_w_up, ffn2_w_down, ffn2_post_g):
    n_p, s_p, _ = x_prompt.shape
    n_s, t_s, _ = x_sample.shape
    tm = TOKEN_TILE
    row = lambda a: a.reshape(1, -1).astype(F32)

    ffn1_w = _ffn_weights(ffn1_w_gate, ffn1_w_up, ffn1_w_down)
    win = w_in.astype(BF16)
    merge_w = (ssm_w_glu.astype(BF16), row(ssm_b_glu), row(ssm_out_g), w_out.astype(BF16), row(mix_post_g),
               row(xa_pre_g), w_mem_q.astype(BF16))
    out_w = (w_mem_o.astype(BF16), row(xa_post_g), row(ffn2_pre_g),
             *_ffn_weights(ffn2_w_gate, ffn2_w_up, ffn2_w_down), row(ffn2_post_g))
    wkv = jnp.concatenate([w_mem_k, w_mem_v], axis=1).astype(BF16)
    d_row = row(ssm_d)
    ssm_args = (ssm_a_re.astype(F32), ssm_a_im.astype(F32), ssm_log_step.astype(F32), ssm_b_re.astype(F32),
                ssm_b_im.astype(F32), ssm_c_re.astype(F32), ssm_c_im.astype(F32))
    sinks = attn_sinks.astype(F32)

    pm_k, pm_v, pm_kb, pm_vb = _mem_kv(mem_prompt.reshape(n_p * N_MEM, D_MODEL), row(mem_norm_g), wkv, N_MEM)

    def from_stored(c):
        halves = MEM_HEAD_DIM // LANES
        return c.reshape(n_p, N_MEM, halves, MEM_HEADS, LANES).transpose(0, 1, 3, 2, 4).reshape(
            n_p, N_MEM, MEM_HEADS, MEM_HEAD_DIM)

    def tokenwise_in(x2, pos_tab):
        return _ffn_in(x2, row(ffn1_pre_g), ffn1_w, row(ffn1_post_g), row(mix_pre_g), win, *pos_tab, tm)

    lc_p = 2 * t_s
    ssm_m, ssm_w, ssm_v, lam_p, lam_s = _ssm_tables(*ssm_args, lc_p)
    _, tab_p_t = _rope_tables(jnp.arange(s_p, dtype=jnp.int32))
    h1, u, k_win, vt_win, at = _ffn_in_swa(x_prompt.reshape(n_p * s_p, D_MODEL), row(ffn1_pre_g), ffn1_w, row(ffn1_post_g),
                                   row(mix_pre_g), win[:, :SSM_WIDTH], win[:, SSM_WIDTH:].T, *tab_p_t, sinks,
                                   attn_out_g.astype(F32), s_p, tm)
    y4, st_p = _ssm_prompt(u.reshape(N_LANE_BLOCKS, n_p, s_p, LANES), ssm_m, ssm_w, ssm_v, lam_p,
                           d_row.reshape(N_LANE_BLOCKS, 1, LANES), lc_p, SSM_TIME_TILE, SSM_LANE_BLOCKS_PER_STEP)
    y_prompt = _post(h1, y4.reshape(N_LANE_BLOCKS, n_p * s_p, LANES), at, pm_kb.reshape(n_p, N_MEM, D_MODEL),
                     pm_vb.reshape(n_p, N_MEM, D_MODEL), merge_w, out_w, tm).reshape(n_p, s_p, D_MODEL)
    p_sre, p_sim = _lane_block_states(st_p, n_p)
    p_wk = k_win.reshape(n_p, WINDOW, N_KV_HEADS, HEAD_DIM)
    p_wv = vt_win.reshape(N_KV_HEADS, HEAD_DIM, n_p, WINDOW).transpose(2, 3, 0, 1)

    pos_s = jnp.tile(PAST_LEN + jnp.arange(t_s, dtype=jnp.int32), n_s)
    h1s, us, qs, ks, vs = tokenwise_in(x_sample.reshape(n_s * t_s, D_MODEL), _rope_tables(pos_s)[0])
    ys, s_sre, s_sim = _ssm_sample(us, state_ssm_re.reshape(n_s, -1).astype(F32), state_ssm_im.reshape(n_s, -1).astype(F32),
                                   ssm_m, ssm_w, ssm_v, lam_s, d_row, t_s)
    win_len = cache_swa_k.shape[1]
    ats, s_wk, s_wv = _swa_sample(qs.reshape(n_s, t_s, Q_WIDTH), ks.reshape(n_s, t_s, KV_WIDTH), vs.reshape(n_s, t_s, KV_WIDTH),
                                  cache_swa_k.transpose(0, 2, 3, 1).reshape(n_s, KV_WIDTH, win_len),
                                  cache_swa_v.transpose(0, 2, 3, 1).reshape(n_s, KV_WIDTH, win_len),
                                  sinks, row(attn_out_g), SWA_SAMPLE_SEQS)
    s_wk = s_wk.reshape(n_s, N_KV_HEADS, HEAD_DIM, win_len).transpose(0, 3, 1, 2)
    s_wv = s_wv.reshape(n_s, N_KV_HEADS, HEAD_DIM, win_len).transpose(0, 3, 1, 2)
    h2s, qms = _merge(h1s, ys, ats.reshape(n_s * t_s, Q_WIDTH), merge_w, tm)
    oms = _mem_attn_sample(qms, cache_mem_k, cache_mem_v, n_s, t_s, MEM_SAMPLE_SEQS)
    y_sample = _ffn_out(h2s, oms, out_w, tm).reshape(n_s, t_s, D_MODEL)

    return (y_prompt, y_sample, p_sre, p_sim, p_wk, p_wv,
            from_stored(pm_k), from_stored(pm_v),
            s_sre.reshape(n_s, N_SSM_GROUPS, SSM_STATE), s_sim.reshape(n_s, N_SSM_GROUPS, SSM_STATE),
            s_wk, s_wv)
```

```python
import functools
import math

import jax
import jax.numpy as jnp
from jax import lax
from jax.experimental import pallas as pl
from jax.experimental.pallas import tpu as pltpu

F32 = jnp.float32
BF16 = jnp.bfloat16

D_MODEL = 1024
PAST_LEN = 16384
SSM_WIDTH = 512
SSM_GROUP = 16
N_SSM_GROUPS = 32
SSM_STATE = 64
HEAD_DIM = 64
N_HEADS = 8
N_KV_HEADS = 2
GQA_GROUP = 4
Q_WIDTH = 512
KV_WIDTH = 128
WINDOW = 128
ROPE_THETA = 500000.0
ROPE_DIM = 16
N_MEM = 256
MEM_HEADS = 4
MEM_HEAD_DIM = 256
D_FF = 2816
RMS_EPS = 1e-6
NEG_INF = -1e30

LANES = 128
FF_CHUNK = 256
ROW_BLOCKS = 2
N_FF_CHUNKS = D_FF // FF_CHUNK
GROUPS_PER_LANE_BLOCK = LANES // SSM_GROUP
N_LANE_BLOCKS = SSM_WIDTH // LANES
STATE_LANES = GROUPS_PER_LANE_BLOCK * SSM_STATE
VMEM_LIMIT = 56 * 1024 * 1024
TOKEN_TILE = 512
SSM_TIME_TILE = 1024
SSM_LANE_BLOCKS_PER_STEP = 2
SWA_SAMPLE_SEQS = 16
MEM_SAMPLE_SEQS = 2
MEM_SAMPLE_BUFFERS = 4


def _rms(x, g):
    return x * lax.rsqrt(jnp.mean(x * x, axis=-1, keepdims=True) + RMS_EPS) * g


def _const_spec(shape):
    nd = len(shape)
    return pl.BlockSpec(shape, lambda *_: (0,) * nd, pipeline_mode=pl.Buffered(1))


def _params(sem):
    return pltpu.CompilerParams(dimension_semantics=sem, vmem_limit_bytes=VMEM_LIMIT)


def _run(gen):
    try:
        while True:
            next(gen)
    except StopIteration as done:
        return done.value


def _interleave(gens):
    live = list(gens)
    while live:
        for g in list(live):
            try:
                next(g)
            except StopIteration:
                live.remove(g)


def _ffn_stages(x, pre_g, wg_ref, wu_ref, wd_ref, post_g, act_ref, rows=slice(None), after_chunk=None):
    xn = _rms(x, pre_g).astype(BF16)
    yield
    for c in range(N_FF_CHUNKS):
        if after_chunk is not None and c == after_chunk[0] + 1:
            after_chunk[1]()
        cols = slice(c * FF_CHUNK, (c + 1) * FF_CHUNK)
        gate = jnp.dot(xn, wg_ref[:, cols], preferred_element_type=F32)
        up = jnp.dot(xn, wu_ref[:, cols], preferred_element_type=F32)
        act = gate * (1.0 / (1.0 + jnp.exp(-gate))) * up
        act_ref[rows, cols] = act.astype(BF16)
        yield
    down = jnp.dot(act_ref[rows, :], wd_ref[...], preferred_element_type=F32)
    yield
    return x + 0.5 * _rms(down, post_g)


def _ffn_tile(*args, **kwargs):
    return _run(_ffn_stages(*args, **kwargs))


def _rope(x, cos, sin_lo, sin_hi):
    w = x.shape[1]
    half = ROPE_DIM // 2
    return (x * cos + pltpu.roll(x, w - half, 1) * sin_lo + pltpu.roll(x, half, 1) * sin_hi)


def _ffn_in_kernel(x_ref, pre_g_ref, wg_ref, wu_ref, wd_ref, post_g_ref, mix_g_ref, win_ref,
                   cos_ref, slo_ref, shi_ref,
                   h_ref, u_ref, q_ref, k_ref, v_ref, act_ref):
    h = _ffn_tile(x_ref[...], pre_g_ref[...], wg_ref, wu_ref, wd_ref, post_g_ref[...], act_ref)
    h_ref[...] = h
    z = jnp.dot(_rms(h, mix_g_ref[...]).astype(BF16), win_ref[...], preferred_element_type=F32)
    u_ref[...] = z[:, :SSM_WIDTH]
    o1 = SSM_WIDTH + Q_WIDTH
    cos, slo, shi = cos_ref[...], slo_ref[...], shi_ref[...]
    rep = Q_WIDTH // LANES
    q = _rope(z[:, SSM_WIDTH:o1], jnp.tile(cos, (1, rep)), jnp.tile(slo, (1, rep)), jnp.tile(shi, (1, rep)))
    q_ref[...] = (q * (HEAD_DIM ** -0.5)).astype(BF16)
    k_ref[...] = _rope(z[:, o1:o1 + KV_WIDTH], cos, slo, shi)
    v_ref[...] = z[:, o1 + KV_WIDTH:]


def _ffn_in(x, pre_g, ffn_w, post_g, mix_g, win, cos, slo, shi, tm):
    t = x.shape[0]
    n_pos_tiles = cos.shape[0] // tm
    row = lambda w: pl.BlockSpec((tm, w), lambda i: (i, 0))
    tab = pl.BlockSpec((tm, LANES), lambda i: (i % n_pos_tiles, 0))
    return pl.pallas_call(
        _ffn_in_kernel,
        grid=(t // tm,),
        in_specs=[row(D_MODEL), _const_spec((1, D_MODEL)), *[_const_spec(w.shape) for w in ffn_w],
                  _const_spec((1, D_MODEL)), _const_spec((1, D_MODEL)), _const_spec(win.shape),
                  tab, tab, tab],
        out_specs=[row(D_MODEL), row(SSM_WIDTH), row(Q_WIDTH), row(KV_WIDTH), row(KV_WIDTH)],
        out_shape=[jax.ShapeDtypeStruct((t, D_MODEL), F32), jax.ShapeDtypeStruct((t, SSM_WIDTH), F32),
                   jax.ShapeDtypeStruct((t, Q_WIDTH), BF16), jax.ShapeDtypeStruct((t, KV_WIDTH), F32),
                   jax.ShapeDtypeStruct((t, KV_WIDTH), F32)],
        scratch_shapes=[pltpu.VMEM((tm, D_FF), BF16)],
        compiler_params=_params(("arbitrary",)),
        name="ffn_in",
    )(x, pre_g, *ffn_w, post_g, mix_g, win, cos, slo, shi)


def _complex_step(s_re, s_im, l_re, l_im, x_re, x_im):
    return l_re * s_re - l_im * s_im + x_re, l_re * s_im + l_im * s_re + x_im


def _ssm_prompt_kernel(u_ref, m_ref, w_ref, v_ref, lam_ref, d_ref, y_ref, st_ref,
                       x_scr, ss_scr, s_scr, *, lc, nb, ncl, nlb):
    tt = pl.program_id(1)
    nrow = nlb * nb
    nq = STATE_LANES // LANES

    @pl.when(tt == 0)
    def _():
        s_scr[...] = jnp.zeros_like(s_scr)

    def piece(b, n, j):
        return u_ref[b, n, pl.ds(j, ncl, stride=lc), :]

    a = []
    for b in range(nlb):
        a.append(jnp.concatenate(
            [jnp.concatenate([piece(b, n, j) for j in range(lc)], axis=1) for n in range(nb)], axis=0).astype(BF16))
        x = jnp.dot(a[b], w_ref[b], preferred_element_type=F32)
        for qq in range(2 * nq):
            for n in range(nb):
                x_scr[qq, pl.ds(b * nb + n, ncl, stride=nrow), :] = x[n * ncl:(n + 1) * ncl, qq * LANES:(qq + 1) * LANES]
    mt = 2 * LANES
    y_intra = [jnp.concatenate(
        [jnp.dot(a[b][:, :(jt + 1) * mt], m_ref[b, :(jt + 1) * mt, jt * mt:(jt + 1) * mt], preferred_element_type=F32)
         for jt in range(lc * LANES // mt)], axis=1) for b in range(nlb)]
    lam_rows = lambda qq: jnp.concatenate(
        [jnp.broadcast_to(lam_ref[b, :, qq * LANES:(qq + 1) * LANES], (nb, LANES)) for b in range(nlb)], axis=0)
    l_re = [lam_rows(qq) for qq in range(nq)]
    l_im = [lam_rows(nq + qq) for qq in range(nq)]

    s = [s_scr[qq] for qq in range(2 * nq)]
    for c in range(ncl):
        rows = slice(c * nrow, (c + 1) * nrow)
        for qq in range(nq):
            ss_scr[qq, rows, :] = s[qq]
            ss_scr[nq + qq, rows, :] = s[nq + qq]
            s[qq], s[nq + qq] = _complex_step(s[qq], s[nq + qq], l_re[qq], l_im[qq],
                                              x_scr[qq, rows, :], x_scr[nq + qq, rows, :])
    for qq in range(2 * nq):
        s_scr[qq] = s[qq]
    for b in range(nlb):
        st_ref[b] = jnp.concatenate([s[qq][b * nb:(b + 1) * nb] for qq in range(2 * nq)], axis=1)
        s_start = jnp.concatenate(
            [jnp.concatenate([ss_scr[qq, pl.ds(b * nb + n, ncl, stride=nrow), :] for n in range(nb)], axis=0)
             for qq in range(2 * nq)], axis=1).astype(BF16)
        y = y_intra[b] + jnp.dot(s_start, v_ref[b], preferred_element_type=F32)
        d = d_ref[b]
        for n in range(nb):
            for j in range(lc):
                y_ref[b, n, pl.ds(j, ncl, stride=lc), :] = (
                    y[n * ncl:(n + 1) * ncl, j * LANES:(j + 1) * LANES] + d * piece(b, n, j))


def _ssm_prompt(u4, m, w, v, lam, d, lc, tl, nlb):
    _, nb, seq, _ = u4.shape
    ncl = tl // lc
    nslab = 2 * STATE_LANES // LANES
    kern = functools.partial(_ssm_prompt_kernel, lc=lc, nb=nb, ncl=ncl, nlb=nlb)
    wspec = lambda a: pl.BlockSpec((nlb,) + a.shape[1:], lambda p, t: (p, 0, 0), pipeline_mode=pl.Buffered(1))
    io = pl.BlockSpec((nlb, nb, tl, LANES), lambda p, t: (p, 0, t, 0))
    return pl.pallas_call(
        kern,
        grid=(N_LANE_BLOCKS // nlb, seq // tl),
        in_specs=[io, wspec(m), wspec(w), wspec(v), wspec(lam), wspec(d)],
        out_specs=[io, pl.BlockSpec((nlb, nb, 2 * STATE_LANES), lambda p, t: (p, 0, 0))],
        out_shape=[jax.ShapeDtypeStruct(u4.shape, F32),
                   jax.ShapeDtypeStruct((N_LANE_BLOCKS, nb, 2 * STATE_LANES), F32)],
        scratch_shapes=[pltpu.VMEM((nslab, nlb * nb * ncl, LANES), F32),
                        pltpu.VMEM((nslab, nlb * nb * ncl, LANES), F32),
                        pltpu.VMEM((nslab, nlb * nb, LANES), F32)],
        compiler_params=_params(("arbitrary", "arbitrary")),
        name="ssm_prompt",
    )(u4, m, w, v, lam, d)


def _ssm_sample_kernel(u_ref, sre_ref, sim_ref, m_ref, w_ref, v_ref, lam_ref, d_ref,
                       y_ref, ore_ref, oim_ref, *, lc, ns):
    def piece(j):
        return u_ref[pl.ds(j, ns, stride=lc), :]

    a = jnp.concatenate([piece(j) for j in range(lc)], axis=1).astype(BF16)
    s_re, s_im = sre_ref[...], sim_ref[...]
    x = jnp.dot(a, w_ref[0], preferred_element_type=F32)
    e_re, e_im = _complex_step(s_re, s_im, lam_ref[0, :, :STATE_LANES], lam_ref[0, :, STATE_LANES:],
                               x[:, :STATE_LANES], x[:, STATE_LANES:])
    ore_ref[...] = e_re
    oim_ref[...] = e_im
    s0 = jnp.concatenate([s_re, s_im], axis=1).astype(BF16)
    y = (jnp.dot(a, m_ref[0], preferred_element_type=F32) + jnp.dot(s0, v_ref[0], preferred_element_type=F32))
    d = d_ref[...]
    for j in range(lc):
        y_ref[pl.ds(j, ns, stride=lc), :] = y[:, j * LANES:(j + 1) * LANES] + d * piece(j)


def _ssm_sample(u, s_re, s_im, m, w, v, lam, d, lc):
    t = u.shape[0]
    ns = t // lc
    kern = functools.partial(_ssm_sample_kernel, lc=lc, ns=ns)
    assert m.shape[1] == 2 * lc * LANES
    n = lc * LANES
    col = lambda rows, width: pl.BlockSpec((rows, width), lambda b: (0, b))
    return pl.pallas_call(
        kern,
        grid=(N_LANE_BLOCKS,),
        in_specs=[col(t, LANES), col(ns, STATE_LANES), col(ns, STATE_LANES),
                  pl.BlockSpec((1, n, n), lambda b: (b, 0, 0)),
                  pl.BlockSpec((1, n, 2 * STATE_LANES), lambda b: (b, 1, 0)),
                  pl.BlockSpec((1, 2 * STATE_LANES, n), lambda b: (b, 0, 0)),
                  pl.BlockSpec((1, 1, 2 * STATE_LANES), lambda b: (b, 0, 0)), col(1, LANES)],
        out_specs=[col(t, LANES), col(ns, STATE_LANES), col(ns, STATE_LANES)],
        out_shape=[jax.ShapeDtypeStruct(u.shape, F32), jax.ShapeDtypeStruct(s_re.shape, F32),
                   jax.ShapeDtypeStruct(s_im.shape, F32)],
        compiler_params=_params(("arbitrary",)),
        name="ssm_sample",
    )(u, s_re, s_im, m, w, v, lam, d)


def _ssm_discretise(a_re, a_im, log_step):
    dt = jnp.exp(log_step)
    mag = jnp.exp(a_re * dt)
    l_re, l_im = mag * jnp.cos(a_im * dt), mag * jnp.sin(a_im * dt)
    den = a_re * a_re + a_im * a_im
    n_re, n_im = l_re - 1.0, l_im
    return l_re, l_im, (n_re * a_re + n_im * a_im) / den, (n_im * a_re - n_re * a_im) / den


def _complex_powers(l_re, l_im, n):
    p_re, p_im = [jnp.ones_like(l_re)], [jnp.zeros_like(l_re)]
    for _ in range(n):
        p_re, p_im = p_re + [p_re[-1] * l_re - p_im[-1] * l_im], p_im + [p_re[-1] * l_im + p_im[-1] * l_re]
    return p_re, p_im


def _split_bf16(x):
    hi = x.astype(BF16)
    return hi, (x - hi.astype(F32)).astype(BF16)


def _dot_split(a, b):
    dot = lambda x, y: jnp.dot(x, y, preferred_element_type=F32)
    return dot(a[0], b[0]) + (dot(a[0], b[1]) + dot(a[1], b[0]))


def _ssm_tables_kernel(ac_re_ref, ac_im_ref, lsc_ref, ar_re_ref, ar_im_ref, lsr_ref, b_re_ref, b_im_ref,
                       c_re_ref, c_im_ref, m_ref, w_ref, v_ref, lam_ref, lam_half_ref, *, lc):
    l_re, l_im, cf_re, cf_im = _ssm_discretise(ac_re_ref[0], ac_im_ref[0], lsc_ref[0])
    b_re, b_im, c_re, c_im = b_re_ref[0], b_im_ref[0], c_re_ref[0], c_im_ref[0]
    c_re_parts, c_im_parts = _split_bf16(c_re), _split_bf16(c_im)
    bb_re = cf_re * b_re - cf_im * b_im
    bb_im = cf_re * b_im + cf_im * b_re
    p_re, p_im = _complex_powers(l_re, l_im, lc)
    lag = []
    for k in range(lc):
        et_re = (p_re[k] * bb_re - p_im[k] * bb_im).T
        et_im = (p_re[k] * bb_im + p_im[k] * bb_re).T
        j = lc - 1 - k
        w_ref[0, j * LANES:(j + 1) * LANES, :STATE_LANES] = et_re.astype(BF16)
        w_ref[0, j * LANES:(j + 1) * LANES, STATE_LANES:] = et_im.astype(BF16)
        lag.append((_dot_split(_split_bf16(et_re), c_re_parts) - _dot_split(_split_bf16(et_im), c_im_parts)).astype(BF16))
        v_ref[0, :STATE_LANES, k * LANES:(k + 1) * LANES] = (p_re[k + 1] * c_re - p_im[k + 1] * c_im).astype(BF16)
        v_ref[0, STATE_LANES:, k * LANES:(k + 1) * LANES] = (-(p_im[k + 1] * c_re + p_re[k + 1] * c_im)).astype(BF16)
    zero = jnp.zeros((LANES, LANES), BF16)
    for j in range(lc):
        for jj in range(lc):
            m_ref[0, j * LANES:(j + 1) * LANES, jj * LANES:(jj + 1) * LANES] = lag[jj - j] if jj >= j else zero
    r_re, r_im, _, _ = _ssm_discretise(ar_re_ref[0], ar_im_ref[0], lsr_ref[0])
    q_re, q_im = _complex_powers(r_re, r_im, lc)
    lam_ref[0] = jnp.concatenate([q_re[lc], q_im[lc]], axis=1)
    lam_half_ref[0] = jnp.concatenate([q_re[lc // 2], q_im[lc // 2]], axis=1)


def _ssm_tables(a_re, a_im, log_step, b_re, b_im, c_re, c_im, lc):
    g, p, h = b_re.shape
    nbk, r = N_LANE_BLOCKS, GROUPS_PER_LANE_BLOCK
    ls = jnp.broadcast_to(log_step[:, None], (g, p))
    cols = [jnp.broadcast_to(x.reshape(nbk, STATE_LANES, 1), (nbk, STATE_LANES, LANES)) for x in (a_re, a_im, ls)]
    rows = [x.reshape(nbk, 1, STATE_LANES) for x in (a_re, a_im, ls)]
    eye = jnp.eye(r, dtype=F32)[None, :, None, :, None]

    def block_diag(x):
        return (x[:, :, :, None, :] * eye).reshape(nbk, STATE_LANES, LANES)

    mats = [block_diag(b_re.reshape(nbk, r, p, h)), block_diag(b_im.reshape(nbk, r, p, h)),
            block_diag(c_re.reshape(nbk, r, h, p).transpose(0, 1, 3, 2)),
            block_diag(c_im.reshape(nbk, r, h, p).transpose(0, 1, 3, 2))]
    spec = lambda shape: pl.BlockSpec((1,) + shape, lambda b: (b, 0, 0))
    n = lc * LANES
    return pl.pallas_call(
        functools.partial(_ssm_tables_kernel, lc=lc),
        grid=(nbk,),
        in_specs=[spec((STATE_LANES, LANES))] * 3 + [spec((1, STATE_LANES))] * 3 + [spec((STATE_LANES, LANES))] * 4,
        out_specs=[spec((n, n)), spec((n, 2 * STATE_LANES)), spec((2 * STATE_LANES, n)),
                   spec((1, 2 * STATE_LANES)), spec((1, 2 * STATE_LANES))],
        out_shape=[jax.ShapeDtypeStruct((nbk, n, n), BF16), jax.ShapeDtypeStruct((nbk, n, 2 * STATE_LANES), BF16),
                   jax.ShapeDtypeStruct((nbk, 2 * STATE_LANES, n), BF16),
                   jax.ShapeDtypeStruct((nbk, 1, 2 * STATE_LANES), F32),
                   jax.ShapeDtypeStruct((nbk, 1, 2 * STATE_LANES), F32)],
        compiler_params=_params(("arbitrary",)),
        name="ssm_tables",
    )(*cols, *rows, *mats)


def _nt_dot(a, b):
    return lax.dot_general(a, b, (((1,), (1,)), ((), ())), preferred_element_type=F32)


def _swa_prompt_tile(sink_ref, qt_ref, kcat, vcat_t, mstd_ref, mfirst_ref, g_ref, ot_ref, first_tile, nblk):
    keys = lambda j: slice(j * WINDOW, (j + 2) * WINDOW)
    zeros = jnp.zeros((HEAD_DIM, WINDOW), BF16)
    sinks = [jnp.concatenate([jnp.full((1, WINDOW), sink_ref[kh * GQA_GROUP + g], F32) for g in range(GQA_GROUP)],
                             axis=1) for kh in range(N_KV_HEADS)]
    chains = [(j, kh) for j in range(nblk) for kh in range(N_KV_HEADS)]

    def scores(j, kh):
        def rhs(g):
            h = kh * GQA_GROUP + g
            q = qt_ref[h * HEAD_DIM:(h + 1) * HEAD_DIM, j * WINDOW:(j + 1) * WINDOW]
            return jnp.concatenate([q, zeros] if kh == 0 else [zeros, q], axis=0)
        r = jnp.concatenate([rhs(g) for g in range(GQA_GROUP)], axis=1)
        vmask = mstd_ref[...]
        if j == 0:
            vmask = jnp.where(first_tile, mfirst_ref[...], vmask)
        valid = jnp.tile(vmask, (1, GQA_GROUP)) > 0.0
        return jnp.where(valid, jnp.dot(kcat[keys(j)], r, preferred_element_type=F32), NEG_INF)

    def softmax(kh, sc):
        m = jnp.maximum(jnp.max(sc, axis=0, keepdims=True), sinks[kh])
        e = jnp.exp(sc - m)
        return e.astype(BF16), 1.0 / (jnp.sum(e, axis=0, keepdims=True) + jnp.exp(sinks[kh] - m))

    sc = [scores(*c) for c in chains]
    pr = [softmax(kh, x) for (j, kh), x in zip(chains, sc)]

    def finish():
        out = [jnp.dot(vcat_t[kh * HEAD_DIM:(kh + 1) * HEAD_DIM, keys(j)], e, preferred_element_type=F32) * inv
               for (j, kh), (e, inv) in zip(chains, pr)]
        gain = g_ref[...]
        for j in range(nblk):
            o = jnp.concatenate([out[j * N_KV_HEADS + kh][:, g * WINDOW:(g + 1) * WINDOW]
                                 for kh in range(N_KV_HEADS) for g in range(GQA_GROUP)], axis=0)
            scale = lax.rsqrt(jnp.mean(o * o, axis=0, keepdims=True) + RMS_EPS)
            ot_ref[:, j * WINDOW:(j + 1) * WINDOW] = (o * scale * gain).astype(BF16)

    return finish


def _swa_masks():
    kj = jnp.arange(2 * WINDOW)[:, None]
    diff = jnp.arange(WINDOW)[None, :] + WINDOW - kj
    std = (diff >= 0) & (diff <= WINDOW)
    return std.astype(F32), (std & (kj >= WINDOW)).astype(F32)


def _ffn_in_swa_kernel(sink_ref, x_ref, pre_g_ref, wg_ref, wu_ref, wd_ref, post_g_ref, mix_g_ref, win_u_ref, wqkv_t_ref,
                       cos_t_ref, sin_t_ref, mstd_ref, mfirst_ref, swa_g_ref,
                       h_ref, u_ref, k_win_ref, vt_win_ref, ot_ref, act_ref, q_scr, kcat_scr, vcat_scr, *, nt, nblk):
    i = pl.program_id(0)
    tm = x_ref.shape[0]

    @pl.when(i == 0)
    def _():
        q_scr[...] = jnp.zeros_like(q_scr)
        kcat_scr[...] = jnp.zeros_like(kcat_scr)
        vcat_scr[...] = jnp.zeros_like(vcat_scr)

    first_tile = lax.rem(i + (nt - 1), nt) == 0
    swa_finish = _swa_prompt_tile(sink_ref, q_scr, kcat_scr[...], vcat_scr[...], mstd_ref, mfirst_ref, swa_g_ref,
                                  ot_ref, first_tile, nblk)

    kcat_scr[:WINDOW, :] = kcat_scr[tm:, :]
    vcat_scr[:, :WINDOW] = vcat_scr[:, tm:]
    half = ROPE_DIM // 2

    def row_block(r, swa_hook):
        n = r.stop - r.start
        h = yield from _ffn_stages(x_ref[r, :], pre_g_ref[...], wg_ref, wu_ref, wd_ref, post_g_ref[...], act_ref, r,
                                   after_chunk=swa_hook)
        h_ref[r, :] = h
        hn = _rms(h, mix_g_ref[...]).astype(BF16)
        z = jnp.dot(hn, win_u_ref[...], preferred_element_type=F32)
        yield
        for b in range(N_LANE_BLOCKS):
            u_ref[b, r, :] = z[:, b * LANES:(b + 1) * LANES]
        zt = _nt_dot(wqkv_t_ref[...], hn)
        yield
        vt = zt[Q_WIDTH + KV_WIDTH:]
        if r.stop == tm:
            vt_win_ref[...] = vt[:, -WINDOW:]
        qk = zt[:Q_WIDTH + KV_WIDTH].reshape(N_HEADS + N_KV_HEADS, HEAD_DIM, n)
        x1, x2 = qk[:, :half], qk[:, half:ROPE_DIM]
        cos, sin = cos_t_ref[:, r][None], sin_t_ref[:, r][None]
        qk = jnp.concatenate([x1 * cos - x2 * sin, x2 * cos + x1 * sin, qk[:, ROPE_DIM:]], axis=1)
        k = qk[N_HEADS:].reshape(KV_WIDTH, n).T
        if r.stop == tm:
            k_win_ref[...] = k[-WINDOW:]
        shifted = slice(WINDOW + r.start, WINDOW + r.stop)
        kcat_scr[shifted, :] = k.astype(BF16)
        vcat_scr[:, shifted] = vt.astype(BF16)
        q_scr[:, r] = (qk[:N_HEADS].reshape(Q_WIDTH, n) * (HEAD_DIM ** -0.5)).astype(BF16)

    blocks = [slice(b * tm // ROW_BLOCKS, (b + 1) * tm // ROW_BLOCKS) for b in range(ROW_BLOCKS)]
    _interleave([row_block(r, (N_FF_CHUNKS - 3, swa_finish) if r.start == 0 else None) for r in blocks])


def _ffn_in_swa(x, pre_g, ffn_w, post_g, mix_g, win_u, wqkv_t, cos_t, sin_t, sinks, swa_g, seq, tm):
    t = x.shape[0]
    n_tiles, nt, nblk, n_seq = t // tm, seq // tm, tm // WINDOW, t // seq
    half = ROPE_DIM // 2
    mstd, mfirst = _swa_masks()
    gain = jnp.broadcast_to(swa_g.reshape(Q_WIDTH, 1), (Q_WIDTH, WINDOW))
    cur = lambda i: jnp.minimum(i, n_tiles - 1)
    row = lambda w: pl.BlockSpec((tm, w), lambda i: (cur(i), 0))
    tab_t = pl.BlockSpec((half, tm), lambda i: (0, cur(i) % nt))
    return pl.pallas_call(
        functools.partial(_ffn_in_swa_kernel, nt=nt, nblk=nblk),
        grid=(n_tiles + 1,),
        in_specs=[pl.BlockSpec(memory_space=pltpu.SMEM), row(D_MODEL), _const_spec((1, D_MODEL)),
                  *[_const_spec(w.shape) for w in ffn_w], _const_spec((1, D_MODEL)), _const_spec((1, D_MODEL)),
                  _const_spec(win_u.shape), _const_spec(wqkv_t.shape), tab_t, tab_t,
                  _const_spec(mstd.shape), _const_spec(mfirst.shape), _const_spec(gain.shape)],
        out_specs=[row(D_MODEL), pl.BlockSpec((N_LANE_BLOCKS, tm, LANES), lambda i: (0, cur(i), 0)),
                   pl.BlockSpec((WINDOW, KV_WIDTH), lambda i: (cur(i) // nt, 0)),
                   pl.BlockSpec((KV_WIDTH, WINDOW), lambda i: (0, cur(i) // nt)),
                   pl.BlockSpec((Q_WIDTH, tm), lambda i: (0, jnp.maximum(i - 1, 0)))],
        out_shape=[jax.ShapeDtypeStruct((t, D_MODEL), F32), jax.ShapeDtypeStruct((N_LANE_BLOCKS, t, LANES), F32),
                   jax.ShapeDtypeStruct((n_seq * WINDOW, KV_WIDTH), F32),
                   jax.ShapeDtypeStruct((KV_WIDTH, n_seq * WINDOW), F32), jax.ShapeDtypeStruct((Q_WIDTH, t), BF16)],
        scratch_shapes=[pltpu.VMEM((tm, D_FF), BF16), pltpu.VMEM((Q_WIDTH, tm), BF16),
                        pltpu.VMEM((WINDOW + tm, KV_WIDTH), BF16), pltpu.VMEM((KV_WIDTH, WINDOW + tm), BF16)],
        compiler_params=_params(("arbitrary",)),
        name="ffn_in_swa",
    )(sinks, x, pre_g, *ffn_w, post_g, mix_g, win_u, wqkv_t, cos_t, sin_t, mstd, mfirst, gain)


def _swa_sample_kernel(sink_ref, q_ref, kn_ref, vn_ref, ck_ref, cv_ref, g_ref, o_ref, nk_ref, nv_ref, *, ns, t):
    rows = GQA_GROUP * t
    tok = lax.broadcasted_iota(jnp.int32, (rows, WINDOW), 0) % t
    valid_c = lax.broadcasted_iota(jnp.int32, (rows, WINDOW), 1) >= tok
    tok_n = lax.broadcasted_iota(jnp.int32, (rows, t), 0) % t
    valid_n = lax.broadcasted_iota(jnp.int32, (rows, t), 1) <= tok_n
    gain = g_ref[...]

    sinks = [jnp.concatenate([jnp.full((t, 1), sink_ref[kh * GQA_GROUP + g], F32) for g in range(GQA_GROUP)], axis=0)
             for kh in range(N_KV_HEADS)]
    heads = [(s, kh) for s in range(ns) for kh in range(N_KV_HEADS)]
    hs = lambda kh: slice(kh * HEAD_DIM, (kh + 1) * HEAD_DIM)

    newest = lax.broadcasted_iota(jnp.int32, (KV_WIDTH, WINDOW), 1) >= WINDOW - t
    pad = jnp.zeros((WINDOW - 2 * t, KV_WIDTH), F32)

    def shifted(cache_ref, new_ref, s):
        new_rows = jnp.concatenate([pad, jnp.zeros((t, KV_WIDTH), F32), new_ref[s]], axis=0)
        return jnp.where(newest, new_rows.T, pltpu.roll(cache_ref[s], WINDOW - t, 1))

    for s in range(ns):
        nk_ref[s] = shifted(ck_ref, kn_ref, s)
        nv_ref[s] = shifted(cv_ref, vn_ref, s)

    def scores(s, kh):
        q = q_ref[s].astype(F32)
        q4 = jnp.concatenate([q[:, (kh * GQA_GROUP + g) * HEAD_DIM:(kh * GQA_GROUP + g + 1) * HEAD_DIM]
                              for g in range(GQA_GROUP)], axis=0).astype(BF16)
        sc_c = jnp.where(valid_c, jnp.dot(q4, ck_ref[s, hs(kh), :].astype(BF16), preferred_element_type=F32), NEG_INF)
        sc_n = jnp.where(valid_n, _nt_dot(q4, kn_ref[s, :, hs(kh)].astype(BF16)), NEG_INF)
        return sc_c, sc_n

    def softmax(kh, sc_c, sc_n):
        m = jnp.maximum(jnp.maximum(jnp.max(sc_c, axis=-1, keepdims=True),
                                    jnp.max(sc_n, axis=-1, keepdims=True)), sinks[kh])
        e_c, e_n = jnp.exp(sc_c - m), jnp.exp(sc_n - m)
        inv = 1.0 / (jnp.sum(e_c, axis=-1, keepdims=True) + jnp.sum(e_n, axis=-1, keepdims=True)
                     + jnp.exp(sinks[kh] - m))
        return (e_c * inv).astype(BF16), (e_n * inv).astype(BF16)

    def values(s, kh, p_c, p_n):
        return (_nt_dot(p_c, cv_ref[s, hs(kh), :].astype(BF16))
                + jnp.dot(p_n, vn_ref[s, :, hs(kh)].astype(BF16), preferred_element_type=F32))

    sc = [scores(s, kh) for s, kh in heads]
    pr = [softmax(kh, *x) for (s, kh), x in zip(heads, sc)]
    o4 = [values(s, kh, *x) for (s, kh), x in zip(heads, pr)]
    for s in range(ns):
        o = jnp.concatenate([o4[s * N_KV_HEADS + kh][g * t:(g + 1) * t]
                             for kh in range(N_KV_HEADS) for g in range(GQA_GROUP)], axis=1)
        o_ref[s] = _rms(o, gain).astype(BF16)


def _swa_sample(q3, kn3, vn3, ck, cv, sinks, out_g, sb):
    ns, t, _ = q3.shape
    blk = lambda a: pl.BlockSpec((sb,) + a.shape[1:], lambda i: (i, 0, 0))
    kern = functools.partial(_swa_sample_kernel, ns=sb, t=t)
    return pl.pallas_call(
        kern,
        grid=(ns // sb,),
        in_specs=[pl.BlockSpec(memory_space=pltpu.SMEM), blk(q3), blk(kn3), blk(vn3), blk(ck), blk(cv),
                  pl.BlockSpec((1, Q_WIDTH), lambda i: (0, 0))],
        out_specs=[blk(q3), blk(ck), blk(cv)],
        out_shape=[jax.ShapeDtypeStruct(q3.shape, BF16), jax.ShapeDtypeStruct(ck.shape, F32),
                   jax.ShapeDtypeStruct(cv.shape, F32)],
        compiler_params=_params(("arbitrary",)),
        name="swa_sample",
    )(sinks, q3, kn3, vn3, ck, cv, out_g)


def _gelu_tanh(x):
    return 0.5 * x * (1.0 + jnp.tanh(math.sqrt(2.0 / math.pi) * (x + 0.044715 * (x * x * x))))


def _merge_stages(h, y, at, wglu_ref, bglu, sg, wout_ref, post_g, xa_g, wq_ref, at_transposed=False):
    g = _gelu_tanh(y)
    lin = jnp.dot(g.astype(BF16), wglu_ref[...], preferred_element_type=F32) + bglu
    yield
    y_ssm = g * (1.0 / (1.0 + jnp.exp(-lin)))
    ssm_n = _rms(y_ssm, sg).astype(BF16)
    at_dims = (((0,), (0,)), ((), ())) if at_transposed else (((1,), (0,)), ((), ()))
    mixed = (jnp.dot(ssm_n, wout_ref[:SSM_WIDTH, :], preferred_element_type=F32)
             + lax.dot_general(at, wout_ref[SSM_WIDTH:, :], at_dims, preferred_element_type=F32))
    yield
    h2 = h + _rms(mixed, post_g)
    qm = jnp.dot(_rms(h2, xa_g).astype(BF16), wq_ref[...], preferred_element_type=F32)
    yield
    return h2, (qm * (MEM_HEAD_DIM ** -0.5)).astype(BF16)


def _merge_tile(*args, **kwargs):
    return _run(_merge_stages(*args, **kwargs))


def _merge_kernel(h_ref, y_ref, at_ref, wglu_ref, bglu_ref, sg_ref, wout_ref, post_g_ref, xa_g_ref, wq_ref,
                  h2_ref, qm_ref):
    h2_ref[...], qm_ref[...] = _merge_tile(h_ref[...], y_ref[...], at_ref[...], wglu_ref, bglu_ref[...], sg_ref[...],
                                           wout_ref, post_g_ref[...], xa_g_ref[...], wq_ref)


def _merge_specs(wglu, wout, wq):
    return [_const_spec(wglu.shape), _const_spec((1, SSM_WIDTH)), _const_spec((1, SSM_WIDTH)), _const_spec(wout.shape),
            _const_spec((1, D_MODEL)), _const_spec((1, D_MODEL)), _const_spec(wq.shape)]


def _merge(h, y, at, merge_w, tm):
    t = h.shape[0]
    row = lambda w: pl.BlockSpec((tm, w), lambda i: (i, 0))
    return pl.pallas_call(
        _merge_kernel,
        grid=(t // tm,),
        in_specs=[row(D_MODEL), row(SSM_WIDTH), row(Q_WIDTH)] + _merge_specs(merge_w[0], merge_w[3], merge_w[6]),
        out_specs=[row(D_MODEL), row(D_MODEL)],
        out_shape=[jax.ShapeDtypeStruct((t, D_MODEL), F32), jax.ShapeDtypeStruct((t, D_MODEL), BF16)],
        compiler_params=_params(("arbitrary",)),
        name="merge",
    )(h, y, at, *merge_w)


def _mem_attn_sample_kernel(q_ref, k_hbm, v_hbm, o_ref, kbuf, vbuf, sem, *, gb, t, nbuf, n_chunks):
    rows = t * MEM_HEADS
    halves = MEM_HEAD_DIM // LANES
    kv_rows = N_MEM * halves * MEM_HEADS
    period = halves * MEM_HEADS
    lane = lax.broadcasted_iota(jnp.int32, (rows, kv_rows), 1) % period
    head = lax.broadcasted_iota(jnp.int32, (rows, kv_rows), 0) % MEM_HEADS
    in_half = [lane == head + hf * MEM_HEADS for hf in range(halves)]

    def copies(c, slot):
        return (pltpu.make_async_copy(k_hbm.at[pl.ds(c * gb, gb)], kbuf.at[slot], sem.at[0, slot]),
                pltpu.make_async_copy(v_hbm.at[pl.ds(c * gb, gb)], vbuf.at[slot], sem.at[1, slot]))

    for c in range(nbuf):
        for cp in copies(c, c):
            cp.start()

    def chunk(c, carry):
        slot = lax.rem(c, nbuf)
        for cp in copies(c, slot):
            cp.wait()
        for b in range(gb):
            attend(c * gb + b, kbuf[slot, b].astype(BF16), vbuf[slot, b].astype(BF16))

        @pl.when(c + nbuf < n_chunks)
        def _():
            for cp in copies(c + nbuf, slot):
                cp.start()
        return carry

    def attend(s, kb, vb):
        part = _nt_dot(q_ref[s], kb)
        sc = jnp.where(in_half[0], part[:rows], 0.0)
        for hf in range(1, halves):
            sc = sc + pltpu.roll(jnp.where(in_half[hf], part[hf * rows:(hf + 1) * rows], 0.0),
                                 kv_rows - hf * MEM_HEADS, 1)
        sc = jnp.where(in_half[0], sc, -jnp.inf)
        e = jnp.exp(sc - jnp.max(sc, axis=-1, keepdims=True))
        inv = 1.0 / jnp.sum(e, axis=-1, keepdims=True)
        e_all = jnp.concatenate([e] + [pltpu.roll(e, hf * MEM_HEADS, 1) for hf in range(1, halves)], axis=0)
        o = jnp.dot(e_all.astype(BF16), vb, preferred_element_type=F32)
        o_ref[s] = (o * jnp.concatenate([inv] * halves, axis=0)).astype(BF16)

    lax.fori_loop(0, n_chunks, chunk, 0)


def _mem_attn_sample(qm, cache_k, cache_v, n_s, t_s, gb):
    halves = MEM_HEAD_DIM // LANES
    rows = halves * t_s * MEM_HEADS
    kv_rows = N_MEM * halves * MEM_HEADS

    def stored_rows(c):
        c = c.reshape(n_s, N_MEM, MEM_HEADS, halves, LANES).transpose(0, 1, 3, 2, 4)
        return c.reshape(n_s, kv_rows, LANES)

    q = qm.reshape(n_s, t_s, MEM_HEADS, halves, LANES).transpose(0, 3, 1, 2, 4).reshape(n_s, rows, LANES)
    nbuf = MEM_SAMPLE_BUFFERS
    whole = pl.BlockSpec((n_s, rows, LANES), lambda: (0, 0, 0))
    o = pl.pallas_call(
        functools.partial(_mem_attn_sample_kernel, gb=gb, t=t_s, nbuf=nbuf, n_chunks=n_s // gb),
        in_specs=[whole, pl.BlockSpec(memory_space=pl.ANY), pl.BlockSpec(memory_space=pl.ANY)],
        out_specs=whole,
        out_shape=jax.ShapeDtypeStruct((n_s, rows, LANES), BF16),
        scratch_shapes=[pltpu.VMEM((nbuf, gb, kv_rows, LANES), F32), pltpu.VMEM((nbuf, gb, kv_rows, LANES), F32),
                        pltpu.SemaphoreType.DMA((2, nbuf))],
        compiler_params=pltpu.CompilerParams(vmem_limit_bytes=VMEM_LIMIT),
        name="mem_attn_sample",
    )(q, stored_rows(cache_k), stored_rows(cache_v))
    o = o.reshape(n_s, halves, t_s, MEM_HEADS, LANES).transpose(0, 2, 3, 1, 4)
    return o.reshape(n_s * t_s, D_MODEL)


def _mem_heads_stages(q, k_ref, v_ref):
    heads = [slice(hh * MEM_HEAD_DIM, (hh + 1) * MEM_HEAD_DIM) for hh in range(MEM_HEADS)]
    scs = [_nt_dot(q[:, hs], k_ref[0, :, hs]) for hs in heads]
    yield
    outs = []
    for hs, sc in zip(heads, scs):
        e = jnp.exp(sc - jnp.max(sc, axis=-1, keepdims=True))
        inv = 1.0 / jnp.sum(e, axis=-1, keepdims=True)
        outs.append((jnp.dot(e.astype(BF16), v_ref[0, :, hs], preferred_element_type=F32) * inv).astype(BF16))
        yield
    return jnp.concatenate(outs, axis=1)


def _ffn_out_stages(h2, om, wo_ref, xa_post, pre_g, wg_ref, wu_ref, wd_ref, post_g, act_ref, rows=slice(None)):
    c = jnp.dot(om, wo_ref[...], preferred_element_type=F32)
    yield
    h3 = h2 + _rms(c, xa_post)
    return (yield from _ffn_stages(h3, pre_g, wg_ref, wu_ref, wd_ref, post_g, act_ref, rows))


def _ffn_out_tile(*args, **kwargs):
    return _run(_ffn_out_stages(*args, **kwargs))


def _ffn_out_specs(wo, ffn_w):
    return [_const_spec(wo.shape), _const_spec((1, D_MODEL)), _const_spec((1, D_MODEL)),
            *[_const_spec(w.shape) for w in ffn_w], _const_spec((1, D_MODEL))]


def _ffn_out_kernel(h_ref, o_ref, wo_ref, xa_post_ref, pre_g_ref, wg_ref, wu_ref, wd_ref, post_g_ref, out_ref, act_ref):
    out_ref[...] = _ffn_out_tile(h_ref[...], o_ref[...], wo_ref, xa_post_ref[...], pre_g_ref[...], wg_ref, wu_ref,
                                 wd_ref, post_g_ref[...], act_ref)


def _ffn_out(h, o, out_w, tm):
    t = h.shape[0]
    row = pl.BlockSpec((tm, D_MODEL), lambda i: (i, 0))
    return pl.pallas_call(
        _ffn_out_kernel,
        grid=(t // tm,),
        in_specs=[row, row] + _ffn_out_specs(out_w[0], out_w[3:6]),
        out_specs=row,
        out_shape=jax.ShapeDtypeStruct((t, D_MODEL), F32),
        scratch_shapes=[pltpu.VMEM((tm, D_FF), BF16)],
        compiler_params=_params(("arbitrary",)),
        name="ffn_out",
    )(h, o, *out_w)


def _post_kernel(h_ref, y_ref, at_ref, k_ref, v_ref,
                 wglu_ref, bglu_ref, sg_ref, wout_ref, post_g_ref, xa_g_ref, wq_ref,
                 wo_ref, xa_post_ref, pre_g_ref, wg_ref, wu_ref, wd_ref, ffn_post_ref, out_ref, act_ref):
    def row_block(r):
        y = jnp.concatenate([y_ref[b, r, :] for b in range(N_LANE_BLOCKS)], axis=1)
        h2, qm = yield from _merge_stages(h_ref[r, :], y, at_ref[:, r], wglu_ref, bglu_ref[...], sg_ref[...],
                                          wout_ref, post_g_ref[...], xa_g_ref[...], wq_ref, at_transposed=True)
        om = yield from _mem_heads_stages(qm, k_ref, v_ref)
        out_ref[r, :] = yield from _ffn_out_stages(h2, om, wo_ref, xa_post_ref[...], pre_g_ref[...], wg_ref, wu_ref,
                                                   wd_ref, ffn_post_ref[...], act_ref, r)

    tm = h_ref.shape[0]
    _interleave([row_block(slice(i * tm // ROW_BLOCKS, (i + 1) * tm // ROW_BLOCKS)) for i in range(ROW_BLOCKS)])


def _post(h, y, at, k3, v3, merge_w, out_w, tm):
    t = h.shape[0]
    tiles_per_batch = t // k3.shape[0] // tm
    row = lambda w: pl.BlockSpec((tm, w), lambda i: (i, 0))
    kv = pl.BlockSpec((1, N_MEM, D_MODEL), lambda i: (i // tiles_per_batch, 0, 0))
    return pl.pallas_call(
        _post_kernel,
        grid=(t // tm,),
        in_specs=([row(D_MODEL), pl.BlockSpec((N_LANE_BLOCKS, tm, LANES), lambda i: (0, i, 0)),
                   pl.BlockSpec((Q_WIDTH, tm), lambda i: (0, i)), kv, kv]
                  + _merge_specs(merge_w[0], merge_w[3], merge_w[6]) + _ffn_out_specs(out_w[0], out_w[3:6])),
        out_specs=row(D_MODEL),
        out_shape=jax.ShapeDtypeStruct((t, D_MODEL), F32),
        scratch_shapes=[pltpu.VMEM((tm, D_FF), BF16)],
        compiler_params=_params(("arbitrary",)),
        name="post",
    )(h, y, at, k3, v3, *merge_w, *out_w)


def _mem_kv_kernel(m_ref, g_ref, wkv_ref, k_ref, v_ref, kb_ref, vb_ref):
    kv = jnp.dot(_rms(m_ref[...], g_ref[...]).astype(BF16), wkv_ref[...], preferred_element_type=F32)
    kb_ref[...] = kv[:, :D_MODEL].astype(BF16)
    vb_ref[...] = kv[:, D_MODEL:].astype(BF16)
    halves = MEM_HEAD_DIM // LANES
    n = m_ref.shape[0]
    for out_ref, base in ((k_ref, 0), (v_ref, D_MODEL)):
        for hh in range(MEM_HEADS):
            for hf in range(halves):
                col = base + hh * MEM_HEAD_DIM + hf * LANES
                out_ref[0, pl.ds(hf * MEM_HEADS + hh, n, stride=halves * MEM_HEADS), :] = kv[:, col:col + LANES]


def _mem_kv(mem, g, wkv, tm):
    t = mem.shape[0]
    row = pl.BlockSpec((tm, D_MODEL), lambda i: (i, 0))
    per_slot = (MEM_HEAD_DIM // LANES) * MEM_HEADS
    stored = pl.BlockSpec((1, tm * per_slot, LANES), lambda i: (i, 0, 0))
    return pl.pallas_call(
        _mem_kv_kernel,
        grid=(t // tm,),
        in_specs=[row, _const_spec((1, D_MODEL)), _const_spec(wkv.shape)],
        out_specs=[stored, stored, row, row],
        out_shape=[jax.ShapeDtypeStruct((t // tm, tm * per_slot, LANES), F32)] * 2
        + [jax.ShapeDtypeStruct((t, D_MODEL), BF16)] * 2,
        compiler_params=_params(("arbitrary",)),
        name="mem_kv",
    )(mem, g, wkv)


def _rope_tables(pos):
    half = ROPE_DIM // 2
    inv = ROPE_THETA ** (-jnp.arange(half, dtype=F32) * (2.0 / ROPE_DIM))
    ang = pos.astype(F32)[:, None] * inv[None, :]
    cos, sin = jnp.cos(ang), jnp.sin(ang)
    n = pos.shape[0]
    pad = jnp.zeros((n, HEAD_DIM - ROPE_DIM), F32)
    zero = jnp.zeros((n, half), F32)
    cos_h = jnp.concatenate([cos, cos, pad + 1.0], axis=1)
    lo_h = jnp.concatenate([-sin, zero, pad], axis=1)
    hi_h = jnp.concatenate([zero, sin, pad], axis=1)
    rep = LANES // HEAD_DIM
    return tuple(jnp.tile(a, (1, rep)) for a in (cos_h, lo_h, hi_h)), (cos.T, sin.T)


def _ffn_weights(w_gate, w_up, w_down):
    return w_gate.astype(BF16), w_up.astype(BF16), w_down.astype(BF16)


def _lane_block_states(st, n):
    st = st.reshape(N_LANE_BLOCKS, n, 2, GROUPS_PER_LANE_BLOCK, SSM_STATE).transpose(2, 1, 0, 3, 4)
    st = st.reshape(2, n, N_SSM_GROUPS, SSM_STATE)
    return st[0], st[1]


def kernel(x_prompt, x_sample, state_ssm_re, state_ssm_im, cache_swa_k, cache_swa_v, cache_mem_k, cache_mem_v, mem_prompt, ffn1_pre_g, ffn1_w_gate, ffn1_w_up, ffn1_w_down, ffn1_post_g, mix_pre_g, w_in, ssm_a_re, ssm_a_im, ssm_log_step, ssm_b_re, ssm_b_im, ssm_c_re, ssm_c_im, ssm_d, ssm_w_glu, ssm_b_glu, attn_sinks, ssm_out_g, attn_out_g, w_out, mix_post_g, mem_norm_g, w_mem_q, w_mem_k, w_mem_v, w_mem_o, xa_pre_g, xa_post_g, ffn2_pre_g, ffn2_w_gate, ffn2_w_up, ffn2_w_down, ffn2_post_g):
    n_p, s_p, _ = x_prompt.shape
    n_s, t_s, _ = x_sample.shape
    tm = TOKEN_TILE
    row = lambda a: a.reshape(1, -1).astype(F32)

    ffn1_w = _ffn_weights(ffn1_w_gate, ffn1_w_up, ffn1_w_down)
    win = w_in.astype(BF16)
    merge_w = (ssm_w_glu.astype(BF16), row(ssm_b_glu), row(ssm_out_g), w_out.astype(BF16), row(mix_post_g),
               row(xa_pre_g), w_mem_q.astype(BF16))
    out_w = (w_mem_o.astype(BF16), row(xa_post_g), row(ffn2_pre_g),
             *_ffn_weights(ffn2_w_gate, ffn2_w_up, ffn2_w_down), row(ffn2_post_g))
    wkv = jnp.concatenate([w_mem_k, w_mem_v], axis=1).astype(BF16)
    d_row = row(ssm_d)
    ssm_args = (ssm_a_re.astype(F32), ssm_a_im.astype(F32), ssm_log_step.astype(F32), ssm_b_re.astype(F32),
                ssm_b_im.astype(F32), ssm_c_re.astype(F32), ssm_c_im.astype(F32))
    sinks = attn_sinks.astype(F32)

    pm_k, pm_v, pm_kb, pm_vb = _mem_kv(mem_prompt.reshape(n_p * N_MEM, D_MODEL), row(mem_norm_g), wkv, N_MEM)

    def from_stored(c):
        halves = MEM_HEAD_DIM // LANES
        return c.reshape(n_p, N_MEM, halves, MEM_HEADS, LANES).transpose(0, 1, 3, 2, 4).reshape(
            n_p, N_MEM, MEM_HEADS, MEM_HEAD_DIM)

    def tokenwise_in(x2, pos_tab):
        return _ffn_in(x2, row(ffn1_pre_g), ffn1_w, row(ffn1_post_g), row(mix_pre_g), win, *pos_tab, tm)

    lc_p = 2 * t_s
    ssm_m, ssm_w, ssm_v, lam_p, lam_s = _ssm_tables(*ssm_args, lc_p)
    _, tab_p_t = _rope_tables(jnp.arange(s_p, dtype=jnp.int32))
    h1, u, k_win, vt_win, at = _ffn_in_swa(x_prompt.reshape(n_p * s_p, D_MODEL), row(ffn1_pre_g), ffn1_w, row(ffn1_post_g),
                                   row(mix_pre_g), win[:, :SSM_WIDTH], win[:, SSM_WIDTH:].T, *tab_p_t, sinks,
                                   attn_out_g.astype(F32), s_p, tm)
    y4, st_p = _ssm_prompt(u.reshape(N_LANE_BLOCKS, n_p, s_p, LANES), ssm_m, ssm_w, ssm_v, lam_p,
                           d_row.reshape(N_LANE_BLOCKS, 1, LANES), lc_p, SSM_TIME_TILE, SSM_LANE_BLOCKS_PER_STEP)
    y_prompt = _post(h1, y4.reshape(N_LANE_BLOCKS, n_p * s_p, LANES), at, pm_kb.reshape(n_p, N_MEM, D_MODEL),
                     pm_vb.reshape(n_p, N_MEM, D_MODEL), merge_w, out_w, tm).reshape(n_p, s_p, D_MODEL)
    p_sre, p_sim = _lane_block_states(st_p, n_p)
    p_wk = k_win.reshape(n_p, WINDOW, N_KV_HEADS, HEAD_DIM)
    p_wv = vt_win.reshape(N_KV_HEADS, HEAD_DIM, n_p, WINDOW).transpose(2, 3, 0, 1)

    pos_s = jnp.tile(PAST_LEN + jnp.arange(t_s, dtype=jnp.int32), n_s)
    h1s, us, qs, ks, vs = tokenwise_in(x_sample.reshape(n_s * t_s, D_MODEL), _rope_tables(pos_s)[0])
    ys, s_sre, s_sim = _ssm_sample(us, state_ssm_re.reshape(n_s, -1).astype(F32), state_ssm_im.reshape(n_s, -1).astype(F32),
                                   ssm_m, ssm_w, ssm_v, lam_s, d_row, t_s)
    win_len = cache_swa_k.shape[1]
    ats, s_wk, s_wv = _swa_sample(qs.reshape(n_s, t_s, Q_WIDTH), ks.reshape(n_s, t_s, KV_WIDTH), vs.reshape(n_s, t_s, KV_WIDTH),
                                  cache_swa_k.transpose(0, 2, 3, 1).reshape(n_s, KV_WIDTH, win_len),
                                  cache_swa_v.transpose(0, 2, 3, 1).reshape(n_s, KV_WIDTH, win_len),
                                  sinks, row(attn_out_g), SWA_SAMPLE_SEQS)
    s_wk = s_wk.reshape(n_s, N_KV_HEADS, HEAD_DIM, win_len).transpose(0, 3, 1, 2)
    s_wv = s_wv.reshape(n_s, N_KV_HEADS, HEAD_DIM, win_len).transpose(0, 3, 1, 2)
    h2s, qms = _merge(h1s, ys, ats.reshape(n_s * t_s, Q_WIDTH), merge_w, tm)
    oms = _mem_attn_sample(qms, cache_mem_k, cache_mem_v, n_s, t_s, MEM_SAMPLE_SEQS)
    y_sample = _ffn_out(h2s, oms, out_w, tm).reshape(n_s, t_s, D_MODEL)

    return (y_prompt, y_sample, p_sre, p_sim, p_wk, p_wv,
            from_stored(pm_k), from_stored(pm_v),
            s_sre.reshape(n_s, N_SSM_GROUPS, SSM_STATE), s_sim.reshape(n_s, N_SSM_GROUPS, SSM_STATE),
            s_wk, s_wv)
```
